```python
import math
import jax, jax.numpy as jnp
from jax import lax
import numpy as np

D_MODEL = 1024
BATCH = 4
SEQ = 4096
DEPTH = 4
DEC_BATCH = 32
DEC_SEQ = 8
PAST_LEN = 8192
PAGE_SIZE = 128

HEAD_DIM = 64
MIX_HALF = D_MODEL // 2
A_HEADS = MIX_HALF // HEAD_DIM
B_HEADS = MIX_HALF // HEAD_DIM
N_IN_COLS = 7 * MIX_HALF
MOBA_BLOCK = 256
MOBA_TOPK = 3
MOBA_QCHUNK = 64
RET_CHUNK = 128
S5_GROUP = 16
S5_GROUPS = D_MODEL // S5_GROUP
S5_STATE = 64
S5_CHUNK = 256
D_FF = ((8 * D_MODEL // 3 + 127) // 128) * 128
ROPE_THETA = 10000.0
LN_EPS = 1e-5
GN_EPS = 1e-6
NEG_INF = -1e30
N_AB_LAYERS = (DEPTH + 1) // 2
N_SSM_LAYERS = DEPTH // 2
DEEPNORM_ALPHA = (2 * DEPTH) ** 0.25
DEEPNORM_BETA = (8 * DEPTH) ** -0.25

kernel_name = 'moba_retnet_s5_macaron_deepnorm_decoder_step'


def layer_norm(x, g, b):
    xf = x.astype(jnp.float32)
    mu = jnp.mean(xf, -1, keepdims=True)
    var = jnp.mean(jnp.square(xf - mu), -1, keepdims=True)
    y = (xf - mu) * lax.rsqrt(var + LN_EPS) * g.astype(jnp.float32) + b.astype(jnp.float32)
    return y.astype(x.dtype)


def swiglu(x, w_gate, w_up, w_down):
    return (jax.nn.silu(x @ w_gate) * (x @ w_up)) @ w_down


def rope(x, pos):
    half = x.shape[-1] // 2
    inv = ROPE_THETA ** (-jnp.arange(half, dtype=jnp.float32) / half)
    ang = pos.astype(jnp.float32)[:, None] * inv[None, :]
    cos = jnp.cos(ang)[None, :, None, :]
    sin = jnp.sin(ang)[None, :, None, :]
    xf = x.astype(jnp.float32)
    x1, x2 = xf[..., :half], xf[..., half:]
    return jnp.concatenate([x1 * cos - x2 * sin, x1 * sin + x2 * cos], -1).astype(x.dtype)


def moba_seq(q, k, v, q_pos):
    n_heads, hd = q.shape[1], q.shape[2]
    nb = k.shape[0] // MOBA_BLOCK
    kb = k.reshape(nb, MOBA_BLOCK, n_heads, hd)
    vb = v.reshape(nb, MOBA_BLOCK, n_heads, hd)
    k_mean = jnp.mean(kb.astype(jnp.float32), axis=1)
    kh = jnp.transpose(kb, (2, 0, 1, 3))
    vh = jnp.transpose(vb, (2, 0, 1, 3))
    n_top = min(MOBA_TOPK, nb)
    n_q = q.shape[0]
    qc = math.gcd(n_q, MOBA_QCHUNK)
    head_ix = jnp.arange(n_heads)[None, :, None]
    blk_ix = jnp.arange(nb)
    scale = hd ** -0.5

    def chunk(args):
        qq, qp = args
        qf = qq.astype(jnp.float32) * scale
        q_blk = qp // MOBA_BLOCK
        own = q_blk[0]
        gate = jnp.einsum('qhd,nhd->qhn', qf, k_mean)
        gate = jnp.where(blk_ix[None, None, :] < q_blk[:, None, None], gate, NEG_INF)
        _, sel = lax.top_k(gate, n_top)
        sel_ok = sel < q_blk[:, None, None]
        kg = kh[head_ix, sel].astype(jnp.float32)
        vg = vh[head_ix, sel].astype(jnp.float32)
        s_sel = jnp.einsum('qhd,qhskd->qhsk', qf, kg)
        s_sel = jnp.where(sel_ok[..., None], s_sel, NEG_INF)
        k_own = lax.dynamic_slice_in_dim(k, own * MOBA_BLOCK, MOBA_BLOCK, 0).astype(jnp.float32)
        v_own = lax.dynamic_slice_in_dim(v, own * MOBA_BLOCK, MOBA_BLOCK, 0).astype(jnp.float32)
        s_own = jnp.einsum('qhd,khd->qhk', qf, k_own)
        k_pos = own * MOBA_BLOCK + jnp.arange(MOBA_BLOCK)
        s_own = jnp.where(k_pos[None, None, :] <= qp[:, None, None], s_own, NEG_INF)
        logits = jnp.concatenate([s_sel.reshape(qc, n_heads, n_top * MOBA_BLOCK), s_own], axis=-1)
        p = jax.nn.softmax(logits, axis=-1)
        p_sel = p[..., : n_top * MOBA_BLOCK].reshape(qc, n_heads, n_top, MOBA_BLOCK)
        out = (jnp.einsum('qhsk,qhskd->qhd', p_sel, vg)
               + jnp.einsum('qhk,khd->qhd', p[..., n_top * MOBA_BLOCK:], v_own))
        return out.astype(qq.dtype)

    out = lax.map(chunk, (q.reshape(n_q // qc, qc, n_heads, hd), q_pos.reshape(n_q // qc, qc)))
    return out.reshape(n_q, n_heads, hd)


def moba_prompt(q, k, v, pos):
    pad = (-k.shape[1]) % MOBA_BLOCK
    widths = ((0, 0), (0, pad), (0, 0), (0, 0))
    kp = jnp.pad(k, widths)
    vp = jnp.pad(v, widths)
    return lax.map(lambda a: moba_seq(a[0], a[1], a[2], pos), (q, kp, vp))


def moba_sample(q, k_new, v_new, k_pool, v_pool, page_table, pos):
    def one(a):
        qq, kn, vn, pt = a
        hs = kn.shape[1:]
        k_full = jnp.concatenate([k_pool[pt].reshape((-1,) + hs).astype(kn.dtype), kn], 0)
        v_full = jnp.concatenate([v_pool[pt].reshape((-1,) + hs).astype(vn.dtype), vn], 0)
        pad = (-k_full.shape[0]) % MOBA_BLOCK
        k_full = jnp.pad(k_full, ((0, pad), (0, 0), (0, 0)))
        v_full = jnp.pad(v_full, ((0, pad), (0, 0), (0, 0)))
        return moba_seq(qq, k_full, v_full, pos)
    return lax.map(one, (q, k_new, v_new, page_table))


def retention(q, k, v, s0):
    bn, n_len, n_heads, dk = q.shape
    dv = v.shape[-1]
    c = math.gcd(n_len, RET_CHUNK)
    n = n_len // c
    log_g = jnp.log1p(-jnp.exp2(-5.0 - jnp.arange(n_heads, dtype=jnp.float32)))
    idx = jnp.arange(c, dtype=jnp.float32)
    diff = idx[:, None] - idx[None, :]
    dmask = jnp.where(diff >= 0, jnp.exp(log_g[:, None, None] * jnp.maximum(diff, 0.0)), 0.0)
    q_dec = jnp.exp(log_g[None, :] * (idx[:, None] + 1.0))
    k_dec = jnp.exp(log_g[None, :] * (c - 1.0 - idx[:, None]))
    c_dec = jnp.exp(log_g * c)
    qs = jnp.moveaxis(q.astype(jnp.float32).reshape(bn, n, c, n_heads, dk), 1, 0)
    ks = jnp.moveaxis((k.astype(jnp.float32) * dk ** -0.5).reshape(bn, n, c, n_heads, dk), 1, 0)
    vs = jnp.moveaxis(v.astype(jnp.float32).reshape(bn, n, c, n_heads, dv), 1, 0)

    def step(s, inp):
        qc, kc, vc = inp
        inner = jnp.einsum('bnhd,bmhd->bhnm', qc, kc) * dmask[None]
        o = (jnp.einsum('bhnm,bmhe->bnhe', inner, vc)
             + jnp.einsum('bnhd,bhde->bnhe', qc, s) * q_dec[None, :, :, None])
        s = c_dec[None, :, None, None] * s + jnp.einsum('bmhd,bmhe->bhde', kc * k_dec[None, :, :, None], vc)
        return s, o

    s_fin, o = lax.scan(step, s0.astype(jnp.float32), (qs, ks, vs))
    return jnp.moveaxis(o, 0, 1).reshape(bn, n_len, n_heads, dv), s_fin


def ab_project(x, pos, w_in):
    bn, n_len, _ = x.shape
    qa, ka, va, qb, kb, vb, gb = jnp.split(x @ w_in, 7, axis=-1)
    heads = lambda t: t.reshape(bn, n_len, -1, HEAD_DIM)
    return (rope(heads(qa), pos), rope(heads(ka), pos), heads(va),
            rope(heads(qb), pos), rope(heads(kb), pos), heads(vb), gb)


def ab_merge(attn, ret, gb, w_out):
    bn, n_len = attn.shape[:2]
    mu = jnp.mean(ret, -1, keepdims=True)
    var = jnp.mean(jnp.square(ret - mu), -1, keepdims=True)
    rn = ((ret - mu) * lax.rsqrt(var + GN_EPS)).reshape(bn, n_len, MIX_HALF).astype(gb.dtype)
    out_b = jax.nn.silu(gb) * rn
    return jnp.concatenate([attn.reshape(bn, n_len, MIX_HALF), out_b], -1) @ w_out


def _s5_combine(e1, e2):
    a1, b1 = e1
    a2, b2 = e2
    return a1 * a2, a2 * b1 + b2


def s5_ssm(u, s0_re, s0_im, lam_re, lam_im, b_re, b_im, c_re, c_im, d_skip, log_dt):
    bn, n_len, _ = u.shape
    uf = u.astype(jnp.float32)
    lam = lax.complex(lam_re.astype(jnp.float32), lam_im.astype(jnp.float32))
    dt = jnp.exp(log_dt.astype(jnp.float32))[:, None]
    lam_bar = jnp.exp(lam * dt)
    b_bar = ((lam_bar - 1.0) / lam)[:, :, None] * lax.complex(b_re.astype(jnp.float32), b_im.astype(jnp.float32))
    cm = lax.complex(c_re.astype(jnp.float32), c_im.astype(jnp.float32))
    c = math.gcd(n_len, S5_CHUNK)
    n = n_len // c
    us = jnp.moveaxis(uf.reshape(bn, n, c, S5_GROUPS, S5_GROUP), 1, 0)
    a_elems = jnp.broadcast_to(lam_bar, (bn, c, S5_GROUPS, S5_STATE))

    def step(s, uc):
        bu = jnp.einsum('bcgh,gph->bcgp', uc.astype(jnp.complex64), b_bar)
        a_cum, s_loc = lax.associative_scan(_s5_combine, (a_elems, bu), axis=1)
        s_all = s_loc + a_cum * s[:, None]
        y = jnp.real(jnp.einsum('bcgp,ghp->bcgh', s_all, cm))
        return s_all[:, -1], y

    s0 = lax.complex(s0_re.astype(jnp.float32), s0_im.astype(jnp.float32))
    s_fin, ys = lax.scan(step, s0, us)
    y = jnp.moveaxis(ys, 0, 1).reshape(bn, n_len, D_MODEL) + uf * d_skip.astype(jnp.float32)
    return y, jnp.real(s_fin), jnp.imag(s_fin)


def s5_layer(x, s0_re, s0_im, lam_re, lam_im, b_re, b_im, c_re, c_im, d_skip, log_dt, w_out, w_gate):
    y, s_re, s_im = s5_ssm(x, s0_re, s0_im, lam_re, lam_im, b_re, b_im, c_re, c_im, d_skip, log_dt)
    g = jax.nn.gelu(y.astype(x.dtype))
    return (g @ w_out) * jax.nn.sigmoid(g @ w_gate), s_re, s_im


def setup_inputs(seed: int = 0) -> dict:
    key = jax.random.key(seed)
    ks = jax.random.split(key, 32)
    f32 = jnp.float32
    n_pages = PAST_LEN // PAGE_SIZE
    n_pool = (DEC_BATCH * n_pages * 5) // 4
    nrm = lambda k, shape, s: jax.random.normal(k, shape, f32) * s
    x_prompt = nrm(ks[0], (BATCH, SEQ, D_MODEL), 1.0)
    x_sample = nrm(ks[1], (DEC_BATCH, DEC_SEQ, D_MODEL), 1.0)
    cache_k = nrm(ks[2], (N_AB_LAYERS, n_pool, PAGE_SIZE, A_HEADS, HEAD_DIM), 1.0)
    cache_v = nrm(ks[3], (N_AB_LAYERS, n_pool, PAGE_SIZE, A_HEADS, HEAD_DIM), 1.0)
    page_table = jax.random.permutation(ks[4], n_pool)[: DEC_BATCH * n_pages].reshape(DEC_BATCH, n_pages).astype(jnp.int32)
    state_ret = nrm(ks[5], (N_AB_LAYERS, DEC_BATCH, B_HEADS, HEAD_DIM, HEAD_DIM), 0.5)
    state_s5_re = nrm(ks[6], (N_SSM_LAYERS, DEC_BATCH, S5_GROUPS, S5_STATE), 0.1)
    state_s5_im = nrm(ks[7], (N_SSM_LAYERS, DEC_BATCH, S5_GROUPS, S5_STATE), 0.1)
    ffn1_w_gate = nrm(ks[8], (DEPTH, D_MODEL, D_FF), D_MODEL ** -0.5)
    ffn1_w_up = nrm(ks[9], (DEPTH, D_MODEL, D_FF), D_MODEL ** -0.5)
    ffn1_w_down = nrm(ks[10], (DEPTH, D_FF, D_MODEL), D_FF ** -0.5 * DEEPNORM_BETA)
    ffn2_w_gate = nrm(ks[11], (DEPTH, D_MODEL, D_FF), D_MODEL ** -0.5)
    ffn2_w_up = nrm(ks[12], (DEPTH, D_MODEL, D_FF), D_MODEL ** -0.5)
    ffn2_w_down = nrm(ks[13], (DEPTH, D_FF, D_MODEL), D_FF ** -0.5 * DEEPNORM_BETA)
    ln_g = 1.0 + nrm(ks[14], (DEPTH, 3, D_MODEL), 0.02)
    ln_b = nrm(ks[15], (DEPTH, 3, D_MODEL), 0.02)
    w_in_ab = nrm(ks[16], (N_AB_LAYERS, D_MODEL, N_IN_COLS), D_MODEL ** -0.5)
    w_out_ab = nrm(ks[17], (N_AB_LAYERS, 2 * MIX_HALF, D_MODEL), (2 * MIX_HALF) ** -0.5 * DEEPNORM_BETA)
    s5_lam_re = -0.5 + nrm(ks[18], (N_SSM_LAYERS, S5_GROUPS, S5_STATE), 0.01)
    s5_lam_im = math.pi * jnp.arange(S5_STATE, dtype=f32) + nrm(ks[19], (N_SSM_LAYERS, S5_GROUPS, S5_STATE), 0.01)
    s5_b_re = nrm(ks[20], (N_SSM_LAYERS, S5_GROUPS, S5_STATE, S5_GROUP), (2 * S5_GROUP) ** -0.5)
    s5_b_im = nrm(ks[21], (N_SSM_LAYERS, S5_GROUPS, S5_STATE, S5_GROUP), (2 * S5_GROUP) ** -0.5)
    s5_c_re = nrm(ks[22], (N_SSM_LAYERS, S5_GROUPS, S5_GROUP, S5_STATE), S5_STATE ** -0.5)
    s5_c_im = nrm(ks[23], (N_SSM_LAYERS, S5_GROUPS, S5_GROUP, S5_STATE), S5_STATE ** -0.5)
    s5_d = nrm(ks[24], (N_SSM_LAYERS, D_MODEL), 1.0)
    s5_log_dt = jax.random.uniform(ks[25], (N_SSM_LAYERS, S5_GROUPS), f32, math.log(1e-3), math.log(1e-1))
    s5_w_out = nrm(ks[26], (N_SSM_LAYERS, D_MODEL, D_MODEL), D_MODEL ** -0.5 * DEEPNORM_BETA)
    s5_w_gate = nrm(ks[27], (N_SSM_LAYERS, D_MODEL, D_MODEL), D_MODEL ** -0.5)
    return {'x_prompt': x_prompt, 'x_sample': x_sample, 'cache_k': cache_k, 'cache_v': cache_v,
            'page_table': page_table, 'state_ret': state_ret, 'state_s5_re': state_s5_re,
            'state_s5_im': state_s5_im, 'ffn1_w_gate': ffn1_w_gate, 'ffn1_w_up': ffn1_w_up,
            'ffn1_w_down': ffn1_w_down, 'ffn2_w_gate': ffn2_w_gate, 'ffn2_w_up': ffn2_w_up,
            'ffn2_w_down': ffn2_w_down, 'ln_g': ln_g, 'ln_b': ln_b, 'w_in_ab': w_in_ab,
            'w_out_ab': w_out_ab, 's5_lam_re': s5_lam_re, 's5_lam_im': s5_lam_im, 's5_b_re': s5_b_re,
            's5_b_im': s5_b_im, 's5_c_re': s5_c_re, 's5_c_im': s5_c_im, 's5_d': s5_d,
            's5_log_dt': s5_log_dt, 's5_w_out': s5_w_out, 's5_w_gate': s5_w_gate}


def reference(x_prompt, x_sample, cache_k, cache_v, page_table, state_ret, state_s5_re, state_s5_im,
              ffn1_w_gate, ffn1_w_up, ffn1_w_down, ffn2_w_gate, ffn2_w_up, ffn2_w_down, ln_g, ln_b,
              w_in_ab, w_out_ab, s5_lam_re, s5_lam_im, s5_b_re, s5_b_im, s5_c_re, s5_c_im, s5_d,
              s5_log_dt, s5_w_out, s5_w_gate):
    bp, lp, _ = x_prompt.shape
    bs, ls, _ = x_sample.shape
    page = cache_k.shape[2]
    past_len = page_table.shape[1] * page
    pos_p = jnp.arange(lp, dtype=jnp.int32)
    pos_s = past_len + jnp.arange(ls, dtype=jnp.int32)
    xp, xs = x_prompt, x_sample
    k_p, v_p, k_s, v_s, r_p, r_s = [], [], [], [], [], []
    sre_p, sim_p, sre_s, sim_s = [], [], [], []
    for layer in range(DEPTH):
        li = layer // 2
        f1 = (ffn1_w_gate[layer], ffn1_w_up[layer], ffn1_w_down[layer])
        f2 = (ffn2_w_gate[layer], ffn2_w_up[layer], ffn2_w_down[layer])
        xp = layer_norm(DEEPNORM_ALPHA * xp + 0.5 * swiglu(xp, *f1), ln_g[layer, 0], ln_b[layer, 0])
        xs = layer_norm(DEEPNORM_ALPHA * xs + 0.5 * swiglu(xs, *f1), ln_g[layer, 0], ln_b[layer, 0])
        if layer % 2 == 0:
            qa, ka, va, qb, kb, vb, gb = ab_project(xp, pos_p, w_in_ab[li])
            attn = moba_prompt(qa, ka, va, pos_p)
            ret, s_fin = retention(qb, kb, vb, jnp.zeros((bp, B_HEADS, HEAD_DIM, HEAD_DIM), jnp.float32))
            mix_p = ab_merge(attn, ret, gb, w_out_ab[li])
            k_p.append(ka.reshape(bp, lp // page, page, A_HEADS, HEAD_DIM))
            v_p.append(va.reshape(bp, lp // page, page, A_HEADS, HEAD_DIM))
            r_p.append(s_fin)
            qa, ka, va, qb, kb, vb, gb = ab_project(xs, pos_s, w_in_ab[li])
            attn = moba_sample(qa, ka, va, cache_k[li], cache_v[li], page_table, pos_s)
            ret, s_fin = retention(qb, kb, vb, state_ret[li])
            mix_s = ab_merge(attn, ret, gb, w_out_ab[li])
            k_s.append(ka)
            v_s.append(va)
            r_s.append(s_fin)
        else:
            ssm = (s5_lam_re[li], s5_lam_im[li], s5_b_re[li], s5_b_im[li], s5_c_re[li], s5_c_im[li],
                   s5_d[li], s5_log_dt[li], s5_w_out[li], s5_w_gate[li])
            zero = jnp.zeros((bp, S5_GROUPS, S5_STATE), jnp.float32)
            mix_p, a_re, a_im = s5_layer(xp, zero, zero, *ssm)
            sre_p.append(a_re)
            sim_p.append(a_im)
            mix_s, a_re, a_im = s5_layer(xs, state_s5_re[li], state_s5_im[li], *ssm)
            sre_s.append(a_re)
            sim_s.append(a_im)
        xp = layer_norm(DEEPNORM_ALPHA * xp + mix_p, ln_g[layer, 1], ln_b[layer, 1])
        xs = layer_norm(DEEPNORM_ALPHA * xs + mix_s, ln_g[layer, 1], ln_b[layer, 1])
        xp = layer_norm(DEEPNORM_ALPHA * xp + 0.5 * swiglu(xp, *f2), ln_g[layer, 2], ln_b[layer, 2])
        xs = layer_norm(DEEPNORM_ALPHA * xs + 0.5 * swiglu(xs, *f2), ln_g[layer, 2], ln_b[layer, 2])
    new_cache_k_prompt = jnp.stack(k_p)
    new_cache_v_prompt = jnp.stack(v_p)
    new_cache_k_sample = jnp.stack(k_s)
    new_cache_v_sample = jnp.stack(v_s)
    new_state_ret_prompt = jnp.stack(r_p)
    new_state_ret_sample = jnp.stack(r_s)
    new_state_s5_re_prompt = jnp.stack(sre_p)
    new_state_s5_im_prompt = jnp.stack(sim_p)
    new_state_s5_re_sample = jnp.stack(sre_s)
    new_state_s5_im_sample = jnp.stack(sim_s)
    return (xp, xs, new_cache_k_prompt, new_cache_v_prompt, new_cache_k_sample, new_cache_v_sample,
            new_state_ret_prompt, new_state_ret_sample, new_state_s5_re_prompt, new_state_s5_im_prompt,
            new_state_s5_re_sample, new_state_s5_im_sample)
```

```python
import functools
import math

import jax
import jax.numpy as jnp
from jax import lax
from jax.experimental import pallas as pl
from jax.experimental.pallas import tpu as pltpu

F32 = jnp.float32
BF16 = jnp.bfloat16

HEAD_DIM = 64
HEAD_PAIR = 2 * HEAD_DIM
MOBA_BLOCK = 256
MOBA_TOPK = 3
RET_CHUNK = 128
S5_GROUP = 16
S5_STATE = 64
ROPE_THETA = 10000.0
LN_EPS = 1e-5
GN_EPS = 1e-6
NEG_INF = -1e30
SUBLANES = 8
LANES = 128
VMEM_LIMIT = 48 * 1024 * 1024


def _dot(a, b, precision=None):
    return jnp.dot(a, b, preferred_element_type=F32, precision=precision)


def _dot_nt(a, b, precision=None):
    return lax.dot_general(a, b, (((1,), (1,)), ((), ())), preferred_element_type=F32, precision=precision)


def _dot_tn(a, b):
    return lax.dot_general(a, b, (((0,), (0,)), ((), ())), preferred_element_type=F32)


def _layer_norm(r, g, b):
    mu = jnp.mean(r, -1, keepdims=True)
    d = r - mu
    var = jnp.mean(d * d, -1, keepdims=True)
    return d * lax.rsqrt(var + LN_EPS) * g + b


def _silu(x):
    return x * jax.nn.sigmoid(x)


def _params(*sem):
    return pltpu.CompilerParams(dimension_semantics=sem, vmem_limit_bytes=VMEM_LIMIT)


def _ffn_ln_kernel(x_ref, wg_ref, wu_ref, wd_ref, g_ref, b_ref, o_ref, xb_ref, acc_ref, *, alpha):
    j = pl.program_id(1)

    @pl.when(j == 0)
    def _():
        xb_ref[...] = x_ref[...].astype(BF16)
        acc_ref[...] = jnp.zeros_like(acc_ref)

    xb = xb_ref[...]
    hg = _dot(xb, wg_ref[...])
    hu = _dot(xb, wu_ref[...])
    a = (_silu(hg) * hu).astype(BF16)
    acc_ref[...] += _dot(a, wd_ref[...])

    @pl.when(j == pl.num_programs(1) - 1)
    def _():
        r = alpha * x_ref[...] + 0.5 * acc_ref[...]
        o_ref[...] = _layer_norm(r, g_ref[...], b_ref[...])


def ffn_ln(x, wg, wu, wd, g, b, *, alpha, tm, tf=256):
    t, d = x.shape
    f = wg.shape[1]
    return pl.pallas_call(
        functools.partial(_ffn_ln_kernel, alpha=alpha),
        grid=(t // tm, f // tf),
        in_specs=[
            pl.BlockSpec((tm, d), lambda i, j: (i, 0)),
            pl.BlockSpec((d, tf), lambda i, j: (0, j)),
            pl.BlockSpec((d, tf), lambda i, j: (0, j)),
            pl.BlockSpec((tf, d), lambda i, j: (j, 0)),
            pl.BlockSpec((1, d), lambda i, j: (0, 0)),
            pl.BlockSpec((1, d), lambda i, j: (0, 0)),
        ],
        out_specs=pl.BlockSpec((tm, d), lambda i, j: (i, 0)),
        out_shape=jax.ShapeDtypeStruct((t, d), F32),
        scratch_shapes=[pltpu.VMEM((tm, d), BF16), pltpu.VMEM((tm, d), F32)],
        compiler_params=_params("parallel", "arbitrary"),
    )(x, wg, wu, wd, g, b)


def _rope(y, cos, sin_signed):
    width = y.shape[-1]
    lane = lax.broadcasted_iota(jnp.int32, (1, width), 1)
    first = (lane % HEAD_DIM) < (HEAD_DIM // 2)
    rot = jnp.where(first, pltpu.roll(y, width - HEAD_DIM // 2, 1), pltpu.roll(y, HEAD_DIM // 2, 1))
    return y * cos + rot * sin_signed


def _ab_proj_kernel(x_ref, w_ref, cos_ref, sin_ref, qa_ref, ka_ref, va_ref, kbf_ref, vbf_ref, ksum_ref,
                    qb_ref, kb_ref, vb_ref, gb_ref, *, half, scale):
    xb = x_ref[...].astype(BF16)
    cos = cos_ref[...]
    sin = sin_ref[...]
    col = lambda c: _dot(xb, w_ref[:, c * half:(c + 1) * half])
    qa_ref[...] = _rope(col(0), cos, sin) * scale
    ka = _rope(col(1), cos, sin)
    ka_ref[...] = ka
    kbf_ref[...] = ka.astype(BF16)
    nblk = ka.shape[0] // MOBA_BLOCK
    for n in range(nblk):
        ksum_ref[n] = jnp.sum(ka[n * MOBA_BLOCK:(n + 1) * MOBA_BLOCK], axis=0, keepdims=True)
    va = col(2)
    va_ref[...] = va
    vbf_ref[...] = va.astype(BF16)
    qb_ref[...] = _rope(col(3), cos, sin)
    kb_ref[...] = _rope(col(4), cos, sin) * scale
    vb_ref[...] = col(5)
    gb_ref[...] = col(6)


def ab_project(x, w_in, cos, sin, *, tm, seq_len):
    t, d = x.shape
    half = w_in.shape[1] // 7
    tiles_per_seq = max(seq_len // tm, 1)
    tab = pl.BlockSpec((tm, half), lambda i: (i % tiles_per_seq, 0))
    row = pl.BlockSpec((tm, half), lambda i: (i, 0))
    nblk = tm // MOBA_BLOCK
    f32o = jax.ShapeDtypeStruct((t, half), F32)
    bfo = jax.ShapeDtypeStruct((t, half), BF16)
    return pl.pallas_call(
        functools.partial(_ab_proj_kernel, half=half, scale=HEAD_DIM ** -0.5),
        grid=(t // tm,),
        in_specs=[pl.BlockSpec((tm, d), lambda i: (i, 0)),
                  pl.BlockSpec((d, 7 * half), lambda i: (0, 0)), tab, tab],
        out_specs=[row, row, row, row, row,
                   pl.BlockSpec((nblk, 1, half), lambda i: (i, 0, 0)),
                   row, row, row, row],
        out_shape=[f32o, f32o, f32o, bfo, bfo,
                   jax.ShapeDtypeStruct((t // MOBA_BLOCK, 1, half), F32),
                   f32o, f32o, f32o, f32o],
        compiler_params=_params("parallel"),
    )(x, w_in, cos, sin)


def _topk_select(gate, n_valid):
    nb = gate.shape[1]
    blk = lax.broadcasted_iota(jnp.int32, (1, nb), 1)
    gate = jnp.where(blk < n_valid, gate, NEG_INF)
    rank = jnp.zeros(gate.shape, jnp.int32)
    for m in range(nb):
        gm = gate[:, m:m + 1]
        ahead = (gm > gate) | ((gm == gate) & (m < blk))
        rank = rank + ahead.astype(jnp.int32)
    return ((rank < MOBA_TOPK) & (blk < n_valid)).astype(F32)


def _moba_prompt_kernel(q_ref, ksum_ref, k_ref, v_ref, o_ref):
    j = pl.program_id(1)
    nb = ksum_ref.shape[0]
    tq = q_ref.shape[0]
    lane = lax.broadcasted_iota(jnp.int32, (1, HEAD_PAIR), 1)
    blk = lax.broadcasted_iota(jnp.int32, (1, nb), 1)
    row = lax.broadcasted_iota(jnp.int32, (tq, MOBA_BLOCK), 0)
    colk = lax.broadcasted_iota(jnp.int32, (tq, MOBA_BLOCK), 1)
    causal = colk <= row
    own0 = pl.multiple_of(j * MOBA_BLOCK, MOBA_BLOCK)
    for hp in range(q_ref.shape[1] // HEAD_PAIR):
        lanes = slice(hp * HEAD_PAIR, (hp + 1) * HEAD_PAIR)
        qp = q_ref[:, lanes]
        kmean = ksum_ref[:, 0, lanes] * (1.0 / MOBA_BLOCK)
        k_own = k_ref[pl.ds(own0, MOBA_BLOCK), lanes]
        v_own = v_ref[pl.ds(own0, MOBA_BLOCK), lanes]
        out = jnp.zeros((tq, HEAD_PAIR), F32)
        for half in range(2):
            hm = (lane // HEAD_DIM) == half
            qm = jnp.where(hm, qp, 0.0)
            sel = _topk_select(_dot_nt(qm, kmean, precision=lax.Precision.HIGHEST), j)
            qmb = qm.astype(BF16)
            s = jnp.where(causal, _dot_nt(qmb, k_own), NEG_INF)
            m0 = jnp.max(s, axis=1, keepdims=True)
            p = jnp.exp(s - m0)
            l0 = jnp.sum(p, axis=1, keepdims=True)
            acc0 = _dot(p.astype(BF16), v_own)

            def body(n, carry, qmb=qmb, sel=sel, lanes=lanes):
                m, l, acc = carry
                r0 = pl.multiple_of(n * MOBA_BLOCK, MOBA_BLOCK)
                k_n = k_ref[pl.ds(r0, MOBA_BLOCK), lanes]
                v_n = v_ref[pl.ds(r0, MOBA_BLOCK), lanes]
                sel_n = jnp.sum(jnp.where(blk == n, sel, 0.0), axis=1, keepdims=True)
                s = jnp.where(sel_n > 0.0, _dot_nt(qmb, k_n), NEG_INF)
                m_new = jnp.maximum(m, jnp.max(s, axis=1, keepdims=True))
                a = jnp.exp(m - m_new)
                p = jnp.exp(s - m_new)
                l = a * l + jnp.sum(p, axis=1, keepdims=True)
                acc = a * acc + _dot(p.astype(BF16), v_n)
                return m_new, l, acc

            m, l, acc = lax.fori_loop(0, j, body, (m0, l0, acc0))
            out = out + jnp.where(hm, acc / l, 0.0)
        o_ref[:, lanes] = out


def moba_prompt(q, ksum, kbf, vbf, *, batch, seq_len):
    t, w = q.shape
    nb = seq_len // MOBA_BLOCK
    return pl.pallas_call(
        _moba_prompt_kernel,
        grid=(batch, nb),
        in_specs=[pl.BlockSpec((MOBA_BLOCK, w), lambda b, j: (b * nb + j, 0)),
                  pl.BlockSpec((nb, 1, w), lambda b, j: (b, 0, 0)),
                  pl.BlockSpec((seq_len, w), lambda b, j: (b, 0)),
                  pl.BlockSpec((seq_len, w), lambda b, j: (b, 0))],
        out_specs=pl.BlockSpec((MOBA_BLOCK, w), lambda b, j: (b * nb + j, 0)),
        out_shape=jax.ShapeDtypeStruct((t, w), F32),
        compiler_params=_params("parallel", "arbitrary"),
    )(q, ksum, kbf, vbf)


def _moba_sample_kernel(pt_ref, q_ref, knew_ref, vnew_ref, *rest, pages_per_step, page, n_heads):
    del pt_ref
    kpages = rest[:pages_per_step]
    vpages = rest[pages_per_step:2 * pages_per_step]
    o_ref = rest[2 * pages_per_step]
    qf_ref, qb_ref, ksum_ref, m_ref, l_ref, acc_ref, kpad_ref, vpad_ref = rest[2 * pages_per_step + 1:]
    s_idx = pl.program_id(1)
    n_steps = pl.num_programs(1)
    lq, w = q_ref.shape
    rows = n_heads * lq
    nb = ksum_ref.shape[0]
    pages_per_blk = MOBA_BLOCK // page
    rowhead = lax.broadcasted_iota(jnp.int32, (rows, 1), 0) // lq
    lanehead = lax.broadcasted_iota(jnp.int32, (1, w), 1) // HEAD_DIM

    @pl.when(s_idx == 0)
    def _():
        qt = jnp.concatenate([q_ref[...]] * n_heads, axis=0)
        qbd = jnp.where(rowhead == lanehead, qt, 0.0)
        qf_ref[...] = qbd
        qb_ref[...] = qbd.astype(BF16)
        ksum_ref[...] = jnp.zeros_like(ksum_ref)

    qb = qb_ref[...]

    def partial_softmax(s, v):
        m = jnp.max(s, axis=1, keepdims=True)
        e = jnp.exp(s - m)
        return m, jnp.sum(e, axis=1, keepdims=True), _dot(e.astype(BF16), v)

    for i in range(pages_per_step):
        p_idx = s_idx * pages_per_step + i
        kf = kpages[i][0]
        b_idx = p_idx // pages_per_blk
        ksum_ref[pl.ds(b_idx, 1), :] += jnp.sum(kf, axis=0, keepdims=True)
        m, l, acc = partial_softmax(_dot_nt(qb, kf.astype(BF16)), vpages[i][0].astype(BF16))
        m_ref[p_idx] = m
        l_ref[p_idx] = l
        acc_ref[p_idx] = acc

    @pl.when(s_idx == n_steps - 1)
    def _():
        kpad_ref[...] = jnp.zeros_like(kpad_ref)
        vpad_ref[...] = jnp.zeros_like(vpad_ref)
        kpad_ref[0:lq, :] = knew_ref[...]
        vpad_ref[0:lq, :] = vnew_ref[...]
        s_own = _dot_nt(qb, kpad_ref[...].astype(BF16))
        tq = lax.broadcasted_iota(jnp.int32, s_own.shape, 0) % lq
        tk = lax.broadcasted_iota(jnp.int32, s_own.shape, 1)
        s_own = jnp.where(tk <= tq, s_own, NEG_INF)
        m_own, l_own, acc_own = partial_softmax(s_own, vpad_ref[...].astype(BF16))

        kmean = ksum_ref[...] * (1.0 / MOBA_BLOCK)
        sel = _topk_select(_dot_nt(qf_ref[...], kmean, precision=lax.Precision.HIGHEST), nb)
        n_pages = m_ref.shape[0]
        sel_p = [sel[:, p // pages_per_blk:p // pages_per_blk + 1] > 0.0 for p in range(n_pages)]
        m_all = m_own
        for p in range(n_pages):
            m_all = jnp.maximum(m_all, jnp.where(sel_p[p], m_ref[p], NEG_INF))
        w_own = jnp.exp(m_own - m_all)
        num = w_own * acc_own
        den = w_own * l_own
        for p in range(n_pages):
            wp = jnp.where(sel_p[p], jnp.exp(m_ref[p] - m_all), 0.0)
            num = num + wp * acc_ref[p]
            den = den + wp * l_ref[p]
        o_all = num / den
        out = jnp.zeros((lq, w), F32)
        for h in range(n_heads):
            out = out + jnp.where(lanehead == h, o_all[h * lq:(h + 1) * lq], 0.0)
        o_ref[...] = out


def moba_sample(q, knew, vnew, pool_k, pool_v, page_ids, *, batch, n_pages, pages_per_step=8):
    t, w = q.shape
    lq = t // batch
    page = pool_k.shape[1]
    n_heads = w // HEAD_DIM
    rows = n_heads * lq
    nb = n_pages * page // MOBA_BLOCK
    seq_blk = pl.BlockSpec((lq, w), lambda b, s, pt: (b, 0))

    def page_spec(i):
        return pl.BlockSpec((1, page, w), lambda b, s, pt: (pt[b * n_pages + s * pages_per_step + i], 0, 0))

    grid_spec = pltpu.PrefetchScalarGridSpec(
        num_scalar_prefetch=1,
        grid=(batch, n_pages // pages_per_step),
        in_specs=[seq_blk, seq_blk, seq_blk] + [page_spec(i) for i in range(pages_per_step)] * 2,
        out_specs=seq_blk,
        scratch_shapes=[pltpu.VMEM((rows, w), F32), pltpu.VMEM((rows, w), BF16), pltpu.VMEM((nb, w), F32),
                        pltpu.VMEM((n_pages, rows, 1), F32), pltpu.VMEM((n_pages, rows, 1), F32),
                        pltpu.VMEM((n_pages, rows, w), F32),
                        pltpu.VMEM((LANES, w), F32), pltpu.VMEM((LANES, w), F32)],
    )
    return pl.pallas_call(
        functools.partial(_moba_sample_kernel, pages_per_step=pages_per_step, page=page, n_heads=n_heads),
        grid_spec=grid_spec,
        out_shape=jax.ShapeDtypeStruct((t, w), F32),
        compiler_params=_params("parallel", "arbitrary"),
    )(page_ids, q, knew, vnew, *([pool_k] * pages_per_step), *([pool_v] * pages_per_step))


def _retention_kernel(q_ref, k_ref, v_ref, g_ref, s0_ref, dmask_ref, qdec_ref, kdec_ref, cdec_ref,
                      o_ref, sout_ref, st_ref, *, n_seq, chunk, tiles, steps_per_seq):
    step = pl.program_id(0)
    tile_rows = n_seq * chunk
    n_pairs = q_ref.shape[1] // HEAD_PAIR
    lane = lax.broadcasted_iota(jnp.int32, (1, HEAD_PAIR), 1)
    rowh = lax.broadcasted_iota(jnp.int32, (HEAD_PAIR, 1), 0) // HEAD_DIM
    same_head = rowh == (lane // HEAD_DIM)
    rseq = lax.broadcasted_iota(jnp.int32, (tile_rows, 1), 0) // chunk

    @pl.when(step % steps_per_seq == 0)
    def _():
        st_ref[...] = s0_ref[...]

    for ti in range(tiles):
        rows = slice(ti * tile_rows, (ti + 1) * tile_rows)
        for hp in range(n_pairs):
            lanes = slice(hp * HEAD_PAIR, (hp + 1) * HEAD_PAIR)
            q = q_ref[rows, lanes]
            k = k_ref[rows, lanes]
            v = v_ref[rows, lanes]
            qb = q.astype(BF16)
            kb = k.astype(BF16)
            vb = v.astype(BF16)
            kd = (k * kdec_ref[:, lanes]).astype(BF16)
            cdec = cdec_ref[:, lanes]
            ret = jnp.zeros((tile_rows, HEAD_PAIR), F32)
            for half in range(2):
                hm = (lane // HEAD_DIM) == half
                inner = _dot_nt(jnp.where(hm, qb, jnp.zeros_like(qb)), kb)
                inner = inner * dmask_ref[2 * hp + half]
                ret = ret + jnp.where(hm, _dot(inner.astype(BF16), vb), 0.0)
            carried = []
            for s in range(n_seq):
                srows = slice(s * chunk, (s + 1) * chunk)
                state = st_ref[s, hp]
                carried.append(_dot(qb[srows], state.astype(BF16)))
                if n_seq == 1:
                    kd_s = kd
                else:
                    kd_s = jnp.where(rseq == s, kd, jnp.zeros_like(kd))
                st_ref[s, hp] = cdec * state + jnp.where(same_head, _dot_tn(kd_s, vb), 0.0)
            carried = carried[0] if n_seq == 1 else jnp.concatenate(carried, axis=0)
            ret = ret + carried * qdec_ref[:, lanes]
            normed = jnp.zeros_like(ret)
            for half in range(2):
                hm = (lane // HEAD_DIM) == half
                mu = jnp.sum(jnp.where(hm, ret, 0.0), axis=1, keepdims=True) * (1.0 / HEAD_DIM)
                d = jnp.where(hm, ret - mu, 0.0)
                var = jnp.sum(d * d, axis=1, keepdims=True) * (1.0 / HEAD_DIM)
                normed = normed + d * lax.rsqrt(var + GN_EPS)
            o_ref[rows, lanes] = _silu(g_ref[rows, lanes]) * normed

    @pl.when(step % steps_per_seq == steps_per_seq - 1)
    def _():
        sout_ref[...] = st_ref[...]


def _retention_tables(n_heads, n_seq, chunk):
    log_g = jnp.log1p(-jnp.exp2(-5.0 - jnp.arange(n_heads, dtype=F32)))
    idx = jnp.arange(chunk, dtype=F32)
    diff = idx[:, None] - idx[None, :]
    dmask = jnp.where(diff >= 0, jnp.exp(log_g[:, None, None] * jnp.maximum(diff, 0.0)), 0.0)
    q_dec = jnp.exp(log_g[None, :] * (idx[:, None] + 1.0))
    k_dec = jnp.exp(log_g[None, :] * (chunk - 1.0 - idx[:, None]))
    c_dec = jnp.exp(log_g * chunk)
    seq_eye = jnp.eye(n_seq, dtype=F32)
    dmask = jnp.einsum("ab,hij->haibj", seq_eye, dmask).reshape(n_heads, n_seq * chunk, n_seq * chunk)
    lanes = lambda t: jnp.repeat(t, HEAD_DIM, axis=-1)
    return dmask, jnp.tile(lanes(q_dec), (n_seq, 1)), jnp.tile(lanes(k_dec), (n_seq, 1)), lanes(c_dec[None, :])


def retention_gated(q, k, v, g, s0_bd, *, n_seq, chunk, tiles, steps_per_seq):
    t, w = q.shape
    n_pairs = w // HEAD_PAIR
    rows = n_seq * chunk * tiles
    n_steps = t // rows
    dmask, qdec, kdec, cdec = _retention_tables(w // HEAD_DIM, n_seq, chunk)
    tile_rows = n_seq * chunk
    row = pl.BlockSpec((rows, w), lambda i: (i, 0))
    st = pl.BlockSpec((n_seq, n_pairs, HEAD_PAIR, HEAD_PAIR), lambda i: (i // steps_per_seq, 0, 0, 0))
    const = lambda shape: pl.BlockSpec(shape, lambda i: (0,) * len(shape))
    return pl.pallas_call(
        functools.partial(_retention_kernel, n_seq=n_seq, chunk=chunk, tiles=tiles, steps_per_seq=steps_per_seq),
        grid=(n_steps,),
        in_specs=[row, row, row, row, st, const(dmask.shape), const((tile_rows, w)), const((tile_rows, w)),
                  const((1, w))],
        out_specs=[row, st],
        out_shape=[jax.ShapeDtypeStruct((t, w), F32), jax.ShapeDtypeStruct(s0_bd.shape, F32)],
        scratch_shapes=[pltpu.VMEM((n_seq, n_pairs, HEAD_PAIR, HEAD_PAIR), F32)],
        compiler_params=_params("arbitrary"),
    )(q, k, v, g, s0_bd, dmask, qdec, kdec, cdec)


def _states_to_block_diag(s):
    b, h, dk, dv = s.shape
    sp = s.reshape(b, h // 2, 2, dk, dv)
    z = jnp.zeros_like(sp[:, :, 0])
    top = jnp.concatenate([sp[:, :, 0], z], axis=-1)
    bot = jnp.concatenate([z, sp[:, :, 1]], axis=-1)
    return jnp.concatenate([top, bot], axis=-2)


def _block_diag_to_states(s):
    b, p = s.shape[:2]
    a = s[:, :, :HEAD_DIM, :HEAD_DIM]
    c = s[:, :, HEAD_DIM:, HEAD_DIM:]
    return jnp.stack([a, c], axis=2).reshape(b, 2 * p, HEAD_DIM, HEAD_DIM)


def _merge_ln_kernel(a_ref, r_ref, wa_ref, wr_ref, x_ref, g_ref, b_ref, o_ref, *, alpha):
    mix = _dot(a_ref[...].astype(BF16), wa_ref[...]) + _dot(r_ref[...].astype(BF16), wr_ref[...])
    o_ref[...] = _layer_norm(alpha * x_ref[...] + mix, g_ref[...], b_ref[...])


def merge_ln(attn, ret, w_a, w_r, x, g, b, *, alpha, tm):
    t, d = x.shape
    w = attn.shape[1]
    row = lambda n: pl.BlockSpec((tm, n), lambda i: (i, 0))
    const = lambda r, c: pl.BlockSpec((r, c), lambda i: (0, 0))
    return pl.pallas_call(
        functools.partial(_merge_ln_kernel, alpha=alpha),
        grid=(t // tm,),
        in_specs=[row(w), row(w), const(w, d), const(w, d), row(d), const(1, d), const(1, d)],
        out_specs=row(d),
        out_shape=jax.ShapeDtypeStruct((t, d), F32),
        compiler_params=_params("parallel"),
    )(attn, ret, w_a, w_r, x, g, b)


def _s5_scan_kernel(x_ref, s0re_ref, s0im_ref, wbre_ref, wbim_ref, are_ref, aim_ref, wcre_ref, wcim_ref, d_ref,
                    y_ref, sre_ref, sim_ref, xs_ref, bre_ref, bim_ref, stre_ref, stim_ref, *, lane_chunk):
    step = pl.program_id(0)
    nb, tt, d = x_ref.shape
    b8 = stre_ref.shape[0]
    ns = stre_ref.shape[1]
    n_kb = d // LANES
    per_kb = ns // n_kb

    @pl.when(step == 0)
    def _():
        stre_ref[...] = jnp.zeros_like(stre_ref)
        stim_ref[...] = jnp.zeros_like(stim_ref)
        stre_ref[0:nb, :] = s0re_ref[...]
        stim_ref[0:nb, :] = s0im_ref[...]
        xs_ref[...] = jnp.zeros_like(xs_ref)

    for b in range(nb):
        for kb in range(n_kb):
            xs_ref[kb, pl.ds(b, tt, stride=b8), :] = x_ref[b, :, kb * LANES:(kb + 1) * LANES]

    for kb in range(n_kb):
        xk = xs_ref[kb].astype(BF16)
        cols = slice(kb * per_kb, (kb + 1) * per_kb)
        bre_ref[:, cols] = _dot(xk, wbre_ref[kb])
        bim_ref[:, cols] = _dot(xk, wbim_ref[kb])

    for r0 in range(0, b8, SUBLANES):
        for c0 in range(0, ns, lane_chunk):
            cs = slice(c0, c0 + lane_chunk)
            are = jnp.broadcast_to(are_ref[:, cs], (SUBLANES, lane_chunk))
            aim = jnp.broadcast_to(aim_ref[:, cs], (SUBLANES, lane_chunk))

            def body(t, carry, r0=r0, cs=cs, are=are, aim=aim):
                sre, sim = carry
                rows = pl.ds(pl.multiple_of(t * b8 + r0, SUBLANES), SUBLANES)
                nre = are * sre - aim * sim + bre_ref[rows, cs]
                nim = are * sim + aim * sre + bim_ref[rows, cs]
                bre_ref[rows, cs] = nre
                bim_ref[rows, cs] = nim
                return nre, nim

            sre, sim = lax.fori_loop(0, tt, body, (stre_ref[r0:r0 + SUBLANES, cs], stim_ref[r0:r0 + SUBLANES, cs]))
            stre_ref[r0:r0 + SUBLANES, cs] = sre
            stim_ref[r0:r0 + SUBLANES, cs] = sim

    n_out = wcre_ref.shape[0]
    kin = ns // n_out
    wout = d // n_out
    for j in range(n_out):
        sre = bre_ref[:, j * kin:(j + 1) * kin].astype(BF16)
        sim = bim_ref[:, j * kin:(j + 1) * kin].astype(BF16)
        y = _dot(sre, wcre_ref[j]) + _dot(sim, wcim_ref[j])
        for i in range(wout // LANES):
            kb = j * (wout // LANES) + i
            xs_ref[kb] = y[:, i * LANES:(i + 1) * LANES] + xs_ref[kb] * d_ref[:, kb * LANES:(kb + 1) * LANES]

    for b in range(nb):
        for kb in range(n_kb):
            y_ref[b, :, kb * LANES:(kb + 1) * LANES] = xs_ref[kb, pl.ds(b, tt, stride=b8), :]

    @pl.when(step == pl.num_programs(0) - 1)
    def _():
        sre_ref[...] = stre_ref[0:nb, :]
        sim_ref[...] = stim_ref[0:nb, :]


def _s5_weights(lam_re, lam_im, b_re, b_im, c_re, c_im, log_dt):
    g, p, h = b_re.shape
    lam = lax.complex(lam_re.astype(F32), lam_im.astype(F32))
    dt = jnp.exp(log_dt.astype(F32))[:, None]
    lam_bar = jnp.exp(lam * dt)
    b_bar = ((lam_bar - 1.0) / lam)[:, :, None] * lax.complex(b_re.astype(F32), b_im.astype(F32))
    gpk = LANES // h
    n_kb = g // gpk
    eye = jnp.eye(gpk, dtype=F32)

    def in_blocks(m):
        m = m.reshape(n_kb, gpk, p, h)
        return jnp.einsum("kgph,gf->kghfp", m, eye).reshape(n_kb, gpk * h, gpk * p).astype(BF16)

    gpo = 2 * LANES // h
    n_out = g // gpo
    eye_o = jnp.eye(gpo, dtype=F32)

    def out_blocks(m):
        m = m.reshape(n_out, gpo, h, p)
        return jnp.einsum("kghp,gf->kgpfh", m, eye_o).reshape(n_out, gpo * p, gpo * h).astype(BF16)

    return (in_blocks(jnp.real(b_bar)), in_blocks(jnp.imag(b_bar)),
            jnp.real(lam_bar).reshape(1, g * p), jnp.imag(lam_bar).reshape(1, g * p),
            out_blocks(c_re.astype(F32)), out_blocks(-c_im.astype(F32)))


def s5_scan(x, s0_re, s0_im, weights, d_skip, *, tt, lane_chunk=1024):
    nb, seq, d = x.shape
    wbre, wbim, are, aim, wcre, wcim = weights
    ns = are.shape[1]
    b8 = -(-nb // SUBLANES) * SUBLANES
    rows = tt * b8
    const = lambda a: pl.BlockSpec(a.shape, lambda i: (0,) * a.ndim)
    xblk = pl.BlockSpec((nb, tt, d), lambda i: (0, i, 0))
    sblk = pl.BlockSpec((nb, ns), lambda i: (0, 0))
    return pl.pallas_call(
        functools.partial(_s5_scan_kernel, lane_chunk=lane_chunk),
        grid=(seq // tt,),
        in_specs=[xblk, sblk, sblk, const(wbre), const(wbim), const(are), const(aim), const(wcre), const(wcim),
                  pl.BlockSpec((1, d), lambda i: (0, 0))],
        out_specs=[xblk, sblk, sblk],
        out_shape=[jax.ShapeDtypeStruct((nb, seq, d), F32), jax.ShapeDtypeStruct((nb, ns), F32),
                   jax.ShapeDtypeStruct((nb, ns), F32)],
        scratch_shapes=[pltpu.VMEM((d // LANES, rows, LANES), F32), pltpu.VMEM((rows, ns), F32),
                        pltpu.VMEM((rows, ns), F32),
                        pltpu.VMEM((b8, ns), F32), pltpu.VMEM((b8, ns), F32)],
        compiler_params=_params("arbitrary"),
    )(x, s0_re, s0_im, wbre, wbim, are, aim, wcre, wcim, d_skip)


def _s5_out_ln_kernel(y_ref, wo_ref, wg_ref, x_ref, g_ref, b_ref, o_ref, *, alpha):
    gl = jax.nn.gelu(y_ref[...]).astype(BF16)
    mix = _dot(gl, wo_ref[...]) * jax.nn.sigmoid(_dot(gl, wg_ref[...]))
    o_ref[...] = _layer_norm(alpha * x_ref[...] + mix, g_ref[...], b_ref[...])


def s5_out_ln(y, w_out, w_gate, x, g, b, *, alpha, tm):
    t, d = x.shape
    row = pl.BlockSpec((tm, d), lambda i: (i, 0))
    const = lambda r, c: pl.BlockSpec((r, c), lambda i: (0, 0))
    return pl.pallas_call(
        functools.partial(_s5_out_ln_kernel, alpha=alpha),
        grid=(t // tm,),
        in_specs=[row, const(d, d), const(d, d), row, const(1, d), const(1, d)],
        out_specs=row,
        out_shape=jax.ShapeDtypeStruct((t, d), F32),
        compiler_params=_params("parallel"),
    )(y, w_out, w_gate, x, g, b)


def _rope_tables(pos, n_heads):
    half = HEAD_DIM // 2
    inv = ROPE_THETA ** (-jnp.arange(half, dtype=F32) / half)
    ang = pos.astype(F32)[:, None] * inv[None, :]
    cos = jnp.cos(ang)
    sin = jnp.sin(ang)
    cos_h = jnp.concatenate([cos, cos], axis=-1)
    sin_h = jnp.concatenate([-sin, sin], axis=-1)
    return jnp.tile(cos_h, (1, n_heads)), jnp.tile(sin_h, (1, n_heads))


def kernel(x_prompt, x_sample, cache_k, cache_v, page_table, state_ret, state_s5_re, state_s5_im, ffn1_w_gate, ffn1_w_up, ffn1_w_down, ffn2_w_gate, ffn2_w_up, ffn2_w_down, ln_g, ln_b, w_in_ab, w_out_ab, s5_lam_re, s5_lam_im, s5_b_re, s5_b_im, s5_c_re, s5_c_im, s5_d, s5_log_dt, s5_w_out, s5_w_gate):
    bp, lp, d = x_prompt.shape
    bs, ls, _ = x_sample.shape
    depth = ffn1_w_gate.shape[0]
    n_layers_ab, n_pool, page, a_heads, hd = cache_k.shape
    n_pages = page_table.shape[1]
    past_len = n_pages * page
    half = a_heads * hd
    n_heads = half // HEAD_DIM
    assert hd == HEAD_DIM and lp % MOBA_BLOCK == 0 and past_len % MOBA_BLOCK == 0 and ls <= MOBA_BLOCK
    assert MOBA_BLOCK % page == 0 and lp % page == 0 and lp % RET_CHUNK == 0
    alpha = (2 * depth) ** 0.25
    tp, ts = bp * lp, bs * ls
    tm_p = min(1024, lp)
    tm_proj = min(512, lp)

    xp = x_prompt.reshape(tp, d)
    xs = x_sample.reshape(ts, d)
    bf = lambda a: a.astype(BF16)
    cos_p, sin_p = _rope_tables(jnp.arange(lp, dtype=jnp.int32), n_heads)
    cos_s, sin_s = _rope_tables(past_len + jnp.arange(ls, dtype=jnp.int32), n_heads)
    cos_s, sin_s = jnp.tile(cos_s, (bs, 1)), jnp.tile(sin_s, (bs, 1))
    pool_k = cache_k.reshape(n_layers_ab * n_pool, page, half)
    pool_v = cache_v.reshape(n_layers_ab * n_pool, page, half)
    seq_per_tile = max(RET_CHUNK // ls, 1)
    ret_tiles_p = min(4, lp // RET_CHUNK)

    k_p, v_p, k_s, v_s, r_p, r_s = [], [], [], [], [], []
    sre_p, sim_p, sre_s, sim_s = [], [], [], []
    for layer in range(depth):
        li = layer // 2
        g = lambda i: ln_g[layer, i][None, :]
        b = lambda i: ln_b[layer, i][None, :]
        f1 = (bf(ffn1_w_gate[layer]), bf(ffn1_w_up[layer]), bf(ffn1_w_down[layer]))
        f2 = (bf(ffn2_w_gate[layer]), bf(ffn2_w_up[layer]), bf(ffn2_w_down[layer]))
        xp = ffn_ln(xp, *f1, g(0), b(0), alpha=alpha, tm=tm_p)
        xs = ffn_ln(xs, *f1, g(0), b(0), alpha=alpha, tm=ts)
        if layer % 2 == 0:
            w_in = bf(w_in_ab[li])
            w_a, w_r = bf(w_out_ab[li][:half]), bf(w_out_ab[li][half:])
            qa, ka, va, kbf, vbf, ksum, qb, kb, vb, gb = ab_project(xp, w_in, cos_p, sin_p, tm=tm_proj, seq_len=lp)
            attn = moba_prompt(qa, ksum, kbf, vbf, batch=bp, seq_len=lp)
            zero_state = jnp.zeros((bp, n_heads // 2, HEAD_PAIR, HEAD_PAIR), F32)
            ret, s_fin = retention_gated(qb, kb, vb, gb, zero_state, n_seq=1, chunk=RET_CHUNK, tiles=ret_tiles_p,
                                         steps_per_seq=lp // (RET_CHUNK * ret_tiles_p))
            xp = merge_ln(attn, ret, w_a, w_r, xp, g(1), b(1), alpha=alpha, tm=tm_proj)
            k_p.append(ka.reshape(bp, lp // page, page, a_heads, hd))
            v_p.append(va.reshape(bp, lp // page, page, a_heads, hd))
            r_p.append(_block_diag_to_states(s_fin))
            qa, ka, va, _, _, _, qb, kb, vb, gb = ab_project(xs, w_in, cos_s, sin_s, tm=ts, seq_len=ts)
            page_ids = (page_table.astype(jnp.int32) + li * n_pool).reshape(-1)
            attn = moba_sample(qa, ka, va, pool_k, pool_v, page_ids, batch=bs, n_pages=n_pages)
            ret, s_fin = retention_gated(qb, kb, vb, gb, _states_to_block_diag(state_ret[li]), n_seq=seq_per_tile,
                                         chunk=ls, tiles=1, steps_per_seq=1)
            xs = merge_ln(attn, ret, w_a, w_r, xs, g(1), b(1), alpha=alpha, tm=ts)
            k_s.append(ka.reshape(bs, ls, a_heads, hd))
            v_s.append(va.reshape(bs, ls, a_heads, hd))
            r_s.append(_block_diag_to_states(s_fin))
        else:
            weights = _s5_weights(s5_lam_re[li], s5_lam_im[li], s5_b_re[li], s5_b_im[li], s5_c_re[li],
                                  s5_c_im[li], s5_log_dt[li])
            n_state = weights[2].shape[1]
            d_skip = s5_d[li][None, :]
            w_o, w_g = bf(s5_w_out[li]), bf(s5_w_gate[li])
            zero = jnp.zeros((bp, n_state), F32)
            y, a_re, a_im = s5_scan(xp.reshape(bp, lp, d), zero, zero, weights, d_skip, tt=min(32, lp))
            xp = s5_out_ln(y.reshape(tp, d), w_o, w_g, xp, g(1), b(1), alpha=alpha, tm=tm_proj)
            sre_p.append(a_re.reshape(bp, -1, S5_STATE))
            sim_p.append(a_im.reshape(bp, -1, S5_STATE))
            y, a_re, a_im = s5_scan(xs.reshape(bs, ls, d), state_s5_re[li].reshape(bs, n_state),
                                    state_s5_im[li].reshape(bs, n_state), weights, d_skip, tt=ls)
            xs = s5_out_ln(y.reshape(ts, d), w_o, w_g, xs, g(1), b(1), alpha=alpha, tm=ts)
            sre_s.append(a_re.reshape(bs, -1, S5_STATE))
            sim_s.append(a_im.reshape(bs, -1, S5_STATE))
        xp = ffn_ln(xp, *f2, g(2), b(2), alpha=alpha, tm=tm_p)
        xs = ffn_ln(xs, *f2, g(2), b(2), alpha=alpha, tm=ts)
    return (xp.reshape(bp, lp, d), xs.reshape(bs, ls, d), jnp.stack(k_p), jnp.stack(v_p), jnp.stack(k_s),
            jnp.stack(v_s), jnp.stack(r_p), jnp.stack(r_s), jnp.stack(sre_p), jnp.stack(sim_p),
            jnp.stack(sre_s), jnp.stack(sim_s))
```

```python
import functools
import math

import jax
import jax.numpy as jnp
from jax import lax
from jax.experimental import pallas as pl
from jax.experimental.pallas import tpu as pltpu

F32 = jnp.float32
BF16 = jnp.bfloat16

HEAD_DIM = 64
HEAD_PAIR = 2 * HEAD_DIM
MOBA_BLOCK = 256
MOBA_TOPK = 3
RET_CHUNK = 128
S5_GROUP = 16
S5_STATE = 64
ROPE_THETA = 10000.0
LN_EPS = 1e-5
GN_EPS = 1e-6
NEG_INF = -1e30
SUBLANES = 8
LANES = 128
VMEM_LIMIT = 48 * 1024 * 1024


def _dot(a, b, precision=None):
    return jnp.dot(a, b, preferred_element_type=F32, precision=precision)


def _dot_nt(a, b, precision=None):
    return lax.dot_general(a, b, (((1,), (1,)), ((), ())), preferred_element_type=F32, precision=precision)


def _dot_tn(a, b):
    return lax.dot_general(a, b, (((0,), (0,)), ((), ())), preferred_element_type=F32)


def _layer_norm(r, g, b):
    mu = jnp.mean(r, -1, keepdims=True)
    d = r - mu
    var = jnp.mean(d * d, -1, keepdims=True)
    return d * lax.rsqrt(var + LN_EPS) * g + b


def _silu(x):
    return x * jax.nn.sigmoid(x)


def _params(*sem):
    return pltpu.CompilerParams(dimension_semantics=sem, vmem_limit_bytes=VMEM_LIMIT)


def _ffn_ln_kernel(x_ref, wg_ref, wu_ref, wd_ref, g_ref, b_ref, o_ref, a_ref, *, alpha, tf):
    xb = x_ref[...].astype(BF16)
    for c in range(wg_ref.shape[1] // tf):
        cols = slice(c * tf, (c + 1) * tf)
        hg = _dot(xb, wg_ref[:, cols])
        hu = _dot(xb, wu_ref[:, cols])
        a_ref[:, cols] = (_silu(hg) * hu).astype(BF16)
    r = alpha * x_ref[...] + 0.5 * _dot(a_ref[...], wd_ref[...])
    o_ref[...] = _layer_norm(r, g_ref[...], b_ref[...])


def ffn_ln(x, wg, wu, wd, g, b, *, alpha, tm, tf=256):
    t, d = x.shape
    f = wg.shape[1]
    resident = lambda r, c: pl.BlockSpec((r, c), lambda i: (0, 0), pipeline_mode=pl.Buffered(1))
    return pl.pallas_call(
        functools.partial(_ffn_ln_kernel, alpha=alpha, tf=tf),
        grid=(t // tm,),
        in_specs=[pl.BlockSpec((tm, d), lambda i: (i, 0)), resident(d, f), resident(d, f), resident(f, d),
                  resident(1, d), resident(1, d)],
        out_specs=pl.BlockSpec((tm, d), lambda i: (i, 0)),
        out_shape=jax.ShapeDtypeStruct((t, d), F32),
        scratch_shapes=[pltpu.VMEM((tm, f), BF16)],
        compiler_params=_params("parallel"),
    )(x, wg, wu, wd, g, b)


def _rope(y, cos, sin_signed):
    width = y.shape[-1]
    lane = lax.broadcasted_iota(jnp.int32, (1, width), 1)
    first = (lane % HEAD_DIM) < (HEAD_DIM // 2)
    rot = jnp.where(first, pltpu.roll(y, width - HEAD_DIM // 2, 1), pltpu.roll(y, HEAD_DIM // 2, 1))
    return y * cos + rot * sin_signed


def _ab_proj_kernel(x_ref, w_ref, cos_ref, sin_ref, qa_ref, ka_ref, va_ref, kbf_ref, vt_ref, ksum_ref,
                    qb_ref, kb_ref, vb_ref, gb_ref, *, half, scale):
    xb = x_ref[...].astype(BF16)
    cos = cos_ref[...]
    sin = sin_ref[...]
    col = lambda c: _dot(xb, w_ref[:, c * half:(c + 1) * half])
    qa_ref[...] = _rope(col(0), cos, sin) * scale
    ka = _rope(col(1), cos, sin)
    ka_ref[...] = ka
    kbf_ref[...] = ka.astype(BF16)
    nblk = ka.shape[0] // MOBA_BLOCK
    for n in range(nblk):
        ksum_ref[n] = jnp.sum(ka[n * MOBA_BLOCK:(n + 1) * MOBA_BLOCK], axis=0, keepdims=True)
    va = col(2)
    va_ref[...] = va
    for n in range(nblk):
        vt_ref[n] = va[n * MOBA_BLOCK:(n + 1) * MOBA_BLOCK].T.astype(BF16)
    qb_ref[...] = _rope(col(3), cos, sin)
    kb_ref[...] = _rope(col(4), cos, sin) * scale
    vb_ref[...] = col(5)
    gb_ref[...] = col(6)


def ab_project(x, w_in, cos, sin, *, tm, seq_len):
    t, d = x.shape
    half = w_in.shape[1] // 7
    tiles_per_seq = max(seq_len // tm, 1)
    tab = pl.BlockSpec((tm, half), lambda i: (i % tiles_per_seq, 0))
    row = pl.BlockSpec((tm, half), lambda i: (i, 0))
    nblk = tm // MOBA_BLOCK
    f32o = jax.ShapeDtypeStruct((t, half), F32)
    bfo = jax.ShapeDtypeStruct((t, half), BF16)
    return pl.pallas_call(
        functools.partial(_ab_proj_kernel, half=half, scale=HEAD_DIM ** -0.5),
        grid=(t // tm,),
        in_specs=[pl.BlockSpec((tm, d), lambda i: (i, 0)),
                  pl.BlockSpec((d, 7 * half), lambda i: (0, 0)), tab, tab],
        out_specs=[row, row, row, row,
                   pl.BlockSpec((nblk, half, MOBA_BLOCK), lambda i: (i, 0, 0)),
                   pl.BlockSpec((nblk, 1, half), lambda i: (i, 0, 0)),
                   row, row, row, row],
        out_shape=[f32o, f32o, f32o, bfo,
                   jax.ShapeDtypeStruct((t // MOBA_BLOCK, half, MOBA_BLOCK), BF16),
                   jax.ShapeDtypeStruct((t // MOBA_BLOCK, 1, half), F32),
                   f32o, f32o, f32o, f32o],
        compiler_params=_params("parallel"),
    )(x, w_in, cos, sin)


def _topk_select(gate, n_valid):
    nb = gate.shape[1]
    blk = lax.broadcasted_iota(jnp.int32, (1, nb), 1)
    gate = jnp.where(blk < n_valid, gate, NEG_INF)
    rank = jnp.zeros(gate.shape, jnp.int32)
    for m in range(nb):
        gm = gate[:, m:m + 1]
        ahead = (gm > gate) | ((gm == gate) & (m < blk))
        rank = rank + ahead.astype(jnp.int32)
    return ((rank < MOBA_TOPK) & (blk < n_valid)).astype(F32)


def _topk_select_t(gate_t, n_valid):
    nb = gate_t.shape[0]
    blk = lax.broadcasted_iota(jnp.int32, (nb, 1), 0)
    gate_t = jnp.where(blk < n_valid, gate_t, NEG_INF)
    rank = jnp.zeros(gate_t.shape, jnp.int32)
    for m in range(nb):
        gm = gate_t[m:m + 1, :]
        ahead = (gm > gate_t) | ((gm == gate_t) & (m < blk))
        rank = rank + ahead.astype(jnp.int32)
    return ((rank < MOBA_TOPK) & (blk < n_valid)).astype(F32)


def _moba_prompt_kernel(q_ref, ksum_ref, k_ref, vt_ref, o_ref, qm_ref, sel_ref, m_ref, l_ref, a_ref, acc_ref,
                        s_ref, p_ref):
    j = pl.program_id(1)
    tq = q_ref.shape[0]
    n_heads = q_ref.shape[1] // HEAD_DIM
    lane = lax.broadcasted_iota(jnp.int32, (1, HEAD_PAIR), 1)
    key_i = lax.broadcasted_iota(jnp.int32, (MOBA_BLOCK, tq), 0)
    qry_i = lax.broadcasted_iota(jnp.int32, (MOBA_BLOCK, tq), 1)
    causal = key_i <= qry_i

    for h in range(n_heads):
        lanes = slice((h // 2) * HEAD_PAIR, (h // 2 + 1) * HEAD_PAIR)
        hm = (lane // HEAD_DIM) == (h % 2)
        qp = q_ref[:, lanes]
        kmean = jnp.where(hm, ksum_ref[:, 0, lanes] * (1.0 / MOBA_BLOCK), 0.0)
        sel_ref[h] = _topk_select_t(_dot_nt(kmean, qp, precision=lax.Precision.HIGHEST), j)
        qm_ref[h] = jnp.where(hm, qp, 0.0).astype(BF16)

    def update(k_blk, vt_blk, mask_of_head, first):
        for h in range(n_heads):
            k_pair = k_blk[:, (h // 2) * HEAD_PAIR:(h // 2 + 1) * HEAD_PAIR]
            s_ref[h] = _dot_nt(k_pair, qm_ref[h])
        for h in range(n_heads):
            s = jnp.where(mask_of_head(h), s_ref[h], NEG_INF)
            cmax = jnp.max(s, axis=0, keepdims=True)
            m_new = cmax if first else jnp.maximum(m_ref[h], cmax)
            p = jnp.exp(s - m_new)
            psum = jnp.sum(p, axis=0, keepdims=True)
            p_ref[h] = p.astype(BF16)
            if first:
                l_ref[h] = psum
            else:
                a = jnp.exp(m_ref[h] - m_new)
                a_ref[h] = a
                l_ref[h] = a * l_ref[h] + psum
            m_ref[h] = m_new
        for h in range(n_heads):
            hp, half = h // 2, h % 2
            vt_pair = vt_blk[hp * HEAD_PAIR:(hp + 1) * HEAD_PAIR, :]
            pv = _dot(vt_pair, p_ref[h])[half * HEAD_DIM:(half + 1) * HEAD_DIM]
            acc_ref[h] = pv if first else a_ref[h] * acc_ref[h] + pv

    own0 = pl.multiple_of(j * MOBA_BLOCK, MOBA_BLOCK)
    update(k_ref[pl.ds(own0, MOBA_BLOCK), :], vt_ref[j], lambda h: causal, True)

    def body(n, carry):
        r0 = pl.multiple_of(n * MOBA_BLOCK, MOBA_BLOCK)
        update(k_ref[pl.ds(r0, MOBA_BLOCK), :], vt_ref[n], lambda h: sel_ref[h, pl.ds(n, 1), :] > 0.0, False)
        return carry

    lax.fori_loop(0, j, body, 0)
    out_t = jnp.concatenate([acc_ref[h] / l_ref[h] for h in range(n_heads)], axis=0)
    o_ref[...] = out_t.T


def moba_prompt(q, ksum, kbf, vt, *, batch, seq_len):
    t, w = q.shape
    nb = seq_len // MOBA_BLOCK
    n_heads = w // HEAD_DIM
    return pl.pallas_call(
        _moba_prompt_kernel,
        grid=(batch, nb),
        in_specs=[pl.BlockSpec((MOBA_BLOCK, w), lambda b, j: (b * nb + j, 0)),
                  pl.BlockSpec((nb, 1, w), lambda b, j: (b, 0, 0)),
                  pl.BlockSpec((seq_len, w), lambda b, j: (b, 0)),
                  pl.BlockSpec((nb, w, MOBA_BLOCK), lambda b, j: (b, 0, 0))],
        out_specs=pl.BlockSpec((MOBA_BLOCK, w), lambda b, j: (b * nb + j, 0)),
        out_shape=jax.ShapeDtypeStruct((t, w), F32),
        scratch_shapes=[pltpu.VMEM((n_heads, MOBA_BLOCK, HEAD_PAIR), BF16),
                        pltpu.VMEM((n_heads, nb, MOBA_BLOCK), F32),
                        pltpu.VMEM((n_heads, 1, MOBA_BLOCK), F32),
                        pltpu.VMEM((n_heads, 1, MOBA_BLOCK), F32),
                        pltpu.VMEM((n_heads, 1, MOBA_BLOCK), F32),
                        pltpu.VMEM((n_heads, HEAD_DIM, MOBA_BLOCK), F32),
                        pltpu.VMEM((n_heads, MOBA_BLOCK, MOBA_BLOCK), F32),
                        pltpu.VMEM((n_heads, MOBA_BLOCK, MOBA_BLOCK), BF16)],
        compiler_params=_params("parallel", "arbitrary"),
    )(q, ksum, kbf, vt)


def _moba_sample_kernel(pt_ref, q_ref, knew_ref, vnew_ref, *rest, pages_per_step, page, n_heads):
    del pt_ref
    kpages = rest[:pages_per_step]
    vpages = rest[pages_per_step:2 * pages_per_step]
    o_ref = rest[2 * pages_per_step]
    qf_ref, qb_ref, ksum_ref, m_ref, l_ref, acc_ref, kpad_ref, vpad_ref = rest[2 * pages_per_step + 1:]
    s_idx = pl.program_id(1)
    n_steps = pl.num_programs(1)
    lq, w = q_ref.shape
    rows = n_heads * lq
    nb = acc_ref.shape[0]
    pages_per_blk = MOBA_BLOCK // page
    blks_per_step = pages_per_step // pages_per_blk
    rowhead = lax.broadcasted_iota(jnp.int32, (rows, 1), 0) // lq
    lanehead = lax.broadcasted_iota(jnp.int32, (1, w), 1) // HEAD_DIM
    blk_lane = lax.broadcasted_iota(jnp.int32, (1, LANES), 1)

    @pl.when(s_idx == 0)
    def _():
        qt = jnp.concatenate([q_ref[...]] * n_heads, axis=0)
        qbd = jnp.where(rowhead == lanehead, qt, 0.0)
        qf_ref[...] = qbd
        qb_ref[...] = qbd.astype(BF16)
        ksum_ref[...] = jnp.zeros_like(ksum_ref)
        m_ref[...] = jnp.zeros_like(m_ref)
        l_ref[...] = jnp.zeros_like(l_ref)

    qb = qb_ref[...]

    def partial_softmax(s):
        m = jnp.max(s, axis=1, keepdims=True)
        e = jnp.exp(s - m)
        return m, jnp.sum(e, axis=1, keepdims=True), e.astype(BF16)

    for bi in range(blks_per_step):
        blk = s_idx * blks_per_step + bi
        pages = range(bi * pages_per_blk, (bi + 1) * pages_per_blk)
        kt = jnp.concatenate([kpages[i][0] for i in pages], axis=1)
        vt = jnp.concatenate([vpages[i][0] for i in pages], axis=1)
        here = blk_lane == blk
        ksum_ref[...] = jnp.where(here, jnp.sum(kt, axis=1, keepdims=True), ksum_ref[...])
        m, l, e = partial_softmax(_dot(qb, kt.astype(BF16)))
        acc_ref[blk] = _dot_nt(e, vt.astype(BF16))
        m_ref[...] = jnp.where(here, m, m_ref[...])
        l_ref[...] = jnp.where(here, l, l_ref[...])

    @pl.when(s_idx == n_steps - 1)
    def _():
        kpad_ref[...] = jnp.zeros_like(kpad_ref)
        vpad_ref[...] = jnp.zeros_like(vpad_ref)
        kpad_ref[0:lq, :] = knew_ref[...]
        vpad_ref[0:lq, :] = vnew_ref[...]
        s_own = _dot_nt(qb, kpad_ref[...].astype(BF16))
        tq = lax.broadcasted_iota(jnp.int32, s_own.shape, 0) % lq
        tk = lax.broadcasted_iota(jnp.int32, s_own.shape, 1)
        m_own, l_own, e_own = partial_softmax(jnp.where(tk <= tq, s_own, NEG_INF))
        acc_own = _dot(e_own, vpad_ref[...].astype(BF16))

        gate = _dot(qf_ref[...], ksum_ref[...] * (1.0 / MOBA_BLOCK), precision=lax.Precision.HIGHEST)
        sel = _topk_select(gate[:, :nb], nb) > 0.0
        m_blk = m_ref[:, :nb]
        m_all = jnp.maximum(m_own, jnp.max(jnp.where(sel, m_blk, NEG_INF), axis=1, keepdims=True))
        wgt = jnp.where(sel, jnp.exp(m_blk - m_all), 0.0)
        w_own = jnp.exp(m_own - m_all)
        den = w_own * l_own + jnp.sum(wgt * l_ref[:, :nb], axis=1, keepdims=True)
        num = w_own * acc_own
        for n in range(nb):
            num = num + wgt[:, n:n + 1] * acc_ref[n]
        o_all = num / den
        out = jnp.zeros((lq, w), F32)
        for h in range(n_heads):
            out = out + jnp.where(lanehead == h, o_all[h * lq:(h + 1) * lq], 0.0)
        o_ref[...] = out


def moba_sample(q, knew, vnew, pool_kt, pool_vt, page_ids, *, batch, n_pages, pages_per_step=8):
    t, w = q.shape
    lq = t // batch
    page = pool_kt.shape[2]
    n_heads = w // HEAD_DIM
    rows = n_heads * lq
    nb = n_pages * page // MOBA_BLOCK
    assert nb <= LANES and pages_per_step % (MOBA_BLOCK // page) == 0
    seq_blk = pl.BlockSpec((lq, w), lambda b, s, pt: (b, 0))

    def page_spec(i):
        return pl.BlockSpec((1, w, page), lambda b, s, pt: (pt[b * n_pages + s * pages_per_step + i], 0, 0))

    grid_spec = pltpu.PrefetchScalarGridSpec(
        num_scalar_prefetch=1,
        grid=(batch, n_pages // pages_per_step),
        in_specs=[seq_blk, seq_blk, seq_blk] + [page_spec(i) for i in range(pages_per_step)] * 2,
        out_specs=seq_blk,
        scratch_shapes=[pltpu.VMEM((rows, w), F32), pltpu.VMEM((rows, w), BF16), pltpu.VMEM((w, LANES), F32),
                        pltpu.VMEM((rows, LANES), F32), pltpu.VMEM((rows, LANES), F32),
                        pltpu.VMEM((nb, rows, w), F32),
                        pltpu.VMEM((LANES, w), F32), pltpu.VMEM((LANES, w), F32)],
    )
    return pl.pallas_call(
        functools.partial(_moba_sample_kernel, pages_per_step=pages_per_step, page=page, n_heads=n_heads),
        grid_spec=grid_spec,
        out_shape=jax.ShapeDtypeStruct((t, w), F32),
        compiler_params=_params("parallel", "arbitrary"),
    )(page_ids, q, knew, vnew, *([pool_kt] * pages_per_step), *([pool_vt] * pages_per_step))


def _retention_kernel(q_ref, k_ref, v_ref, g_ref, s0_ref, dmask_ref, qdec_ref, kdec_ref, cdec_ref,
                      o_ref, sout_ref, st_ref, *, n_seq, chunk, tiles, steps_per_seq):
    step = pl.program_id(0)
    tile_rows = n_seq * chunk
    n_pairs = q_ref.shape[1] // HEAD_PAIR
    lane = lax.broadcasted_iota(jnp.int32, (1, HEAD_PAIR), 1)
    rowh = lax.broadcasted_iota(jnp.int32, (HEAD_PAIR, 1), 0) // HEAD_DIM
    same_head = rowh == (lane // HEAD_DIM)
    rseq = lax.broadcasted_iota(jnp.int32, (tile_rows, 1), 0) // chunk

    @pl.when(step % steps_per_seq == 0)
    def _():
        st_ref[...] = s0_ref[...]

    for ti in range(tiles):
        rows = slice(ti * tile_rows, (ti + 1) * tile_rows)
        for hp in range(n_pairs):
            lanes = slice(hp * HEAD_PAIR, (hp + 1) * HEAD_PAIR)
            q = q_ref[rows, lanes]
            k = k_ref[rows, lanes]
            v = v_ref[rows, lanes]
            qb = q.astype(BF16)
            kb = k.astype(BF16)
            vb = v.astype(BF16)
            kd = (k * kdec_ref[:, lanes]).astype(BF16)
            cdec = cdec_ref[:, lanes]
            ret = jnp.zeros((tile_rows, HEAD_PAIR), F32)
            for half in range(2):
                hm = (lane // HEAD_DIM) == half
                inner = _dot_nt(jnp.where(hm, qb, jnp.zeros_like(qb)), kb)
                inner = inner * dmask_ref[2 * hp + half]
                ret = ret + jnp.where(hm, _dot(inner.astype(BF16), vb), 0.0)
            carried = []
            for s in range(n_seq):
                srows = slice(s * chunk, (s + 1) * chunk)
                state = st_ref[s, hp]
                carried.append(_dot(qb[srows], state.astype(BF16)))
                if n_seq == 1:
                    kd_s = kd
                else:
                    kd_s = jnp.where(rseq == s, kd, jnp.zeros_like(kd))
                st_ref[s, hp] = cdec * state + jnp.where(same_head, _dot_tn(kd_s, vb), 0.0)
            carried = carried[0] if n_seq == 1 else jnp.concatenate(carried, axis=0)
            ret = ret + carried * qdec_ref[:, lanes]
            normed = jnp.zeros_like(ret)
            for half in range(2):
                hm = (lane // HEAD_DIM) == half
                mu = jnp.sum(jnp.where(hm, ret, 0.0), axis=1, keepdims=True) * (1.0 / HEAD_DIM)
                d = jnp.where(hm, ret - mu, 0.0)
                var = jnp.sum(d * d, axis=1, keepdims=True) * (1.0 / HEAD_DIM)
                normed = normed + d * lax.rsqrt(var + GN_EPS)
            o_ref[rows, lanes] = _silu(g_ref[rows, lanes]) * normed

    @pl.when(step % steps_per_seq == steps_per_seq - 1)
    def _():
        sout_ref[...] = st_ref[...]


def _retention_tables(n_heads, n_seq, chunk):
    log_g = jnp.log1p(-jnp.exp2(-5.0 - jnp.arange(n_heads, dtype=F32)))
    idx = jnp.arange(chunk, dtype=F32)
    diff = idx[:, None] - idx[None, :]
    dmask = jnp.where(diff >= 0, jnp.exp(log_g[:, None, None] * jnp.maximum(diff, 0.0)), 0.0)
    q_dec = jnp.exp(log_g[None, :] * (idx[:, None] + 1.0))
    k_dec = jnp.exp(log_g[None, :] * (chunk - 1.0 - idx[:, None]))
    c_dec = jnp.exp(log_g * chunk)
    seq_eye = jnp.eye(n_seq, dtype=F32)
    dmask = jnp.einsum("ab,hij->haibj", seq_eye, dmask).reshape(n_heads, n_seq * chunk, n_seq * chunk)
    lanes = lambda t: jnp.repeat(t, HEAD_DIM, axis=-1)
    return dmask, jnp.tile(lanes(q_dec), (n_seq, 1)), jnp.tile(lanes(k_dec), (n_seq, 1)), lanes(c_dec[None, :])


def retention_gated(q, k, v, g, s0_bd, *, n_seq, chunk, tiles, steps_per_seq):
    t, w = q.shape
    n_pairs = w // HEAD_PAIR
    rows = n_seq * chunk * tiles
    n_steps = t // rows
    dmask, qdec, kdec, cdec = _retention_tables(w // HEAD_DIM, n_seq, chunk)
    tile_rows = n_seq * chunk
    row = pl.BlockSpec((rows, w), lambda i: (i, 0))
    st = pl.BlockSpec((n_seq, n_pairs, HEAD_PAIR, HEAD_PAIR), lambda i: (i // steps_per_seq, 0, 0, 0))
    const = lambda shape: pl.BlockSpec(shape, lambda i: (0,) * len(shape))
    return pl.pallas_call(
        functools.partial(_retention_kernel, n_seq=n_seq, chunk=chunk, tiles=tiles, steps_per_seq=steps_per_seq),
        grid=(n_steps,),
        in_specs=[row, row, row, row, st, const(dmask.shape), const((tile_rows, w)), const((tile_rows, w)),
                  const((1, w))],
        out_specs=[row, st],
        out_shape=[jax.ShapeDtypeStruct((t, w), F32), jax.ShapeDtypeStruct(s0_bd.shape, F32)],
        scratch_shapes=[pltpu.VMEM((n_seq, n_pairs, HEAD_PAIR, HEAD_PAIR), F32)],
        compiler_params=_params("arbitrary"),
    )(q, k, v, g, s0_bd, dmask, qdec, kdec, cdec)


def _states_to_block_diag(s):
    b, h, dk, dv = s.shape
    sp = s.reshape(b, h // 2, 2, dk, dv)
    z = jnp.zeros_like(sp[:, :, 0])
    top = jnp.concatenate([sp[:, :, 0], z], axis=-1)
    bot = jnp.concatenate([z, sp[:, :, 1]], axis=-1)
    return jnp.concatenate([top, bot], axis=-2)


def _block_diag_to_states(s):
    b, p = s.shape[:2]
    a = s[:, :, :HEAD_DIM, :HEAD_DIM]
    c = s[:, :, HEAD_DIM:, HEAD_DIM:]
    return jnp.stack([a, c], axis=2).reshape(b, 2 * p, HEAD_DIM, HEAD_DIM)


def _merge_ln_kernel(a_ref, r_ref, wa_ref, wr_ref, x_ref, g_ref, b_ref, o_ref, *, alpha):
    mix = _dot(a_ref[...].astype(BF16), wa_ref[...]) + _dot(r_ref[...].astype(BF16), wr_ref[...])
    o_ref[...] = _layer_norm(alpha * x_ref[...] + mix, g_ref[...], b_ref[...])


def merge_ln(attn, ret, w_a, w_r, x, g, b, *, alpha, tm):
    t, d = x.shape
    w = attn.shape[1]
    row = lambda n: pl.BlockSpec((tm, n), lambda i: (i, 0))
    const = lambda r, c: pl.BlockSpec((r, c), lambda i: (0, 0))
    return pl.pallas_call(
        functools.partial(_merge_ln_kernel, alpha=alpha),
        grid=(t // tm,),
        in_specs=[row(w), row(w), const(w, d), const(w, d), row(d), const(1, d), const(1, d)],
        out_specs=row(d),
        out_shape=jax.ShapeDtypeStruct((t, d), F32),
        compiler_params=_params("parallel"),
    )(attn, ret, w_a, w_r, x, g, b)


def _s5_scan_kernel(x_ref, s0re_ref, s0im_ref, wbre_ref, wbim_ref, are_ref, aim_ref, wcre_ref, wcim_ref, d_ref,
                    y_ref, sre_ref, sim_ref, xs_ref, bre_ref, bim_ref, stre_ref, stim_ref, *, lane_chunk):
    step = pl.program_id(0)
    nb, tt, d = x_ref.shape
    b8 = stre_ref.shape[0]
    ns = stre_ref.shape[1]
    n_kb = d // LANES
    per_kb = ns // n_kb

    @pl.when(step == 0)
    def _():
        stre_ref[...] = jnp.zeros_like(stre_ref)
        stim_ref[...] = jnp.zeros_like(stim_ref)
        stre_ref[0:nb, :] = s0re_ref[...]
        stim_ref[0:nb, :] = s0im_ref[...]
        xs_ref[...] = jnp.zeros_like(xs_ref)

    for b in range(nb):
        for kb in range(n_kb):
            xs_ref[kb, pl.ds(b, tt, stride=b8), :] = x_ref[b, :, kb * LANES:(kb + 1) * LANES]

    for kb in range(n_kb):
        xk = xs_ref[kb].astype(BF16)
        cols = slice(kb * per_kb, (kb + 1) * per_kb)
        bre_ref[:, cols] = _dot(xk, wbre_ref[kb])
        bim_ref[:, cols] = _dot(xk, wbim_ref[kb])

    for r0 in range(0, b8, SUBLANES):
        for c0 in range(0, ns, lane_chunk):
            cs = slice(c0, c0 + lane_chunk)
            are = jnp.broadcast_to(are_ref[:, cs], (SUBLANES, lane_chunk))
            aim = jnp.broadcast_to(aim_ref[:, cs], (SUBLANES, lane_chunk))

            def body(t, carry, r0=r0, cs=cs, are=are, aim=aim):
                sre, sim = carry
                rows = pl.ds(pl.multiple_of(t * b8 + r0, SUBLANES), SUBLANES)
                nre = are * sre - aim * sim + bre_ref[rows, cs]
                nim = are * sim + aim * sre + bim_ref[rows, cs]
                bre_ref[rows, cs] = nre
                bim_ref[rows, cs] = nim
                return nre, nim

            sre, sim = lax.fori_loop(0, tt, body, (stre_ref[r0:r0 + SUBLANES, cs], stim_ref[r0:r0 + SUBLANES, cs]))
            stre_ref[r0:r0 + SUBLANES, cs] = sre
            stim_ref[r0:r0 + SUBLANES, cs] = sim

    n_out = wcre_ref.shape[0]
    kin = ns // n_out
    wout = d // n_out
    for j in range(n_out):
        sre = bre_ref[:, j * kin:(j + 1) * kin].astype(BF16)
        sim = bim_ref[:, j * kin:(j + 1) * kin].astype(BF16)
        y = _dot(sre, wcre_ref[j]) + _dot(sim, wcim_ref[j])
        for i in range(wout // LANES):
            kb = j * (wout // LANES) + i
            xs_ref[kb] = y[:, i * LANES:(i + 1) * LANES] + xs_ref[kb] * d_ref[:, kb * LANES:(kb + 1) * LANES]

    for b in range(nb):
        for kb in range(n_kb):
            y_ref[b, :, kb * LANES:(kb + 1) * LANES] = xs_ref[kb, pl.ds(b, tt, stride=b8), :]

    @pl.when(step == pl.num_programs(0) - 1)
    def _():
        sre_ref[...] = stre_ref[0:nb, :]
        sim_ref[...] = stim_ref[0:nb, :]


def _s5_weights(lam_re, lam_im, b_re, b_im, c_re, c_im, log_dt):
    g, p, h = b_re.shape
    lam = lax.complex(lam_re.astype(F32), lam_im.astype(F32))
    dt = jnp.exp(log_dt.astype(F32))[:, None]
    lam_bar = jnp.exp(lam * dt)
    b_bar = ((lam_bar - 1.0) / lam)[:, :, None] * lax.complex(b_re.astype(F32), b_im.astype(F32))
    gpk = LANES // h
    n_kb = g // gpk
    eye = jnp.eye(gpk, dtype=F32)

    def in_blocks(m):
        m = m.reshape(n_kb, gpk, p, h)
        return jnp.einsum("kgph,gf->kghfp", m, eye).reshape(n_kb, gpk * h, gpk * p).astype(BF16)

    gpo = 2 * LANES // h
    n_out = g // gpo
    eye_o = jnp.eye(gpo, dtype=F32)

    def out_blocks(m):
        m = m.reshape(n_out, gpo, h, p)
        return jnp.einsum("kghp,gf->kgpfh", m, eye_o).reshape(n_out, gpo * p, gpo * h).astype(BF16)

    return (in_blocks(jnp.real(b_bar)), in_blocks(jnp.imag(b_bar)),
            jnp.real(lam_bar).reshape(1, g * p), jnp.imag(lam_bar).reshape(1, g * p),
            out_blocks(c_re.astype(F32)), out_blocks(-c_im.astype(F32)))


def s5_scan(x, s0_re, s0_im, weights, d_skip, *, tt, lane_chunk=1024):
    nb, seq, d = x.shape
    wbre, wbim, are, aim, wcre, wcim = weights
    ns = are.shape[1]
    b8 = -(-nb // SUBLANES) * SUBLANES
    rows = tt * b8
    const = lambda a: pl.BlockSpec(a.shape, lambda i: (0,) * a.ndim)
    xblk = pl.BlockSpec((nb, tt, d), lambda i: (0, i, 0))
    sblk = pl.BlockSpec((nb, ns), lambda i: (0, 0))
    return pl.pallas_call(
        functools.partial(_s5_scan_kernel, lane_chunk=lane_chunk),
        grid=(seq // tt,),
        in_specs=[xblk, sblk, sblk, const(wbre), const(wbim), const(are), const(aim), const(wcre), const(wcim),
                  pl.BlockSpec((1, d), lambda i: (0, 0))],
        out_specs=[xblk, sblk, sblk],
        out_shape=[jax.ShapeDtypeStruct((nb, seq, d), F32), jax.ShapeDtypeStruct((nb, ns), F32),
                   jax.ShapeDtypeStruct((nb, ns), F32)],
        scratch_shapes=[pltpu.VMEM((d // LANES, rows, LANES), F32), pltpu.VMEM((rows, ns), F32),
                        pltpu.VMEM((rows, ns), F32),
                        pltpu.VMEM((b8, ns), F32), pltpu.VMEM((b8, ns), F32)],
        compiler_params=_params("arbitrary"),
    )(x, s0_re, s0_im, wbre, wbim, are, aim, wcre, wcim, d_skip)


def _s5_out_ln_kernel(y_ref, wo_ref, wg_ref, x_ref, g_ref, b_ref, o_ref, *, alpha):
    gl = jax.nn.gelu(y_ref[...]).astype(BF16)
    mix = _dot(gl, wo_ref[...]) * jax.nn.sigmoid(_dot(gl, wg_ref[...]))
    o_ref[...] = _layer_norm(alpha * x_ref[...] + mix, g_ref[...], b_ref[...])


def s5_out_ln(y, w_out, w_gate, x, g, b, *, alpha, tm):
    t, d = x.shape
    row = pl.BlockSpec((tm, d), lambda i: (i, 0))
    const = lambda r, c: pl.BlockSpec((r, c), lambda i: (0, 0))
    return pl.pallas_call(
        functools.partial(_s5_out_ln_kernel, alpha=alpha),
        grid=(t // tm,),
        in_specs=[row, const(d, d), const(d, d), row, const(1, d), const(1, d)],
        out_specs=row,
        out_shape=jax.ShapeDtypeStruct((t, d), F32),
        compiler_params=_params("parallel"),
    )(y, w_out, w_gate, x, g, b)


def _rope_tables(pos, n_heads):
    half = HEAD_DIM // 2
    inv = ROPE_THETA ** (-jnp.arange(half, dtype=F32) / half)
    ang = pos.astype(F32)[:, None] * inv[None, :]
    cos = jnp.cos(ang)
    sin = jnp.sin(ang)
    cos_h = jnp.concatenate([cos, cos], axis=-1)
    sin_h = jnp.concatenate([-sin, sin], axis=-1)
    return jnp.tile(cos_h, (1, n_heads)), jnp.tile(sin_h, (1, n_heads))


def kernel(x_prompt, x_sample, cache_k, cache_v, page_table, state_ret, state_s5_re, state_s5_im, ffn1_w_gate, ffn1_w_up, ffn1_w_down, ffn2_w_gate, ffn2_w_up, ffn2_w_down, ln_g, ln_b, w_in_ab, w_out_ab, s5_lam_re, s5_lam_im, s5_b_re, s5_b_im, s5_c_re, s5_c_im, s5_d, s5_log_dt, s5_w_out, s5_w_gate):
    bp, lp, d = x_prompt.shape
    bs, ls, _ = x_sample.shape
    depth = ffn1_w_gate.shape[0]
    n_layers_ab, n_pool, page, a_heads, hd = cache_k.shape
    n_pages = page_table.shape[1]
    past_len = n_pages * page
    half = a_heads * hd
    n_heads = half // HEAD_DIM
    assert hd == HEAD_DIM and lp % MOBA_BLOCK == 0 and past_len % MOBA_BLOCK == 0 and ls <= MOBA_BLOCK
    assert MOBA_BLOCK % page == 0 and lp % page == 0 and lp % RET_CHUNK == 0
    alpha = (2 * depth) ** 0.25
    tp, ts = bp * lp, bs * ls
    tm_p = min(1024, lp)
    tm_proj = min(512, lp)

    xp = x_prompt.reshape(tp, d)
    xs = x_sample.reshape(ts, d)
    bf = lambda a: a.astype(BF16)
    cos_p, sin_p = _rope_tables(jnp.arange(lp, dtype=jnp.int32), n_heads)
    cos_s, sin_s = _rope_tables(past_len + jnp.arange(ls, dtype=jnp.int32), n_heads)
    cos_s, sin_s = jnp.tile(cos_s, (bs, 1)), jnp.tile(sin_s, (bs, 1))
    pool_kt = jnp.transpose(cache_k, (0, 1, 3, 4, 2)).reshape(n_layers_ab * n_pool, half, page)
    pool_vt = jnp.transpose(cache_v, (0, 1, 3, 4, 2)).reshape(n_layers_ab * n_pool, half, page)
    seq_per_tile = max(RET_CHUNK // ls, 1)
    ret_tiles_p = min(4, lp // RET_CHUNK)

    k_p, v_p, k_s, v_s, r_p, r_s = [], [], [], [], [], []
    sre_p, sim_p, sre_s, sim_s = [], [], [], []
    for layer in range(depth):
        li = layer // 2
        g = lambda i: ln_g[layer, i][None, :]
        b = lambda i: ln_b[layer, i][None, :]
        f1 = (bf(ffn1_w_gate[layer]), bf(ffn1_w_up[layer]), bf(ffn1_w_down[layer]))
        f2 = (bf(ffn2_w_gate[layer]), bf(ffn2_w_up[layer]), bf(ffn2_w_down[layer]))
        xp = ffn_ln(xp, *f1, g(0), b(0), alpha=alpha, tm=tm_p)
        xs = ffn_ln(xs, *f1, g(0), b(0), alpha=alpha, tm=ts)
        if layer % 2 == 0:
            w_in = bf(w_in_ab[li])
            w_a, w_r = bf(w_out_ab[li][:half]), bf(w_out_ab[li][half:])
            qa, ka, va, kbf, vt, ksum, qb, kb, vb, gb = ab_project(xp, w_in, cos_p, sin_p, tm=tm_proj, seq_len=lp)
            attn = moba_prompt(qa, ksum, kbf, vt, batch=bp, seq_len=lp)
            zero_state = jnp.zeros((bp, n_heads // 2, HEAD_PAIR, HEAD_PAIR), F32)
            ret, s_fin = retention_gated(qb, kb, vb, gb, zero_state, n_seq=1, chunk=RET_CHUNK, tiles=ret_tiles_p,
                                         steps_per_seq=lp // (RET_CHUNK * ret_tiles_p))
            xp = merge_ln(attn, ret, w_a, w_r, xp, g(1), b(1), alpha=alpha, tm=tm_proj)
            k_p.append(ka.reshape(bp, lp // page, page, a_heads, hd))
            v_p.append(va.reshape(bp, lp // page, page, a_heads, hd))
            r_p.append(_block_diag_to_states(s_fin))
            qa, ka, va, _, _, _, qb, kb, vb, gb = ab_project(xs, w_in, cos_s, sin_s, tm=ts, seq_len=ts)
            page_ids = (page_table.astype(jnp.int32) + li * n_pool).reshape(-1)
            attn = moba_sample(qa, ka, va, pool_kt, pool_vt, page_ids, batch=bs, n_pages=n_pages)
            ret, s_fin = retention_gated(qb, kb, vb, gb, _states_to_block_diag(state_ret[li]), n_seq=seq_per_tile,
                                         chunk=ls, tiles=1, steps_per_seq=1)
            xs = merge_ln(attn, ret, w_a, w_r, xs, g(1), b(1), alpha=alpha, tm=ts)
            k_s.append(ka.reshape(bs, ls, a_heads, hd))
            v_s.append(va.reshape(bs, ls, a_heads, hd))
            r_s.append(_block_diag_to_states(s_fin))
        else:
            weights = _s5_weights(s5_lam_re[li], s5_lam_im[li], s5_b_re[li], s5_b_im[li], s5_c_re[li],
                                  s5_c_im[li], s5_log_dt[li])
            n_state = weights[2].shape[1]
            d_skip = s5_d[li][None, :]
            w_o, w_g = bf(s5_w_out[li]), bf(s5_w_gate[li])
            zero = jnp.zeros((bp, n_state), F32)
            y, a_re, a_im = s5_scan(xp.reshape(bp, lp, d), zero, zero, weights, d_skip, tt=min(32, lp))
            xp = s5_out_ln(y.reshape(tp, d), w_o, w_g, xp, g(1), b(1), alpha=alpha, tm=tm_proj)
            sre_p.append(a_re.reshape(bp, -1, S5_STATE))
            sim_p.append(a_im.reshape(bp, -1, S5_STATE))
            y, a_re, a_im = s5_scan(xs.reshape(bs, ls, d), state_s5_re[li].reshape(bs, n_state),
                                    state_s5_im[li].reshape(bs, n_state), weights, d_skip, tt=ls)
            xs = s5_out_ln(y.reshape(ts, d), w_o, w_g, xs, g(1), b(1), alpha=alpha, tm=ts)
            sre_s.append(a_re.reshape(bs, -1, S5_STATE))
            sim_s.append(a_im.reshape(bs, -1, S5_STATE))
        xp = ffn_ln(xp, *f2, g(2), b(2), alpha=alpha, tm=tm_p)
        xs = ffn_ln(xs, *f2, g(2), b(2), alpha=alpha, tm=ts)
    return (xp.reshape(bp, lp, d), xs.reshape(bs, ls, d), jnp.stack(k_p), jnp.stack(v_p), jnp.stack(k_s),
            jnp.stack(v_s), jnp.stack(r_p), jnp.stack(r_s), jnp.stack(sre_p), jnp.stack(sim_p),
            jnp.stack(sre_s), jnp.stack(sim_s))
```

```python
import functools
import math

import jax
import jax.numpy as jnp
from jax import lax
from jax.experimental import pallas as pl
from jax.experimental.pallas import tpu as pltpu

F32 = jnp.float32
BF16 = jnp.bfloat16

HEAD_DIM = 64
HEAD_PAIR = 2 * HEAD_DIM
MOBA_BLOCK = 256
MOBA_TOPK = 3
RET_CHUNK = 128
S5_GROUP = 16
S5_STATE = 64
ROPE_THETA = 10000.0
LN_EPS = 1e-5
GN_EPS = 1e-6
NEG_INF = -1e30
SUBLANES = 8
LANES = 128
VMEM_LIMIT = 48 * 1024 * 1024


def _dot(a, b, precision=None):
    return jnp.dot(a, b, preferred_element_type=F32, precision=precision)


def _dot_nt(a, b, precision=None):
    return lax.dot_general(a, b, (((1,), (1,)), ((), ())), preferred_element_type=F32, precision=precision)


def _dot_tn(a, b):
    return lax.dot_general(a, b, (((0,), (0,)), ((), ())), preferred_element_type=F32)


def _layer_norm(r, g, b):
    mu = jnp.mean(r, -1, keepdims=True)
    d = r - mu
    var = jnp.mean(d * d, -1, keepdims=True)
    return d * lax.rsqrt(var + LN_EPS) * g + b


def _silu(x):
    return x * jax.nn.sigmoid(x)


def _params(*sem):
    return pltpu.CompilerParams(dimension_semantics=sem, vmem_limit_bytes=VMEM_LIMIT)


def _ffn_ln_kernel(x_ref, wg_ref, wu_ref, wd_ref, g_ref, b_ref, o_ref, a_ref, *, alpha, tf):
    xb = x_ref[...].astype(BF16)
    for c in range(wg_ref.shape[1] // tf):
        cols = slice(c * tf, (c + 1) * tf)
        hg = _dot(xb, wg_ref[:, cols])
        hu = _dot(xb, wu_ref[:, cols])
        a_ref[:, cols] = (_silu(hg) * hu).astype(BF16)
    r = alpha * x_ref[...] + 0.5 * _dot(a_ref[...], wd_ref[...])
    o_ref[...] = _layer_norm(r, g_ref[...], b_ref[...])


def _layer_weight(layer, r, c, row_blk=0, **kw):
    return pl.BlockSpec((None, r, c), lambda *_: (layer, row_blk, 0), **kw)


def ffn_ln(x, wg, wu, wd, g, b, *, layer, alpha, tm, tf=256):
    t, d = x.shape
    f = wg.shape[2]
    resident = lambda r, c: _layer_weight(layer, r, c, pipeline_mode=pl.Buffered(1))
    return pl.pallas_call(
        functools.partial(_ffn_ln_kernel, alpha=alpha, tf=tf),
        grid=(t // tm,),
        in_specs=[pl.BlockSpec((tm, d), lambda i: (i, 0)), resident(d, f), resident(d, f), resident(f, d),
                  pl.BlockSpec((1, d), lambda i: (0, 0)), pl.BlockSpec((1, d), lambda i: (0, 0))],
        out_specs=pl.BlockSpec((tm, d), lambda i: (i, 0)),
        out_shape=jax.ShapeDtypeStruct((t, d), F32),
        scratch_shapes=[pltpu.VMEM((tm, f), BF16)],
        compiler_params=_params("parallel"),
    )(x, wg, wu, wd, g, b)


def _rope(y, cos, sin_signed):
    width = y.shape[-1]
    lane = lax.broadcasted_iota(jnp.int32, (1, width), 1)
    first = (lane % HEAD_DIM) < (HEAD_DIM // 2)
    rot = jnp.where(first, pltpu.roll(y, width - HEAD_DIM // 2, 1), pltpu.roll(y, HEAD_DIM // 2, 1))
    return y * cos + rot * sin_signed


def _ab_proj_kernel(x_ref, w_ref, cos_ref, sin_ref, qa_ref, ka_ref, va_ref, kbf_ref, vt_ref, ksum_ref,
                    qb_ref, kb_ref, vb_ref, gb_ref, *, half, scale):
    xb = x_ref[...].astype(BF16)
    cos = cos_ref[...]
    sin = sin_ref[...]
    col = lambda c: _dot(xb, w_ref[:, c * half:(c + 1) * half])
    qa_ref[...] = _rope(col(0), cos, sin) * scale
    ka = _rope(col(1), cos, sin)
    ka_ref[...] = ka
    kbf_ref[...] = ka.astype(BF16)
    nblk = ka.shape[0] // MOBA_BLOCK
    for n in range(nblk):
        ksum_ref[n] = jnp.sum(ka[n * MOBA_BLOCK:(n + 1) * MOBA_BLOCK], axis=0, keepdims=True)
    va = col(2)
    va_ref[...] = va
    for n in range(nblk):
        vt_ref[n] = va[n * MOBA_BLOCK:(n + 1) * MOBA_BLOCK].T.astype(BF16)
    qb_ref[...] = _rope(col(3), cos, sin)
    kb_ref[...] = _rope(col(4), cos, sin) * scale
    vb_ref[...] = col(5)
    gb_ref[...] = col(6)


def ab_project(x, w_in, cos, sin, *, layer, tm, seq_len):
    t, d = x.shape
    half = w_in.shape[2] // 7
    tiles_per_seq = max(seq_len // tm, 1)
    tab = pl.BlockSpec((tm, half), lambda i: (i % tiles_per_seq, 0))
    row = pl.BlockSpec((tm, half), lambda i: (i, 0))
    nblk = tm // MOBA_BLOCK
    f32o = jax.ShapeDtypeStruct((t, half), F32)
    bfo = jax.ShapeDtypeStruct((t, half), BF16)
    return pl.pallas_call(
        functools.partial(_ab_proj_kernel, half=half, scale=HEAD_DIM ** -0.5),
        grid=(t // tm,),
        in_specs=[pl.BlockSpec((tm, d), lambda i: (i, 0)),
                  _layer_weight(layer, d, 7 * half), tab, tab],
        out_specs=[row, row, row, row,
                   pl.BlockSpec((nblk, half, MOBA_BLOCK), lambda i: (i, 0, 0)),
                   pl.BlockSpec((nblk, 1, half), lambda i: (i, 0, 0)),
                   row, row, row, row],
        out_shape=[f32o, f32o, f32o, bfo,
                   jax.ShapeDtypeStruct((t // MOBA_BLOCK, half, MOBA_BLOCK), BF16),
                   jax.ShapeDtypeStruct((t // MOBA_BLOCK, 1, half), F32),
                   f32o, f32o, f32o, f32o],
        compiler_params=_params("parallel"),
    )(x, w_in, cos, sin)


def _topk_select(gate, n_valid):
    nb = gate.shape[1]
    blk = lax.broadcasted_iota(jnp.int32, (1, nb), 1)
    gate = jnp.where(blk < n_valid, gate, NEG_INF)
    rank = jnp.zeros(gate.shape, jnp.int32)
    for m in range(nb):
        gm = gate[:, m:m + 1]
        ahead = (gm > gate) | ((gm == gate) & (m < blk))
        rank = rank + ahead.astype(jnp.int32)
    return ((rank < MOBA_TOPK) & (blk < n_valid)).astype(F32)


def _topk_select_t(gate_t, n_valid):
    nb = gate_t.shape[0]
    blk = lax.broadcasted_iota(jnp.int32, (nb, 1), 0)
    gate_t = jnp.where(blk < n_valid, gate_t, NEG_INF)
    rank = jnp.zeros(gate_t.shape, jnp.int32)
    for m in range(nb):
        gm = gate_t[m:m + 1, :]
        ahead = (gm > gate_t) | ((gm == gate_t) & (m < blk))
        rank = rank + ahead.astype(jnp.int32)
    return ((rank < MOBA_TOPK) & (blk < n_valid)).astype(F32)


def _moba_prompt_kernel(q_ref, ksum_ref, k_ref, vt_ref, o_ref, qm_ref, sel_ref, m_ref, l_ref, a_ref, acc_ref,
                        s_ref, p_ref):
    j = pl.program_id(1)
    tq = q_ref.shape[0]
    n_heads = q_ref.shape[1] // HEAD_DIM
    lane = lax.broadcasted_iota(jnp.int32, (1, HEAD_PAIR), 1)
    key_i = lax.broadcasted_iota(jnp.int32, (MOBA_BLOCK, tq), 0)
    qry_i = lax.broadcasted_iota(jnp.int32, (MOBA_BLOCK, tq), 1)
    causal = key_i <= qry_i

    for h in range(n_heads):
        lanes = slice((h // 2) * HEAD_PAIR, (h // 2 + 1) * HEAD_PAIR)
        hm = (lane // HEAD_DIM) == (h % 2)
        qp = q_ref[:, lanes]
        kmean = jnp.where(hm, ksum_ref[:, 0, lanes] * (1.0 / MOBA_BLOCK), 0.0)
        sel_ref[h] = _topk_select_t(_dot_nt(kmean, qp, precision=lax.Precision.HIGHEST), j)
        qm_ref[h] = jnp.where(hm, qp, 0.0).astype(BF16)

    def update(k_blk, vt_blk, picked, first):
        for h in range(n_heads):
            k_pair = k_blk[:, (h // 2) * HEAD_PAIR:(h // 2 + 1) * HEAD_PAIR]
            s_ref[h] = _dot_nt(k_pair, qm_ref[h])
        for h in range(n_heads):
            if first:
                s = jnp.where(causal, s_ref[h], NEG_INF)
                m_new = jnp.max(s, axis=0, keepdims=True)
                p = jnp.exp(s - m_new)
                l_ref[h] = jnp.sum(p, axis=0, keepdims=True)
            else:
                s = s_ref[h]
                cmax = jnp.where(picked(h), jnp.max(s, axis=0, keepdims=True), NEG_INF)
                m_new = jnp.maximum(m_ref[h], cmax)
                p = jnp.exp(s - m_new)
                a = jnp.exp(m_ref[h] - m_new)
                a_ref[h] = a
                l_ref[h] = a * l_ref[h] + jnp.where(picked(h), jnp.sum(p, axis=0, keepdims=True), 0.0)
            p_ref[h] = p.astype(BF16)
            m_ref[h] = m_new
        for h in range(n_heads):
            hp, half = h // 2, h % 2
            vt_pair = vt_blk[hp * HEAD_PAIR:(hp + 1) * HEAD_PAIR, :]
            pv = _dot(vt_pair, p_ref[h])[half * HEAD_DIM:(half + 1) * HEAD_DIM]
            acc_ref[h] = pv if first else a_ref[h] * acc_ref[h] + jnp.where(picked(h), pv, 0.0)

    own0 = pl.multiple_of(j * MOBA_BLOCK, MOBA_BLOCK)
    update(k_ref[pl.ds(own0, MOBA_BLOCK), :], vt_ref[j], None, True)

    def body(n, carry):
        r0 = pl.multiple_of(n * MOBA_BLOCK, MOBA_BLOCK)
        update(k_ref[pl.ds(r0, MOBA_BLOCK), :], vt_ref[n], lambda h: sel_ref[h, pl.ds(n, 1), :] > 0.0, False)
        return carry

    lax.fori_loop(0, j, body, 0)
    out_t = jnp.concatenate([acc_ref[h] / l_ref[h] for h in range(n_heads)], axis=0)
    o_ref[...] = out_t.T


def moba_prompt(q, ksum, kbf, vt, *, batch, seq_len):
    t, w = q.shape
    nb = seq_len // MOBA_BLOCK
    n_heads = w // HEAD_DIM
    return pl.pallas_call(
        _moba_prompt_kernel,
        grid=(batch, nb),
        in_specs=[pl.BlockSpec((MOBA_BLOCK, w), lambda b, j: (b * nb + j, 0)),
                  pl.BlockSpec((nb, 1, w), lambda b, j: (b, 0, 0)),
                  pl.BlockSpec((seq_len, w), lambda b, j: (b, 0)),
                  pl.BlockSpec((nb, w, MOBA_BLOCK), lambda b, j: (b, 0, 0))],
        out_specs=pl.BlockSpec((MOBA_BLOCK, w), lambda b, j: (b * nb + j, 0)),
        out_shape=jax.ShapeDtypeStruct((t, w), F32),
        scratch_shapes=[pltpu.VMEM((n_heads, MOBA_BLOCK, HEAD_PAIR), BF16),
                        pltpu.VMEM((n_heads, nb, MOBA_BLOCK), F32),
                        pltpu.VMEM((n_heads, 1, MOBA_BLOCK), F32),
                        pltpu.VMEM((n_heads, 1, MOBA_BLOCK), F32),
                        pltpu.VMEM((n_heads, 1, MOBA_BLOCK), F32),
                        pltpu.VMEM((n_heads, HEAD_DIM, MOBA_BLOCK), F32),
                        pltpu.VMEM((n_heads, MOBA_BLOCK, MOBA_BLOCK), F32),
                        pltpu.VMEM((n_heads, MOBA_BLOCK, MOBA_BLOCK), BF16)],
        compiler_params=_params("parallel", "arbitrary"),
    )(q, ksum, kbf, vt)


def _moba_sample_kernel(pt_ref, q_ref, knew_ref, vnew_ref, *rest, pages_per_step, page, n_heads):
    del pt_ref
    kpages = rest[:pages_per_step]
    vpages = rest[pages_per_step:2 * pages_per_step]
    o_ref = rest[2 * pages_per_step]
    qf_ref, qb_ref, ksum_ref, m_ref, l_ref, acc_ref, kpad_ref, vpad_ref = rest[2 * pages_per_step + 1:]
    s_idx = pl.program_id(1)
    n_steps = pl.num_programs(1)
    lq, w = q_ref.shape
    rows = n_heads * lq
    nb = acc_ref.shape[0]
    pages_per_blk = MOBA_BLOCK // page
    blks_per_step = pages_per_step // pages_per_blk
    rowhead = lax.broadcasted_iota(jnp.int32, (rows, 1), 0) // lq
    lanehead = lax.broadcasted_iota(jnp.int32, (1, w), 1) // HEAD_DIM
    blk_lane = lax.broadcasted_iota(jnp.int32, (1, LANES), 1)

    @pl.when(s_idx == 0)
    def _():
        qt = jnp.concatenate([q_ref[...]] * n_heads, axis=0)
        qbd = jnp.where(rowhead == lanehead, qt, 0.0)
        qf_ref[...] = qbd
        qb_ref[...] = qbd.astype(BF16)
        ksum_ref[...] = jnp.zeros_like(ksum_ref)
        m_ref[...] = jnp.zeros_like(m_ref)
        l_ref[...] = jnp.zeros_like(l_ref)

    qb = qb_ref[...]

    def partial_softmax(s):
        m = jnp.max(s, axis=1, keepdims=True)
        e = jnp.exp(s - m)
        return m, jnp.sum(e, axis=1, keepdims=True), e.astype(BF16)

    blks = range(blks_per_step)
    pages_of = lambda bi: range(bi * pages_per_blk, (bi + 1) * pages_per_blk)
    here = [blk_lane == s_idx * blks_per_step + bi for bi in blks]
    kt = [jnp.concatenate([kpages[i][0] for i in pages_of(bi)], axis=1) for bi in blks]
    scores = [_dot(qb, kt[bi].astype(BF16)) for bi in blks]
    ksum = ksum_ref[...]
    for bi in blks:
        ksum = jnp.where(here[bi], jnp.sum(kt[bi], axis=1, keepdims=True), ksum)
    ksum_ref[...] = ksum
    stats = [partial_softmax(scores[bi]) for bi in blks]
    for bi in blks:
        vt = jnp.concatenate([vpages[i][0] for i in pages_of(bi)], axis=1)
        acc_ref[s_idx * blks_per_step + bi] = _dot_nt(stats[bi][2], vt.astype(BF16))
    m_all, l_all = m_ref[...], l_ref[...]
    for bi in blks:
        m_all = jnp.where(here[bi], stats[bi][0], m_all)
        l_all = jnp.where(here[bi], stats[bi][1], l_all)
    m_ref[...] = m_all
    l_ref[...] = l_all

    @pl.when(s_idx == n_steps - 1)
    def _():
        kpad_ref[...] = jnp.zeros_like(kpad_ref)
        vpad_ref[...] = jnp.zeros_like(vpad_ref)
        kpad_ref[0:lq, :] = knew_ref[...]
        vpad_ref[0:lq, :] = vnew_ref[...]
        s_own = _dot_nt(qb, kpad_ref[...].astype(BF16))
        tq = lax.broadcasted_iota(jnp.int32, s_own.shape, 0) % lq
        tk = lax.broadcasted_iota(jnp.int32, s_own.shape, 1)
        m_own, l_own, e_own = partial_softmax(jnp.where(tk <= tq, s_own, NEG_INF))
        acc_own = _dot(e_own, vpad_ref[...].astype(BF16))

        gate = _dot(qf_ref[...], ksum_ref[...] * (1.0 / MOBA_BLOCK), precision=lax.Precision.HIGHEST)
        sel = _topk_select(gate[:, :nb], nb) > 0.0
        m_blk = m_ref[:, :nb]
        m_all = jnp.maximum(m_own, jnp.max(jnp.where(sel, m_blk, NEG_INF), axis=1, keepdims=True))
        wgt = jnp.where(sel, jnp.exp(m_blk - m_all), 0.0)
        w_own = jnp.exp(m_own - m_all)
        den = w_own * l_own + jnp.sum(wgt * l_ref[:, :nb], axis=1, keepdims=True)
        num = w_own * acc_own
        for n in range(nb):
            num = num + wgt[:, n:n + 1] * acc_ref[n]
        o_all = num / den
        out = jnp.zeros((lq, w), F32)
        for h in range(n_heads):
            out = out + jnp.where(lanehead == h, o_all[h * lq:(h + 1) * lq], 0.0)
        o_ref[...] = out


def moba_sample(q, knew, vnew, pool_kt, pool_vt, page_ids, *, batch, n_pages, pages_per_step=8):
    t, w = q.shape
    lq = t // batch
    page = pool_kt.shape[2]
    n_heads = w // HEAD_DIM
    rows = n_heads * lq
    nb = n_pages * page // MOBA_BLOCK
    assert nb <= LANES and pages_per_step % (MOBA_BLOCK // page) == 0
    seq_blk = pl.BlockSpec((lq, w), lambda b, s, pt: (b, 0))

    def page_spec(i):
        return pl.BlockSpec((1, w, page), lambda b, s, pt: (pt[b * n_pages + s * pages_per_step + i], 0, 0))

    grid_spec = pltpu.PrefetchScalarGridSpec(
        num_scalar_prefetch=1,
        grid=(batch, n_pages // pages_per_step),
        in_specs=[seq_blk, seq_blk, seq_blk] + [page_spec(i) for i in range(pages_per_step)] * 2,
        out_specs=seq_blk,
        scratch_shapes=[pltpu.VMEM((rows, w), F32), pltpu.VMEM((rows, w), BF16), pltpu.VMEM((w, LANES), F32),
                        pltpu.VMEM((rows, LANES), F32), pltpu.VMEM((rows, LANES), F32),
                        pltpu.VMEM((nb, rows, w), F32),
                        pltpu.VMEM((LANES, w), F32), pltpu.VMEM((LANES, w), F32)],
    )
    return pl.pallas_call(
        functools.partial(_moba_sample_kernel, pages_per_step=pages_per_step, page=page, n_heads=n_heads),
        grid_spec=grid_spec,
        out_shape=jax.ShapeDtypeStruct((t, w), F32),
        compiler_params=_params("parallel", "arbitrary"),
    )(page_ids, q, knew, vnew, *([pool_kt] * pages_per_step), *([pool_vt] * pages_per_step))


def _retention_kernel(q_ref, k_ref, v_ref, g_ref, s0_ref, dmask_ref, qdec_ref, kdec_ref, cdec_ref,
                      o_ref, sout_ref, st_ref, *, n_seq, chunk, tiles, steps_per_seq):
    step = pl.program_id(0)
    tile_rows = n_seq * chunk
    n_pairs = q_ref.shape[1] // HEAD_PAIR
    lane = lax.broadcasted_iota(jnp.int32, (1, HEAD_PAIR), 1)
    rowh = lax.broadcasted_iota(jnp.int32, (HEAD_PAIR, 1), 0) // HEAD_DIM
    same_head = rowh == (lane // HEAD_DIM)
    rseq = lax.broadcasted_iota(jnp.int32, (tile_rows, 1), 0) // chunk

    @pl.when(step % steps_per_seq == 0)
    def _():
        st_ref[...] = s0_ref[...]

    pairs = range(n_pairs)
    lanes_of = lambda hp: slice(hp * HEAD_PAIR, (hp + 1) * HEAD_PAIR)
    half_masks = [(lane // HEAD_DIM) == half for half in range(2)]
    for ti in range(tiles):
        rows = slice(ti * tile_rows, (ti + 1) * tile_rows)
        qb = [q_ref[rows, lanes_of(hp)].astype(BF16) for hp in pairs]
        kf = [k_ref[rows, lanes_of(hp)] for hp in pairs]
        kb = [k.astype(BF16) for k in kf]
        vb = [v_ref[rows, lanes_of(hp)].astype(BF16) for hp in pairs]
        inner = [[_dot_nt(jnp.where(hm, qb[hp], jnp.zeros_like(qb[hp])), kb[hp]) for hm in half_masks]
                 for hp in pairs]
        scaled = [[(inner[hp][half] * dmask_ref[2 * hp + half]).astype(BF16) for half in range(2)] for hp in pairs]
        ret = [sum(jnp.where(half_masks[half], _dot(scaled[hp][half], vb[hp]), 0.0) for half in range(2))
               for hp in pairs]
        for hp in pairs:
            kd = (kf[hp] * kdec_ref[:, lanes_of(hp)]).astype(BF16)
            cdec = cdec_ref[:, lanes_of(hp)]
            carried = []
            for s in range(n_seq):
                srows = slice(s * chunk, (s + 1) * chunk)
                state = st_ref[s, hp]
                carried.append(_dot(qb[hp][srows], state.astype(BF16)))
                kd_s = kd if n_seq == 1 else jnp.where(rseq == s, kd, jnp.zeros_like(kd))
                st_ref[s, hp] = cdec * state + jnp.where(same_head, _dot_tn(kd_s, vb[hp]), 0.0)
            carried = carried[0] if n_seq == 1 else jnp.concatenate(carried, axis=0)
            ret[hp] = ret[hp] + carried * qdec_ref[:, lanes_of(hp)]
        for hp in pairs:
            normed = jnp.zeros_like(ret[hp])
            for hm in half_masks:
                mu = jnp.sum(jnp.where(hm, ret[hp], 0.0), axis=1, keepdims=True) * (1.0 / HEAD_DIM)
                d = jnp.where(hm, ret[hp] - mu, 0.0)
                var = jnp.sum(d * d, axis=1, keepdims=True) * (1.0 / HEAD_DIM)
                normed = normed + d * lax.rsqrt(var + GN_EPS)
            o_ref[rows, lanes_of(hp)] = _silu(g_ref[rows, lanes_of(hp)]) * normed

    @pl.when(step % steps_per_seq == steps_per_seq - 1)
    def _():
        sout_ref[...] = st_ref[...]


def _retention_tables(n_heads, n_seq, chunk):
    log_g = jnp.log1p(-jnp.exp2(-5.0 - jnp.arange(n_heads, dtype=F32)))
    idx = jnp.arange(chunk, dtype=F32)
    diff = idx[:, None] - idx[None, :]
    dmask = jnp.where(diff >= 0, jnp.exp(log_g[:, None, None] * jnp.maximum(diff, 0.0)), 0.0)
    q_dec = jnp.exp(log_g[None, :] * (idx[:, None] + 1.0))
    k_dec = jnp.exp(log_g[None, :] * (chunk - 1.0 - idx[:, None]))
    c_dec = jnp.exp(log_g * chunk)
    seq_eye = jnp.eye(n_seq, dtype=F32)
    dmask = jnp.einsum("ab,hij->haibj", seq_eye, dmask).reshape(n_heads, n_seq * chunk, n_seq * chunk)
    lanes = lambda t: jnp.repeat(t, HEAD_DIM, axis=-1)
    return dmask, jnp.tile(lanes(q_dec), (n_seq, 1)), jnp.tile(lanes(k_dec), (n_seq, 1)), lanes(c_dec[None, :])


def retention_gated(q, k, v, g, s0_bd, *, n_seq, chunk, tiles, steps_per_seq):
    t, w = q.shape
    n_pairs = w // HEAD_PAIR
    rows = n_seq * chunk * tiles
    n_steps = t // rows
    dmask, qdec, kdec, cdec = _retention_tables(w // HEAD_DIM, n_seq, chunk)
    tile_rows = n_seq * chunk
    row = pl.BlockSpec((rows, w), lambda i: (i, 0))
    st = pl.BlockSpec((n_seq, n_pairs, HEAD_PAIR, HEAD_PAIR), lambda i: (i // steps_per_seq, 0, 0, 0))
    const = lambda shape: pl.BlockSpec(shape, lambda i: (0,) * len(shape))
    return pl.pallas_call(
        functools.partial(_retention_kernel, n_seq=n_seq, chunk=chunk, tiles=tiles, steps_per_seq=steps_per_seq),
        grid=(n_steps,),
        in_specs=[row, row, row, row, st, const(dmask.shape), const((tile_rows, w)), const((tile_rows, w)),
                  const((1, w))],
        out_specs=[row, st],
        out_shape=[jax.ShapeDtypeStruct((t, w), F32), jax.ShapeDtypeStruct(s0_bd.shape, F32)],
        scratch_shapes=[pltpu.VMEM((n_seq, n_pairs, HEAD_PAIR, HEAD_PAIR), F32)],
        compiler_params=_params("arbitrary"),
    )(q, k, v, g, s0_bd, dmask, qdec, kdec, cdec)


def _states_to_block_diag(s):
    b, h, dk, dv = s.shape
    sp = s.reshape(b, h // 2, 2, dk, dv)
    z = jnp.zeros_like(sp[:, :, 0])
    top = jnp.concatenate([sp[:, :, 0], z], axis=-1)
    bot = jnp.concatenate([z, sp[:, :, 1]], axis=-1)
    return jnp.concatenate([top, bot], axis=-2)


def _block_diag_to_states(s):
    b, p = s.shape[:2]
    a = s[:, :, :HEAD_DIM, :HEAD_DIM]
    c = s[:, :, HEAD_DIM:, HEAD_DIM:]
    return jnp.stack([a, c], axis=2).reshape(b, 2 * p, HEAD_DIM, HEAD_DIM)


def _merge_ln_kernel(a_ref, r_ref, wa_ref, wr_ref, x_ref, g_ref, b_ref, o_ref, *, alpha):
    mix = _dot(a_ref[...].astype(BF16), wa_ref[...]) + _dot(r_ref[...].astype(BF16), wr_ref[...])
    o_ref[...] = _layer_norm(alpha * x_ref[...] + mix, g_ref[...], b_ref[...])


def merge_ln(attn, ret, w_out, x, g, b, *, layer, alpha, tm):
    t, d = x.shape
    w = attn.shape[1]
    row = lambda n: pl.BlockSpec((tm, n), lambda i: (i, 0))
    const = lambda r, c: pl.BlockSpec((r, c), lambda i: (0, 0))
    return pl.pallas_call(
        functools.partial(_merge_ln_kernel, alpha=alpha),
        grid=(t // tm,),
        in_specs=[row(w), row(w), _layer_weight(layer, w, d, 0), _layer_weight(layer, w, d, 1), row(d),
                  const(1, d), const(1, d)],
        out_specs=row(d),
        out_shape=jax.ShapeDtypeStruct((t, d), F32),
        compiler_params=_params("parallel"),
    )(attn, ret, w_out, w_out, x, g, b)


def _s5_scan_kernel(x_ref, s0re_ref, s0im_ref, wbre_ref, wbim_ref, are_ref, aim_ref, wcre_ref, wcim_ref, d_ref,
                    y_ref, sre_ref, sim_ref, xs_ref, bre_ref, bim_ref, stre_ref, stim_ref, *, lane_chunk):
    step = pl.program_id(0)
    nb, tt, d = x_ref.shape
    b8 = stre_ref.shape[0]
    ns = stre_ref.shape[1]
    n_kb = d // LANES
    per_kb = ns // n_kb

    @pl.when(step == 0)
    def _():
        stre_ref[...] = jnp.zeros_like(stre_ref)
        stim_ref[...] = jnp.zeros_like(stim_ref)
        stre_ref[0:nb, :] = s0re_ref[...]
        stim_ref[0:nb, :] = s0im_ref[...]
        xs_ref[...] = jnp.zeros_like(xs_ref)

    for b in range(nb):
        for kb in range(n_kb):
            xs_ref[kb, pl.ds(b, tt, stride=b8), :] = x_ref[b, :, kb * LANES:(kb + 1) * LANES]

    for kb in range(n_kb):
        xk = xs_ref[kb].astype(BF16)
        cols = slice(kb * per_kb, (kb + 1) * per_kb)
        bre_ref[:, cols] = _dot(xk, wbre_ref[kb])
        bim_ref[:, cols] = _dot(xk, wbim_ref[kb])

    for r0 in range(0, b8, SUBLANES):
        for c0 in range(0, ns, lane_chunk):
            cs = slice(c0, c0 + lane_chunk)
            are = jnp.broadcast_to(are_ref[:, cs], (SUBLANES, lane_chunk))
            aim = jnp.broadcast_to(aim_ref[:, cs], (SUBLANES, lane_chunk))

            def body(t, carry, r0=r0, cs=cs, are=are, aim=aim):
                sre, sim = carry
                rows = pl.ds(pl.multiple_of(t * b8 + r0, SUBLANES), SUBLANES)
                nre = are * sre - aim * sim + bre_ref[rows, cs]
                nim = are * sim + aim * sre + bim_ref[rows, cs]
                bre_ref[rows, cs] = nre
                bim_ref[rows, cs] = nim
                return nre, nim

            sre, sim = lax.fori_loop(0, tt, body, (stre_ref[r0:r0 + SUBLANES, cs], stim_ref[r0:r0 + SUBLANES, cs]))
            stre_ref[r0:r0 + SUBLANES, cs] = sre
            stim_ref[r0:r0 + SUBLANES, cs] = sim

    n_out = wcre_ref.shape[0]
    kin = ns // n_out
    wout = d // n_out
    for j in range(n_out):
        sre = bre_ref[:, j * kin:(j + 1) * kin].astype(BF16)
        sim = bim_ref[:, j * kin:(j + 1) * kin].astype(BF16)
        y = _dot(sre, wcre_ref[j]) + _dot(sim, wcim_ref[j])
        for i in range(wout // LANES):
            kb = j * (wout // LANES) + i
            xs_ref[kb] = y[:, i * LANES:(i + 1) * LANES] + xs_ref[kb] * d_ref[:, kb * LANES:(kb + 1) * LANES]

    for b in range(nb):
        for kb in range(n_kb):
            y_ref[b, :, kb * LANES:(kb + 1) * LANES] = xs_ref[kb, pl.ds(b, tt, stride=b8), :]

    @pl.when(step == pl.num_programs(0) - 1)
    def _():
        sre_ref[...] = stre_ref[0:nb, :]
        sim_ref[...] = stim_ref[0:nb, :]


def _s5_weights(lam_re, lam_im, b_re, b_im, c_re, c_im, log_dt):
    g, p, h = b_re.shape
    lam = lax.complex(lam_re.astype(F32), lam_im.astype(F32))
    dt = jnp.exp(log_dt.astype(F32))[:, None]
    lam_bar = jnp.exp(lam * dt)
    b_bar = ((lam_bar - 1.0) / lam)[:, :, None] * lax.complex(b_re.astype(F32), b_im.astype(F32))
    gpk = LANES // h
    n_kb = g // gpk
    eye = jnp.eye(gpk, dtype=F32)

    def in_blocks(m):
        m = m.reshape(n_kb, gpk, p, h)
        return jnp.einsum("kgph,gf->kghfp", m, eye).reshape(n_kb, gpk * h, gpk * p).astype(BF16)

    gpo = 2 * LANES // h
    n_out = g // gpo
    eye_o = jnp.eye(gpo, dtype=F32)

    def out_blocks(m):
        m = m.reshape(n_out, gpo, h, p)
        return jnp.einsum("kghp,gf->kgpfh", m, eye_o).reshape(n_out, gpo * p, gpo * h).astype(BF16)

    return (in_blocks(jnp.real(b_bar)), in_blocks(jnp.imag(b_bar)),
            jnp.real(lam_bar).reshape(1, g * p), jnp.imag(lam_bar).reshape(1, g * p),
            out_blocks(c_re.astype(F32)), out_blocks(-c_im.astype(F32)))


def s5_scan(x, s0_re, s0_im, weights, d_skip, *, tt, lane_chunk=1024):
    nb, seq, d = x.shape
    wbre, wbim, are, aim, wcre, wcim = weights
    ns = are.shape[1]
    b8 = -(-nb // SUBLANES) * SUBLANES
    rows = tt * b8
    const = lambda a: pl.BlockSpec(a.shape, lambda i: (0,) * a.ndim)
    xblk = pl.BlockSpec((nb, tt, d), lambda i: (0, i, 0))
    sblk = pl.BlockSpec((nb, ns), lambda i: (0, 0))
    return pl.pallas_call(
        functools.partial(_s5_scan_kernel, lane_chunk=lane_chunk),
        grid=(seq // tt,),
        in_specs=[xblk, sblk, sblk, const(wbre), const(wbim), const(are), const(aim), const(wcre), const(wcim),
                  pl.BlockSpec((1, d), lambda i: (0, 0))],
        out_specs=[xblk, sblk, sblk],
        out_shape=[jax.ShapeDtypeStruct((nb, seq, d), F32), jax.ShapeDtypeStruct((nb, ns), F32),
                   jax.ShapeDtypeStruct((nb, ns), F32)],
        scratch_shapes=[pltpu.VMEM((d // LANES, rows, LANES), F32), pltpu.VMEM((rows, ns), F32),
                        pltpu.VMEM((rows, ns), F32),
                        pltpu.VMEM((b8, ns), F32), pltpu.VMEM((b8, ns), F32)],
        compiler_params=_params("arbitrary"),
    )(x, s0_re, s0_im, wbre, wbim, are, aim, wcre, wcim, d_skip)


def _s5_out_ln_kernel(y_ref, wo_ref, wg_ref, x_ref, g_ref, b_ref, o_ref, *, alpha):
    gl = jax.nn.gelu(y_ref[...]).astype(BF16)
    mix = _dot(gl, wo_ref[...]) * jax.nn.sigmoid(_dot(gl, wg_ref[...]))
    o_ref[...] = _layer_norm(alpha * x_ref[...] + mix, g_ref[...], b_ref[...])


def s5_out_ln(y, w_out, w_gate, x, g, b, *, layer, alpha, tm):
    t, d = x.shape
    row = pl.BlockSpec((tm, d), lambda i: (i, 0))
    const = lambda r, c: pl.BlockSpec((r, c), lambda i: (0, 0))
    return pl.pallas_call(
        functools.partial(_s5_out_ln_kernel, alpha=alpha),
        grid=(t // tm,),
        in_specs=[row, _layer_weight(layer, d, d), _layer_weight(layer, d, d), row, const(1, d), const(1, d)],
        out_specs=row,
        out_shape=jax.ShapeDtypeStruct((t, d), F32),
        compiler_params=_params("parallel"),
    )(y, w_out, w_gate, x, g, b)


def _rope_tables(pos, n_heads):
    half = HEAD_DIM // 2
    inv = ROPE_THETA ** (-jnp.arange(half, dtype=F32) / half)
    ang = pos.astype(F32)[:, None] * inv[None, :]
    cos = jnp.cos(ang)
    sin = jnp.sin(ang)
    cos_h = jnp.concatenate([cos, cos], axis=-1)
    sin_h = jnp.concatenate([-sin, sin], axis=-1)
    return jnp.tile(cos_h, (1, n_heads)), jnp.tile(sin_h, (1, n_heads))


def kernel(x_prompt, x_sample, cache_k, cache_v, page_table, state_ret, state_s5_re, state_s5_im, ffn1_w_gate, ffn1_w_up, ffn1_w_down, ffn2_w_gate, ffn2_w_up, ffn2_w_down, ln_g, ln_b, w_in_ab, w_out_ab, s5_lam_re, s5_lam_im, s5_b_re, s5_b_im, s5_c_re, s5_c_im, s5_d, s5_log_dt, s5_w_out, s5_w_gate):
    bp, lp, d = x_prompt.shape
    bs, ls, _ = x_sample.shape
    depth = ffn1_w_gate.shape[0]
    n_layers_ab, n_pool, page, a_heads, hd = cache_k.shape
    n_pages = page_table.shape[1]
    past_len = n_pages * page
    half = a_heads * hd
    n_heads = half // HEAD_DIM
    assert hd == HEAD_DIM and lp % MOBA_BLOCK == 0 and past_len % MOBA_BLOCK == 0 and ls <= MOBA_BLOCK
    assert MOBA_BLOCK % page == 0 and lp % page == 0 and lp % RET_CHUNK == 0
    alpha = (2 * depth) ** 0.25
    tp, ts = bp * lp, bs * ls
    tm_p = min(1024, lp)
    tm_proj = min(512, lp)

    xp = x_prompt.reshape(tp, d)
    xs = x_sample.reshape(ts, d)
    bf = lambda a: a.astype(BF16)
    f1 = (bf(ffn1_w_gate), bf(ffn1_w_up), bf(ffn1_w_down))
    f2 = (bf(ffn2_w_gate), bf(ffn2_w_up), bf(ffn2_w_down))
    w_in, w_out = bf(w_in_ab), bf(w_out_ab)
    w_o, w_g = bf(s5_w_out), bf(s5_w_gate)
    cos_p, sin_p = _rope_tables(jnp.arange(lp, dtype=jnp.int32), n_heads)
    cos_s, sin_s = _rope_tables(past_len + jnp.arange(ls, dtype=jnp.int32), n_heads)
    cos_s, sin_s = jnp.tile(cos_s, (bs, 1)), jnp.tile(sin_s, (bs, 1))
    pool_kt = jnp.transpose(cache_k, (0, 1, 3, 4, 2)).reshape(n_layers_ab * n_pool, half, page)
    pool_vt = jnp.transpose(cache_v, (0, 1, 3, 4, 2)).reshape(n_layers_ab * n_pool, half, page)
    seq_per_tile = max(RET_CHUNK // ls, 1)
    ret_tiles_p = min(4, lp // RET_CHUNK)

    k_p, v_p, k_s, v_s, r_p, r_s = [], [], [], [], [], []
    sre_p, sim_p, sre_s, sim_s = [], [], [], []
    for layer in range(depth):
        li = layer // 2
        g = lambda i: ln_g[layer, i][None, :]
        b = lambda i: ln_b[layer, i][None, :]
        xp = ffn_ln(xp, *f1, g(0), b(0), layer=layer, alpha=alpha, tm=tm_p)
        xs = ffn_ln(xs, *f1, g(0), b(0), layer=layer, alpha=alpha, tm=ts)
        if layer % 2 == 0:
            qa, ka, va, kbf, vt, ksum, qb, kb, vb, gb = ab_project(xp, w_in, cos_p, sin_p, layer=li, tm=tm_proj,
                                                                   seq_len=lp)
            attn = moba_prompt(qa, ksum, kbf, vt, batch=bp, seq_len=lp)
            zero_state = jnp.zeros((bp, n_heads // 2, HEAD_PAIR, HEAD_PAIR), F32)
            ret, s_fin = retention_gated(qb, kb, vb, gb, zero_state, n_seq=1, chunk=RET_CHUNK, tiles=ret_tiles_p,
                                         steps_per_seq=lp // (RET_CHUNK * ret_tiles_p))
            xp = merge_ln(attn, ret, w_out, xp, g(1), b(1), layer=li, alpha=alpha, tm=tm_proj)
            k_p.append(ka.reshape(bp, lp // page, page, a_heads, hd))
            v_p.append(va.reshape(bp, lp // page, page, a_heads, hd))
            r_p.append(_block_diag_to_states(s_fin))
            qa, ka, va, _, _, _, qb, kb, vb, gb = ab_project(xs, w_in, cos_s, sin_s, layer=li, tm=ts, seq_len=ts)
            page_ids = (page_table.astype(jnp.int32) + li * n_pool).reshape(-1)
            attn = moba_sample(qa, ka, va, pool_kt, pool_vt, page_ids, batch=bs, n_pages=n_pages)
            ret, s_fin = retention_gated(qb, kb, vb, gb, _states_to_block_diag(state_ret[li]), n_seq=seq_per_tile,
                                         chunk=ls, tiles=1, steps_per_seq=1)
            xs = merge_ln(attn, ret, w_out, xs, g(1), b(1), layer=li, alpha=alpha, tm=ts)
            k_s.append(ka.reshape(bs, ls, a_heads, hd))
            v_s.append(va.reshape(bs, ls, a_heads, hd))
            r_s.append(_block_diag_to_states(s_fin))
        else:
            weights = _s5_weights(s5_lam_re[li], s5_lam_im[li], s5_b_re[li], s5_b_im[li], s5_c_re[li],
                                  s5_c_im[li], s5_log_dt[li])
            n_state = weights[2].shape[1]
            d_skip = s5_d[li][None, :]
            zero = jnp.zeros((bp, n_state), F32)
            y, a_re, a_im = s5_scan(xp.reshape(bp, lp, d), zero, zero, weights, d_skip, tt=min(32, lp))
            xp = s5_out_ln(y.reshape(tp, d), w_o, w_g, xp, g(1), b(1), layer=li, alpha=alpha, tm=tm_proj)
            sre_p.append(a_re.reshape(bp, -1, S5_STATE))
            sim_p.append(a_im.reshape(bp, -1, S5_STATE))
            y, a_re, a_im = s5_scan(xs.reshape(bs, ls, d), state_s5_re[li].reshape(bs, n_state),
                                    state_s5_im[li].reshape(bs, n_state), weights, d_skip, tt=ls)
            xs = s5_out_ln(y.reshape(ts, d), w_o, w_g, xs, g(1), b(1), layer=li, alpha=alpha, tm=ts)
            sre_s.append(a_re.reshape(bs, -1, S5_STATE))
            sim_s.append(a_im.reshape(bs, -1, S5_STATE))
        xp = ffn_ln(xp, *f2, g(2), b(2), layer=layer, alpha=alpha, tm=tm_p)
        xs = ffn_ln(xs, *f2, g(2), b(2), layer=layer, alpha=alpha, tm=ts)
    return (xp.reshape(bp, lp, d), xs.reshape(bs, ls, d), jnp.stack(k_p), jnp.stack(v_p), jnp.stack(k_s),
            jnp.stack(v_s), jnp.stack(r_p), jnp.stack(r_s), jnp.stack(sre_p), jnp.stack(sim_p),
            jnp.stack(sre_s), jnp.stack(sim_s))
```

```python
import functools
import math

import jax
import jax.numpy as jnp
from jax import lax
from jax.experimental import pallas as pl
from jax.experimental.pallas import tpu as pltpu

F32 = jnp.float32
BF16 = jnp.bfloat16

HEAD_DIM = 64
HEAD_PAIR = 2 * HEAD_DIM
MOBA_BLOCK = 256
MOBA_TOPK = 3
RET_CHUNK = 128
S5_GROUP = 16
S5_STATE = 64
ROPE_THETA = 10000.0
LN_EPS = 1e-5
GN_EPS = 1e-6
NEG_INF = -1e30
LOG2_E = math.log2(math.e)
SUBLANES = 8
LANES = 128
VMEM_LIMIT = 48 * 1024 * 1024


def _dot(a, b, precision=None):
    return jnp.dot(a, b, preferred_element_type=F32, precision=precision)


def _dot_nt(a, b, precision=None):
    return lax.dot_general(a, b, (((1,), (1,)), ((), ())), preferred_element_type=F32, precision=precision)


def _dot_tn(a, b):
    return lax.dot_general(a, b, (((0,), (0,)), ((), ())), preferred_element_type=F32)


def _layer_norm(r, g, b):
    mu = jnp.mean(r, -1, keepdims=True)
    d = r - mu
    var = jnp.mean(d * d, -1, keepdims=True)
    return d * lax.rsqrt(var + LN_EPS) * g + b


def _silu(x):
    return x * jax.nn.sigmoid(x)


def _params(*sem):
    return pltpu.CompilerParams(dimension_semantics=sem, vmem_limit_bytes=VMEM_LIMIT)


def _ffn_ln_kernel(x_ref, wg_ref, wu_ref, wd_ref, g_ref, b_ref, o_ref, a_ref, *, alpha, tf):
    xb = x_ref[...].astype(BF16)
    for c in range(wg_ref.shape[1] // tf):
        cols = slice(c * tf, (c + 1) * tf)
        hg = _dot(xb, wg_ref[:, cols])
        hu = _dot(xb, wu_ref[:, cols])
        a_ref[:, cols] = (_silu(hg) * hu).astype(BF16)
    r = alpha * x_ref[...] + 0.5 * _dot(a_ref[...], wd_ref[...])
    o_ref[...] = _layer_norm(r, g_ref[...], b_ref[...])


def _layer_weight(layer, r, c, row_blk=0, **kw):
    return pl.BlockSpec((None, r, c), lambda *_: (layer, row_blk, 0), **kw)


def ffn_ln(x, wg, wu, wd, g, b, *, layer, alpha, tm, tf=256):
    t, d = x.shape
    f = wg.shape[2]
    resident = lambda r, c: _layer_weight(layer, r, c, pipeline_mode=pl.Buffered(1))
    return pl.pallas_call(
        functools.partial(_ffn_ln_kernel, alpha=alpha, tf=tf),
        grid=(t // tm,),
        in_specs=[pl.BlockSpec((tm, d), lambda i: (i, 0)), resident(d, f), resident(d, f), resident(f, d),
                  pl.BlockSpec((1, d), lambda i: (0, 0)), pl.BlockSpec((1, d), lambda i: (0, 0))],
        out_specs=pl.BlockSpec((tm, d), lambda i: (i, 0)),
        out_shape=jax.ShapeDtypeStruct((t, d), F32),
        scratch_shapes=[pltpu.VMEM((tm, f), BF16)],
        compiler_params=_params("parallel"),
    )(x, wg, wu, wd, g, b)


def _rope(y, cos, sin_signed):
    width = y.shape[-1]
    lane = lax.broadcasted_iota(jnp.int32, (1, width), 1)
    first = (lane % HEAD_DIM) < (HEAD_DIM // 2)
    rot = jnp.where(first, pltpu.roll(y, width - HEAD_DIM // 2, 1), pltpu.roll(y, HEAD_DIM // 2, 1))
    return y * cos + rot * sin_signed


def _ab_proj_kernel(*refs, half, scale, page):
    if page is None:
        x_ref, w_ref, cos_ref, sin_ref, qa_ref, ka_ref, va_ref, qb_ref, kb_ref, vb_ref, gb_ref = refs
    else:
        (x_ref, w_ref, cos_ref, sin_ref, _, _, qa_ref, kbf_ref, vt_ref, ksum_ref, kpg_ref, vpg_ref,
         qb_ref, kb_ref, vb_ref, gb_ref) = refs
    xb = x_ref[...].astype(BF16)
    cos = cos_ref[...]
    sin = sin_ref[...]
    col = lambda c: _dot(xb, w_ref[:, c * half:(c + 1) * half])
    qa_ref[...] = _rope(col(0), cos, sin) * scale
    ka = _rope(col(1), cos, sin)
    va = col(2)
    if page is None:
        ka_ref[...] = ka
        va_ref[...] = va
    else:
        kbf_ref[...] = ka.astype(BF16)
        for n in range(ka.shape[0] // MOBA_BLOCK):
            blk = slice(n * MOBA_BLOCK, (n + 1) * MOBA_BLOCK)
            ksum_ref[n] = jnp.sum(ka[blk], axis=0, keepdims=True)
            vt_ref[n] = va[blk].T.astype(BF16)
        for n in range(ka.shape[0] // page):
            rows = slice(n * page, (n + 1) * page)
            kpg_ref[n] = ka[rows].T
            vpg_ref[n] = va[rows].T
    qb_ref[...] = _rope(col(3), cos, sin)
    kb_ref[...] = _rope(col(4), cos, sin) * scale
    vb_ref[...] = col(5)
    gb_ref[...] = col(6)


def ab_project(x, w_in, cos, sin, *, layer, tm, seq_len, pages=None, page=None):
    t, d = x.shape
    half = w_in.shape[2] // 7
    tiles_per_seq = max(seq_len // tm, 1)
    tab = pl.BlockSpec((tm, half), lambda i: (i % tiles_per_seq, 0))
    row = pl.BlockSpec((tm, half), lambda i: (i, 0))
    f32o = jax.ShapeDtypeStruct((t, half), F32)
    in_specs = [pl.BlockSpec((tm, d), lambda i: (i, 0)), _layer_weight(layer, d, 7 * half), tab, tab]
    operands = [x, w_in, cos, sin]
    aliases = {}
    if pages is None:
        out_specs = [row] * 7
        out_shape = [f32o] * 7
    else:
        nblk = tm // MOBA_BLOCK
        page_blk = pl.BlockSpec((None, tm // page, half, page), lambda i: (layer, i, 0, 0))
        in_specs += [pl.BlockSpec(memory_space=pl.ANY)] * 2
        operands += list(pages)
        aliases = {4: 4, 5: 5}
        out_specs = [row, row,
                     pl.BlockSpec((nblk, half, MOBA_BLOCK), lambda i: (i, 0, 0)),
                     pl.BlockSpec((nblk, 1, half), lambda i: (i, 0, 0)),
                     page_blk, page_blk, row, row, row, row]
        out_shape = [f32o, jax.ShapeDtypeStruct((t, half), BF16),
                     jax.ShapeDtypeStruct((t // MOBA_BLOCK, half, MOBA_BLOCK), BF16),
                     jax.ShapeDtypeStruct((t // MOBA_BLOCK, 1, half), F32),
                     jax.ShapeDtypeStruct(pages[0].shape, F32), jax.ShapeDtypeStruct(pages[1].shape, F32),
                     f32o, f32o, f32o, f32o]
    return pl.pallas_call(
        functools.partial(_ab_proj_kernel, half=half, scale=HEAD_DIM ** -0.5, page=page),
        grid=(t // tm,),
        in_specs=in_specs,
        out_specs=out_specs,
        out_shape=out_shape,
        input_output_aliases=aliases,
        compiler_params=_params("parallel"),
    )(*operands)


def _topk_select(gate, n_valid):
    nb = gate.shape[1]
    blk = lax.broadcasted_iota(jnp.int32, (1, nb), 1)
    gate = jnp.where(blk < n_valid, gate, NEG_INF)
    rank = jnp.zeros(gate.shape, jnp.int32)
    for m in range(nb):
        gm = gate[:, m:m + 1]
        ahead = (gm > gate) | ((gm == gate) & (m < blk))
        rank = rank + ahead.astype(jnp.int32)
    return ((rank < MOBA_TOPK) & (blk < n_valid)).astype(F32)


def _topk_select_t(gate_t, n_valid):
    nb = gate_t.shape[0]
    blk = lax.broadcasted_iota(jnp.int32, (nb, 1), 0)
    gate_t = jnp.where(blk < n_valid, gate_t, NEG_INF)
    rank = jnp.zeros(gate_t.shape, jnp.int32)
    for m in range(nb):
        gm = gate_t[m:m + 1, :]
        ahead = (gm > gate_t) | ((gm == gate_t) & (m < blk))
        rank = rank + ahead.astype(jnp.int32)
    return ((rank < MOBA_TOPK) & (blk < n_valid)).astype(F32)


def _moba_prompt_kernel(q_ref, ksum_ref, k_ref, vt_ref, o_ref, qm_ref, sel_ref, m_ref, l_ref, a_ref, acc_ref,
                        s_ref, p_ref):
    j = pl.program_id(1)
    tq = q_ref.shape[0]
    n_heads = q_ref.shape[1] // HEAD_DIM
    lane = lax.broadcasted_iota(jnp.int32, (1, HEAD_PAIR), 1)
    key_i = lax.broadcasted_iota(jnp.int32, (MOBA_BLOCK, tq), 0)
    qry_i = lax.broadcasted_iota(jnp.int32, (MOBA_BLOCK, tq), 1)
    causal = key_i <= qry_i

    for h in range(n_heads):
        lanes = slice((h // 2) * HEAD_PAIR, (h // 2 + 1) * HEAD_PAIR)
        hm = (lane // HEAD_DIM) == (h % 2)
        qp = q_ref[:, lanes]
        kmean = jnp.where(hm, ksum_ref[:, 0, lanes] * (1.0 / MOBA_BLOCK), 0.0)
        sel_ref[h] = _topk_select_t(_dot_nt(kmean, qp, precision=lax.Precision.HIGHEST), j)
        qm_ref[h] = jnp.where(hm, qp * LOG2_E, 0.0).astype(BF16)

    ones_rows = jnp.ones((HEAD_DIM, MOBA_BLOCK), BF16)

    def update(blocks, first):
        def rows0(n):
            return pl.multiple_of((j if n is None else n) * MOBA_BLOCK, MOBA_BLOCK)

        def picked(n, h):
            return sel_ref[h, pl.ds(n, 1), :] > 0.0

        def scores(slot, n, h):
            k_pair = k_ref[pl.ds(rows0(n), MOBA_BLOCK), (h // 2) * HEAD_PAIR:(h // 2 + 1) * HEAD_PAIR]
            s_ref[slot, h] = _dot_nt(k_pair, qm_ref[h])

        def softmax(slot, n, h):
            if first:
                s = jnp.where(causal, s_ref[slot, h], NEG_INF)
                m_new = jnp.max(s, axis=0, keepdims=True)
            else:
                s = s_ref[slot, h]
                cmax = jnp.where(picked(n, h), jnp.max(s, axis=0, keepdims=True), NEG_INF)
                m_new = jnp.maximum(m_ref[h], cmax)
                a_ref[slot, h] = jnp.exp2(m_ref[h] - m_new)
            p_ref[slot, h] = jnp.exp2(s - m_new).astype(BF16)
            m_ref[h] = m_new

        def values(slot, n, h):
            v_h = vt_ref[j if n is None else n, h * HEAD_DIM:(h + 1) * HEAD_DIM, :]
            res = _dot(jnp.concatenate([v_h, ones_rows], axis=0), p_ref[slot, h])
            pv, psum = res[:HEAD_DIM], res[HEAD_DIM:HEAD_DIM + 1]
            if first:
                acc_ref[h] = pv
                l_ref[h] = psum
            else:
                acc_ref[h] = a_ref[slot, h] * acc_ref[h] + jnp.where(picked(n, h), pv, 0.0)
                l_ref[h] = a_ref[slot, h] * l_ref[h] + jnp.where(picked(n, h), psum, 0.0)

        heads = range(n_heads)
        stages = (scores, softmax, values)
        for t in range(len(stages) + len(blocks) - 1):
            for h in heads:
                for slot, n in enumerate(blocks):
                    if 0 <= t - slot < len(stages):
                        stages[t - slot](slot, n, h)

    update([None], True)

    def body(i, carry):
        update([2 * i, 2 * i + 1], False)
        return carry

    lax.fori_loop(0, j // 2, body, 0)

    @pl.when(j % 2 == 1)
    def _():
        update([j - 1], False)
    out_t = jnp.concatenate([acc_ref[h] / l_ref[h] for h in range(n_heads)], axis=0)
    o_ref[...] = out_t.T


def moba_prompt(q, ksum, kbf, vt, *, batch, seq_len):
    t, w = q.shape
    nb = seq_len // MOBA_BLOCK
    n_heads = w // HEAD_DIM
    return pl.pallas_call(
        _moba_prompt_kernel,
        grid=(batch, nb),
        in_specs=[pl.BlockSpec((MOBA_BLOCK, w), lambda b, j: (b * nb + j, 0)),
                  pl.BlockSpec((nb, 1, w), lambda b, j: (b, 0, 0)),
                  pl.BlockSpec((seq_len, w), lambda b, j: (b, 0)),
                  pl.BlockSpec((nb, w, MOBA_BLOCK), lambda b, j: (b, 0, 0))],
        out_specs=pl.BlockSpec((MOBA_BLOCK, w), lambda b, j: (b * nb + j, 0)),
        out_shape=jax.ShapeDtypeStruct((t, w), F32),
        scratch_shapes=[pltpu.VMEM((n_heads, MOBA_BLOCK, HEAD_PAIR), BF16),
                        pltpu.VMEM((n_heads, nb, MOBA_BLOCK), F32),
                        pltpu.VMEM((n_heads, 1, MOBA_BLOCK), F32),
                        pltpu.VMEM((n_heads, 1, MOBA_BLOCK), F32),
                        pltpu.VMEM((2, n_heads, 1, MOBA_BLOCK), F32),
                        pltpu.VMEM((n_heads, HEAD_DIM, MOBA_BLOCK), F32),
                        pltpu.VMEM((2, n_heads, MOBA_BLOCK, MOBA_BLOCK), F32),
                        pltpu.VMEM((2, n_heads, MOBA_BLOCK, MOBA_BLOCK), BF16)],
        compiler_params=_params("parallel", "arbitrary"),
    )(q, ksum, kbf, vt)


def _moba_sample_kernel(pt_ref, q_ref, knew_ref, vnew_ref, *rest, pages_per_step, page, n_heads):
    del pt_ref
    kpages = rest[:pages_per_step]
    vpages = rest[pages_per_step:2 * pages_per_step]
    o_ref = rest[2 * pages_per_step]
    qf_ref, qb_ref, ksum_ref, m_ref, l_ref, acc_ref, kpad_ref, vpad_ref = rest[2 * pages_per_step + 1:]
    s_idx = pl.program_id(1)
    n_steps = pl.num_programs(1)
    lq, w = q_ref.shape
    rows = n_heads * lq
    nb = acc_ref.shape[0]
    pages_per_blk = MOBA_BLOCK // page
    blks_per_step = pages_per_step // pages_per_blk
    rowhead = lax.broadcasted_iota(jnp.int32, (rows, 1), 0) // lq
    lanehead = lax.broadcasted_iota(jnp.int32, (1, w), 1) // HEAD_DIM
    blk_lane = lax.broadcasted_iota(jnp.int32, (1, LANES), 1)

    @pl.when(s_idx == 0)
    def _():
        qt = jnp.concatenate([q_ref[...]] * n_heads, axis=0)
        qbd = jnp.where(rowhead == lanehead, qt, 0.0)
        qf_ref[...] = qbd
        qb_ref[...] = qbd.astype(BF16)
        ksum_ref[...] = jnp.zeros_like(ksum_ref)
        m_ref[...] = jnp.zeros_like(m_ref)
        l_ref[...] = jnp.zeros_like(l_ref)

    qb = qb_ref[...]

    def partial_softmax(s):
        m = jnp.max(s, axis=1, keepdims=True)
        e = jnp.exp(s - m)
        return m, jnp.sum(e, axis=1, keepdims=True), e.astype(BF16)

    blks = range(blks_per_step)
    pages_of = lambda bi: range(bi * pages_per_blk, (bi + 1) * pages_per_blk)
    here = [blk_lane == s_idx * blks_per_step + bi for bi in blks]
    kt = [jnp.concatenate([kpages[i][0] for i in pages_of(bi)], axis=1) for bi in blks]
    scores = [_dot(qb, kt[bi].astype(BF16)) for bi in blks]
    ksum = ksum_ref[...]
    for bi in blks:
        ksum = jnp.where(here[bi], jnp.sum(kt[bi], axis=1, keepdims=True), ksum)
    ksum_ref[...] = ksum
    stats = [partial_softmax(scores[bi]) for bi in blks]
    for bi in blks:
        vt = jnp.concatenate([vpages[i][0] for i in pages_of(bi)], axis=1)
        acc_ref[s_idx * blks_per_step + bi] = _dot_nt(stats[bi][2], vt.astype(BF16))
    m_all, l_all = m_ref[...], l_ref[...]
    for bi in blks:
        m_all = jnp.where(here[bi], stats[bi][0], m_all)
        l_all = jnp.where(here[bi], stats[bi][1], l_all)
    m_ref[...] = m_all
    l_ref[...] = l_all

    @pl.when(s_idx == n_steps - 1)
    def _():
        kpad_ref[...] = jnp.zeros_like(kpad_ref)
        vpad_ref[...] = jnp.zeros_like(vpad_ref)
        kpad_ref[0:lq, :] = knew_ref[...]
        vpad_ref[0:lq, :] = vnew_ref[...]
        s_own = _dot_nt(qb, kpad_ref[...].astype(BF16))
        tq = lax.broadcasted_iota(jnp.int32, s_own.shape, 0) % lq
        tk = lax.broadcasted_iota(jnp.int32, s_own.shape, 1)
        m_own, l_own, e_own = partial_softmax(jnp.where(tk <= tq, s_own, NEG_INF))
        acc_own = _dot(e_own, vpad_ref[...].astype(BF16))

        gate = _dot(qf_ref[...], ksum_ref[...] * (1.0 / MOBA_BLOCK), precision=lax.Precision.HIGHEST)
        sel = _topk_select(gate[:, :nb], nb) > 0.0
        m_blk = m_ref[:, :nb]
        m_all = jnp.maximum(m_own, jnp.max(jnp.where(sel, m_blk, NEG_INF), axis=1, keepdims=True))
        wgt = jnp.where(sel, jnp.exp(m_blk - m_all), 0.0)
        w_own = jnp.exp(m_own - m_all)
        den = w_own * l_own + jnp.sum(wgt * l_ref[:, :nb], axis=1, keepdims=True)
        num = w_own * acc_own
        for n in range(nb):
            num = num + wgt[:, n:n + 1] * acc_ref[n]
        o_all = num / den
        out = jnp.zeros((lq, w), F32)
        for h in range(n_heads):
            out = out + jnp.where(lanehead == h, o_all[h * lq:(h + 1) * lq], 0.0)
        o_ref[...] = out


def moba_sample(q, knew, vnew, pool_kt, pool_vt, page_ids, *, batch, n_pages, pages_per_step=16):
    t, w = q.shape
    lq = t // batch
    page = pool_kt.shape[2]
    n_heads = w // HEAD_DIM
    rows = n_heads * lq
    nb = n_pages * page // MOBA_BLOCK
    assert nb <= LANES and pages_per_step % (MOBA_BLOCK // page) == 0
    seq_blk = pl.BlockSpec((lq, w), lambda b, s, pt: (b, 0))

    def page_spec(i):
        return pl.BlockSpec((1, w, page), lambda b, s, pt: (pt[b * n_pages + s * pages_per_step + i], 0, 0))

    grid_spec = pltpu.PrefetchScalarGridSpec(
        num_scalar_prefetch=1,
        grid=(batch, n_pages // pages_per_step),
        in_specs=[seq_blk, seq_blk, seq_blk] + [page_spec(i) for i in range(pages_per_step)] * 2,
        out_specs=seq_blk,
        scratch_shapes=[pltpu.VMEM((rows, w), F32), pltpu.VMEM((rows, w), BF16), pltpu.VMEM((w, LANES), F32),
                        pltpu.VMEM((rows, LANES), F32), pltpu.VMEM((rows, LANES), F32),
                        pltpu.VMEM((nb, rows, w), F32),
                        pltpu.VMEM((LANES, w), F32), pltpu.VMEM((LANES, w), F32)],
    )
    return pl.pallas_call(
        functools.partial(_moba_sample_kernel, pages_per_step=pages_per_step, page=page, n_heads=n_heads),
        grid_spec=grid_spec,
        out_shape=jax.ShapeDtypeStruct((t, w), F32),
        compiler_params=_params("parallel", "arbitrary"),
    )(page_ids, q, knew, vnew, *([pool_kt] * pages_per_step), *([pool_vt] * pages_per_step))


def _retention_kernel(q_ref, k_ref, v_ref, g_ref, s0_ref, dmask_ref, qdec_ref, kdec_ref, cdec_ref,
                      o_ref, sout_ref, st_ref, *, n_seq, chunk, tiles, steps_per_seq):
    step = pl.program_id(0)
    tile_rows = n_seq * chunk
    n_pairs = q_ref.shape[1] // HEAD_PAIR
    lane = lax.broadcasted_iota(jnp.int32, (1, HEAD_PAIR), 1)
    rowh = lax.broadcasted_iota(jnp.int32, (HEAD_PAIR, 1), 0) // HEAD_DIM
    same_head = rowh == (lane // HEAD_DIM)
    rseq = lax.broadcasted_iota(jnp.int32, (tile_rows, 1), 0) // chunk

    @pl.when(step % steps_per_seq == 0)
    def _():
        st_ref[...] = s0_ref[...]

    pairs = range(n_pairs)
    lanes_of = lambda hp: slice(hp * HEAD_PAIR, (hp + 1) * HEAD_PAIR)
    half_masks = [(lane // HEAD_DIM) == half for half in range(2)]
    for ti in range(tiles):
        rows = slice(ti * tile_rows, (ti + 1) * tile_rows)
        qb = [q_ref[rows, lanes_of(hp)].astype(BF16) for hp in pairs]
        kf = [k_ref[rows, lanes_of(hp)] for hp in pairs]
        kb = [k.astype(BF16) for k in kf]
        vb = [v_ref[rows, lanes_of(hp)].astype(BF16) for hp in pairs]
        inner = [[_dot_nt(jnp.where(hm, qb[hp], jnp.zeros_like(qb[hp])), kb[hp]) for hm in half_masks]
                 for hp in pairs]
        scaled = [[(inner[hp][half] * dmask_ref[2 * hp + half]).astype(BF16) for half in range(2)] for hp in pairs]
        ret = [sum(jnp.where(half_masks[half], _dot(scaled[hp][half], vb[hp]), 0.0) for half in range(2))
               for hp in pairs]
        for hp in pairs:
            kd = (kf[hp] * kdec_ref[:, lanes_of(hp)]).astype(BF16)
            cdec = cdec_ref[:, lanes_of(hp)]
            carried = []
            for s in range(n_seq):
                srows = slice(s * chunk, (s + 1) * chunk)
                state = st_ref[s, hp]
                carried.append(_dot(qb[hp][srows], state.astype(BF16)))
                kd_s = kd if n_seq == 1 else jnp.where(rseq == s, kd, jnp.zeros_like(kd))
                st_ref[s, hp] = cdec * state + jnp.where(same_head, _dot_tn(kd_s, vb[hp]), 0.0)
            carried = carried[0] if n_seq == 1 else jnp.concatenate(carried, axis=0)
            ret[hp] = ret[hp] + carried * qdec_ref[:, lanes_of(hp)]
        for hp in pairs:
            normed = jnp.zeros_like(ret[hp])
            for hm in half_masks:
                mu = jnp.sum(jnp.where(hm, ret[hp], 0.0), axis=1, keepdims=True) * (1.0 / HEAD_DIM)
                d = jnp.where(hm, ret[hp] - mu, 0.0)
                var = jnp.sum(d * d, axis=1, keepdims=True) * (1.0 / HEAD_DIM)
                normed = normed + d * lax.rsqrt(var + GN_EPS)
            o_ref[rows, lanes_of(hp)] = _silu(g_ref[rows, lanes_of(hp)]) * normed

    @pl.when(step % steps_per_seq == steps_per_seq - 1)
    def _():
        sout_ref[...] = st_ref[...]


def _retention_tables(n_heads, n_seq, chunk):
    log_g = jnp.log1p(-jnp.exp2(-5.0 - jnp.arange(n_heads, dtype=F32)))
    idx = jnp.arange(chunk, dtype=F32)
    diff = idx[:, None] - idx[None, :]
    dmask = jnp.where(diff >= 0, jnp.exp(log_g[:, None, None] * jnp.maximum(diff, 0.0)), 0.0)
    q_dec = jnp.exp(log_g[None, :] * (idx[:, None] + 1.0))
    k_dec = jnp.exp(log_g[None, :] * (chunk - 1.0 - idx[:, None]))
    c_dec = jnp.exp(log_g * chunk)
    seq_eye = jnp.eye(n_seq, dtype=F32)
    dmask = jnp.einsum("ab,hij->haibj", seq_eye, dmask).reshape(n_heads, n_seq * chunk, n_seq * chunk)
    lanes = lambda t: jnp.repeat(t, HEAD_DIM, axis=-1)
    return dmask, jnp.tile(lanes(q_dec), (n_seq, 1)), jnp.tile(lanes(k_dec), (n_seq, 1)), lanes(c_dec[None, :])


def retention_gated(q, k, v, g, s0_bd, *, n_seq, chunk, tiles, steps_per_seq):
    t, w = q.shape
    n_pairs = w // HEAD_PAIR
    rows = n_seq * chunk * tiles
    n_steps = t // rows
    dmask, qdec, kdec, cdec = _retention_tables(w // HEAD_DIM, n_seq, chunk)
    tile_rows = n_seq * chunk
    row = pl.BlockSpec((rows, w), lambda i: (i, 0))
    st = pl.BlockSpec((n_seq, n_pairs, HEAD_PAIR, HEAD_PAIR), lambda i: (i // steps_per_seq, 0, 0, 0))
    const = lambda shape: pl.BlockSpec(shape, lambda i: (0,) * len(shape))
    return pl.pallas_call(
        functools.partial(_retention_kernel, n_seq=n_seq, chunk=chunk, tiles=tiles, steps_per_seq=steps_per_seq),
        grid=(n_steps,),
        in_specs=[row, row, row, row, st, const(dmask.shape), const((tile_rows, w)), const((tile_rows, w)),
                  const((1, w))],
        out_specs=[row, st],
        out_shape=[jax.ShapeDtypeStruct((t, w), F32), jax.ShapeDtypeStruct(s0_bd.shape, F32)],
        scratch_shapes=[pltpu.VMEM((n_seq, n_pairs, HEAD_PAIR, HEAD_PAIR), F32)],
        compiler_params=_params("arbitrary"),
    )(q, k, v, g, s0_bd, dmask, qdec, kdec, cdec)


def _states_to_block_diag(s):
    b, h, dk, dv = s.shape
    sp = s.reshape(b, h // 2, 2, dk, dv)
    z = jnp.zeros_like(sp[:, :, 0])
    top = jnp.concatenate([sp[:, :, 0], z], axis=-1)
    bot = jnp.concatenate([z, sp[:, :, 1]], axis=-1)
    return jnp.concatenate([top, bot], axis=-2)


def _block_diag_to_states(s):
    b, p = s.shape[:2]
    a = s[:, :, :HEAD_DIM, :HEAD_DIM]
    c = s[:, :, HEAD_DIM:, HEAD_DIM:]
    return jnp.stack([a, c], axis=2).reshape(b, 2 * p, HEAD_DIM, HEAD_DIM)


def _merge_ln_kernel(a_ref, r_ref, wa_ref, wr_ref, x_ref, g_ref, b_ref, o_ref, *, alpha):
    mix = _dot(a_ref[...].astype(BF16), wa_ref[...]) + _dot(r_ref[...].astype(BF16), wr_ref[...])
    o_ref[...] = _layer_norm(alpha * x_ref[...] + mix, g_ref[...], b_ref[...])


def merge_ln(attn, ret, w_out, x, g, b, *, layer, alpha, tm):
    t, d = x.shape
    w = attn.shape[1]
    row = lambda n: pl.BlockSpec((tm, n), lambda i: (i, 0))
    const = lambda r, c: pl.BlockSpec((r, c), lambda i: (0, 0))
    return pl.pallas_call(
        functools.partial(_merge_ln_kernel, alpha=alpha),
        grid=(t // tm,),
        in_specs=[row(w), row(w), _layer_weight(layer, w, d, 0), _layer_weight(layer, w, d, 1), row(d),
                  const(1, d), const(1, d)],
        out_specs=row(d),
        out_shape=jax.ShapeDtypeStruct((t, d), F32),
        compiler_params=_params("parallel"),
    )(attn, ret, w_out, w_out, x, g, b)


def _s5_scan_kernel(x_ref, s0re_ref, s0im_ref, wbre_ref, wbim_ref, are_ref, aim_ref, wcre_ref, wcim_ref, d_ref,
                    y_ref, sre_ref, sim_ref, xs_ref, bre_ref, bim_ref, stre_ref, stim_ref, *, lane_chunk):
    step = pl.program_id(0)
    nb, tt, d = x_ref.shape
    b8 = stre_ref.shape[0]
    ns = stre_ref.shape[1]
    n_kb = d // LANES
    per_kb = ns // n_kb

    @pl.when(step == 0)
    def _():
        stre_ref[...] = jnp.zeros_like(stre_ref)
        stim_ref[...] = jnp.zeros_like(stim_ref)
        stre_ref[0:nb, :] = s0re_ref[...]
        stim_ref[0:nb, :] = s0im_ref[...]
        xs_ref[...] = jnp.zeros_like(xs_ref)

    for b in range(nb):
        for kb in range(n_kb):
            xs_ref[kb, pl.ds(b, tt, stride=b8), :] = x_ref[b, :, kb * LANES:(kb + 1) * LANES]

    for kb in range(n_kb):
        xk = xs_ref[kb].astype(BF16)
        cols = slice(kb * per_kb, (kb + 1) * per_kb)
        bre_ref[:, cols] = _dot(xk, wbre_ref[kb])
        bim_ref[:, cols] = _dot(xk, wbim_ref[kb])

    for r0 in range(0, b8, SUBLANES):
        for c0 in range(0, ns, lane_chunk):
            cs = slice(c0, c0 + lane_chunk)
            are = jnp.broadcast_to(are_ref[:, cs], (SUBLANES, lane_chunk))
            aim = jnp.broadcast_to(aim_ref[:, cs], (SUBLANES, lane_chunk))

            def body(t, carry, r0=r0, cs=cs, are=are, aim=aim):
                sre, sim = carry
                rows = pl.ds(pl.multiple_of(t * b8 + r0, SUBLANES), SUBLANES)
                nre = are * sre - aim * sim + bre_ref[rows, cs]
                nim = are * sim + aim * sre + bim_ref[rows, cs]
                bre_ref[rows, cs] = nre
                bim_ref[rows, cs] = nim
                return nre, nim

            sre, sim = lax.fori_loop(0, tt, body, (stre_ref[r0:r0 + SUBLANES, cs], stim_ref[r0:r0 + SUBLANES, cs]))
            stre_ref[r0:r0 + SUBLANES, cs] = sre
            stim_ref[r0:r0 + SUBLANES, cs] = sim

    n_out = wcre_ref.shape[0]
    kin = ns // n_out
    wout = d // n_out
    for j in range(n_out):
        sre = bre_ref[:, j * kin:(j + 1) * kin].astype(BF16)
        sim = bim_ref[:, j * kin:(j + 1) * kin].astype(BF16)
        y = _dot(sre, wcre_ref[j]) + _dot(sim, wcim_ref[j])
        for i in range(wout // LANES):
            kb = j * (wout // LANES) + i
            xs_ref[kb] = y[:, i * LANES:(i + 1) * LANES] + xs_ref[kb] * d_ref[:, kb * LANES:(kb + 1) * LANES]

    for b in range(nb):
        for kb in range(n_kb):
            y_ref[b, :, kb * LANES:(kb + 1) * LANES] = xs_ref[kb, pl.ds(b, tt, stride=b8), :]

    @pl.when(step == pl.num_programs(0) - 1)
    def _():
        sre_ref[...] = stre_ref[0:nb, :]
        sim_ref[...] = stim_ref[0:nb, :]


def _s5_weights(lam_re, lam_im, b_re, b_im, c_re, c_im, log_dt):
    g, p, h = b_re.shape
    lam = lax.complex(lam_re.astype(F32), lam_im.astype(F32))
    dt = jnp.exp(log_dt.astype(F32))[:, None]
    lam_bar = jnp.exp(lam * dt)
    b_bar = ((lam_bar - 1.0) / lam)[:, :, None] * lax.complex(b_re.astype(F32), b_im.astype(F32))
    gpk = LANES // h
    n_kb = g // gpk
    eye = jnp.eye(gpk, dtype=F32)

    def in_blocks(m):
        m = m.reshape(n_kb, gpk, p, h)
        return jnp.einsum("kgph,gf->kghfp", m, eye).reshape(n_kb, gpk * h, gpk * p).astype(BF16)

    gpo = 2 * LANES // h
    n_out = g // gpo
    eye_o = jnp.eye(gpo, dtype=F32)

    def out_blocks(m):
        m = m.reshape(n_out, gpo, h, p)
        return jnp.einsum("kghp,gf->kgpfh", m, eye_o).reshape(n_out, gpo * p, gpo * h).astype(BF16)

    return (in_blocks(jnp.real(b_bar)), in_blocks(jnp.imag(b_bar)),
            jnp.real(lam_bar).reshape(1, g * p), jnp.imag(lam_bar).reshape(1, g * p),
            out_blocks(c_re.astype(F32)), out_blocks(-c_im.astype(F32)))


def s5_scan(x, s0_re, s0_im, weights, d_skip, *, tt, lane_chunk=1024):
    nb, seq, d = x.shape
    wbre, wbim, are, aim, wcre, wcim = weights
    ns = are.shape[1]
    b8 = -(-nb // SUBLANES) * SUBLANES
    rows = tt * b8
    const = lambda a: pl.BlockSpec(a.shape, lambda i: (0,) * a.ndim)
    xblk = pl.BlockSpec((nb, tt, d), lambda i: (0, i, 0))
    sblk = pl.BlockSpec((nb, ns), lambda i: (0, 0))
    return pl.pallas_call(
        functools.partial(_s5_scan_kernel, lane_chunk=lane_chunk),
        grid=(seq // tt,),
        in_specs=[xblk, sblk, sblk, const(wbre), const(wbim), const(are), const(aim), const(wcre), const(wcim),
                  pl.BlockSpec((1, d), lambda i: (0, 0))],
        out_specs=[xblk, sblk, sblk],
        out_shape=[jax.ShapeDtypeStruct((nb, seq, d), F32), jax.ShapeDtypeStruct((nb, ns), F32),
                   jax.ShapeDtypeStruct((nb, ns), F32)],
        scratch_shapes=[pltpu.VMEM((d // LANES, rows, LANES), F32), pltpu.VMEM((rows, ns), F32),
                        pltpu.VMEM((rows, ns), F32),
                        pltpu.VMEM((b8, ns), F32), pltpu.VMEM((b8, ns), F32)],
        compiler_params=_params("arbitrary"),
    )(x, s0_re, s0_im, wbre, wbim, are, aim, wcre, wcim, d_skip)


def _s5_out_ln_kernel(y_ref, wo_ref, wg_ref, x_ref, g_ref, b_ref, o_ref, *, alpha):
    gl = jax.nn.gelu(y_ref[...]).astype(BF16)
    mix = _dot(gl, wo_ref[...]) * jax.nn.sigmoid(_dot(gl, wg_ref[...]))
    o_ref[...] = _layer_norm(alpha * x_ref[...] + mix, g_ref[...], b_ref[...])


def s5_out_ln(y, w_out, w_gate, x, g, b, *, layer, alpha, tm):
    t, d = x.shape
    row = pl.BlockSpec((tm, d), lambda i: (i, 0))
    const = lambda r, c: pl.BlockSpec((r, c), lambda i: (0, 0))
    return pl.pallas_call(
        functools.partial(_s5_out_ln_kernel, alpha=alpha),
        grid=(t // tm,),
        in_specs=[row, _layer_weight(layer, d, d), _layer_weight(layer, d, d), row, const(1, d), const(1, d)],
        out_specs=row,
        out_shape=jax.ShapeDtypeStruct((t, d), F32),
        compiler_params=_params("parallel"),
    )(y, w_out, w_gate, x, g, b)


def _rope_tables(pos, n_heads):
    half = HEAD_DIM // 2
    inv = ROPE_THETA ** (-jnp.arange(half, dtype=F32) / half)
    ang = pos.astype(F32)[:, None] * inv[None, :]
    cos = jnp.cos(ang)
    sin = jnp.sin(ang)
    cos_h = jnp.concatenate([cos, cos], axis=-1)
    sin_h = jnp.concatenate([-sin, sin], axis=-1)
    return jnp.tile(cos_h, (1, n_heads)), jnp.tile(sin_h, (1, n_heads))


def kernel(x_prompt, x_sample, cache_k, cache_v, page_table, state_ret, state_s5_re, state_s5_im, ffn1_w_gate, ffn1_w_up, ffn1_w_down, ffn2_w_gate, ffn2_w_up, ffn2_w_down, ln_g, ln_b, w_in_ab, w_out_ab, s5_lam_re, s5_lam_im, s5_b_re, s5_b_im, s5_c_re, s5_c_im, s5_d, s5_log_dt, s5_w_out, s5_w_gate):
    bp, lp, d = x_prompt.shape
    bs, ls, _ = x_sample.shape
    depth = ffn1_w_gate.shape[0]
    n_layers_ab, n_pool, page, a_heads, hd = cache_k.shape
    n_pages = page_table.shape[1]
    past_len = n_pages * page
    half = a_heads * hd
    n_heads = half // HEAD_DIM
    assert hd == HEAD_DIM and lp % MOBA_BLOCK == 0 and past_len % MOBA_BLOCK == 0 and ls <= MOBA_BLOCK
    assert MOBA_BLOCK % page == 0 and lp % page == 0 and lp % RET_CHUNK == 0
    alpha = (2 * depth) ** 0.25
    tp, ts = bp * lp, bs * ls
    tm_p = min(1024, lp)
    tm_proj = min(512, lp)

    xp = x_prompt.reshape(tp, d)
    xs = x_sample.reshape(ts, d)
    bf = lambda a: a.astype(BF16)
    f1 = (bf(ffn1_w_gate), bf(ffn1_w_up), bf(ffn1_w_down))
    f2 = (bf(ffn2_w_gate), bf(ffn2_w_up), bf(ffn2_w_down))
    w_in, w_out = bf(w_in_ab), bf(w_out_ab)
    w_o, w_g = bf(s5_w_out), bf(s5_w_gate)
    cos_p, sin_p = _rope_tables(jnp.arange(lp, dtype=jnp.int32), n_heads)
    cos_s, sin_s = _rope_tables(past_len + jnp.arange(ls, dtype=jnp.int32), n_heads)
    cos_s, sin_s = jnp.tile(cos_s, (bs, 1)), jnp.tile(sin_s, (bs, 1))
    pool_kt = jnp.transpose(cache_k, (0, 1, 3, 4, 2)).reshape(n_layers_ab * n_pool, half, page)
    pool_vt = jnp.transpose(cache_v, (0, 1, 3, 4, 2)).reshape(n_layers_ab * n_pool, half, page)
    seq_per_tile = max(RET_CHUNK // ls, 1)
    ret_tiles_p = min(4, lp // RET_CHUNK)

    kpg = jnp.zeros((n_layers_ab, tp // page, half, page), F32)
    vpg = jnp.zeros((n_layers_ab, tp // page, half, page), F32)
    k_s, v_s, r_p, r_s = [], [], [], []
    sre_p, sim_p, sre_s, sim_s = [], [], [], []
    for layer in range(depth):
        li = layer // 2
        g = lambda i: ln_g[layer, i][None, :]
        b = lambda i: ln_b[layer, i][None, :]
        xp = ffn_ln(xp, *f1, g(0), b(0), layer=layer, alpha=alpha, tm=tm_p)
        xs = ffn_ln(xs, *f1, g(0), b(0), layer=layer, alpha=alpha, tm=ts)
        if layer % 2 == 0:
            qa, kbf, vt, ksum, kpg, vpg, qb, kb, vb, gb = ab_project(xp, w_in, cos_p, sin_p, layer=li, tm=tm_proj,
                                                                     seq_len=lp, pages=(kpg, vpg), page=page)
            attn = moba_prompt(qa, ksum, kbf, vt, batch=bp, seq_len=lp)
            zero_state = jnp.zeros((bp, n_heads // 2, HEAD_PAIR, HEAD_PAIR), F32)
            ret, s_fin = retention_gated(qb, kb, vb, gb, zero_state, n_seq=1, chunk=RET_CHUNK, tiles=ret_tiles_p,
                                         steps_per_seq=lp // (RET_CHUNK * ret_tiles_p))
            xp = merge_ln(attn, ret, w_out, xp, g(1), b(1), layer=li, alpha=alpha, tm=tm_proj)
            r_p.append(_block_diag_to_states(s_fin))
            qa, ka, va, qb, kb, vb, gb = ab_project(xs, w_in, cos_s, sin_s, layer=li, tm=ts, seq_len=ts)
            page_ids = (page_table.astype(jnp.int32) + li * n_pool).reshape(-1)
            attn = moba_sample(qa, ka, va, pool_kt, pool_vt, page_ids, batch=bs, n_pages=n_pages)
            ret, s_fin = retention_gated(qb, kb, vb, gb, _states_to_block_diag(state_ret[li]), n_seq=seq_per_tile,
                                         chunk=ls, tiles=1, steps_per_seq=1)
            xs = merge_ln(attn, ret, w_out, xs, g(1), b(1), layer=li, alpha=alpha, tm=ts)
            k_s.append(ka.reshape(bs, ls, a_heads, hd))
            v_s.append(va.reshape(bs, ls, a_heads, hd))
            r_s.append(_block_diag_to_states(s_fin))
        else:
            weights = _s5_weights(s5_lam_re[li], s5_lam_im[li], s5_b_re[li], s5_b_im[li], s5_c_re[li],
                                  s5_c_im[li], s5_log_dt[li])
            n_state = weights[2].shape[1]
            d_skip = s5_d[li][None, :]
            zero = jnp.zeros((bp, n_state), F32)
            y, a_re, a_im = s5_scan(xp.reshape(bp, lp, d), zero, zero, weights, d_skip, tt=min(32, lp))
            xp = s5_out_ln(y.reshape(tp, d), w_o, w_g, xp, g(1), b(1), layer=li, alpha=alpha, tm=tm_proj)
            sre_p.append(a_re.reshape(bp, -1, S5_STATE))
            sim_p.append(a_im.reshape(bp, -1, S5_STATE))
            y, a_re, a_im = s5_scan(xs.reshape(bs, ls, d), state_s5_re[li].reshape(bs, n_state),
                                    state_s5_im[li].reshape(bs, n_state), weights, d_skip, tt=ls)
            xs = s5_out_ln(y.reshape(ts, d), w_o, w_g, xs, g(1), b(1), layer=li, alpha=alpha, tm=ts)
            sre_s.append(a_re.reshape(bs, -1, S5_STATE))
            sim_s.append(a_im.reshape(bs, -1, S5_STATE))
        xp = ffn_ln(xp, *f2, g(2), b(2), layer=layer, alpha=alpha, tm=tm_p)
        xs = ffn_ln(xs, *f2, g(2), b(2), layer=layer, alpha=alpha, tm=ts)
    unpage = lambda t: jnp.transpose(t.reshape(n_layers_ab, bp, lp // page, a_heads, hd, page), (0, 1, 2, 5, 3, 4))
    return (xp.reshape(bp, lp, d), xs.reshape(bs, ls, d), unpage(kpg), unpage(vpg), jnp.stack(k_s),
            jnp.stack(v_s), jnp.stack(r_p), jnp.stack(r_s), jnp.stack(sre_p), jnp.stack(sim_p),
            jnp.stack(sre_s), jnp.stack(sim_s))
```

```python
import functools
import math

import jax
import jax.numpy as jnp
from jax import lax
from jax.experimental import pallas as pl
from jax.experimental.pallas import tpu as pltpu

F32 = jnp.float32
BF16 = jnp.bfloat16

HEAD_DIM = 64
HEAD_PAIR = 2 * HEAD_DIM
MOBA_BLOCK = 256
MOBA_TOPK = 3
RET_CHUNK = 128
S5_GROUP = 16
S5_STATE = 64
ROPE_THETA = 10000.0
LN_EPS = 1e-5
GN_EPS = 1e-6
NEG_INF = -1e30
LOG2_E = math.log2(math.e)
SUBLANES = 8
LANES = 128
VMEM_LIMIT = 48 * 1024 * 1024


def _dot(a, b, precision=None):
    return jnp.dot(a, b, preferred_element_type=F32, precision=precision)


def _dot_nt(a, b, precision=None):
    return lax.dot_general(a, b, (((1,), (1,)), ((), ())), preferred_element_type=F32, precision=precision)


def _dot_tn(a, b):
    return lax.dot_general(a, b, (((0,), (0,)), ((), ())), preferred_element_type=F32)


def _layer_norm(r, g, b):
    mu = jnp.mean(r, -1, keepdims=True)
    d = r - mu
    var = jnp.mean(d * d, -1, keepdims=True)
    return d * lax.rsqrt(var + LN_EPS) * g + b


def _silu(x):
    return x * jax.nn.sigmoid(x)


def _params(*sem):
    return pltpu.CompilerParams(dimension_semantics=sem, vmem_limit_bytes=VMEM_LIMIT)


def _ffn_ln_kernel(x_ref, wg_ref, wu_ref, wd_ref, g_ref, b_ref, o_ref, a_ref, *, alpha, tf):
    xb = x_ref[...].astype(BF16)
    for c in range(wg_ref.shape[1] // tf):
        cols = slice(c * tf, (c + 1) * tf)
        hg = _dot(xb, wg_ref[:, cols])
        hu = _dot(xb, wu_ref[:, cols])
        a_ref[:, cols] = (_silu(hg) * hu).astype(BF16)
    r = alpha * x_ref[...] + 0.5 * _dot(a_ref[...], wd_ref[...])
    o_ref[...] = _layer_norm(r, g_ref[...], b_ref[...])


def _layer_weight(layer, r, c, row_blk=0, **kw):
    return pl.BlockSpec((None, r, c), lambda *_: (layer, row_blk, 0), **kw)


def ffn_ln(x, wg, wu, wd, g, b, *, layer, alpha, tm, tf=256):
    t, d = x.shape
    f = wg.shape[2]
    resident = lambda r, c: _layer_weight(layer, r, c, pipeline_mode=pl.Buffered(1))
    return pl.pallas_call(
        functools.partial(_ffn_ln_kernel, alpha=alpha, tf=tf),
        grid=(t // tm,),
        in_specs=[pl.BlockSpec((tm, d), lambda i: (i, 0)), resident(d, f), resident(d, f), resident(f, d),
                  pl.BlockSpec((1, d), lambda i: (0, 0)), pl.BlockSpec((1, d), lambda i: (0, 0))],
        out_specs=pl.BlockSpec((tm, d), lambda i: (i, 0)),
        out_shape=jax.ShapeDtypeStruct((t, d), F32),
        scratch_shapes=[pltpu.VMEM((tm, f), BF16)],
        compiler_params=_params("parallel"),
    )(x, wg, wu, wd, g, b)


def _rope(y, cos, sin_signed):
    width = y.shape[-1]
    lane = lax.broadcasted_iota(jnp.int32, (1, width), 1)
    first = (lane % HEAD_DIM) < (HEAD_DIM // 2)
    rot = jnp.where(first, pltpu.roll(y, width - HEAD_DIM // 2, 1), pltpu.roll(y, HEAD_DIM // 2, 1))
    return y * cos + rot * sin_signed


def _ab_proj_kernel(*refs, half, scale, page):
    if page is None:
        x_ref, w_ref, cos_ref, sin_ref, qa_ref, ka_ref, va_ref, qb_ref, kb_ref, vb_ref, gb_ref = refs
    else:
        (x_ref, w_ref, cos_ref, sin_ref, _, _, qa_ref, kbf_ref, vt_ref, ksum_ref, kpg_ref, vpg_ref,
         qb_ref, kb_ref, vb_ref, gb_ref) = refs
    xb = x_ref[...].astype(BF16)
    cos = cos_ref[...]
    sin = sin_ref[...]
    col = lambda c: _dot(xb, w_ref[:, c * half:(c + 1) * half])
    qa_ref[...] = _rope(col(0), cos, sin) * scale
    ka = _rope(col(1), cos, sin)
    va = col(2)
    if page is None:
        ka_ref[...] = ka
        va_ref[...] = va
    else:
        kbf_ref[...] = ka.astype(BF16)
        for n in range(ka.shape[0] // MOBA_BLOCK):
            blk = slice(n * MOBA_BLOCK, (n + 1) * MOBA_BLOCK)
            ksum_ref[n] = jnp.sum(ka[blk], axis=0, keepdims=True)
            vt_ref[n] = va[blk].T.astype(BF16)
        for n in range(ka.shape[0] // page):
            rows = slice(n * page, (n + 1) * page)
            kpg_ref[n] = ka[rows].T
            vpg_ref[n] = va[rows].T
    qb_ref[...] = _rope(col(3), cos, sin)
    kb_ref[...] = _rope(col(4), cos, sin) * scale
    vb_ref[...] = col(5)
    gb_ref[...] = col(6)


def ab_project(x, w_in, cos, sin, *, layer, tm, seq_len, pages=None, page=None):
    t, d = x.shape
    half = w_in.shape[2] // 7
    tiles_per_seq = max(seq_len // tm, 1)
    tab = pl.BlockSpec((tm, half), lambda i: (i % tiles_per_seq, 0))
    row = pl.BlockSpec((tm, half), lambda i: (i, 0))
    f32o = jax.ShapeDtypeStruct((t, half), F32)
    in_specs = [pl.BlockSpec((tm, d), lambda i: (i, 0)), _layer_weight(layer, d, 7 * half), tab, tab]
    operands = [x, w_in, cos, sin]
    aliases = {}
    if pages is None:
        out_specs = [row] * 7
        out_shape = [f32o] * 7
    else:
        nblk = tm // MOBA_BLOCK
        page_blk = pl.BlockSpec((None, tm // page, half, page), lambda i: (layer, i, 0, 0))
        in_specs += [pl.BlockSpec(memory_space=pl.ANY)] * 2
        operands += list(pages)
        aliases = {4: 4, 5: 5}
        out_specs = [row, row,
                     pl.BlockSpec((nblk, half, MOBA_BLOCK), lambda i: (i, 0, 0)),
                     pl.BlockSpec((nblk, 1, half), lambda i: (i, 0, 0)),
                     page_blk, page_blk, row, row, row, row]
        out_shape = [f32o, jax.ShapeDtypeStruct((t, half), BF16),
                     jax.ShapeDtypeStruct((t // MOBA_BLOCK, half, MOBA_BLOCK), BF16),
                     jax.ShapeDtypeStruct((t // MOBA_BLOCK, 1, half), F32),
                     jax.ShapeDtypeStruct(pages[0].shape, F32), jax.ShapeDtypeStruct(pages[1].shape, F32),
                     f32o, f32o, f32o, f32o]
    return pl.pallas_call(
        functools.partial(_ab_proj_kernel, half=half, scale=HEAD_DIM ** -0.5, page=page),
        grid=(t // tm,),
        in_specs=in_specs,
        out_specs=out_specs,
        out_shape=out_shape,
        input_output_aliases=aliases,
        compiler_params=_params("parallel"),
    )(*operands)


def _topk_select(gate, n_valid):
    nb = gate.shape[1]
    blk = lax.broadcasted_iota(jnp.int32, (1, nb), 1)
    gate = jnp.where(blk < n_valid, gate, NEG_INF)
    rank = jnp.zeros(gate.shape, jnp.int32)
    for m in range(nb):
        gm = gate[:, m:m + 1]
        ahead = (gm > gate) | ((gm == gate) & (m < blk))
        rank = rank + ahead.astype(jnp.int32)
    return ((rank < MOBA_TOPK) & (blk < n_valid)).astype(F32)


def _topk_select_t(gate_t, n_valid):
    nb = gate_t.shape[0]
    blk = lax.broadcasted_iota(jnp.int32, (nb, 1), 0)
    gate_t = jnp.where(blk < n_valid, gate_t, NEG_INF)
    rank = jnp.zeros(gate_t.shape, jnp.int32)
    for m in range(nb):
        gm = gate_t[m:m + 1, :]
        ahead = (gm > gate_t) | ((gm == gate_t) & (m < blk))
        rank = rank + ahead.astype(jnp.int32)
    return ((rank < MOBA_TOPK) & (blk < n_valid)).astype(F32)


def _moba_prompt_kernel(q_ref, ksum_ref, k_ref, vt_ref, o_ref, qm_ref, sel_ref, m_ref, l_ref, a_ref, acc_ref,
                        s_ref, p_ref):
    j = pl.program_id(1)
    tq = q_ref.shape[0]
    n_heads = q_ref.shape[1] // HEAD_DIM
    lane = lax.broadcasted_iota(jnp.int32, (1, HEAD_PAIR), 1)
    key_i = lax.broadcasted_iota(jnp.int32, (MOBA_BLOCK, tq), 0)
    qry_i = lax.broadcasted_iota(jnp.int32, (MOBA_BLOCK, tq), 1)
    causal = key_i <= qry_i

    for h in range(n_heads):
        lanes = slice((h // 2) * HEAD_PAIR, (h // 2 + 1) * HEAD_PAIR)
        hm = (lane // HEAD_DIM) == (h % 2)
        qp = q_ref[:, lanes]
        kmean = jnp.where(hm, ksum_ref[:, 0, lanes] * (1.0 / MOBA_BLOCK), 0.0)
        sel_ref[h] = _topk_select_t(_dot_nt(kmean, qp, precision=lax.Precision.HIGHEST), j)
        qm_ref[h] = jnp.where(hm, qp * LOG2_E, 0.0).astype(BF16)

    ones_rows = jnp.ones((HEAD_DIM, MOBA_BLOCK), BF16)

    def update(blocks, first):
        def rows0(n):
            return pl.multiple_of((j if n is None else n) * MOBA_BLOCK, MOBA_BLOCK)

        def picked(n, h):
            return sel_ref[h, pl.ds(n, 1), :] > 0.0

        def scores(slot, n, h):
            k_pair = k_ref[pl.ds(rows0(n), MOBA_BLOCK), (h // 2) * HEAD_PAIR:(h // 2 + 1) * HEAD_PAIR]
            s_ref[slot, h] = _dot_nt(k_pair, qm_ref[h])

        def softmax(slot, n, h):
            if first:
                s = jnp.where(causal, s_ref[slot, h], NEG_INF)
                m_new = jnp.max(s, axis=0, keepdims=True)
            else:
                s = s_ref[slot, h]
                cmax = jnp.where(picked(n, h), jnp.max(s, axis=0, keepdims=True), NEG_INF)
                m_new = jnp.maximum(m_ref[h], cmax)
                a_ref[slot, h] = jnp.exp2(m_ref[h] - m_new)
            p_ref[slot, h] = jnp.exp2(s - m_new).astype(BF16)
            m_ref[h] = m_new

        def values(slot, n, h):
            v_h = vt_ref[j if n is None else n, h * HEAD_DIM:(h + 1) * HEAD_DIM, :]
            res = _dot(jnp.concatenate([v_h, ones_rows], axis=0), p_ref[slot, h])
            pv, psum = res[:HEAD_DIM], res[HEAD_DIM:HEAD_DIM + 1]
            if first:
                acc_ref[h] = pv
                l_ref[h] = psum
            else:
                acc_ref[h] = a_ref[slot, h] * acc_ref[h] + jnp.where(picked(n, h), pv, 0.0)
                l_ref[h] = a_ref[slot, h] * l_ref[h] + jnp.where(picked(n, h), psum, 0.0)

        heads = range(n_heads)
        stages = (scores, softmax, values)
        for t in range(len(stages) + len(blocks) - 1):
            for h in heads:
                for slot, n in enumerate(blocks):
                    if 0 <= t - slot < len(stages):
                        stages[t - slot](slot, n, h)

    update([None], True)

    def body(i, carry):
        update([2 * i, 2 * i + 1], False)
        return carry

    lax.fori_loop(0, j // 2, body, 0)

    @pl.when(j % 2 == 1)
    def _():
        update([j - 1], False)
    out_t = jnp.concatenate([acc_ref[h] / l_ref[h] for h in range(n_heads)], axis=0)
    o_ref[...] = out_t.T.astype(o_ref.dtype)


def moba_prompt(q, ksum, kbf, vt, *, batch, seq_len):
    t, w = q.shape
    nb = seq_len // MOBA_BLOCK
    n_heads = w // HEAD_DIM
    return pl.pallas_call(
        _moba_prompt_kernel,
        grid=(batch, nb),
        in_specs=[pl.BlockSpec((MOBA_BLOCK, w), lambda b, j: (b * nb + j, 0)),
                  pl.BlockSpec((nb, 1, w), lambda b, j: (b, 0, 0)),
                  pl.BlockSpec((seq_len, w), lambda b, j: (b, 0)),
                  pl.BlockSpec((nb, w, MOBA_BLOCK), lambda b, j: (b, 0, 0))],
        out_specs=pl.BlockSpec((MOBA_BLOCK, w), lambda b, j: (b * nb + j, 0)),
        out_shape=jax.ShapeDtypeStruct((t, w), BF16),
        scratch_shapes=[pltpu.VMEM((n_heads, MOBA_BLOCK, HEAD_PAIR), BF16),
                        pltpu.VMEM((n_heads, nb, MOBA_BLOCK), F32),
                        pltpu.VMEM((n_heads, 1, MOBA_BLOCK), F32),
                        pltpu.VMEM((n_heads, 1, MOBA_BLOCK), F32),
                        pltpu.VMEM((2, n_heads, 1, MOBA_BLOCK), F32),
                        pltpu.VMEM((n_heads, HEAD_DIM, MOBA_BLOCK), F32),
                        pltpu.VMEM((2, n_heads, MOBA_BLOCK, MOBA_BLOCK), F32),
                        pltpu.VMEM((2, n_heads, MOBA_BLOCK, MOBA_BLOCK), BF16)],
        compiler_params=_params("parallel", "arbitrary"),
    )(q, ksum, kbf, vt)


def _moba_sample_kernel(pt_ref, q_ref, knew_ref, vnew_ref, *rest, pages_per_step, page, n_heads):
    del pt_ref
    kpages = rest[:pages_per_step]
    vpages = rest[pages_per_step:2 * pages_per_step]
    o_ref = rest[2 * pages_per_step]
    qf_ref, qb_ref, ksum_ref, m_ref, l_ref, acc_ref, kpad_ref, vpad_ref = rest[2 * pages_per_step + 1:]
    s_idx = pl.program_id(1)
    n_steps = pl.num_programs(1)
    lq, w = q_ref.shape
    rows = n_heads * lq
    nb = acc_ref.shape[0]
    pages_per_blk = MOBA_BLOCK // page
    blks_per_step = pages_per_step // pages_per_blk
    rowhead = lax.broadcasted_iota(jnp.int32, (rows, 1), 0) // lq
    lanehead = lax.broadcasted_iota(jnp.int32, (1, w), 1) // HEAD_DIM
    blk_lane = lax.broadcasted_iota(jnp.int32, (1, LANES), 1)

    @pl.when(s_idx == 0)
    def _():
        qt = jnp.concatenate([q_ref[...]] * n_heads, axis=0)
        qbd = jnp.where(rowhead == lanehead, qt, 0.0)
        qf_ref[...] = qbd
        qb_ref[...] = qbd.astype(BF16)
        ksum_ref[...] = jnp.zeros_like(ksum_ref)
        m_ref[...] = jnp.zeros_like(m_ref)
        l_ref[...] = jnp.zeros_like(l_ref)

    qb = qb_ref[...]

    def partial_softmax(s):
        m = jnp.max(s, axis=1, keepdims=True)
        e = jnp.exp(s - m)
        return m, jnp.sum(e, axis=1, keepdims=True), e.astype(BF16)

    blks = range(blks_per_step)
    pages_of = lambda bi: range(bi * pages_per_blk, (bi + 1) * pages_per_blk)
    here = [blk_lane == s_idx * blks_per_step + bi for bi in blks]
    kt = [jnp.concatenate([kpages[i][0] for i in pages_of(bi)], axis=1) for bi in blks]
    scores = [_dot(qb, kt[bi].astype(BF16)) for bi in blks]
    ksum = ksum_ref[...]
    for bi in blks:
        ksum = jnp.where(here[bi], jnp.sum(kt[bi], axis=1, keepdims=True), ksum)
    ksum_ref[...] = ksum
    stats = [partial_softmax(scores[bi]) for bi in blks]
    for bi in blks:
        vt = jnp.concatenate([vpages[i][0] for i in pages_of(bi)], axis=1)
        acc_ref[s_idx * blks_per_step + bi] = _dot_nt(stats[bi][2], vt.astype(BF16))
    m_all, l_all = m_ref[...], l_ref[...]
    for bi in blks:
        m_all = jnp.where(here[bi], stats[bi][0], m_all)
        l_all = jnp.where(here[bi], stats[bi][1], l_all)
    m_ref[...] = m_all
    l_ref[...] = l_all

    @pl.when(s_idx == n_steps - 1)
    def _():
        kpad_ref[...] = jnp.zeros_like(kpad_ref)
        vpad_ref[...] = jnp.zeros_like(vpad_ref)
        kpad_ref[0:lq, :] = knew_ref[...]
        vpad_ref[0:lq, :] = vnew_ref[...]
        s_own = _dot_nt(qb, kpad_ref[...].astype(BF16))
        tq = lax.broadcasted_iota(jnp.int32, s_own.shape, 0) % lq
        tk = lax.broadcasted_iota(jnp.int32, s_own.shape, 1)
        m_own, l_own, e_own = partial_softmax(jnp.where(tk <= tq, s_own, NEG_INF))
        acc_own = _dot(e_own, vpad_ref[...].astype(BF16))

        gate = _dot(qf_ref[...], ksum_ref[...] * (1.0 / MOBA_BLOCK), precision=lax.Precision.HIGHEST)
        sel = _topk_select(gate[:, :nb], nb) > 0.0
        m_blk = m_ref[:, :nb]
        m_all = jnp.maximum(m_own, jnp.max(jnp.where(sel, m_blk, NEG_INF), axis=1, keepdims=True))
        wgt = jnp.where(sel, jnp.exp(m_blk - m_all), 0.0)
        w_own = jnp.exp(m_own - m_all)
        den = w_own * l_own + jnp.sum(wgt * l_ref[:, :nb], axis=1, keepdims=True)
        num = w_own * acc_own
        for n in range(nb):
            num = num + wgt[:, n:n + 1] * acc_ref[n]
        o_all = num / den
        out = jnp.zeros((lq, w), F32)
        for h in range(n_heads):
            out = out + jnp.where(lanehead == h, o_all[h * lq:(h + 1) * lq], 0.0)
        o_ref[...] = out


def moba_sample(q, knew, vnew, pool_kt, pool_vt, page_ids, *, batch, n_pages, pages_per_step=16):
    t, w = q.shape
    lq = t // batch
    page = pool_kt.shape[2]
    n_heads = w // HEAD_DIM
    rows = n_heads * lq
    nb = n_pages * page // MOBA_BLOCK
    assert nb <= LANES and pages_per_step % (MOBA_BLOCK // page) == 0
    seq_blk = pl.BlockSpec((lq, w), lambda b, s, pt: (b, 0))

    def page_spec(i):
        return pl.BlockSpec((1, w, page), lambda b, s, pt: (pt[b * n_pages + s * pages_per_step + i], 0, 0))

    grid_spec = pltpu.PrefetchScalarGridSpec(
        num_scalar_prefetch=1,
        grid=(batch, n_pages // pages_per_step),
        in_specs=[seq_blk, seq_blk, seq_blk] + [page_spec(i) for i in range(pages_per_step)] * 2,
        out_specs=seq_blk,
        scratch_shapes=[pltpu.VMEM((rows, w), F32), pltpu.VMEM((rows, w), BF16), pltpu.VMEM((w, LANES), F32),
                        pltpu.VMEM((rows, LANES), F32), pltpu.VMEM((rows, LANES), F32),
                        pltpu.VMEM((nb, rows, w), F32),
                        pltpu.VMEM((LANES, w), F32), pltpu.VMEM((LANES, w), F32)],
    )
    return pl.pallas_call(
        functools.partial(_moba_sample_kernel, pages_per_step=pages_per_step, page=page, n_heads=n_heads),
        grid_spec=grid_spec,
        out_shape=jax.ShapeDtypeStruct((t, w), F32),
        compiler_params=_params("parallel", "arbitrary"),
    )(page_ids, q, knew, vnew, *([pool_kt] * pages_per_step), *([pool_vt] * pages_per_step))


def _retention_kernel(q_ref, k_ref, v_ref, g_ref, s0_ref, dmask_ref, qdec_ref, kdec_ref, cdec_ref,
                      o_ref, sout_ref, st_ref, *, n_seq, chunk, tiles, steps_per_seq):
    step = pl.program_id(0)
    tile_rows = n_seq * chunk
    n_pairs = q_ref.shape[1] // HEAD_PAIR
    lane = lax.broadcasted_iota(jnp.int32, (1, HEAD_PAIR), 1)
    rowh = lax.broadcasted_iota(jnp.int32, (HEAD_PAIR, 1), 0) // HEAD_DIM
    same_head = rowh == (lane // HEAD_DIM)
    rseq = lax.broadcasted_iota(jnp.int32, (tile_rows, 1), 0) // chunk

    @pl.when(step % steps_per_seq == 0)
    def _():
        st_ref[...] = s0_ref[...]

    pairs = range(n_pairs)
    lanes_of = lambda hp: slice(hp * HEAD_PAIR, (hp + 1) * HEAD_PAIR)
    half_masks = [(lane // HEAD_DIM) == half for half in range(2)]
    for ti in range(tiles):
        rows = slice(ti * tile_rows, (ti + 1) * tile_rows)
        qb = [q_ref[rows, lanes_of(hp)].astype(BF16) for hp in pairs]
        kf = [k_ref[rows, lanes_of(hp)] for hp in pairs]
        kb = [k.astype(BF16) for k in kf]
        vb = [v_ref[rows, lanes_of(hp)].astype(BF16) for hp in pairs]
        inner = [[_dot_nt(jnp.where(hm, qb[hp], jnp.zeros_like(qb[hp])), kb[hp]) for hm in half_masks]
                 for hp in pairs]
        scaled = [[(inner[hp][half] * dmask_ref[2 * hp + half]).astype(BF16) for half in range(2)] for hp in pairs]
        ret = [sum(jnp.where(half_masks[half], _dot(scaled[hp][half], vb[hp]), 0.0) for half in range(2))
               for hp in pairs]
        for hp in pairs:
            kd = (kf[hp] * kdec_ref[:, lanes_of(hp)]).astype(BF16)
            cdec = cdec_ref[:, lanes_of(hp)]
            carried = []
            for s in range(n_seq):
                srows = slice(s * chunk, (s + 1) * chunk)
                state = st_ref[s, hp]
                carried.append(_dot(qb[hp][srows], state.astype(BF16)))
                kd_s = kd if n_seq == 1 else jnp.where(rseq == s, kd, jnp.zeros_like(kd))
                st_ref[s, hp] = cdec * state + jnp.where(same_head, _dot_tn(kd_s, vb[hp]), 0.0)
            carried = carried[0] if n_seq == 1 else jnp.concatenate(carried, axis=0)
            ret[hp] = ret[hp] + carried * qdec_ref[:, lanes_of(hp)]
        for hp in pairs:
            normed = jnp.zeros_like(ret[hp])
            for hm in half_masks:
                mu = jnp.sum(jnp.where(hm, ret[hp], 0.0), axis=1, keepdims=True) * (1.0 / HEAD_DIM)
                d = jnp.where(hm, ret[hp] - mu, 0.0)
                var = jnp.sum(d * d, axis=1, keepdims=True) * (1.0 / HEAD_DIM)
                normed = normed + d * lax.rsqrt(var + GN_EPS)
            o_ref[rows, lanes_of(hp)] = (_silu(g_ref[rows, lanes_of(hp)]) * normed).astype(o_ref.dtype)

    @pl.when(step % steps_per_seq == steps_per_seq - 1)
    def _():
        sout_ref[...] = st_ref[...]


def _retention_tables(n_heads, n_seq, chunk):
    log_g = jnp.log1p(-jnp.exp2(-5.0 - jnp.arange(n_heads, dtype=F32)))
    idx = jnp.arange(chunk, dtype=F32)
    diff = idx[:, None] - idx[None, :]
    dmask = jnp.where(diff >= 0, jnp.exp(log_g[:, None, None] * jnp.maximum(diff, 0.0)), 0.0)
    q_dec = jnp.exp(log_g[None, :] * (idx[:, None] + 1.0))
    k_dec = jnp.exp(log_g[None, :] * (chunk - 1.0 - idx[:, None]))
    c_dec = jnp.exp(log_g * chunk)
    seq_eye = jnp.eye(n_seq, dtype=F32)
    dmask = jnp.einsum("ab,hij->haibj", seq_eye, dmask).reshape(n_heads, n_seq * chunk, n_seq * chunk)
    lanes = lambda t: jnp.repeat(t, HEAD_DIM, axis=-1)
    return dmask, jnp.tile(lanes(q_dec), (n_seq, 1)), jnp.tile(lanes(k_dec), (n_seq, 1)), lanes(c_dec[None, :])


def retention_gated(q, k, v, g, s0_bd, *, n_seq, chunk, tiles, steps_per_seq):
    t, w = q.shape
    n_pairs = w // HEAD_PAIR
    rows = n_seq * chunk * tiles
    n_steps = t // rows
    dmask, qdec, kdec, cdec = _retention_tables(w // HEAD_DIM, n_seq, chunk)
    tile_rows = n_seq * chunk
    row = pl.BlockSpec((rows, w), lambda i: (i, 0))
    st = pl.BlockSpec((n_seq, n_pairs, HEAD_PAIR, HEAD_PAIR), lambda i: (i // steps_per_seq, 0, 0, 0))
    const = lambda shape: pl.BlockSpec(shape, lambda i: (0,) * len(shape))
    return pl.pallas_call(
        functools.partial(_retention_kernel, n_seq=n_seq, chunk=chunk, tiles=tiles, steps_per_seq=steps_per_seq),
        grid=(n_steps,),
        in_specs=[row, row, row, row, st, const(dmask.shape), const((tile_rows, w)), const((tile_rows, w)),
                  const((1, w))],
        out_specs=[row, st],
        out_shape=[jax.ShapeDtypeStruct((t, w), BF16), jax.ShapeDtypeStruct(s0_bd.shape, F32)],
        scratch_shapes=[pltpu.VMEM((n_seq, n_pairs, HEAD_PAIR, HEAD_PAIR), F32)],
        compiler_params=_params("arbitrary"),
    )(q, k, v, g, s0_bd, dmask, qdec, kdec, cdec)


def _states_to_block_diag(s):
    b, h, dk, dv = s.shape
    sp = s.reshape(b, h // 2, 2, dk, dv)
    z = jnp.zeros_like(sp[:, :, 0])
    top = jnp.concatenate([sp[:, :, 0], z], axis=-1)
    bot = jnp.concatenate([z, sp[:, :, 1]], axis=-1)
    return jnp.concatenate([top, bot], axis=-2)


def _block_diag_to_states(s):
    b, p = s.shape[:2]
    a = s[:, :, :HEAD_DIM, :HEAD_DIM]
    c = s[:, :, HEAD_DIM:, HEAD_DIM:]
    return jnp.stack([a, c], axis=2).reshape(b, 2 * p, HEAD_DIM, HEAD_DIM)


def _merge_ln_kernel(a_ref, r_ref, wa_ref, wr_ref, x_ref, g_ref, b_ref, o_ref, *, alpha):
    mix = _dot(a_ref[...].astype(BF16), wa_ref[...]) + _dot(r_ref[...].astype(BF16), wr_ref[...])
    o_ref[...] = _layer_norm(alpha * x_ref[...] + mix, g_ref[...], b_ref[...])


def merge_ln(attn, ret, w_out, x, g, b, *, layer, alpha, tm):
    t, d = x.shape
    w = attn.shape[1]
    row = lambda n: pl.BlockSpec((tm, n), lambda i: (i, 0))
    const = lambda r, c: pl.BlockSpec((r, c), lambda i: (0, 0))
    return pl.pallas_call(
        functools.partial(_merge_ln_kernel, alpha=alpha),
        grid=(t // tm,),
        in_specs=[row(w), row(w), _layer_weight(layer, w, d, 0), _layer_weight(layer, w, d, 1), row(d),
                  const(1, d), const(1, d)],
        out_specs=row(d),
        out_shape=jax.ShapeDtypeStruct((t, d), F32),
        compiler_params=_params("parallel"),
    )(attn, ret, w_out, w_out, x, g, b)


def _s5_scan_kernel(x_ref, s0re_ref, s0im_ref, wbre_ref, wbim_ref, are_ref, aim_ref, wcre_ref, wcim_ref, d_ref,
                    y_ref, sre_ref, sim_ref, xs_ref, bre_ref, bim_ref, stre_ref, stim_ref):
    step = pl.program_id(0)
    nb, tt, d = x_ref.shape
    b8 = stre_ref.shape[0]
    ns = stre_ref.shape[1]
    n_kb = d // LANES
    per_kb = ns // n_kb

    @pl.when(step == 0)
    def _():
        stre_ref[...] = jnp.zeros_like(stre_ref)
        stim_ref[...] = jnp.zeros_like(stim_ref)
        stre_ref[0:nb, :] = s0re_ref[...]
        stim_ref[0:nb, :] = s0im_ref[...]
        xs_ref[...] = jnp.zeros_like(xs_ref)

    for b in range(nb):
        for kb in range(n_kb):
            xs_ref[kb, pl.ds(b, tt, stride=b8), :] = x_ref[b, :, kb * LANES:(kb + 1) * LANES]

    n_out = wcre_ref.shape[0]
    kin = ns // n_out
    wout = d // n_out
    kb_per_chunk = kin // per_kb

    def project_in(c):
        for kb in range(c * kb_per_chunk, (c + 1) * kb_per_chunk):
            xk = xs_ref[kb].astype(BF16)
            cols = slice(kb * per_kb, (kb + 1) * per_kb)
            bre_ref[:, cols] = _dot(xk, wbre_ref[kb])
            bim_ref[:, cols] = _dot(xk, wbim_ref[kb])

    def scan(c):
        cs = slice(c * kin, (c + 1) * kin)
        are = jnp.broadcast_to(are_ref[:, cs], (SUBLANES, kin))
        aim = jnp.broadcast_to(aim_ref[:, cs], (SUBLANES, kin))
        for r0 in range(0, b8, SUBLANES):
            sre = stre_ref[r0:r0 + SUBLANES, cs]
            sim = stim_ref[r0:r0 + SUBLANES, cs]
            for t in range(tt):
                rows = slice(t * b8 + r0, t * b8 + r0 + SUBLANES)
                sre, sim = (are * sre - aim * sim + bre_ref[rows, cs], are * sim + aim * sre + bim_ref[rows, cs])
                bre_ref[rows, cs] = sre
                bim_ref[rows, cs] = sim
            stre_ref[r0:r0 + SUBLANES, cs] = sre
            stim_ref[r0:r0 + SUBLANES, cs] = sim

    def project_out(c):
        cs = slice(c * kin, (c + 1) * kin)
        y = _dot(bre_ref[:, cs].astype(BF16), wcre_ref[c]) + _dot(bim_ref[:, cs].astype(BF16), wcim_ref[c])
        for i in range(wout // LANES):
            kb = c * (wout // LANES) + i
            xs_ref[kb] = y[:, i * LANES:(i + 1) * LANES] + xs_ref[kb] * d_ref[:, kb * LANES:(kb + 1) * LANES]

    project_in(0)
    for c in range(n_out):
        if c + 1 < n_out:
            project_in(c + 1)
        scan(c)
        if c >= 1:
            project_out(c - 1)
    project_out(n_out - 1)

    for b in range(nb):
        for kb in range(n_kb):
            y_ref[b, :, kb * LANES:(kb + 1) * LANES] = xs_ref[kb, pl.ds(b, tt, stride=b8), :]

    @pl.when(step == pl.num_programs(0) - 1)
    def _():
        sre_ref[...] = stre_ref[0:nb, :]
        sim_ref[...] = stim_ref[0:nb, :]


def _s5_weights(lam_re, lam_im, b_re, b_im, c_re, c_im, log_dt):
    g, p, h = b_re.shape
    lam = lax.complex(lam_re.astype(F32), lam_im.astype(F32))
    dt = jnp.exp(log_dt.astype(F32))[:, None]
    lam_bar = jnp.exp(lam * dt)
    b_bar = ((lam_bar - 1.0) / lam)[:, :, None] * lax.complex(b_re.astype(F32), b_im.astype(F32))
    gpk = LANES // h
    n_kb = g // gpk
    eye = jnp.eye(gpk, dtype=F32)

    def in_blocks(m):
        m = m.reshape(n_kb, gpk, p, h)
        return jnp.einsum("kgph,gf->kghfp", m, eye).reshape(n_kb, gpk * h, gpk * p).astype(BF16)

    gpo = 2 * LANES // h
    n_out = g // gpo
    eye_o = jnp.eye(gpo, dtype=F32)

    def out_blocks(m):
        m = m.reshape(n_out, gpo, h, p)
        return jnp.einsum("kghp,gf->kgpfh", m, eye_o).reshape(n_out, gpo * p, gpo * h).astype(BF16)

    return (in_blocks(jnp.real(b_bar)), in_blocks(jnp.imag(b_bar)),
            jnp.real(lam_bar).reshape(1, g * p), jnp.imag(lam_bar).reshape(1, g * p),
            out_blocks(c_re.astype(F32)), out_blocks(-c_im.astype(F32)))


def s5_scan(x, s0_re, s0_im, weights, d_skip, *, tt):
    nb, seq, d = x.shape
    wbre, wbim, are, aim, wcre, wcim = weights
    ns = are.shape[1]
    b8 = -(-nb // SUBLANES) * SUBLANES
    rows = tt * b8
    const = lambda a: pl.BlockSpec(a.shape, lambda i: (0,) * a.ndim)
    xblk = pl.BlockSpec((nb, tt, d), lambda i: (0, i, 0))
    sblk = pl.BlockSpec((nb, ns), lambda i: (0, 0))
    return pl.pallas_call(
        _s5_scan_kernel,
        grid=(seq // tt,),
        in_specs=[xblk, sblk, sblk, const(wbre), const(wbim), const(are), const(aim), const(wcre), const(wcim),
                  pl.BlockSpec((1, d), lambda i: (0, 0))],
        out_specs=[xblk, sblk, sblk],
        out_shape=[jax.ShapeDtypeStruct((nb, seq, d), F32), jax.ShapeDtypeStruct((nb, ns), F32),
                   jax.ShapeDtypeStruct((nb, ns), F32)],
        scratch_shapes=[pltpu.VMEM((d // LANES, rows, LANES), F32), pltpu.VMEM((rows, ns), F32),
                        pltpu.VMEM((rows, ns), F32),
                        pltpu.VMEM((b8, ns), F32), pltpu.VMEM((b8, ns), F32)],
        compiler_params=_params("arbitrary"),
    )(x, s0_re, s0_im, wbre, wbim, are, aim, wcre, wcim, d_skip)


def _s5_out_ln_kernel(y_ref, wo_ref, wg_ref, x_ref, g_ref, b_ref, o_ref, *, alpha):
    gl = jax.nn.gelu(y_ref[...]).astype(BF16)
    mix = _dot(gl, wo_ref[...]) * jax.nn.sigmoid(_dot(gl, wg_ref[...]))
    o_ref[...] = _layer_norm(alpha * x_ref[...] + mix, g_ref[...], b_ref[...])


def s5_out_ln(y, w_out, w_gate, x, g, b, *, layer, alpha, tm):
    t, d = x.shape
    row = pl.BlockSpec((tm, d), lambda i: (i, 0))
    const = lambda r, c: pl.BlockSpec((r, c), lambda i: (0, 0))
    return pl.pallas_call(
        functools.partial(_s5_out_ln_kernel, alpha=alpha),
        grid=(t // tm,),
        in_specs=[row, _layer_weight(layer, d, d), _layer_weight(layer, d, d), row, const(1, d), const(1, d)],
        out_specs=row,
        out_shape=jax.ShapeDtypeStruct((t, d), F32),
        compiler_params=_params("parallel"),
    )(y, w_out, w_gate, x, g, b)


def _rope_tables(pos, n_heads):
    half = HEAD_DIM // 2
    inv = ROPE_THETA ** (-jnp.arange(half, dtype=F32) / half)
    ang = pos.astype(F32)[:, None] * inv[None, :]
    cos = jnp.cos(ang)
    sin = jnp.sin(ang)
    cos_h = jnp.concatenate([cos, cos], axis=-1)
    sin_h = jnp.concatenate([-sin, sin], axis=-1)
    return jnp.tile(cos_h, (1, n_heads)), jnp.tile(sin_h, (1, n_heads))


def kernel(x_prompt, x_sample, cache_k, cache_v, page_table, state_ret, state_s5_re, state_s5_im, ffn1_w_gate, ffn1_w_up, ffn1_w_down, ffn2_w_gate, ffn2_w_up, ffn2_w_down, ln_g, ln_b, w_in_ab, w_out_ab, s5_lam_re, s5_lam_im, s5_b_re, s5_b_im, s5_c_re, s5_c_im, s5_d, s5_log_dt, s5_w_out, s5_w_gate):
    bp, lp, d = x_prompt.shape
    bs, ls, _ = x_sample.shape
    depth = ffn1_w_gate.shape[0]
    n_layers_ab, n_pool, page, a_heads, hd = cache_k.shape
    n_pages = page_table.shape[1]
    past_len = n_pages * page
    half = a_heads * hd
    n_heads = half // HEAD_DIM
    assert hd == HEAD_DIM and lp % MOBA_BLOCK == 0 and past_len % MOBA_BLOCK == 0 and ls <= MOBA_BLOCK
    assert MOBA_BLOCK % page == 0 and lp % page == 0 and lp % RET_CHUNK == 0
    alpha = (2 * depth) ** 0.25
    tp, ts = bp * lp, bs * ls
    tm_p = min(1024, lp)
    tm_proj = min(512, lp)

    xp = x_prompt.reshape(tp, d)
    xs = x_sample.reshape(ts, d)
    bf = lambda a: a.astype(BF16)
    f1 = (bf(ffn1_w_gate), bf(ffn1_w_up), bf(ffn1_w_down))
    f2 = (bf(ffn2_w_gate), bf(ffn2_w_up), bf(ffn2_w_down))
    w_in, w_out = bf(w_in_ab), bf(w_out_ab)
    w_o, w_g = bf(s5_w_out), bf(s5_w_gate)
    cos_p, sin_p = _rope_tables(jnp.arange(lp, dtype=jnp.int32), n_heads)
    cos_s, sin_s = _rope_tables(past_len + jnp.arange(ls, dtype=jnp.int32), n_heads)
    cos_s, sin_s = jnp.tile(cos_s, (bs, 1)), jnp.tile(sin_s, (bs, 1))
    pool_kt = jnp.transpose(cache_k, (0, 1, 3, 4, 2)).reshape(n_layers_ab * n_pool, half, page)
    pool_vt = jnp.transpose(cache_v, (0, 1, 3, 4, 2)).reshape(n_layers_ab * n_pool, half, page)
    seq_per_tile = max(RET_CHUNK // ls, 1)
    ret_tiles_p = min(4, lp // RET_CHUNK)

    kpg = jnp.zeros((n_layers_ab, tp // page, half, page), F32)
    vpg = jnp.zeros((n_layers_ab, tp // page, half, page), F32)
    k_s, v_s, r_p, r_s = [], [], [], []
    sre_p, sim_p, sre_s, sim_s = [], [], [], []
    for layer in range(depth):
        li = layer // 2
        g = lambda i: ln_g[layer, i][None, :]
        b = lambda i: ln_b[layer, i][None, :]
        xp = ffn_ln(xp, *f1, g(0), b(0), layer=layer, alpha=alpha, tm=tm_p)
        xs = ffn_ln(xs, *f1, g(0), b(0), layer=layer, alpha=alpha, tm=ts)
        if layer % 2 == 0:
            qa, kbf, vt, ksum, kpg, vpg, qb, kb, vb, gb = ab_project(xp, w_in, cos_p, sin_p, layer=li, tm=tm_proj,
                                                                     seq_len=lp, pages=(kpg, vpg), page=page)
            attn = moba_prompt(qa, ksum, kbf, vt, batch=bp, seq_len=lp)
            zero_state = jnp.zeros((bp, n_heads // 2, HEAD_PAIR, HEAD_PAIR), F32)
            ret, s_fin = retention_gated(qb, kb, vb, gb, zero_state, n_seq=1, chunk=RET_CHUNK, tiles=ret_tiles_p,
                                         steps_per_seq=lp // (RET_CHUNK * ret_tiles_p))
            xp = merge_ln(attn, ret, w_out, xp, g(1), b(1), layer=li, alpha=alpha, tm=tm_p)
            r_p.append(_block_diag_to_states(s_fin))
            qa, ka, va, qb, kb, vb, gb = ab_project(xs, w_in, cos_s, sin_s, layer=li, tm=ts, seq_len=ts)
            page_ids = (page_table.astype(jnp.int32) + li * n_pool).reshape(-1)
            attn = moba_sample(qa, ka, va, pool_kt, pool_vt, page_ids, batch=bs, n_pages=n_pages)
            ret, s_fin = retention_gated(qb, kb, vb, gb, _states_to_block_diag(state_ret[li]), n_seq=seq_per_tile,
                                         chunk=ls, tiles=1, steps_per_seq=1)
            xs = merge_ln(attn, ret, w_out, xs, g(1), b(1), layer=li, alpha=alpha, tm=ts)
            k_s.append(ka.reshape(bs, ls, a_heads, hd))
            v_s.append(va.reshape(bs, ls, a_heads, hd))
            r_s.append(_block_diag_to_states(s_fin))
        else:
            weights = _s5_weights(s5_lam_re[li], s5_lam_im[li], s5_b_re[li], s5_b_im[li], s5_c_re[li],
                                  s5_c_im[li], s5_log_dt[li])
            n_state = weights[2].shape[1]
            d_skip = s5_d[li][None, :]
            zero = jnp.zeros((bp, n_state), F32)
            y, a_re, a_im = s5_scan(xp.reshape(bp, lp, d), zero, zero, weights, d_skip, tt=min(32, lp))
            xp = s5_out_ln(y.reshape(tp, d), w_o, w_g, xp, g(1), b(1), layer=li, alpha=alpha, tm=tm_p)
            sre_p.append(a_re.reshape(bp, -1, S5_STATE))
            sim_p.append(a_im.reshape(bp, -1, S5_STATE))
            y, a_re, a_im = s5_scan(xs.reshape(bs, ls, d), state_s5_re[li].reshape(bs, n_state),
                                    state_s5_im[li].reshape(bs, n_state), weights, d_skip, tt=ls)
            xs = s5_out_ln(y.reshape(ts, d), w_o, w_g, xs, g(1), b(1), layer=li, alpha=alpha, tm=ts)
            sre_s.append(a_re.reshape(bs, -1, S5_STATE))
            sim_s.append(a_im.reshape(bs, -1, S5_STATE))
        xp = ffn_ln(xp, *f2, g(2), b(2), layer=layer, alpha=alpha, tm=tm_p)
        xs = ffn_ln(xs, *f2, g(2), b(2), layer=layer, alpha=alpha, tm=ts)
    unpage = lambda t: jnp.transpose(t.reshape(n_layers_ab, bp, lp // page, a_heads, hd, page), (0, 1, 2, 5, 3, 4))
    return (xp.reshape(bp, lp, d), xs.reshape(bs, ls, d), unpage(kpg), unpage(vpg), jnp.stack(k_s),
            jnp.stack(v_s), jnp.stack(r_p), jnp.stack(r_s), jnp.stack(sre_p), jnp.stack(sim_p),
            jnp.stack(sre_s), jnp.stack(sim_s))
```

```python
import functools
import math

import jax
import jax.numpy as jnp
from jax import lax
from jax.experimental import pallas as pl
from jax.experimental.pallas import tpu as pltpu

F32 = jnp.float32
BF16 = jnp.bfloat16

HEAD_DIM = 64
HEAD_PAIR = 2 * HEAD_DIM
MOBA_BLOCK = 256
MOBA_TOPK = 3
MOBA_BLOCKS_PER_TRIP = 3
RET_CHUNK = 128
S5_GROUP = 16
S5_STATE = 64
ROPE_THETA = 10000.0
LN_EPS = 1e-5
GN_EPS = 1e-6
NEG_INF = -1e30
LOG2_E = math.log2(math.e)
SUBLANES = 8
BF16_SUBLANES = 16
LANES = 128
VMEM_LIMIT = 48 * 1024 * 1024


def _dot(a, b, precision=None):
    return jnp.dot(a, b, preferred_element_type=F32, precision=precision)


def _dot_nt(a, b, precision=None):
    return lax.dot_general(a, b, (((1,), (1,)), ((), ())), preferred_element_type=F32, precision=precision)


def _dot_tn(a, b):
    return lax.dot_general(a, b, (((0,), (0,)), ((), ())), preferred_element_type=F32)


def _layer_norm(r, g, b):
    mu = jnp.mean(r, -1, keepdims=True)
    d = r - mu
    var = jnp.mean(d * d, -1, keepdims=True)
    return d * lax.rsqrt(var + LN_EPS) * g + b


def _silu(x):
    return x * jax.nn.sigmoid(x)


def _params(*sem):
    return pltpu.CompilerParams(dimension_semantics=sem, vmem_limit_bytes=VMEM_LIMIT)


def _ffn_ln_kernel(x_ref, wg_ref, wu_ref, wd_ref, g_ref, b_ref, o_ref, a_ref, *, alpha, tf):
    xb = x_ref[...].astype(BF16)
    for c in range(wg_ref.shape[1] // tf):
        cols = slice(c * tf, (c + 1) * tf)
        hg = _dot(xb, wg_ref[:, cols])
        hu = _dot(xb, wu_ref[:, cols])
        a_ref[:, cols] = (_silu(hg) * hu).astype(BF16)
    r = alpha * x_ref[...] + 0.5 * _dot(a_ref[...], wd_ref[...])
    o_ref[...] = _layer_norm(r, g_ref[...], b_ref[...])


def _layer_weight(layer, r, c, row_blk=0, **kw):
    return pl.BlockSpec((None, r, c), lambda *_: (layer, row_blk, 0), **kw)


def ffn_ln(x, wg, wu, wd, g, b, *, layer, alpha, tm, tf=256):
    t, d = x.shape
    f = wg.shape[2]
    resident = lambda r, c: _layer_weight(layer, r, c, pipeline_mode=pl.Buffered(1))
    return pl.pallas_call(
        functools.partial(_ffn_ln_kernel, alpha=alpha, tf=tf),
        grid=(t // tm,),
        in_specs=[pl.BlockSpec((tm, d), lambda i: (i, 0)), resident(d, f), resident(d, f), resident(f, d),
                  pl.BlockSpec((1, d), lambda i: (0, 0)), pl.BlockSpec((1, d), lambda i: (0, 0))],
        out_specs=pl.BlockSpec((tm, d), lambda i: (i, 0)),
        out_shape=jax.ShapeDtypeStruct((t, d), F32),
        scratch_shapes=[pltpu.VMEM((tm, f), BF16)],
        compiler_params=_params("parallel"),
    )(x, wg, wu, wd, g, b)


def _rope(y, cos, sin_signed):
    width = y.shape[-1]
    lane = lax.broadcasted_iota(jnp.int32, (1, width), 1)
    first = (lane % HEAD_DIM) < (HEAD_DIM // 2)
    rot = jnp.where(first, pltpu.roll(y, width - HEAD_DIM // 2, 1), pltpu.roll(y, HEAD_DIM // 2, 1))
    return y * cos + rot * sin_signed


def _ab_proj_kernel(*refs, half, scale, page):
    if page is None:
        x_ref, w_ref, cos_ref, sin_ref, qa_ref, ka_ref, va_ref, qb_ref, kb_ref, vb_ref, gb_ref = refs
    else:
        (x_ref, w_ref, cos_ref, sin_ref, _, _, qa_ref, kbf_ref, vt_ref, ksum_ref, kpg_ref, vpg_ref,
         qb_ref, kb_ref, vb_ref, gb_ref) = refs
    xb = x_ref[...].astype(BF16)
    cos = cos_ref[...]
    sin = sin_ref[...]
    col = lambda c: _dot(xb, w_ref[:, c * half:(c + 1) * half])
    qa_ref[...] = _rope(col(0), cos, sin) * scale
    ka = _rope(col(1), cos, sin)
    va = col(2)
    if page is None:
        ka_ref[...] = ka
        va_ref[...] = va
    else:
        kbf_ref[...] = ka.astype(BF16)
        for n in range(ka.shape[0] // MOBA_BLOCK):
            blk = slice(n * MOBA_BLOCK, (n + 1) * MOBA_BLOCK)
            ksum_ref[n] = jnp.sum(ka[blk], axis=0, keepdims=True)
            vt_ref[n] = va[blk].T.astype(BF16)
        for n in range(ka.shape[0] // page):
            rows = slice(n * page, (n + 1) * page)
            kpg_ref[n] = ka[rows].T
            vpg_ref[n] = va[rows].T
    qb_ref[...] = _rope(col(3), cos, sin)
    kb_ref[...] = _rope(col(4), cos, sin) * scale
    vb_ref[...] = col(5)
    gb_ref[...] = col(6)


def ab_project(x, w_in, cos, sin, *, layer, tm, seq_len, pages=None, page=None):
    t, d = x.shape
    half = w_in.shape[2] // 7
    tiles_per_seq = max(seq_len // tm, 1)
    tab = pl.BlockSpec((tm, half), lambda i: (i % tiles_per_seq, 0))
    row = pl.BlockSpec((tm, half), lambda i: (i, 0))
    f32o = jax.ShapeDtypeStruct((t, half), F32)
    in_specs = [pl.BlockSpec((tm, d), lambda i: (i, 0)), _layer_weight(layer, d, 7 * half), tab, tab]
    operands = [x, w_in, cos, sin]
    aliases = {}
    if pages is None:
        out_specs = [row] * 7
        out_shape = [f32o] * 7
    else:
        nblk = tm // MOBA_BLOCK
        page_blk = pl.BlockSpec((None, tm // page, half, page), lambda i: (layer, i, 0, 0))
        in_specs += [pl.BlockSpec(memory_space=pl.ANY)] * 2
        operands += list(pages)
        aliases = {4: 4, 5: 5}
        out_specs = [row, row,
                     pl.BlockSpec((nblk, half, MOBA_BLOCK), lambda i: (i, 0, 0)),
                     pl.BlockSpec((nblk, 1, half), lambda i: (i, 0, 0)),
                     page_blk, page_blk, row, row, row, row]
        out_shape = [f32o, jax.ShapeDtypeStruct((t, half), BF16),
                     jax.ShapeDtypeStruct((t // MOBA_BLOCK, half, MOBA_BLOCK), BF16),
                     jax.ShapeDtypeStruct((t // MOBA_BLOCK, 1, half), F32),
                     jax.ShapeDtypeStruct(pages[0].shape, F32), jax.ShapeDtypeStruct(pages[1].shape, F32),
                     f32o, f32o, f32o, f32o]
    return pl.pallas_call(
        functools.partial(_ab_proj_kernel, half=half, scale=HEAD_DIM ** -0.5, page=page),
        grid=(t // tm,),
        in_specs=in_specs,
        out_specs=out_specs,
        out_shape=out_shape,
        input_output_aliases=aliases,
        compiler_params=_params("parallel"),
    )(*operands)


def _topk_select(gate, n_valid):
    nb = gate.shape[1]
    blk = lax.broadcasted_iota(jnp.int32, (1, nb), 1)
    gate = jnp.where(blk < n_valid, gate, NEG_INF)
    rank = jnp.zeros(gate.shape, jnp.int32)
    for m in range(nb):
        gm = gate[:, m:m + 1]
        ahead = (gm > gate) | ((gm == gate) & (m < blk))
        rank = rank + ahead.astype(jnp.int32)
    return ((rank < MOBA_TOPK) & (blk < n_valid)).astype(F32)


def _topk_select_t(gate_t, n_valid):
    nb = gate_t.shape[0]
    blk = lax.broadcasted_iota(jnp.int32, (nb, 1), 0)
    gate_t = jnp.where(blk < n_valid, gate_t, NEG_INF)
    rank = jnp.zeros(gate_t.shape, jnp.int32)
    for m in range(nb):
        gm = gate_t[m:m + 1, :]
        ahead = (gm > gate_t) | ((gm == gate_t) & (m < blk))
        rank = rank + ahead.astype(jnp.int32)
    return ((rank < MOBA_TOPK) & (blk < n_valid)).astype(F32)


def _moba_prompt_kernel(q_ref, ksum_ref, k_ref, vt_ref, o_ref, qm_ref, sel_ref, m_ref, l_ref, a_ref, acc_ref,
                        s_ref, p_ref):
    j = pl.program_id(1)
    tq = q_ref.shape[0]
    n_heads = q_ref.shape[1] // HEAD_DIM
    lane = lax.broadcasted_iota(jnp.int32, (1, HEAD_PAIR), 1)
    key_i = lax.broadcasted_iota(jnp.int32, (MOBA_BLOCK, tq), 0)
    qry_i = lax.broadcasted_iota(jnp.int32, (MOBA_BLOCK, tq), 1)
    causal = key_i <= qry_i

    for h in range(n_heads):
        lanes = slice((h // 2) * HEAD_PAIR, (h // 2 + 1) * HEAD_PAIR)
        hm = (lane // HEAD_DIM) == (h % 2)
        qp = q_ref[:, lanes]
        kmean = jnp.where(hm, ksum_ref[:, 0, lanes] * (1.0 / MOBA_BLOCK), 0.0)
        sel_ref[h] = _topk_select_t(_dot_nt(kmean, qp, precision=lax.Precision.HIGHEST), j)
        qm_ref[h] = jnp.where(hm, qp * LOG2_E, 0.0).astype(BF16)

    ones_rows = jnp.ones((BF16_SUBLANES, MOBA_BLOCK), BF16)

    def update(blocks, first):
        def rows0(n):
            return pl.multiple_of((j if n is None else n) * MOBA_BLOCK, MOBA_BLOCK)

        def picked(n, h):
            return sel_ref[h, pl.ds(n, 1), :] > 0.0

        def scores(slot, n, h):
            k_pair = k_ref[pl.ds(rows0(n), MOBA_BLOCK), (h // 2) * HEAD_PAIR:(h // 2 + 1) * HEAD_PAIR]
            s_ref[slot, h] = _dot_nt(k_pair, qm_ref[h])

        def softmax(slot, n, h):
            if first:
                s = jnp.where(causal, s_ref[slot, h], NEG_INF)
                m_new = jnp.max(s, axis=0, keepdims=True)
            else:
                s = s_ref[slot, h]
                cmax = jnp.where(picked(n, h), jnp.max(s, axis=0, keepdims=True), NEG_INF)
                m_new = jnp.maximum(m_ref[h], cmax)
                a_ref[slot, h] = jnp.exp2(m_ref[h] - m_new)
            p_ref[slot, h] = jnp.exp2(s - m_new).astype(BF16)
            m_ref[h] = m_new

        def values(slot, n, h):
            v_h = vt_ref[j if n is None else n, h * HEAD_DIM:(h + 1) * HEAD_DIM, :]
            res = _dot(jnp.concatenate([v_h, ones_rows], axis=0), p_ref[slot, h])
            pv, psum = res[:HEAD_DIM], res[HEAD_DIM:HEAD_DIM + 1]
            if first:
                acc_ref[h] = pv
                l_ref[h] = psum
            else:
                acc_ref[h] = a_ref[slot, h] * acc_ref[h] + jnp.where(picked(n, h), pv, 0.0)
                l_ref[h] = a_ref[slot, h] * l_ref[h] + jnp.where(picked(n, h), psum, 0.0)

        heads = range(n_heads)
        stages = (scores, softmax, values)
        for t in range(len(stages) + len(blocks) - 1):
            for h in heads:
                for slot, n in enumerate(blocks):
                    if 0 <= t - slot < len(stages):
                        stages[t - slot](slot, n, h)

    update([None], True)

    n_slots = s_ref.shape[0]

    def body(i, carry):
        update([n_slots * i + k for k in range(n_slots)], False)
        return carry

    lax.fori_loop(0, j // n_slots, body, 0)
    for rem in range(1, n_slots):
        @pl.when(j % n_slots == rem)
        def _(rem=rem):
            update([j - rem + k for k in range(rem)], False)

    out_t = jnp.concatenate([acc_ref[h] / l_ref[h] for h in range(n_heads)], axis=0)
    o_ref[...] = out_t.T.astype(o_ref.dtype)


def moba_prompt(q, ksum, kbf, vt, *, batch, seq_len):
    t, w = q.shape
    nb = seq_len // MOBA_BLOCK
    n_heads = w // HEAD_DIM
    return pl.pallas_call(
        _moba_prompt_kernel,
        grid=(batch, nb),
        in_specs=[pl.BlockSpec((MOBA_BLOCK, w), lambda b, j: (b * nb + j, 0)),
                  pl.BlockSpec((nb, 1, w), lambda b, j: (b, 0, 0)),
                  pl.BlockSpec((seq_len, w), lambda b, j: (b, 0)),
                  pl.BlockSpec((nb, w, MOBA_BLOCK), lambda b, j: (b, 0, 0))],
        out_specs=pl.BlockSpec((MOBA_BLOCK, w), lambda b, j: (b * nb + j, 0)),
        out_shape=jax.ShapeDtypeStruct((t, w), BF16),
        scratch_shapes=[pltpu.VMEM((n_heads, MOBA_BLOCK, HEAD_PAIR), BF16),
                        pltpu.VMEM((n_heads, nb, MOBA_BLOCK), F32),
                        pltpu.VMEM((n_heads, 1, MOBA_BLOCK), F32),
                        pltpu.VMEM((n_heads, 1, MOBA_BLOCK), F32),
                        pltpu.VMEM((MOBA_BLOCKS_PER_TRIP, n_heads, 1, MOBA_BLOCK), F32),
                        pltpu.VMEM((n_heads, HEAD_DIM, MOBA_BLOCK), F32),
                        pltpu.VMEM((MOBA_BLOCKS_PER_TRIP, n_heads, MOBA_BLOCK, MOBA_BLOCK), F32),
                        pltpu.VMEM((MOBA_BLOCKS_PER_TRIP, n_heads, MOBA_BLOCK, MOBA_BLOCK), BF16)],
        compiler_params=_params("parallel", "arbitrary"),
    )(q, ksum, kbf, vt)


def _moba_sample_kernel(pt_ref, q_ref, knew_ref, vnew_ref, *rest, pages_per_step, page, n_heads):
    del pt_ref
    kpages = rest[:pages_per_step]
    vpages = rest[pages_per_step:2 * pages_per_step]
    o_ref = rest[2 * pages_per_step]
    qf_ref, qb_ref, ksum_ref, m_ref, l_ref, acc_ref, kpad_ref, vpad_ref = rest[2 * pages_per_step + 1:]
    s_idx = pl.program_id(1)
    n_steps = pl.num_programs(1)
    lq, w = q_ref.shape
    rows = n_heads * lq
    nb = acc_ref.shape[0]
    pages_per_blk = MOBA_BLOCK // page
    blks_per_step = pages_per_step // pages_per_blk
    rowhead = lax.broadcasted_iota(jnp.int32, (rows, 1), 0) // lq
    lanehead = lax.broadcasted_iota(jnp.int32, (1, w), 1) // HEAD_DIM
    blk_lane = lax.broadcasted_iota(jnp.int32, (1, LANES), 1)

    @pl.when(s_idx == 0)
    def _():
        qt = jnp.concatenate([q_ref[...]] * n_heads, axis=0)
        qbd = jnp.where(rowhead == lanehead, qt, 0.0)
        qf_ref[...] = qbd
        qb_ref[...] = qbd.astype(BF16)
        ksum_ref[...] = jnp.zeros_like(ksum_ref)
        m_ref[...] = jnp.zeros_like(m_ref)
        l_ref[...] = jnp.zeros_like(l_ref)

    qb = qb_ref[...]

    def partial_softmax(s):
        m = jnp.max(s, axis=1, keepdims=True)
        e = jnp.exp(s - m)
        return m, jnp.sum(e, axis=1, keepdims=True), e.astype(BF16)

    blks = range(blks_per_step)
    pages_of = lambda bi: range(bi * pages_per_blk, (bi + 1) * pages_per_blk)
    here = [blk_lane == s_idx * blks_per_step + bi for bi in blks]
    kt = [jnp.concatenate([kpages[i][0] for i in pages_of(bi)], axis=1) for bi in blks]
    scores = [_dot(qb, kt[bi].astype(BF16)) for bi in blks]
    ksum = ksum_ref[...]
    for bi in blks:
        ksum = jnp.where(here[bi], jnp.sum(kt[bi], axis=1, keepdims=True), ksum)
    ksum_ref[...] = ksum
    stats = [partial_softmax(scores[bi]) for bi in blks]
    for bi in blks:
        vt = jnp.concatenate([vpages[i][0] for i in pages_of(bi)], axis=1)
        acc_ref[s_idx * blks_per_step + bi] = _dot_nt(stats[bi][2], vt.astype(BF16))
    m_all, l_all = m_ref[...], l_ref[...]
    for bi in blks:
        m_all = jnp.where(here[bi], stats[bi][0], m_all)
        l_all = jnp.where(here[bi], stats[bi][1], l_all)
    m_ref[...] = m_all
    l_ref[...] = l_all

    @pl.when(s_idx == n_steps - 1)
    def _():
        kpad_ref[...] = jnp.zeros_like(kpad_ref)
        vpad_ref[...] = jnp.zeros_like(vpad_ref)
        kpad_ref[0:lq, :] = knew_ref[...]
        vpad_ref[0:lq, :] = vnew_ref[...]
        s_own = _dot_nt(qb, kpad_ref[...].astype(BF16))
        tq = lax.broadcasted_iota(jnp.int32, s_own.shape, 0) % lq
        tk = lax.broadcasted_iota(jnp.int32, s_own.shape, 1)
        m_own, l_own, e_own = partial_softmax(jnp.where(tk <= tq, s_own, NEG_INF))
        acc_own = _dot(e_own, vpad_ref[...].astype(BF16))

        gate = _dot(qf_ref[...], ksum_ref[...] * (1.0 / MOBA_BLOCK), precision=lax.Precision.HIGHEST)
        sel = _topk_select(gate[:, :nb], nb) > 0.0
        m_blk = m_ref[:, :nb]
        m_all = jnp.maximum(m_own, jnp.max(jnp.where(sel, m_blk, NEG_INF), axis=1, keepdims=True))
        wgt = jnp.where(sel, jnp.exp(m_blk - m_all), 0.0)
        w_own = jnp.exp(m_own - m_all)
        den = w_own * l_own + jnp.sum(wgt * l_ref[:, :nb], axis=1, keepdims=True)
        num = w_own * acc_own
        for n in range(nb):
            num = num + wgt[:, n:n + 1] * acc_ref[n]
        o_all = num / den
        out = jnp.zeros((lq, w), F32)
        for h in range(n_heads):
            out = out + jnp.where(lanehead == h, o_all[h * lq:(h + 1) * lq], 0.0)
        o_ref[...] = out


def moba_sample(q, knew, vnew, pool_kt, pool_vt, page_ids, *, batch, n_pages, pages_per_step=16):
    t, w = q.shape
    lq = t // batch
    page = pool_kt.shape[2]
    n_heads = w // HEAD_DIM
    rows = n_heads * lq
    nb = n_pages * page // MOBA_BLOCK
    assert nb <= LANES and pages_per_step % (MOBA_BLOCK // page) == 0
    seq_blk = pl.BlockSpec((lq, w), lambda b, s, pt: (b, 0))

    def page_spec(i):
        return pl.BlockSpec((1, w, page), lambda b, s, pt: (pt[b * n_pages + s * pages_per_step + i], 0, 0))

    grid_spec = pltpu.PrefetchScalarGridSpec(
        num_scalar_prefetch=1,
        grid=(batch, n_pages // pages_per_step),
        in_specs=[seq_blk, seq_blk, seq_blk] + [page_spec(i) for i in range(pages_per_step)] * 2,
        out_specs=seq_blk,
        scratch_shapes=[pltpu.VMEM((rows, w), F32), pltpu.VMEM((rows, w), BF16), pltpu.VMEM((w, LANES), F32),
                        pltpu.VMEM((rows, LANES), F32), pltpu.VMEM((rows, LANES), F32),
                        pltpu.VMEM((nb, rows, w), F32),
                        pltpu.VMEM((LANES, w), F32), pltpu.VMEM((LANES, w), F32)],
    )
    return pl.pallas_call(
        functools.partial(_moba_sample_kernel, pages_per_step=pages_per_step, page=page, n_heads=n_heads),
        grid_spec=grid_spec,
        out_shape=jax.ShapeDtypeStruct((t, w), F32),
        compiler_params=_params("parallel", "arbitrary"),
    )(page_ids, q, knew, vnew, *([pool_kt] * pages_per_step), *([pool_vt] * pages_per_step))


def _retention_kernel(q_ref, k_ref, v_ref, g_ref, s0_ref, dmask_ref, qdec_ref, kdec_ref, cdec_ref,
                      o_ref, sout_ref, st_ref, *, n_seq, chunk, tiles, steps_per_seq):
    step = pl.program_id(0)
    tile_rows = n_seq * chunk
    n_pairs = q_ref.shape[1] // HEAD_PAIR
    lane = lax.broadcasted_iota(jnp.int32, (1, HEAD_PAIR), 1)
    rowh = lax.broadcasted_iota(jnp.int32, (HEAD_PAIR, 1), 0) // HEAD_DIM
    same_head = rowh == (lane // HEAD_DIM)
    rseq = lax.broadcasted_iota(jnp.int32, (tile_rows, 1), 0) // chunk

    @pl.when(step % steps_per_seq == 0)
    def _():
        st_ref[...] = s0_ref[...]

    pairs = range(n_pairs)
    lanes_of = lambda hp: slice(hp * HEAD_PAIR, (hp + 1) * HEAD_PAIR)
    half_masks = [(lane // HEAD_DIM) == half for half in range(2)]
    for ti in range(tiles):
        rows = slice(ti * tile_rows, (ti + 1) * tile_rows)
        qb = [q_ref[rows, lanes_of(hp)].astype(BF16) for hp in pairs]
        kf = [k_ref[rows, lanes_of(hp)] for hp in pairs]
        kb = [k.astype(BF16) for k in kf]
        vb = [v_ref[rows, lanes_of(hp)].astype(BF16) for hp in pairs]
        inner = [[_dot_nt(jnp.where(hm, qb[hp], jnp.zeros_like(qb[hp])), kb[hp]) for hm in half_masks]
                 for hp in pairs]
        scaled = [[(inner[hp][half] * dmask_ref[2 * hp + half]).astype(BF16) for half in range(2)] for hp in pairs]
        ret = [sum(jnp.where(half_masks[half], _dot(scaled[hp][half], vb[hp]), 0.0) for half in range(2))
               for hp in pairs]
        for hp in pairs:
            kd = (kf[hp] * kdec_ref[:, lanes_of(hp)]).astype(BF16)
            cdec = cdec_ref[:, lanes_of(hp)]
            carried = []
            for s in range(n_seq):
                srows = slice(s * chunk, (s + 1) * chunk)
                state = st_ref[s, hp]
                carried.append(_dot(qb[hp][srows], state.astype(BF16)))
                kd_s = kd if n_seq == 1 else jnp.where(rseq == s, kd, jnp.zeros_like(kd))
                st_ref[s, hp] = cdec * state + jnp.where(same_head, _dot_tn(kd_s, vb[hp]), 0.0)
            carried = carried[0] if n_seq == 1 else jnp.concatenate(carried, axis=0)
            ret[hp] = ret[hp] + carried * qdec_ref[:, lanes_of(hp)]
        for hp in pairs:
            normed = jnp.zeros_like(ret[hp])
            for hm in half_masks:
                mu = jnp.sum(jnp.where(hm, ret[hp], 0.0), axis=1, keepdims=True) * (1.0 / HEAD_DIM)
                d = jnp.where(hm, ret[hp] - mu, 0.0)
                var = jnp.sum(d * d, axis=1, keepdims=True) * (1.0 / HEAD_DIM)
                normed = normed + d * lax.rsqrt(var + GN_EPS)
            o_ref[rows, lanes_of(hp)] = (_silu(g_ref[rows, lanes_of(hp)]) * normed).astype(o_ref.dtype)

    @pl.when(step % steps_per_seq == steps_per_seq - 1)
    def _():
        sout_ref[...] = st_ref[...]


def _retention_tables(n_heads, n_seq, chunk):
    log_g = jnp.log1p(-jnp.exp2(-5.0 - jnp.arange(n_heads, dtype=F32)))
    idx = jnp.arange(chunk, dtype=F32)
    diff = idx[:, None] - idx[None, :]
    dmask = jnp.where(diff >= 0, jnp.exp(log_g[:, None, None] * jnp.maximum(diff, 0.0)), 0.0)
    q_dec = jnp.exp(log_g[None, :] * (idx[:, None] + 1.0))
    k_dec = jnp.exp(log_g[None, :] * (chunk - 1.0 - idx[:, None]))
    c_dec = jnp.exp(log_g * chunk)
    seq_eye = jnp.eye(n_seq, dtype=F32)
    dmask = jnp.einsum("ab,hij->haibj", seq_eye, dmask).reshape(n_heads, n_seq * chunk, n_seq * chunk)
    lanes = lambda t: jnp.repeat(t, HEAD_DIM, axis=-1)
    return dmask, jnp.tile(lanes(q_dec), (n_seq, 1)), jnp.tile(lanes(k_dec), (n_seq, 1)), lanes(c_dec[None, :])


def retention_gated(q, k, v, g, s0_bd, *, n_seq, chunk, tiles, steps_per_seq):
    t, w = q.shape
    n_pairs = w // HEAD_PAIR
    rows = n_seq * chunk * tiles
    n_steps = t // rows
    dmask, qdec, kdec, cdec = _retention_tables(w // HEAD_DIM, n_seq, chunk)
    tile_rows = n_seq * chunk
    row = pl.BlockSpec((rows, w), lambda i: (i, 0))
    st = pl.BlockSpec((n_seq, n_pairs, HEAD_PAIR, HEAD_PAIR), lambda i: (i // steps_per_seq, 0, 0, 0))
    const = lambda shape: pl.BlockSpec(shape, lambda i: (0,) * len(shape))
    return pl.pallas_call(
        functools.partial(_retention_kernel, n_seq=n_seq, chunk=chunk, tiles=tiles, steps_per_seq=steps_per_seq),
        grid=(n_steps,),
        in_specs=[row, row, row, row, st, const(dmask.shape), const((tile_rows, w)), const((tile_rows, w)),
                  const((1, w))],
        out_specs=[row, st],
        out_shape=[jax.ShapeDtypeStruct((t, w), BF16), jax.ShapeDtypeStruct(s0_bd.shape, F32)],
        scratch_shapes=[pltpu.VMEM((n_seq, n_pairs, HEAD_PAIR, HEAD_PAIR), F32)],
        compiler_params=_params("arbitrary"),
    )(q, k, v, g, s0_bd, dmask, qdec, kdec, cdec)


def _states_to_block_diag(s):
    b, h, dk, dv = s.shape
    sp = s.reshape(b, h // 2, 2, dk, dv)
    z = jnp.zeros_like(sp[:, :, 0])
    top = jnp.concatenate([sp[:, :, 0], z], axis=-1)
    bot = jnp.concatenate([z, sp[:, :, 1]], axis=-1)
    return jnp.concatenate([top, bot], axis=-2)


def _block_diag_to_states(s):
    b, p = s.shape[:2]
    a = s[:, :, :HEAD_DIM, :HEAD_DIM]
    c = s[:, :, HEAD_DIM:, HEAD_DIM:]
    return jnp.stack([a, c], axis=2).reshape(b, 2 * p, HEAD_DIM, HEAD_DIM)


def _merge_ln_kernel(a_ref, r_ref, wa_ref, wr_ref, x_ref, g_ref, b_ref, o_ref, *, alpha):
    mix = _dot(a_ref[...].astype(BF16), wa_ref[...]) + _dot(r_ref[...].astype(BF16), wr_ref[...])
    o_ref[...] = _layer_norm(alpha * x_ref[...] + mix, g_ref[...], b_ref[...])


def merge_ln(attn, ret, w_out, x, g, b, *, layer, alpha, tm):
    t, d = x.shape
    w = attn.shape[1]
    row = lambda n: pl.BlockSpec((tm, n), lambda i: (i, 0))
    const = lambda r, c: pl.BlockSpec((r, c), lambda i: (0, 0))
    return pl.pallas_call(
        functools.partial(_merge_ln_kernel, alpha=alpha),
        grid=(t // tm,),
        in_specs=[row(w), row(w), _layer_weight(layer, w, d, 0), _layer_weight(layer, w, d, 1), row(d),
                  const(1, d), const(1, d)],
        out_specs=row(d),
        out_shape=jax.ShapeDtypeStruct((t, d), F32),
        compiler_params=_params("parallel"),
    )(attn, ret, w_out, w_out, x, g, b)


def _s5_scan_kernel(x_ref, s0re_ref, s0im_ref, wbre_ref, wbim_ref, are_ref, aim_ref, wcre_ref, wcim_ref, d_ref,
                    y_ref, sre_ref, sim_ref, xs_ref, bre_ref, bim_ref, stre_ref, stim_ref, *, pack):
    step = pl.program_id(0)
    nb, tt, d = x_ref.shape
    b8 = xs_ref.shape[1] // tt
    ns = stre_ref.shape[1]
    n_kb = d // LANES
    per_kb = ns // n_kb
    state_rows = slice((pack - 1) * nb, pack * nb)

    @pl.when(step == 0)
    def _():
        stre_ref[...] = jnp.zeros_like(stre_ref)
        stim_ref[...] = jnp.zeros_like(stim_ref)
        stre_ref[state_rows, :] = s0re_ref[...]
        stim_ref[state_rows, :] = s0im_ref[...]
        xs_ref[...] = jnp.zeros_like(xs_ref)

    for b in range(nb):
        for kb in range(n_kb):
            xs_ref[kb, pl.ds(b, tt, stride=b8), :] = x_ref[b, :, kb * LANES:(kb + 1) * LANES]

    n_out = wcre_ref.shape[0]
    kin = ns // n_out
    wout = d // n_out
    kb_per_chunk = kin // per_kb

    def project_in(c):
        for kb in range(c * kb_per_chunk, (c + 1) * kb_per_chunk):
            xk = xs_ref[kb].astype(BF16)
            cols = slice(kb * per_kb, (kb + 1) * per_kb)
            bre_ref[:, cols] = _dot(xk, wbre_ref[kb])
            bim_ref[:, cols] = _dot(xk, wbim_ref[kb])

    def scan_packed(cs):
        width = cs.stop - cs.start
        are = jnp.broadcast_to(are_ref[:, cs], (SUBLANES, width))
        aim = jnp.broadcast_to(aim_ref[:, cs], (SUBLANES, width))
        first = lax.broadcasted_iota(jnp.int32, (SUBLANES, 1), 0) < nb
        lo_re, lo_im = jnp.where(first, 0.0, are), jnp.where(first, 0.0, aim)
        hi_re = jnp.where(first, are, are * are - aim * aim)
        hi_im = jnp.where(first, aim, 2.0 * are * aim)
        sre, sim = stre_ref[:, cs], stim_ref[:, cs]
        for v in range(tt // pack):
            rows = slice(v * SUBLANES, (v + 1) * SUBLANES)
            bre, bim = bre_ref[rows, cs], bim_ref[rows, cs]
            rre, rim = pltpu.roll(bre, nb, 0), pltpu.roll(bim, nb, 0)
            pre = jnp.where(first, pltpu.roll(sre, nb, 0), sre)
            pim = jnp.where(first, pltpu.roll(sim, nb, 0), sim)
            sre = bre + (lo_re * rre - lo_im * rim) + (hi_re * pre - hi_im * pim)
            sim = bim + (lo_re * rim + lo_im * rre) + (hi_re * pim + hi_im * pre)
            bre_ref[rows, cs] = sre
            bim_ref[rows, cs] = sim
        stre_ref[:, cs] = sre
        stim_ref[:, cs] = sim

    def scan(c):
        if pack == 2:
            half = kin // 2
            for c0 in range(c * kin, (c + 1) * kin, half):
                scan_packed(slice(c0, c0 + half))
            return
        cs = slice(c * kin, (c + 1) * kin)
        are = jnp.broadcast_to(are_ref[:, cs], (SUBLANES, kin))
        aim = jnp.broadcast_to(aim_ref[:, cs], (SUBLANES, kin))
        for r0 in range(0, b8, SUBLANES):
            sre = stre_ref[r0:r0 + SUBLANES, cs]
            sim = stim_ref[r0:r0 + SUBLANES, cs]
            for t in range(tt):
                rows = slice(t * b8 + r0, t * b8 + r0 + SUBLANES)
                sre, sim = (are * sre - aim * sim + bre_ref[rows, cs], are * sim + aim * sre + bim_ref[rows, cs])
                bre_ref[rows, cs] = sre
                bim_ref[rows, cs] = sim
            stre_ref[r0:r0 + SUBLANES, cs] = sre
            stim_ref[r0:r0 + SUBLANES, cs] = sim

    def project_out(c):
        cs = slice(c * kin, (c + 1) * kin)
        y = _dot(bre_ref[:, cs].astype(BF16), wcre_ref[c]) + _dot(bim_ref[:, cs].astype(BF16), wcim_ref[c])
        for i in range(wout // LANES):
            kb = c * (wout // LANES) + i
            xs_ref[kb] = y[:, i * LANES:(i + 1) * LANES] + xs_ref[kb] * d_ref[:, kb * LANES:(kb + 1) * LANES]

    project_in(0)
    for c in range(n_out):
        if c + 1 < n_out:
            project_in(c + 1)
        scan(c)
        if c >= 1:
            project_out(c - 1)
    project_out(n_out - 1)

    for b in range(nb):
        for kb in range(n_kb):
            y_ref[b, :, kb * LANES:(kb + 1) * LANES] = xs_ref[kb, pl.ds(b, tt, stride=b8), :]

    @pl.when(step == pl.num_programs(0) - 1)
    def _():
        sre_ref[...] = stre_ref[state_rows, :]
        sim_ref[...] = stim_ref[state_rows, :]


def _s5_weights(lam_re, lam_im, b_re, b_im, c_re, c_im, log_dt):
    g, p, h = b_re.shape
    lam = lax.complex(lam_re.astype(F32), lam_im.astype(F32))
    dt = jnp.exp(log_dt.astype(F32))[:, None]
    lam_bar = jnp.exp(lam * dt)
    b_bar = ((lam_bar - 1.0) / lam)[:, :, None] * lax.complex(b_re.astype(F32), b_im.astype(F32))
    gpk = LANES // h
    n_kb = g // gpk
    eye = jnp.eye(gpk, dtype=F32)

    def in_blocks(m):
        m = m.reshape(n_kb, gpk, p, h)
        return jnp.einsum("kgph,gf->kghfp", m, eye).reshape(n_kb, gpk * h, gpk * p).astype(BF16)

    gpo = 2 * LANES // h
    n_out = g // gpo
    eye_o = jnp.eye(gpo, dtype=F32)

    def out_blocks(m):
        m = m.reshape(n_out, gpo, h, p)
        return jnp.einsum("kghp,gf->kgpfh", m, eye_o).reshape(n_out, gpo * p, gpo * h).astype(BF16)

    return (in_blocks(jnp.real(b_bar)), in_blocks(jnp.imag(b_bar)),
            jnp.real(lam_bar).reshape(1, g * p), jnp.imag(lam_bar).reshape(1, g * p),
            out_blocks(c_re.astype(F32)), out_blocks(-c_im.astype(F32)))


def s5_scan(x, s0_re, s0_im, weights, d_skip, *, tt):
    nb, seq, d = x.shape
    wbre, wbim, are, aim, wcre, wcim = weights
    ns = are.shape[1]
    pack = 2 if (2 * nb == SUBLANES and tt % 2 == 0) else 1
    b8 = nb if pack == 2 else -(-nb // SUBLANES) * SUBLANES
    rows = tt * b8
    state_rows = SUBLANES if pack == 2 else b8
    const = lambda a: pl.BlockSpec(a.shape, lambda i: (0,) * a.ndim)
    xblk = pl.BlockSpec((nb, tt, d), lambda i: (0, i, 0))
    sblk = pl.BlockSpec((nb, ns), lambda i: (0, 0))
    return pl.pallas_call(
        functools.partial(_s5_scan_kernel, pack=pack),
        grid=(seq // tt,),
        in_specs=[xblk, sblk, sblk, const(wbre), const(wbim), const(are), const(aim), const(wcre), const(wcim),
                  pl.BlockSpec((1, d), lambda i: (0, 0))],
        out_specs=[xblk, sblk, sblk],
        out_shape=[jax.ShapeDtypeStruct((nb, seq, d), F32), jax.ShapeDtypeStruct((nb, ns), F32),
                   jax.ShapeDtypeStruct((nb, ns), F32)],
        scratch_shapes=[pltpu.VMEM((d // LANES, rows, LANES), F32), pltpu.VMEM((rows, ns), F32),
                        pltpu.VMEM((rows, ns), F32),
                        pltpu.VMEM((state_rows, ns), F32), pltpu.VMEM((state_rows, ns), F32)],
        compiler_params=_params("arbitrary"),
    )(x, s0_re, s0_im, wbre, wbim, are, aim, wcre, wcim, d_skip)


def _s5_out_ln_kernel(y_ref, wo_ref, wg_ref, x_ref, g_ref, b_ref, o_ref, *, alpha):
    gl = jax.nn.gelu(y_ref[...]).astype(BF16)
    mix = _dot(gl, wo_ref[...]) * jax.nn.sigmoid(_dot(gl, wg_ref[...]))
    o_ref[...] = _layer_norm(alpha * x_ref[...] + mix, g_ref[...], b_ref[...])


def s5_out_ln(y, w_out, w_gate, x, g, b, *, layer, alpha, tm):
    t, d = x.shape
    row = pl.BlockSpec((tm, d), lambda i: (i, 0))
    const = lambda r, c: pl.BlockSpec((r, c), lambda i: (0, 0))
    return pl.pallas_call(
        functools.partial(_s5_out_ln_kernel, alpha=alpha),
        grid=(t // tm,),
        in_specs=[row, _layer_weight(layer, d, d), _layer_weight(layer, d, d), row, const(1, d), const(1, d)],
        out_specs=row,
        out_shape=jax.ShapeDtypeStruct((t, d), F32),
        compiler_params=_params("parallel"),
    )(y, w_out, w_gate, x, g, b)


def _rope_tables(pos, n_heads):
    half = HEAD_DIM // 2
    inv = ROPE_THETA ** (-jnp.arange(half, dtype=F32) / half)
    ang = pos.astype(F32)[:, None] * inv[None, :]
    cos = jnp.cos(ang)
    sin = jnp.sin(ang)
    cos_h = jnp.concatenate([cos, cos], axis=-1)
    sin_h = jnp.concatenate([-sin, sin], axis=-1)
    return jnp.tile(cos_h, (1, n_heads)), jnp.tile(sin_h, (1, n_heads))


def kernel(x_prompt, x_sample, cache_k, cache_v, page_table, state_ret, state_s5_re, state_s5_im, ffn1_w_gate, ffn1_w_up, ffn1_w_down, ffn2_w_gate, ffn2_w_up, ffn2_w_down, ln_g, ln_b, w_in_ab, w_out_ab, s5_lam_re, s5_lam_im, s5_b_re, s5_b_im, s5_c_re, s5_c_im, s5_d, s5_log_dt, s5_w_out, s5_w_gate):
    bp, lp, d = x_prompt.shape
    bs, ls, _ = x_sample.shape
    depth = ffn1_w_gate.shape[0]
    n_layers_ab, n_pool, page, a_heads, hd = cache_k.shape
    n_pages = page_table.shape[1]
    past_len = n_pages * page
    half = a_heads * hd
    n_heads = half // HEAD_DIM
    assert hd == HEAD_DIM and lp % MOBA_BLOCK == 0 and past_len % MOBA_BLOCK == 0 and ls <= MOBA_BLOCK
    assert MOBA_BLOCK % page == 0 and lp % page == 0 and lp % RET_CHUNK == 0
    alpha = (2 * depth) ** 0.25
    tp, ts = bp * lp, bs * ls
    tm_p = min(1024, lp)
    tm_proj = min(512, lp)

    xp = x_prompt.reshape(tp, d)
    xs = x_sample.reshape(ts, d)
    bf = lambda a: a.astype(BF16)
    f1 = (bf(ffn1_w_gate), bf(ffn1_w_up), bf(ffn1_w_down))
    f2 = (bf(ffn2_w_gate), bf(ffn2_w_up), bf(ffn2_w_down))
    w_in, w_out = bf(w_in_ab), bf(w_out_ab)
    w_o, w_g = bf(s5_w_out), bf(s5_w_gate)
    cos_p, sin_p = _rope_tables(jnp.arange(lp, dtype=jnp.int32), n_heads)
    cos_s, sin_s = _rope_tables(past_len + jnp.arange(ls, dtype=jnp.int32), n_heads)
    cos_s, sin_s = jnp.tile(cos_s, (bs, 1)), jnp.tile(sin_s, (bs, 1))
    pool_kt = jnp.transpose(cache_k, (0, 1, 3, 4, 2)).reshape(n_layers_ab * n_pool, half, page)
    pool_vt = jnp.transpose(cache_v, (0, 1, 3, 4, 2)).reshape(n_layers_ab * n_pool, half, page)
    seq_per_tile = max(RET_CHUNK // ls, 1)
    ret_tiles_p = min(4, lp // RET_CHUNK)

    kpg = jnp.zeros((n_layers_ab, tp // page, half, page), F32)
    vpg = jnp.zeros((n_layers_ab, tp // page, half, page), F32)
    k_s, v_s, r_p, r_s = [], [], [], []
    sre_p, sim_p, sre_s, sim_s = [], [], [], []
    for layer in range(depth):
        li = layer // 2
        g = lambda i: ln_g[layer, i][None, :]
        b = lambda i: ln_b[layer, i][None, :]
        xp = ffn_ln(xp, *f1, g(0), b(0), layer=layer, alpha=alpha, tm=tm_p)
        xs = ffn_ln(xs, *f1, g(0), b(0), layer=layer, alpha=alpha, tm=ts)
        if layer % 2 == 0:
            qa, kbf, vt, ksum, kpg, vpg, qb, kb, vb, gb = ab_project(xp, w_in, cos_p, sin_p, layer=li, tm=tm_proj,
                                                                     seq_len=lp, pages=(kpg, vpg), page=page)
            attn = moba_prompt(qa, ksum, kbf, vt, batch=bp, seq_len=lp)
            zero_state = jnp.zeros((bp, n_heads // 2, HEAD_PAIR, HEAD_PAIR), F32)
            ret, s_fin = retention_gated(qb, kb, vb, gb, zero_state, n_seq=1, chunk=RET_CHUNK, tiles=ret_tiles_p,
                                         steps_per_seq=lp // (RET_CHUNK * ret_tiles_p))
            xp = merge_ln(attn, ret, w_out, xp, g(1), b(1), layer=li, alpha=alpha, tm=tm_p)
            r_p.append(_block_diag_to_states(s_fin))
            qa, ka, va, qb, kb, vb, gb = ab_project(xs, w_in, cos_s, sin_s, layer=li, tm=ts, seq_len=ts)
            page_ids = (page_table.astype(jnp.int32) + li * n_pool).reshape(-1)
            attn = moba_sample(qa, ka, va, pool_kt, pool_vt, page_ids, batch=bs, n_pages=n_pages)
            ret, s_fin = retention_gated(qb, kb, vb, gb, _states_to_block_diag(state_ret[li]), n_seq=seq_per_tile,
                                         chunk=ls, tiles=1, steps_per_seq=1)
            xs = merge_ln(attn, ret, w_out, xs, g(1), b(1), layer=li, alpha=alpha, tm=ts)
            k_s.append(ka.reshape(bs, ls, a_heads, hd))
            v_s.append(va.reshape(bs, ls, a_heads, hd))
            r_s.append(_block_diag_to_states(s_fin))
        else:
            weights = _s5_weights(s5_lam_re[li], s5_lam_im[li], s5_b_re[li], s5_b_im[li], s5_c_re[li],
                                  s5_c_im[li], s5_log_dt[li])
            n_state = weights[2].shape[1]
            d_skip = s5_d[li][None, :]
            zero = jnp.zeros((bp, n_state), F32)
            y, a_re, a_im = s5_scan(xp.reshape(bp, lp, d), zero, zero, weights, d_skip, tt=min(64, lp))
            xp = s5_out_ln(y.reshape(tp, d), w_o, w_g, xp, g(1), b(1), layer=li, alpha=alpha, tm=tm_p)
            sre_p.append(a_re.reshape(bp, -1, S5_STATE))
            sim_p.append(a_im.reshape(bp, -1, S5_STATE))
            y, a_re, a_im = s5_scan(xs.reshape(bs, ls, d), state_s5_re[li].reshape(bs, n_state),
                                    state_s5_im[li].reshape(bs, n_state), weights, d_skip, tt=ls)
            xs = s5_out_ln(y.reshape(ts, d), w_o, w_g, xs, g(1), b(1), layer=li, alpha=alpha, tm=ts)
            sre_s.append(a_re.reshape(bs, -1, S5_STATE))
            sim_s.append(a_im.reshape(bs, -1, S5_STATE))
        xp = ffn_ln(xp, *f2, g(2), b(2), layer=layer, alpha=alpha, tm=tm_p)
        xs = ffn_ln(xs, *f2, g(2), b(2), layer=layer, alpha=alpha, tm=ts)
    unpage = lambda t: jnp.transpose(t.reshape(n_layers_ab, bp, lp // page, a_heads, hd, page), (0, 1, 2, 5, 3, 4))
    return (xp.reshape(bp, lp, d), xs.reshape(bs, ls, d), unpage(kpg), unpage(vpg), jnp.stack(k_s),
            jnp.stack(v_s), jnp.stack(r_p), jnp.stack(r_s), jnp.stack(sre_p), jnp.stack(sim_p),
            jnp.stack(sre_s), jnp.stack(sim_s))
```

```python
import functools
import math
from typing import NamedTuple

import jax
import jax.numpy as jnp
from jax import lax
from jax.experimental import pallas as pl
from jax.experimental.pallas import tpu as pltpu

F32 = jnp.float32
BF16 = jnp.bfloat16

HEAD_DIM = 64
HEAD_PAIR = 2 * HEAD_DIM
MOBA_BLOCK = 256
MOBA_TOPK = 3
MOBA_BLOCKS_PER_TRIP = 4
RET_CHUNK = 128
S5_GROUP = 16
S5_STATE = 64
ROPE_THETA = 10000.0
LN_EPS = 1e-5
GN_EPS = 1e-6
NEG_INF = -1e30
LOG2_E = math.log2(math.e)
SUBLANES = 8
BF16_SUBLANES = 16
LANES = 128
VMEM_LIMIT = 48 * 1024 * 1024


def _dot(a, b, precision=None):
    return jnp.dot(a, b, preferred_element_type=F32, precision=precision)


def _dot_nt(a, b, precision=None):
    return lax.dot_general(a, b, (((1,), (1,)), ((), ())), preferred_element_type=F32, precision=precision)


def _dot_tn(a, b):
    return lax.dot_general(a, b, (((0,), (0,)), ((), ())), preferred_element_type=F32)


def _layer_norm(r, g, b):
    mu = jnp.mean(r, -1, keepdims=True)
    d = r - mu
    var = jnp.mean(d * d, -1, keepdims=True)
    return d * lax.rsqrt(var + LN_EPS) * g + b


def _silu(x):
    return x * jax.nn.sigmoid(x)


def _params(*sem):
    return pltpu.CompilerParams(dimension_semantics=sem, vmem_limit_bytes=VMEM_LIMIT)


def _ffn_ln_kernel(x_ref, wg_ref, wu_ref, wd_ref, g_ref, b_ref, o_ref, a_ref, *, alpha, tf):
    xb = x_ref[...].astype(BF16)
    for c in range(wg_ref.shape[1] // tf):
        cols = slice(c * tf, (c + 1) * tf)
        hg = _dot(xb, wg_ref[:, cols])
        hu = _dot(xb, wu_ref[:, cols])
        a_ref[:, cols] = (_silu(hg) * hu).astype(BF16)
    r = alpha * x_ref[...] + 0.5 * _dot(a_ref[...], wd_ref[...])
    o_ref[...] = _layer_norm(r, g_ref[...], b_ref[...])


def _layer_weight(layer, r, c, row_blk=0, **kw):
    return pl.BlockSpec((None, r, c), lambda *_: (layer, row_blk, 0), **kw)


def ffn_ln(x, wg, wu, wd, g, b, *, layer, alpha, tm, tf=256):
    t, d = x.shape
    f = wg.shape[2]
    resident = lambda r, c: _layer_weight(layer, r, c, pipeline_mode=pl.Buffered(1))
    return pl.pallas_call(
        functools.partial(_ffn_ln_kernel, alpha=alpha, tf=tf),
        grid=(t // tm,),
        in_specs=[pl.BlockSpec((tm, d), lambda i: (i, 0)), resident(d, f), resident(d, f), resident(f, d),
                  pl.BlockSpec((1, d), lambda i: (0, 0)), pl.BlockSpec((1, d), lambda i: (0, 0))],
        out_specs=pl.BlockSpec((tm, d), lambda i: (i, 0)),
        out_shape=jax.ShapeDtypeStruct((t, d), F32),
        scratch_shapes=[pltpu.VMEM((tm, f), BF16)],
        compiler_params=_params("parallel"),
    )(x, wg, wu, wd, g, b)


def _rope(y, cos, sin_signed):
    width = y.shape[-1]
    lane = lax.broadcasted_iota(jnp.int32, (1, width), 1)
    first = (lane % HEAD_DIM) < (HEAD_DIM // 2)
    rot = jnp.where(first, pltpu.roll(y, width - HEAD_DIM // 2, 1), pltpu.roll(y, HEAD_DIM // 2, 1))
    return y * cos + rot * sin_signed


def _ab_proj_kernel(*refs, half, scale, page):
    if page is None:
        x_ref, w_ref, cos_ref, sin_ref, qa_ref, ka_ref, va_ref, qb_ref, kb_ref, vb_ref, gb_ref = refs
    else:
        (x_ref, w_ref, cos_ref, sin_ref, _, _, qa_ref, kbf_ref, vt_ref, ksum_ref, kpg_ref, vpg_ref,
         qb_ref, kb_ref, vb_ref, gb_ref) = refs
    xb = x_ref[...].astype(BF16)
    cos = cos_ref[...]
    sin = sin_ref[...]
    col = lambda c: _dot(xb, w_ref[:, c * half:(c + 1) * half])
    qa_ref[...] = _rope(col(0), cos, sin) * scale
    ka = _rope(col(1), cos, sin)
    va = col(2)
    if page is None:
        ka_ref[...] = ka
        va_ref[...] = va
    else:
        kbf_ref[...] = ka.astype(BF16)
        for n in range(ka.shape[0] // MOBA_BLOCK):
            blk = slice(n * MOBA_BLOCK, (n + 1) * MOBA_BLOCK)
            ksum_ref[n] = jnp.sum(ka[blk], axis=0, keepdims=True)
            vt_ref[n] = va[blk].T.astype(BF16)
        for n in range(ka.shape[0] // page):
            rows = slice(n * page, (n + 1) * page)
            kpg_ref[n] = ka[rows].T
            vpg_ref[n] = va[rows].T
    qb_ref[...] = _rope(col(3), cos, sin)
    kb_ref[...] = _rope(col(4), cos, sin) * scale
    vb_ref[...] = col(5)
    gb_ref[...] = col(6)


def ab_project(x, w_in, cos, sin, *, layer, tm, seq_len, pages=None, page=None):
    t, d = x.shape
    half = w_in.shape[2] // 7
    tiles_per_seq = max(seq_len // tm, 1)
    tab = pl.BlockSpec((tm, half), lambda i: (i % tiles_per_seq, 0))
    row = pl.BlockSpec((tm, half), lambda i: (i, 0))
    f32o = jax.ShapeDtypeStruct((t, half), F32)
    in_specs = [pl.BlockSpec((tm, d), lambda i: (i, 0)), _layer_weight(layer, d, 7 * half), tab, tab]
    operands = [x, w_in, cos, sin]
    aliases = {}
    if pages is None:
        out_specs = [row] * 7
        out_shape = [f32o] * 7
    else:
        nblk = tm // MOBA_BLOCK
        page_blk = pl.BlockSpec((None, tm // page, half, page), lambda i: (layer, i, 0, 0))
        in_specs += [pl.BlockSpec(memory_space=pl.ANY)] * 2
        operands += list(pages)
        aliases = {4: 4, 5: 5}
        out_specs = [row, row,
                     pl.BlockSpec((nblk, half, MOBA_BLOCK), lambda i: (i, 0, 0)),
                     pl.BlockSpec((nblk, 1, half), lambda i: (i, 0, 0)),
                     page_blk, page_blk, row, row, row, row]
        out_shape = [f32o, jax.ShapeDtypeStruct((t, half), BF16),
                     jax.ShapeDtypeStruct((t // MOBA_BLOCK, half, MOBA_BLOCK), BF16),
                     jax.ShapeDtypeStruct((t // MOBA_BLOCK, 1, half), F32),
                     jax.ShapeDtypeStruct(pages[0].shape, F32), jax.ShapeDtypeStruct(pages[1].shape, F32),
                     f32o, f32o, f32o, f32o]
    return pl.pallas_call(
        functools.partial(_ab_proj_kernel, half=half, scale=HEAD_DIM ** -0.5, page=page),
        grid=(t // tm,),
        in_specs=in_specs,
        out_specs=out_specs,
        out_shape=out_shape,
        input_output_aliases=aliases,
        compiler_params=_params("parallel"),
    )(*operands)


def _topk_select(gate, blk, n_valid, axis):
    gate = jnp.where(blk < n_valid, gate, NEG_INF)
    taken = jnp.zeros(gate.shape, F32)
    blk_f = blk.astype(F32)
    for _ in range(MOBA_TOPK):
        top = jnp.max(gate, axis=axis, keepdims=True)
        first = jnp.min(jnp.where(gate == top, blk_f, float(gate.shape[axis])), axis=axis, keepdims=True)
        hit = blk_f == first
        taken = jnp.where(hit, 1.0, taken)
        gate = jnp.where(hit, -jnp.inf, gate)
    return jnp.where(blk < n_valid, taken, 0.0)


def _moba_prompt_kernel(q_ref, ksum_ref, k_ref, vt_ref, o_ref, qm_ref, sel_ref, m_ref, l_ref, acc_ref, s_ref):
    j = pl.program_id(1)
    tq = q_ref.shape[0]
    n_heads = q_ref.shape[1] // HEAD_DIM
    lane = lax.broadcasted_iota(jnp.int32, (1, HEAD_PAIR), 1)
    key_i = lax.broadcasted_iota(jnp.int32, (MOBA_BLOCK, tq), 0)
    qry_i = lax.broadcasted_iota(jnp.int32, (MOBA_BLOCK, tq), 1)
    causal = key_i <= qry_i

    nb = ksum_ref.shape[0]
    blk = lax.broadcasted_iota(jnp.int32, (nb, 1), 0)
    for hp in range(n_heads // 2):
        lanes = slice(hp * HEAD_PAIR, (hp + 1) * HEAD_PAIR)
        qp = q_ref[:, lanes]
        kmean = ksum_ref[:, 0, lanes] * (1.0 / MOBA_BLOCK)
        masks = [(lane // HEAD_DIM) == half for half in range(2)]
        gate = _dot_nt(jnp.concatenate([jnp.where(hm, kmean, 0.0) for hm in masks], axis=0), qp,
                       precision=lax.Precision.HIGHEST)
        for half, hm in enumerate(masks):
            h = 2 * hp + half
            sel_ref[h] = _topk_select(gate[half * nb:(half + 1) * nb], blk, j, 0)
            qm_ref[h] = jnp.where(hm, qp * LOG2_E, 0.0).astype(BF16)

    ones_rows = jnp.ones((BF16_SUBLANES, MOBA_BLOCK), BF16)

    def update(blocks, first):
        def rows0(n):
            return pl.multiple_of((j if n is None else n) * MOBA_BLOCK, MOBA_BLOCK)

        def picked(n, h):
            return sel_ref[h, pl.ds(n, 1), :] > 0.0

        def scores(slot, n, h):
            k_pair = k_ref[pl.ds(rows0(n), MOBA_BLOCK), (h // 2) * HEAD_PAIR:(h // 2 + 1) * HEAD_PAIR]
            s_ref[slot, h] = _dot_nt(k_pair, qm_ref[h])

        def softmax_values(slot, n, h):
            if first:
                s = jnp.where(causal, s_ref[slot, h], NEG_INF)
                m_new = jnp.max(s, axis=0, keepdims=True)
            else:
                s = s_ref[slot, h]
                cmax = jnp.where(picked(n, h), jnp.max(s, axis=0, keepdims=True), NEG_INF)
                m_new = jnp.maximum(m_ref[h], cmax)
                a = jnp.exp2(m_ref[h] - m_new)
            p = jnp.exp2(s - m_new).astype(BF16)
            m_ref[h] = m_new
            v_h = vt_ref[j if n is None else n, h * HEAD_DIM:(h + 1) * HEAD_DIM, :]
            res = _dot(jnp.concatenate([v_h, ones_rows], axis=0), p)
            pv, psum = res[:HEAD_DIM], res[HEAD_DIM:HEAD_DIM + 1]
            if first:
                acc_ref[h] = pv
                l_ref[h] = psum
            else:
                acc_ref[h] = a * acc_ref[h] + jnp.where(picked(n, h), pv, 0.0)
                l_ref[h] = a * l_ref[h] + jnp.where(picked(n, h), psum, 0.0)

        heads = range(n_heads)
        stages = (scores, softmax_values)
        for t in range(len(stages) + len(blocks) - 1):
            for h in heads:
                for slot, n in enumerate(blocks):
                    if 0 <= t - slot < len(stages):
                        stages[t - slot](slot, n, h)

    update([None], True)

    n_slots = s_ref.shape[0]

    def body(i, carry):
        update([n_slots * i + k for k in range(n_slots)], False)
        return carry

    lax.fori_loop(0, j // n_slots, body, 0)
    for rem in range(1, n_slots):
        @pl.when(j % n_slots == rem)
        def _(rem=rem):
            update([j - rem + k for k in range(rem)], False)

    out_t = jnp.concatenate([acc_ref[h] / l_ref[h] for h in range(n_heads)], axis=0)
    o_ref[...] = out_t.T.astype(o_ref.dtype)


def moba_prompt(q, ksum, kbf, vt, *, batch, seq_len):
    t, w = q.shape
    nb = seq_len // MOBA_BLOCK
    n_heads = w // HEAD_DIM
    return pl.pallas_call(
        _moba_prompt_kernel,
        grid=(batch, nb),
        in_specs=[pl.BlockSpec((MOBA_BLOCK, w), lambda b, j: (b * nb + j, 0)),
                  pl.BlockSpec((nb, 1, w), lambda b, j: (b, 0, 0)),
                  pl.BlockSpec((seq_len, w), lambda b, j: (b, 0)),
                  pl.BlockSpec((nb, w, MOBA_BLOCK), lambda b, j: (b, 0, 0))],
        out_specs=pl.BlockSpec((MOBA_BLOCK, w), lambda b, j: (b * nb + j, 0)),
        out_shape=jax.ShapeDtypeStruct((t, w), BF16),
        scratch_shapes=[pltpu.VMEM((n_heads, MOBA_BLOCK, HEAD_PAIR), BF16),
                        pltpu.VMEM((n_heads, nb, MOBA_BLOCK), F32),
                        pltpu.VMEM((n_heads, 1, MOBA_BLOCK), F32),
                        pltpu.VMEM((n_heads, 1, MOBA_BLOCK), F32),
                        pltpu.VMEM((n_heads, HEAD_DIM, MOBA_BLOCK), F32),
                        pltpu.VMEM((MOBA_BLOCKS_PER_TRIP, n_heads, MOBA_BLOCK, MOBA_BLOCK), F32)],
        compiler_params=_params("parallel", "arbitrary"),
    )(q, ksum, kbf, vt)


def _moba_sample_kernel(pt_ref, q_ref, knew_ref, vnew_ref, *rest, pages_per_step, page, n_heads):
    del pt_ref
    kpages = rest[:pages_per_step]
    vpages = rest[pages_per_step:2 * pages_per_step]
    o_ref = rest[2 * pages_per_step]
    qf_ref, qb_ref, ksum_ref, m_ref, l_ref, acc_ref, kpad_ref, vpad_ref = rest[2 * pages_per_step + 1:]
    s_idx = pl.program_id(1)
    n_steps = pl.num_programs(1)
    lq, w = q_ref.shape
    rows = n_heads * lq
    nb = acc_ref.shape[0]
    pages_per_blk = MOBA_BLOCK // page
    blks_per_step = pages_per_step // pages_per_blk
    rowhead = lax.broadcasted_iota(jnp.int32, (rows, 1), 0) // lq
    lanehead = lax.broadcasted_iota(jnp.int32, (1, w), 1) // HEAD_DIM
    blk_lane = lax.broadcasted_iota(jnp.int32, (1, LANES), 1)

    @pl.when(s_idx == 0)
    def _():
        qt = jnp.concatenate([q_ref[...]] * n_heads, axis=0)
        qbd = jnp.where(rowhead == lanehead, qt, 0.0)
        qf_ref[...] = qbd
        qb_ref[...] = qbd.astype(BF16)
        ksum_ref[...] = jnp.zeros_like(ksum_ref)
        m_ref[...] = jnp.zeros_like(m_ref)
        l_ref[...] = jnp.zeros_like(l_ref)

    qb = qb_ref[...]

    def partial_softmax(s):
        m = jnp.max(s, axis=1, keepdims=True)
        e = jnp.exp(s - m)
        return m, jnp.sum(e, axis=1, keepdims=True), e.astype(BF16)

    blks = range(blks_per_step)
    pages_of = lambda bi: range(bi * pages_per_blk, (bi + 1) * pages_per_blk)
    here = [blk_lane == s_idx * blks_per_step + bi for bi in blks]
    kt = [jnp.concatenate([kpages[i][0] for i in pages_of(bi)], axis=1) for bi in blks]
    scores = [_dot(qb, kt[bi].astype(BF16)) for bi in blks]
    ksum = ksum_ref[...]
    for bi in blks:
        ksum = jnp.where(here[bi], jnp.sum(kt[bi], axis=1, keepdims=True), ksum)
    ksum_ref[...] = ksum
    stats = [partial_softmax(scores[bi]) for bi in blks]
    for bi in blks:
        vt = jnp.concatenate([vpages[i][0] for i in pages_of(bi)], axis=1)
        acc_ref[s_idx * blks_per_step + bi] = _dot_nt(stats[bi][2], vt.astype(BF16))
    m_all, l_all = m_ref[...], l_ref[...]
    for bi in blks:
        m_all = jnp.where(here[bi], stats[bi][0], m_all)
        l_all = jnp.where(here[bi], stats[bi][1], l_all)
    m_ref[...] = m_all
    l_ref[...] = l_all

    @pl.when(s_idx == n_steps - 1)
    def _():
        kpad_ref[...] = jnp.zeros_like(kpad_ref)
        vpad_ref[...] = jnp.zeros_like(vpad_ref)
        kpad_ref[0:lq, :] = knew_ref[...]
        vpad_ref[0:lq, :] = vnew_ref[...]
        s_own = _dot_nt(qb, kpad_ref[...].astype(BF16))
        tq = lax.broadcasted_iota(jnp.int32, s_own.shape, 0) % lq
        tk = lax.broadcasted_iota(jnp.int32, s_own.shape, 1)
        m_own, l_own, e_own = partial_softmax(jnp.where(tk <= tq, s_own, NEG_INF))
        acc_own = _dot(e_own, vpad_ref[...].astype(BF16))

        gate = _dot(qf_ref[...], ksum_ref[...] * (1.0 / MOBA_BLOCK), precision=lax.Precision.HIGHEST)
        sel = _topk_select(gate[:, :nb], blk_lane[:, :nb], nb, 1) > 0.0
        m_blk = m_ref[:, :nb]
        m_all = jnp.maximum(m_own, jnp.max(jnp.where(sel, m_blk, NEG_INF), axis=1, keepdims=True))
        wgt = jnp.where(sel, jnp.exp(m_blk - m_all), 0.0)
        w_own = jnp.exp(m_own - m_all)
        den = w_own * l_own + jnp.sum(wgt * l_ref[:, :nb], axis=1, keepdims=True)
        num = w_own * acc_own
        for n in range(nb):
            num = num + wgt[:, n:n + 1] * acc_ref[n]
        o_all = num / den
        out = jnp.zeros((lq, w), F32)
        for h in range(n_heads):
            out = out + jnp.where(lanehead == h, o_all[h * lq:(h + 1) * lq], 0.0)
        o_ref[...] = out


def moba_sample(q, knew, vnew, pool_kt, pool_vt, page_ids, *, batch, n_pages, pages_per_step=16):
    t, w = q.shape
    lq = t // batch
    page = pool_kt.shape[2]
    n_heads = w // HEAD_DIM
    rows = n_heads * lq
    nb = n_pages * page // MOBA_BLOCK
    assert nb <= LANES and pages_per_step % (MOBA_BLOCK // page) == 0
    seq_blk = pl.BlockSpec((lq, w), lambda b, s, pt: (b, 0))

    def page_spec(i):
        return pl.BlockSpec((1, w, page), lambda b, s, pt: (pt[b * n_pages + s * pages_per_step + i], 0, 0))

    grid_spec = pltpu.PrefetchScalarGridSpec(
        num_scalar_prefetch=1,
        grid=(batch, n_pages // pages_per_step),
        in_specs=[seq_blk, seq_blk, seq_blk] + [page_spec(i) for i in range(pages_per_step)] * 2,
        out_specs=seq_blk,
        scratch_shapes=[pltpu.VMEM((rows, w), F32), pltpu.VMEM((rows, w), BF16), pltpu.VMEM((w, LANES), F32),
                        pltpu.VMEM((rows, LANES), F32), pltpu.VMEM((rows, LANES), F32),
                        pltpu.VMEM((nb, rows, w), F32),
                        pltpu.VMEM((LANES, w), F32), pltpu.VMEM((LANES, w), F32)],
    )
    return pl.pallas_call(
        functools.partial(_moba_sample_kernel, pages_per_step=pages_per_step, page=page, n_heads=n_heads),
        grid_spec=grid_spec,
        out_shape=jax.ShapeDtypeStruct((t, w), F32),
        compiler_params=_params("parallel", "arbitrary"),
    )(page_ids, q, knew, vnew, *([pool_kt] * pages_per_step), *([pool_vt] * pages_per_step))


def _retention_kernel(q_ref, k_ref, v_ref, g_ref, s0_ref, dmask_ref, qdec_ref, kdec_ref, cdec_ref,
                      o_ref, sout_ref, st_ref, *, n_seq, chunk, tiles, steps_per_seq):
    step = pl.program_id(0)
    tile_rows = n_seq * chunk
    n_pairs = q_ref.shape[1] // HEAD_PAIR
    lane = lax.broadcasted_iota(jnp.int32, (1, HEAD_PAIR), 1)
    rowh = lax.broadcasted_iota(jnp.int32, (HEAD_PAIR, 1), 0) // HEAD_DIM
    same_head = rowh == (lane // HEAD_DIM)
    rseq = lax.broadcasted_iota(jnp.int32, (tile_rows, 1), 0) // chunk

    @pl.when(step % steps_per_seq == 0)
    def _():
        st_ref[...] = s0_ref[...]

    pairs = range(n_pairs)
    lanes_of = lambda hp: slice(hp * HEAD_PAIR, (hp + 1) * HEAD_PAIR)
    half_masks = [(lane // HEAD_DIM) == half for half in range(2)]
    for ti in range(tiles):
        rows = slice(ti * tile_rows, (ti + 1) * tile_rows)
        qb = [q_ref[rows, lanes_of(hp)].astype(BF16) for hp in pairs]
        kf = [k_ref[rows, lanes_of(hp)] for hp in pairs]
        kb = [k.astype(BF16) for k in kf]
        vb = [v_ref[rows, lanes_of(hp)].astype(BF16) for hp in pairs]
        inner = [[_dot_nt(jnp.where(hm, qb[hp], jnp.zeros_like(qb[hp])), kb[hp]) for hm in half_masks]
                 for hp in pairs]
        scaled = [[(inner[hp][half] * dmask_ref[2 * hp + half]).astype(BF16) for half in range(2)] for hp in pairs]
        ret = [sum(jnp.where(half_masks[half], _dot(scaled[hp][half], vb[hp]), 0.0) for half in range(2))
               for hp in pairs]
        for hp in pairs:
            kd = (kf[hp] * kdec_ref[:, lanes_of(hp)]).astype(BF16)
            cdec = cdec_ref[:, lanes_of(hp)]
            carried = []
            for s in range(n_seq):
                srows = slice(s * chunk, (s + 1) * chunk)
                state = st_ref[s, hp]
                carried.append(_dot(qb[hp][srows], state.astype(BF16)))
                kd_s = kd if n_seq == 1 else jnp.where(rseq == s, kd, jnp.zeros_like(kd))
                st_ref[s, hp] = cdec * state + jnp.where(same_head, _dot_tn(kd_s, vb[hp]), 0.0)
            carried = carried[0] if n_seq == 1 else jnp.concatenate(carried, axis=0)
            ret[hp] = ret[hp] + carried * qdec_ref[:, lanes_of(hp)]
        for hp in pairs:
            normed = jnp.zeros_like(ret[hp])
            for hm in half_masks:
                mu = jnp.sum(jnp.where(hm, ret[hp], 0.0), axis=1, keepdims=True) * (1.0 / HEAD_DIM)
                d = jnp.where(hm, ret[hp] - mu, 0.0)
                var = jnp.sum(d * d, axis=1, keepdims=True) * (1.0 / HEAD_DIM)
                normed = normed + d * lax.rsqrt(var + GN_EPS)
            o_ref[rows, lanes_of(hp)] = (_silu(g_ref[rows, lanes_of(hp)]) * normed).astype(o_ref.dtype)

    @pl.when(step % steps_per_seq == steps_per_seq - 1)
    def _():
        sout_ref[...] = st_ref[...]


def _retention_tables(n_heads, n_seq, chunk):
    log_g = jnp.log1p(-jnp.exp2(-5.0 - jnp.arange(n_heads, dtype=F32)))
    idx = jnp.arange(chunk, dtype=F32)
    diff = idx[:, None] - idx[None, :]
    dmask = jnp.where(diff >= 0, jnp.exp(log_g[:, None, None] * jnp.maximum(diff, 0.0)), 0.0)
    q_dec = jnp.exp(log_g[None, :] * (idx[:, None] + 1.0))
    k_dec = jnp.exp(log_g[None, :] * (chunk - 1.0 - idx[:, None]))
    c_dec = jnp.exp(log_g * chunk)
    seq_eye = jnp.eye(n_seq, dtype=F32)
    dmask = jnp.einsum("ab,hij->haibj", seq_eye, dmask).reshape(n_heads, n_seq * chunk, n_seq * chunk)
    lanes = lambda t: jnp.repeat(t, HEAD_DIM, axis=-1)
    return dmask, jnp.tile(lanes(q_dec), (n_seq, 1)), jnp.tile(lanes(k_dec), (n_seq, 1)), lanes(c_dec[None, :])


def retention_gated(q, k, v, g, s0_bd, *, n_seq, chunk, tiles, steps_per_seq):
    t, w = q.shape
    n_pairs = w // HEAD_PAIR
    rows = n_seq * chunk * tiles
    n_steps = t // rows
    dmask, qdec, kdec, cdec = _retention_tables(w // HEAD_DIM, n_seq, chunk)
    tile_rows = n_seq * chunk
    row = pl.BlockSpec((rows, w), lambda i: (i, 0))
    st = pl.BlockSpec((n_seq, n_pairs, HEAD_PAIR, HEAD_PAIR), lambda i: (i // steps_per_seq, 0, 0, 0))
    const = lambda shape: pl.BlockSpec(shape, lambda i: (0,) * len(shape))
    return pl.pallas_call(
        functools.partial(_retention_kernel, n_seq=n_seq, chunk=chunk, tiles=tiles, steps_per_seq=steps_per_seq),
        grid=(n_steps,),
        in_specs=[row, row, row, row, st, const(dmask.shape), const((tile_rows, w)), const((tile_rows, w)),
                  const((1, w))],
        out_specs=[row, st],
        out_shape=[jax.ShapeDtypeStruct((t, w), BF16), jax.ShapeDtypeStruct(s0_bd.shape, F32)],
        scratch_shapes=[pltpu.VMEM((n_seq, n_pairs, HEAD_PAIR, HEAD_PAIR), F32)],
        compiler_params=_params("arbitrary"),
    )(q, k, v, g, s0_bd, dmask, qdec, kdec, cdec)


def _states_to_block_diag(s):
    b, h, dk, dv = s.shape
    sp = s.reshape(b, h // 2, 2, dk, dv)
    z = jnp.zeros_like(sp[:, :, 0])
    top = jnp.concatenate([sp[:, :, 0], z], axis=-1)
    bot = jnp.concatenate([z, sp[:, :, 1]], axis=-1)
    return jnp.concatenate([top, bot], axis=-2)


def _block_diag_to_states(s):
    b, p = s.shape[:2]
    a = s[:, :, :HEAD_DIM, :HEAD_DIM]
    c = s[:, :, HEAD_DIM:, HEAD_DIM:]
    return jnp.stack([a, c], axis=2).reshape(b, 2 * p, HEAD_DIM, HEAD_DIM)


def _merge_ln_kernel(a_ref, r_ref, wa_ref, wr_ref, x_ref, g_ref, b_ref, o_ref, *, alpha):
    mix = _dot(a_ref[...].astype(BF16), wa_ref[...]) + _dot(r_ref[...].astype(BF16), wr_ref[...])
    o_ref[...] = _layer_norm(alpha * x_ref[...] + mix, g_ref[...], b_ref[...])


def merge_ln(attn, ret, w_out, x, g, b, *, layer, alpha, tm):
    t, d = x.shape
    w = attn.shape[1]
    row = lambda n: pl.BlockSpec((tm, n), lambda i: (i, 0))
    const = lambda r, c: pl.BlockSpec((r, c), lambda i: (0, 0))
    return pl.pallas_call(
        functools.partial(_merge_ln_kernel, alpha=alpha),
        grid=(t // tm,),
        in_specs=[row(w), row(w), _layer_weight(layer, w, d, 0), _layer_weight(layer, w, d, 1), row(d),
                  const(1, d), const(1, d)],
        out_specs=row(d),
        out_shape=jax.ShapeDtypeStruct((t, d), F32),
        compiler_params=_params("parallel"),
    )(attn, ret, w_out, w_out, x, g, b)


def _s5_scan_kernel(x_ref, s0re_ref, s0im_ref, wbre_ref, wbim_ref, are_ref, aim_ref, wcre_ref, wcim_ref, d_ref,
                    y_ref, sre_ref, sim_ref, xs_ref, bre_ref, bim_ref, stre_ref, stim_ref, *, pack):
    step = pl.program_id(0)
    nb, tt, d = x_ref.shape
    b8 = xs_ref.shape[1] // tt
    ns = stre_ref.shape[1]
    n_kb = d // LANES
    per_kb = ns // n_kb
    state_rows = slice((pack - 1) * nb, pack * nb)

    @pl.when(step == 0)
    def _():
        stre_ref[...] = jnp.zeros_like(stre_ref)
        stim_ref[...] = jnp.zeros_like(stim_ref)
        stre_ref[state_rows, :] = s0re_ref[...]
        stim_ref[state_rows, :] = s0im_ref[...]
        xs_ref[...] = jnp.zeros_like(xs_ref)

    for b in range(nb):
        for kb in range(n_kb):
            xs_ref[kb, pl.ds(b, tt, stride=b8), :] = x_ref[b, :, kb * LANES:(kb + 1) * LANES]

    n_out = wcre_ref.shape[0]
    kin = ns // n_out
    wout = d // n_out
    kb_per_chunk = kin // per_kb

    def project_in(c):
        for kb in range(c * kb_per_chunk, (c + 1) * kb_per_chunk):
            xk = xs_ref[kb].astype(BF16)
            cols = slice(kb * per_kb, (kb + 1) * per_kb)
            bre_ref[:, cols] = _dot(xk, wbre_ref[kb])
            bim_ref[:, cols] = _dot(xk, wbim_ref[kb])

    def scan_packed(cs):
        width = cs.stop - cs.start
        are = jnp.broadcast_to(are_ref[:, cs], (SUBLANES, width))
        aim = jnp.broadcast_to(aim_ref[:, cs], (SUBLANES, width))
        first = lax.broadcasted_iota(jnp.int32, (SUBLANES, 1), 0) < nb
        lo_re, lo_im = jnp.where(first, 0.0, are), jnp.where(first, 0.0, aim)
        hi_re = jnp.where(first, are, are * are - aim * aim)
        hi_im = jnp.where(first, aim, 2.0 * are * aim)
        sre, sim = stre_ref[:, cs], stim_ref[:, cs]
        for v in range(tt // pack):
            rows = slice(v * SUBLANES, (v + 1) * SUBLANES)
            bre, bim = bre_ref[rows, cs], bim_ref[rows, cs]
            rre, rim = pltpu.roll(bre, nb, 0), pltpu.roll(bim, nb, 0)
            pre = jnp.where(first, pltpu.roll(sre, nb, 0), sre)
            pim = jnp.where(first, pltpu.roll(sim, nb, 0), sim)
            sre = bre + (lo_re * rre - lo_im * rim) + (hi_re * pre - hi_im * pim)
            sim = bim + (lo_re * rim + lo_im * rre) + (hi_re * pim + hi_im * pre)
            bre_ref[rows, cs] = sre
            bim_ref[rows, cs] = sim
        stre_ref[:, cs] = sre
        stim_ref[:, cs] = sim

    def scan(c):
        if pack == 2:
            half = kin // 2
            for c0 in range(c * kin, (c + 1) * kin, half):
                scan_packed(slice(c0, c0 + half))
            return
        cs = slice(c * kin, (c + 1) * kin)
        are = jnp.broadcast_to(are_ref[:, cs], (SUBLANES, kin))
        aim = jnp.broadcast_to(aim_ref[:, cs], (SUBLANES, kin))
        for r0 in range(0, b8, SUBLANES):
            sre = stre_ref[r0:r0 + SUBLANES, cs]
            sim = stim_ref[r0:r0 + SUBLANES, cs]
            for t in range(tt):
                rows = slice(t * b8 + r0, t * b8 + r0 + SUBLANES)
                sre, sim = (are * sre - aim * sim + bre_ref[rows, cs], are * sim + aim * sre + bim_ref[rows, cs])
                bre_ref[rows, cs] = sre
                bim_ref[rows, cs] = sim
            stre_ref[r0:r0 + SUBLANES, cs] = sre
            stim_ref[r0:r0 + SUBLANES, cs] = sim

    def project_out(c):
        cs = slice(c * kin, (c + 1) * kin)
        y = _dot(bre_ref[:, cs].astype(BF16), wcre_ref[c]) + _dot(bim_ref[:, cs].astype(BF16), wcim_ref[c])
        for i in range(wout // LANES):
            kb = c * (wout // LANES) + i
            xs_ref[kb] = y[:, i * LANES:(i + 1) * LANES] + xs_ref[kb] * d_ref[:, kb * LANES:(kb + 1) * LANES]

    project_in(0)
    for c in range(n_out):
        if c + 1 < n_out:
            project_in(c + 1)
        scan(c)
        if c >= 1:
            project_out(c - 1)
    project_out(n_out - 1)

    for b in range(nb):
        for kb in range(n_kb):
            y_ref[b, :, kb * LANES:(kb + 1) * LANES] = xs_ref[kb, pl.ds(b, tt, stride=b8), :]

    @pl.when(step == pl.num_programs(0) - 1)
    def _():
        sre_ref[...] = stre_ref[state_rows, :]
        sim_ref[...] = stim_ref[state_rows, :]


def _s5_weights(lam_re, lam_im, b_re, b_im, c_re, c_im, log_dt):
    g, p, h = b_re.shape
    lam = lax.complex(lam_re.astype(F32), lam_im.astype(F32))
    dt = jnp.exp(log_dt.astype(F32))[:, None]
    lam_bar = jnp.exp(lam * dt)
    b_bar = ((lam_bar - 1.0) / lam)[:, :, None] * lax.complex(b_re.astype(F32), b_im.astype(F32))
    gpk = LANES // h
    n_kb = g // gpk
    eye = jnp.eye(gpk, dtype=F32)

    def in_blocks(m):
        m = m.reshape(n_kb, gpk, p, h)
        return jnp.einsum("kgph,gf->kghfp", m, eye).reshape(n_kb, gpk * h, gpk * p).astype(BF16)

    gpo = 2 * LANES // h
    n_out = g // gpo
    eye_o = jnp.eye(gpo, dtype=F32)

    def out_blocks(m):
        m = m.reshape(n_out, gpo, h, p)
        return jnp.einsum("kghp,gf->kgpfh", m, eye_o).reshape(n_out, gpo * p, gpo * h).astype(BF16)

    return (in_blocks(jnp.real(b_bar)), in_blocks(jnp.imag(b_bar)),
            jnp.real(lam_bar).reshape(1, g * p), jnp.imag(lam_bar).reshape(1, g * p),
            out_blocks(c_re.astype(F32)), out_blocks(-c_im.astype(F32)))


def s5_scan(x, s0_re, s0_im, weights, d_skip, *, tt):
    nb, seq, d = x.shape
    wbre, wbim, are, aim, wcre, wcim = weights
    ns = are.shape[1]
    pack = 2 if (2 * nb == SUBLANES and tt % 2 == 0) else 1
    b8 = nb if pack == 2 else -(-nb // SUBLANES) * SUBLANES
    rows = tt * b8
    state_rows = SUBLANES if pack == 2 else b8
    const = lambda a: pl.BlockSpec(a.shape, lambda i: (0,) * a.ndim)
    xblk = pl.BlockSpec((nb, tt, d), lambda i: (0, i, 0))
    sblk = pl.BlockSpec((nb, ns), lambda i: (0, 0))
    return pl.pallas_call(
        functools.partial(_s5_scan_kernel, pack=pack),
        grid=(seq // tt,),
        in_specs=[xblk, sblk, sblk, const(wbre), const(wbim), const(are), const(aim), const(wcre), const(wcim),
                  pl.BlockSpec((1, d), lambda i: (0, 0))],
        out_specs=[xblk, sblk, sblk],
        out_shape=[jax.ShapeDtypeStruct((nb, seq, d), F32), jax.ShapeDtypeStruct((nb, ns), F32),
                   jax.ShapeDtypeStruct((nb, ns), F32)],
        scratch_shapes=[pltpu.VMEM((d // LANES, rows, LANES), F32), pltpu.VMEM((rows, ns), F32),
                        pltpu.VMEM((rows, ns), F32),
                        pltpu.VMEM((state_rows, ns), F32), pltpu.VMEM((state_rows, ns), F32)],
        compiler_params=_params("arbitrary"),
    )(x, s0_re, s0_im, wbre, wbim, are, aim, wcre, wcim, d_skip)


def _s5_out_ln_kernel(y_ref, wo_ref, wg_ref, x_ref, g_ref, b_ref, o_ref, *, alpha):
    gl = jax.nn.gelu(y_ref[...]).astype(BF16)
    mix = _dot(gl, wo_ref[...]) * jax.nn.sigmoid(_dot(gl, wg_ref[...]))
    o_ref[...] = _layer_norm(alpha * x_ref[...] + mix, g_ref[...], b_ref[...])


def s5_out_ln(y, w_out, w_gate, x, g, b, *, layer, alpha, tm):
    t, d = x.shape
    row = pl.BlockSpec((tm, d), lambda i: (i, 0))
    const = lambda r, c: pl.BlockSpec((r, c), lambda i: (0, 0))
    return pl.pallas_call(
        functools.partial(_s5_out_ln_kernel, alpha=alpha),
        grid=(t // tm,),
        in_specs=[row, _layer_weight(layer, d, d), _layer_weight(layer, d, d), row, const(1, d), const(1, d)],
        out_specs=row,
        out_shape=jax.ShapeDtypeStruct((t, d), F32),
        compiler_params=_params("parallel"),
    )(y, w_out, w_gate, x, g, b)


class _TilePlan(NamedTuple):
    token_rows: int
    proj_rows: int
    s5_steps: int
    ret_tiles: int
    ret_seqs_per_tile: int


def _tile_plan(lp, ls):
    return _TilePlan(token_rows=min(1024, lp), proj_rows=min(512, lp), s5_steps=min(64, lp),
                     ret_tiles=min(4, lp // RET_CHUNK), ret_seqs_per_tile=max(RET_CHUNK // ls, 1))


def _rope_tables(pos, n_heads):
    half = HEAD_DIM // 2
    inv = ROPE_THETA ** (-jnp.arange(half, dtype=F32) / half)
    ang = pos.astype(F32)[:, None] * inv[None, :]
    cos = jnp.cos(ang)
    sin = jnp.sin(ang)
    cos_h = jnp.concatenate([cos, cos], axis=-1)
    sin_h = jnp.concatenate([-sin, sin], axis=-1)
    return jnp.tile(cos_h, (1, n_heads)), jnp.tile(sin_h, (1, n_heads))


def kernel(x_prompt, x_sample, cache_k, cache_v, page_table, state_ret, state_s5_re, state_s5_im, ffn1_w_gate, ffn1_w_up, ffn1_w_down, ffn2_w_gate, ffn2_w_up, ffn2_w_down, ln_g, ln_b, w_in_ab, w_out_ab, s5_lam_re, s5_lam_im, s5_b_re, s5_b_im, s5_c_re, s5_c_im, s5_d, s5_log_dt, s5_w_out, s5_w_gate):
    bp, lp, d = x_prompt.shape
    bs, ls, _ = x_sample.shape
    depth = ffn1_w_gate.shape[0]
    n_layers_ab, n_pool, page, a_heads, hd = cache_k.shape
    n_pages = page_table.shape[1]
    past_len = n_pages * page
    half = a_heads * hd
    n_heads = half // HEAD_DIM
    assert hd == HEAD_DIM and lp % MOBA_BLOCK == 0 and past_len % MOBA_BLOCK == 0 and ls <= MOBA_BLOCK
    assert MOBA_BLOCK % page == 0 and lp % page == 0 and lp % RET_CHUNK == 0
    alpha = (2 * depth) ** 0.25
    tp, ts = bp * lp, bs * ls
    plan = _tile_plan(lp, ls)
    tm_p, tm_proj = plan.token_rows, plan.proj_rows

    xp = x_prompt.reshape(tp, d)
    xs = x_sample.reshape(ts, d)
    bf = lambda a: a.astype(BF16)
    f1 = (bf(ffn1_w_gate), bf(ffn1_w_up), bf(ffn1_w_down))
    f2 = (bf(ffn2_w_gate), bf(ffn2_w_up), bf(ffn2_w_down))
    w_in, w_out = bf(w_in_ab), bf(w_out_ab)
    w_o, w_g = bf(s5_w_out), bf(s5_w_gate)
    cos_p, sin_p = _rope_tables(jnp.arange(lp, dtype=jnp.int32), n_heads)
    cos_s, sin_s = _rope_tables(past_len + jnp.arange(ls, dtype=jnp.int32), n_heads)
    cos_s, sin_s = jnp.tile(cos_s, (bs, 1)), jnp.tile(sin_s, (bs, 1))
    pool_kt = jnp.transpose(cache_k, (0, 1, 3, 4, 2)).reshape(n_layers_ab * n_pool, half, page)
    pool_vt = jnp.transpose(cache_v, (0, 1, 3, 4, 2)).reshape(n_layers_ab * n_pool, half, page)
    seq_per_tile, ret_tiles_p = plan.ret_seqs_per_tile, plan.ret_tiles

    kpg = jnp.zeros((n_layers_ab, tp // page, half, page), F32)
    vpg = jnp.zeros((n_layers_ab, tp // page, half, page), F32)
    k_s, v_s, r_p, r_s = [], [], [], []
    sre_p, sim_p, sre_s, sim_s = [], [], [], []
    for layer in range(depth):
        li = layer // 2
        g = lambda i: ln_g[layer, i][None, :]
        b = lambda i: ln_b[layer, i][None, :]
        xp = ffn_ln(xp, *f1, g(0), b(0), layer=layer, alpha=alpha, tm=tm_p)
        xs = ffn_ln(xs, *f1, g(0), b(0), layer=layer, alpha=alpha, tm=ts)
        if layer % 2 == 0:
            qa, kbf, vt, ksum, kpg, vpg, qb, kb, vb, gb = ab_project(xp, w_in, cos_p, sin_p, layer=li, tm=tm_proj,
                                                                     seq_len=lp, pages=(kpg, vpg), page=page)
            attn = moba_prompt(qa, ksum, kbf, vt, batch=bp, seq_len=lp)
            zero_state = jnp.zeros((bp, n_heads // 2, HEAD_PAIR, HEAD_PAIR), F32)
            ret, s_fin = retention_gated(qb, kb, vb, gb, zero_state, n_seq=1, chunk=RET_CHUNK, tiles=ret_tiles_p,
                                         steps_per_seq=lp // (RET_CHUNK * ret_tiles_p))
            xp = merge_ln(attn, ret, w_out, xp, g(1), b(1), layer=li, alpha=alpha, tm=tm_p)
            r_p.append(_block_diag_to_states(s_fin))
            qa, ka, va, qb, kb, vb, gb = ab_project(xs, w_in, cos_s, sin_s, layer=li, tm=ts, seq_len=ts)
            page_ids = (page_table.astype(jnp.int32) + li * n_pool).reshape(-1)
            attn = moba_sample(qa, ka, va, pool_kt, pool_vt, page_ids, batch=bs, n_pages=n_pages)
            ret, s_fin = retention_gated(qb, kb, vb, gb, _states_to_block_diag(state_ret[li]), n_seq=seq_per_tile,
                                         chunk=ls, tiles=1, steps_per_seq=1)
            xs = merge_ln(attn, ret, w_out, xs, g(1), b(1), layer=li, alpha=alpha, tm=ts)
            k_s.append(ka.reshape(bs, ls, a_heads, hd))
            v_s.append(va.reshape(bs, ls, a_heads, hd))
            r_s.append(_block_diag_to_states(s_fin))
        else:
            weights = _s5_weights(s5_lam_re[li], s5_lam_im[li], s5_b_re[li], s5_b_im[li], s5_c_re[li],
                                  s5_c_im[li], s5_log_dt[li])
            n_state = weights[2].shape[1]
            d_skip = s5_d[li][None, :]
            zero = jnp.zeros((bp, n_state), F32)
            y, a_re, a_im = s5_scan(xp.reshape(bp, lp, d), zero, zero, weights, d_skip, tt=plan.s5_steps)
            xp = s5_out_ln(y.reshape(tp, d), w_o, w_g, xp, g(1), b(1), layer=li, alpha=alpha, tm=tm_p)
            sre_p.append(a_re.reshape(bp, -1, S5_STATE))
            sim_p.append(a_im.reshape(bp, -1, S5_STATE))
            y, a_re, a_im = s5_scan(xs.reshape(bs, ls, d), state_s5_re[li].reshape(bs, n_state),
                                    state_s5_im[li].reshape(bs, n_state), weights, d_skip, tt=ls)
            xs = s5_out_ln(y.reshape(ts, d), w_o, w_g, xs, g(1), b(1), layer=li, alpha=alpha, tm=ts)
            sre_s.append(a_re.reshape(bs, -1, S5_STATE))
            sim_s.append(a_im.reshape(bs, -1, S5_STATE))
        xp = ffn_ln(xp, *f2, g(2), b(2), layer=layer, alpha=alpha, tm=tm_p)
        xs = ffn_ln(xs, *f2, g(2), b(2), layer=layer, alpha=alpha, tm=ts)
    unpage = lambda t: jnp.transpose(t.reshape(n_layers_ab, bp, lp // page, a_heads, hd, page), (0, 1, 2, 5, 3, 4))
    return (xp.reshape(bp, lp, d), xs.reshape(bs, ls, d), unpage(kpg), unpage(vpg), jnp.stack(k_s),
            jnp.stack(v_s), jnp.stack(r_p), jnp.stack(r_s), jnp.stack(sre_p), jnp.stack(sim_p),
            jnp.stack(sre_s), jnp.stack(sim_s))
```

```python
import functools
import math
from typing import NamedTuple

import jax
import jax.numpy as jnp
from jax import lax
from jax.experimental import pallas as pl
from jax.experimental.pallas import tpu as pltpu

F32 = jnp.float32
BF16 = jnp.bfloat16

HEAD_DIM = 64
HEAD_PAIR = 2 * HEAD_DIM
MOBA_BLOCK = 256
MOBA_TOPK = 3
MOBA_BLOCKS_PER_TRIP = 4
RET_CHUNK = 128
S5_GROUP = 16
S5_STATE = 64
ROPE_THETA = 10000.0
LN_EPS = 1e-5
GN_EPS = 1e-6
NEG_INF = -1e30
LOG2_E = math.log2(math.e)
SUBLANES = 8
BF16_SUBLANES = 16
LANES = 128
VMEM_LIMIT = 48 * 1024 * 1024


def _dot(a, b, precision=None):
    return jnp.dot(a, b, preferred_element_type=F32, precision=precision)


def _dot_nt(a, b, precision=None):
    return lax.dot_general(a, b, (((1,), (1,)), ((), ())), preferred_element_type=F32, precision=precision)


def _dot_tn(a, b):
    return lax.dot_general(a, b, (((0,), (0,)), ((), ())), preferred_element_type=F32)


def _layer_norm(r, g, b):
    mu = jnp.mean(r, -1, keepdims=True)
    d = r - mu
    var = jnp.mean(d * d, -1, keepdims=True)
    return d * lax.rsqrt(var + LN_EPS) * g + b


def _silu(x):
    return x * jax.nn.sigmoid(x)


def _params(*sem):
    return pltpu.CompilerParams(dimension_semantics=sem, vmem_limit_bytes=VMEM_LIMIT)


def _ffn_ln_kernel(xp_ref, xs_ref, wg_ref, wu_ref, wd_ref, g_ref, b_ref, op_ref, os_ref, a_ref, *, alpha, tf):
    def run(x_ref, o_ref):
        rows = x_ref.shape[0]
        xb = x_ref[...].astype(BF16)
        for c in range(wg_ref.shape[1] // tf):
            cols = slice(c * tf, (c + 1) * tf)
            hg = _dot(xb, wg_ref[:, cols])
            hu = _dot(xb, wu_ref[:, cols])
            a_ref[0:rows, cols] = (_silu(hg) * hu).astype(BF16)
        r = alpha * x_ref[...] + 0.5 * _dot(a_ref[0:rows, :], wd_ref[...])
        o_ref[...] = _layer_norm(r, g_ref[...], b_ref[...])

    last = pl.num_programs(0) - 1
    pl.when(pl.program_id(0) < last)(lambda: run(xp_ref, op_ref))
    pl.when(pl.program_id(0) == last)(lambda: run(xs_ref, os_ref))


def _layer_weight(layer, r, c, row_blk=0, **kw):
    return pl.BlockSpec((None, r, c), lambda *_: (layer, row_blk, 0), **kw)


def ffn_ln(xp, xs, wg, wu, wd, g, b, *, layer, alpha, tm, tf=256):
    tp, d = xp.shape
    ts = xs.shape[0]
    f = wg.shape[2]
    n_p = tp // tm
    assert ts <= tm
    resident = lambda r, c: _layer_weight(layer, r, c, pipeline_mode=pl.Buffered(1))
    p_rows = pl.BlockSpec((tm, d), lambda i: (jnp.minimum(i, n_p - 1), 0))
    s_rows = pl.BlockSpec((ts, d), lambda i: (0, 0))
    return pl.pallas_call(
        functools.partial(_ffn_ln_kernel, alpha=alpha, tf=tf),
        grid=(n_p + 1,),
        in_specs=[p_rows, s_rows, resident(d, f), resident(d, f), resident(f, d),
                  pl.BlockSpec((1, d), lambda i: (0, 0)), pl.BlockSpec((1, d), lambda i: (0, 0))],
        out_specs=[p_rows, s_rows],
        out_shape=[jax.ShapeDtypeStruct((tp, d), F32), jax.ShapeDtypeStruct((ts, d), F32)],
        scratch_shapes=[pltpu.VMEM((tm, f), BF16)],
        compiler_params=_params("arbitrary"),
    )(xp, xs, wg, wu, wd, g, b)


def _rope(y, cos, sin_signed):
    width = y.shape[-1]
    lane = lax.broadcasted_iota(jnp.int32, (1, width), 1)
    first = (lane % HEAD_DIM) < (HEAD_DIM // 2)
    rot = jnp.where(first, pltpu.roll(y, width - HEAD_DIM // 2, 1), pltpu.roll(y, HEAD_DIM // 2, 1))
    return y * cos + rot * sin_signed


def _ab_proj_kernel(*refs, half, scale, page):
    if page is None:
        x_ref, w_ref, cos_ref, sin_ref, qa_ref, ka_ref, va_ref, qb_ref, kb_ref, vb_ref, gb_ref = refs
    else:
        (x_ref, w_ref, cos_ref, sin_ref, _, _, qa_ref, kbf_ref, vt_ref, ksum_ref, kpg_ref, vpg_ref,
         qb_ref, kb_ref, vb_ref, gb_ref) = refs
    xb = x_ref[...].astype(BF16)
    cos = cos_ref[...]
    sin = sin_ref[...]
    col = lambda c: _dot(xb, w_ref[:, c * half:(c + 1) * half])
    qa_ref[...] = _rope(col(0), cos, sin) * scale
    ka = _rope(col(1), cos, sin)
    va = col(2)
    if page is None:
        ka_ref[...] = ka
        va_ref[...] = va
    else:
        kbf_ref[...] = ka.astype(BF16)
        for n in range(ka.shape[0] // MOBA_BLOCK):
            blk = slice(n * MOBA_BLOCK, (n + 1) * MOBA_BLOCK)
            ksum_ref[n] = jnp.sum(ka[blk], axis=0, keepdims=True)
            vt_ref[n] = va[blk].T.astype(BF16)
        for n in range(ka.shape[0] // page):
            rows = slice(n * page, (n + 1) * page)
            kpg_ref[n] = ka[rows].T
            vpg_ref[n] = va[rows].T
    qb_ref[...] = _rope(col(3), cos, sin)
    kb_ref[...] = _rope(col(4), cos, sin) * scale
    vb_ref[...] = col(5)
    gb_ref[...] = col(6)


def ab_project(x, w_in, cos, sin, *, layer, tm, seq_len, pages=None, page=None):
    t, d = x.shape
    half = w_in.shape[2] // 7
    tiles_per_seq = max(seq_len // tm, 1)
    tab = pl.BlockSpec((tm, half), lambda i: (i % tiles_per_seq, 0))
    row = pl.BlockSpec((tm, half), lambda i: (i, 0))
    f32o = jax.ShapeDtypeStruct((t, half), F32)
    in_specs = [pl.BlockSpec((tm, d), lambda i: (i, 0)), _layer_weight(layer, d, 7 * half), tab, tab]
    operands = [x, w_in, cos, sin]
    aliases = {}
    if pages is None:
        out_specs = [row] * 7
        out_shape = [f32o] * 7
    else:
        nblk = tm // MOBA_BLOCK
        page_blk = pl.BlockSpec((None, tm // page, half, page), lambda i: (layer, i, 0, 0))
        in_specs += [pl.BlockSpec(memory_space=pl.ANY)] * 2
        operands += list(pages)
        aliases = {4: 4, 5: 5}
        out_specs = [row, row,
                     pl.BlockSpec((nblk, half, MOBA_BLOCK), lambda i: (i, 0, 0)),
                     pl.BlockSpec((nblk, 1, half), lambda i: (i, 0, 0)),
                     page_blk, page_blk, row, row, row, row]
        out_shape = [f32o, jax.ShapeDtypeStruct((t, half), BF16),
                     jax.ShapeDtypeStruct((t // MOBA_BLOCK, half, MOBA_BLOCK), BF16),
                     jax.ShapeDtypeStruct((t // MOBA_BLOCK, 1, half), F32),
                     jax.ShapeDtypeStruct(pages[0].shape, F32), jax.ShapeDtypeStruct(pages[1].shape, F32),
                     f32o, f32o, f32o, f32o]
    return pl.pallas_call(
        functools.partial(_ab_proj_kernel, half=half, scale=HEAD_DIM ** -0.5, page=page),
        grid=(t // tm,),
        in_specs=in_specs,
        out_specs=out_specs,
        out_shape=out_shape,
        input_output_aliases=aliases,
        compiler_params=_params("parallel"),
    )(*operands)


def _topk_select(gate, blk, n_valid, axis):
    gate = jnp.where(blk < n_valid, gate, NEG_INF)
    taken = jnp.zeros(gate.shape, F32)
    blk_f = blk.astype(F32)
    for _ in range(MOBA_TOPK):
        top = jnp.max(gate, axis=axis, keepdims=True)
        first = jnp.min(jnp.where(gate == top, blk_f, float(gate.shape[axis])), axis=axis, keepdims=True)
        hit = blk_f == first
        taken = jnp.where(hit, 1.0, taken)
        gate = jnp.where(hit, -jnp.inf, gate)
    return jnp.where(blk < n_valid, taken, 0.0)


def _moba_prompt_kernel(q_ref, ksum_ref, k_ref, vt_ref, o_ref, qm_ref, sel_ref, m_ref, l_ref, acc_ref, s_ref):
    j = pl.program_id(1)
    tq = q_ref.shape[0]
    n_heads = q_ref.shape[1] // HEAD_DIM
    lane = lax.broadcasted_iota(jnp.int32, (1, HEAD_PAIR), 1)
    key_i = lax.broadcasted_iota(jnp.int32, (MOBA_BLOCK, tq), 0)
    qry_i = lax.broadcasted_iota(jnp.int32, (MOBA_BLOCK, tq), 1)
    causal = key_i <= qry_i

    nb = ksum_ref.shape[0]
    blk = lax.broadcasted_iota(jnp.int32, (nb, 1), 0)
    for hp in range(n_heads // 2):
        lanes = slice(hp * HEAD_PAIR, (hp + 1) * HEAD_PAIR)
        qp = q_ref[:, lanes]
        kmean = ksum_ref[:, 0, lanes] * (1.0 / MOBA_BLOCK)
        masks = [(lane // HEAD_DIM) == half for half in range(2)]
        gate = _dot_nt(jnp.concatenate([jnp.where(hm, kmean, 0.0) for hm in masks], axis=0), qp,
                       precision=lax.Precision.HIGHEST)
        for half, hm in enumerate(masks):
            h = 2 * hp + half
            sel_ref[h] = _topk_select(gate[half * nb:(half + 1) * nb], blk, j, 0)
            qm_ref[h] = jnp.where(hm, qp * LOG2_E, 0.0).astype(BF16)

    ones_rows = jnp.ones((BF16_SUBLANES, MOBA_BLOCK), BF16)

    def update(blocks, first):
        def rows0(n):
            return pl.multiple_of((j if n is None else n) * MOBA_BLOCK, MOBA_BLOCK)

        def picked(n, h):
            return sel_ref[h, pl.ds(n, 1), :] > 0.0

        def scores(slot, n, h):
            k_pair = k_ref[pl.ds(rows0(n), MOBA_BLOCK), (h // 2) * HEAD_PAIR:(h // 2 + 1) * HEAD_PAIR]
            s_ref[slot, h] = _dot_nt(k_pair, qm_ref[h])

        def softmax_values(slot, n, h):
            if first:
                s = jnp.where(causal, s_ref[slot, h], NEG_INF)
                m_new = jnp.max(s, axis=0, keepdims=True)
            else:
                s = s_ref[slot, h]
                cmax = jnp.where(picked(n, h), jnp.max(s, axis=0, keepdims=True), NEG_INF)
                m_new = jnp.maximum(m_ref[h], cmax)
                a = jnp.exp2(m_ref[h] - m_new)
            p = jnp.exp2(s - m_new).astype(BF16)
            m_ref[h] = m_new
            v_h = vt_ref[j if n is None else n, h * HEAD_DIM:(h + 1) * HEAD_DIM, :]
            res = _dot(jnp.concatenate([v_h, ones_rows], axis=0), p)
            pv, psum = res[:HEAD_DIM], res[HEAD_DIM:HEAD_DIM + 1]
            if first:
                acc_ref[h] = pv
                l_ref[h] = psum
            else:
                acc_ref[h] = a * acc_ref[h] + jnp.where(picked(n, h), pv, 0.0)
                l_ref[h] = a * l_ref[h] + jnp.where(picked(n, h), psum, 0.0)

        heads = range(n_heads)
        stages = (scores, softmax_values)
        for t in range(len(stages) + len(blocks) - 1):
            for h in heads:
                for slot, n in enumerate(blocks):
                    if 0 <= t - slot < len(stages):
                        stages[t - slot](slot, n, h)

    update([None], True)

    n_slots = s_ref.shape[0]

    def body(i, carry):
        update([n_slots * i + k for k in range(n_slots)], False)
        return carry

    lax.fori_loop(0, j // n_slots, body, 0)
    for rem in range(1, n_slots):
        @pl.when(j % n_slots == rem)
        def _(rem=rem):
            update([j - rem + k for k in range(rem)], False)

    out_t = jnp.concatenate([acc_ref[h] / l_ref[h] for h in range(n_heads)], axis=0)
    o_ref[...] = out_t.T.astype(o_ref.dtype)


def moba_prompt(q, ksum, kbf, vt, *, batch, seq_len):
    t, w = q.shape
    nb = seq_len // MOBA_BLOCK
    n_heads = w // HEAD_DIM
    return pl.pallas_call(
        _moba_prompt_kernel,
        grid=(batch, nb),
        in_specs=[pl.BlockSpec((MOBA_BLOCK, w), lambda b, j: (b * nb + j, 0)),
                  pl.BlockSpec((nb, 1, w), lambda b, j: (b, 0, 0)),
                  pl.BlockSpec((seq_len, w), lambda b, j: (b, 0)),
                  pl.BlockSpec((nb, w, MOBA_BLOCK), lambda b, j: (b, 0, 0))],
        out_specs=pl.BlockSpec((MOBA_BLOCK, w), lambda b, j: (b * nb + j, 0)),
        out_shape=jax.ShapeDtypeStruct((t, w), BF16),
        scratch_shapes=[pltpu.VMEM((n_heads, MOBA_BLOCK, HEAD_PAIR), BF16),
                        pltpu.VMEM((n_heads, nb, MOBA_BLOCK), F32),
                        pltpu.VMEM((n_heads, 1, MOBA_BLOCK), F32),
                        pltpu.VMEM((n_heads, 1, MOBA_BLOCK), F32),
                        pltpu.VMEM((n_heads, HEAD_DIM, MOBA_BLOCK), F32),
                        pltpu.VMEM((MOBA_BLOCKS_PER_TRIP, n_heads, MOBA_BLOCK, MOBA_BLOCK), F32)],
        compiler_params=_params("parallel", "arbitrary"),
    )(q, ksum, kbf, vt)


def _moba_sample_kernel(pt_ref, q_ref, knew_ref, vnew_ref, *rest, pages_per_step, page, n_heads):
    del pt_ref
    kpages = rest[:pages_per_step]
    vpages = rest[pages_per_step:2 * pages_per_step]
    o_ref = rest[2 * pages_per_step]
    qf_ref, qb_ref, ksum_ref, m_ref, l_ref, acc_ref, kpad_ref, vpad_ref = rest[2 * pages_per_step + 1:]
    s_idx = pl.program_id(1)
    n_steps = pl.num_programs(1)
    lq, w = q_ref.shape
    rows = n_heads * lq
    nb = acc_ref.shape[0]
    pages_per_blk = MOBA_BLOCK // page
    blks_per_step = pages_per_step // pages_per_blk
    rowhead = lax.broadcasted_iota(jnp.int32, (rows, 1), 0) // lq
    lanehead = lax.broadcasted_iota(jnp.int32, (1, w), 1) // HEAD_DIM
    blk_lane = lax.broadcasted_iota(jnp.int32, (1, LANES), 1)

    @pl.when(s_idx == 0)
    def _():
        qt = jnp.concatenate([q_ref[...]] * n_heads, axis=0)
        qbd = jnp.where(rowhead == lanehead, qt, 0.0)
        qf_ref[...] = qbd
        qb_ref[...] = qbd.astype(BF16)
        ksum_ref[...] = jnp.zeros_like(ksum_ref)
        m_ref[...] = jnp.zeros_like(m_ref)
        l_ref[...] = jnp.zeros_like(l_ref)

    qb = qb_ref[...]

    def partial_softmax(s):
        m = jnp.max(s, axis=1, keepdims=True)
        e = jnp.exp(s - m)
        return m, jnp.sum(e, axis=1, keepdims=True), e.astype(BF16)

    blks = range(blks_per_step)
    pages_of = lambda bi: range(bi * pages_per_blk, (bi + 1) * pages_per_blk)
    here = [blk_lane == s_idx * blks_per_step + bi for bi in blks]
    kt = [jnp.concatenate([kpages[i][0] for i in pages_of(bi)], axis=1) for bi in blks]
    scores = [_dot(qb, kt[bi].astype(BF16)) for bi in blks]
    ksum = ksum_ref[...]
    for bi in blks:
        ksum = jnp.where(here[bi], jnp.sum(kt[bi], axis=1, keepdims=True), ksum)
    ksum_ref[...] = ksum
    stats = [partial_softmax(scores[bi]) for bi in blks]
    for bi in blks:
        vt = jnp.concatenate([vpages[i][0] for i in pages_of(bi)], axis=1)
        acc_ref[s_idx * blks_per_step + bi] = _dot_nt(stats[bi][2], vt.astype(BF16))
    m_all, l_all = m_ref[...], l_ref[...]
    for bi in blks:
        m_all = jnp.where(here[bi], stats[bi][0], m_all)
        l_all = jnp.where(here[bi], stats[bi][1], l_all)
    m_ref[...] = m_all
    l_ref[...] = l_all

    @pl.when(s_idx == n_steps - 1)
    def _():
        kpad_ref[...] = jnp.zeros_like(kpad_ref)
        vpad_ref[...] = jnp.zeros_like(vpad_ref)
        kpad_ref[0:lq, :] = knew_ref[...]
        vpad_ref[0:lq, :] = vnew_ref[...]
        s_own = _dot_nt(qb, kpad_ref[...].astype(BF16))
        tq = lax.broadcasted_iota(jnp.int32, s_own.shape, 0) % lq
        tk = lax.broadcasted_iota(jnp.int32, s_own.shape, 1)
        m_own, l_own, e_own = partial_softmax(jnp.where(tk <= tq, s_own, NEG_INF))
        acc_own = _dot(e_own, vpad_ref[...].astype(BF16))

        gate = _dot(qf_ref[...], ksum_ref[...] * (1.0 / MOBA_BLOCK), precision=lax.Precision.HIGHEST)
        sel = _topk_select(gate[:, :nb], blk_lane[:, :nb], nb, 1) > 0.0
        m_blk = m_ref[:, :nb]
        m_all = jnp.maximum(m_own, jnp.max(jnp.where(sel, m_blk, NEG_INF), axis=1, keepdims=True))
        wgt = jnp.where(sel, jnp.exp(m_blk - m_all), 0.0)
        w_own = jnp.exp(m_own - m_all)
        den = w_own * l_own + jnp.sum(wgt * l_ref[:, :nb], axis=1, keepdims=True)
        num = w_own * acc_own
        for n in range(nb):
            num = num + wgt[:, n:n + 1] * acc_ref[n]
        o_all = num / den
        out = jnp.zeros((lq, w), F32)
        for h in range(n_heads):
            out = out + jnp.where(lanehead == h, o_all[h * lq:(h + 1) * lq], 0.0)
        o_ref[...] = out


def moba_sample(q, knew, vnew, pool_kt, pool_vt, page_ids, *, batch, n_pages, pages_per_step=32):
    t, w = q.shape
    lq = t // batch
    page = pool_kt.shape[2]
    n_heads = w // HEAD_DIM
    rows = n_heads * lq
    nb = n_pages * page // MOBA_BLOCK
    assert nb <= LANES and pages_per_step % (MOBA_BLOCK // page) == 0
    seq_blk = pl.BlockSpec((lq, w), lambda b, s, pt: (b, 0))

    def page_spec(i):
        return pl.BlockSpec((1, w, page), lambda b, s, pt: (pt[b * n_pages + s * pages_per_step + i], 0, 0))

    grid_spec = pltpu.PrefetchScalarGridSpec(
        num_scalar_prefetch=1,
        grid=(batch, n_pages // pages_per_step),
        in_specs=[seq_blk, seq_blk, seq_blk] + [page_spec(i) for i in range(pages_per_step)] * 2,
        out_specs=seq_blk,
        scratch_shapes=[pltpu.VMEM((rows, w), F32), pltpu.VMEM((rows, w), BF16), pltpu.VMEM((w, LANES), F32),
                        pltpu.VMEM((rows, LANES), F32), pltpu.VMEM((rows, LANES), F32),
                        pltpu.VMEM((nb, rows, w), F32),
                        pltpu.VMEM((LANES, w), F32), pltpu.VMEM((LANES, w), F32)],
    )
    return pl.pallas_call(
        functools.partial(_moba_sample_kernel, pages_per_step=pages_per_step, page=page, n_heads=n_heads),
        grid_spec=grid_spec,
        out_shape=jax.ShapeDtypeStruct((t, w), F32),
        compiler_params=_params("parallel", "arbitrary"),
    )(page_ids, q, knew, vnew, *([pool_kt] * pages_per_step), *([pool_vt] * pages_per_step))


def _retention_kernel(q_ref, k_ref, v_ref, g_ref, s0_ref, dmask_ref, qdec_ref, kdec_ref, cdec_ref,
                      o_ref, sout_ref, st_ref, *, n_seq, chunk, tiles, steps_per_seq):
    step = pl.program_id(0)
    tile_rows = n_seq * chunk
    n_pairs = q_ref.shape[1] // HEAD_PAIR
    lane = lax.broadcasted_iota(jnp.int32, (1, HEAD_PAIR), 1)
    rowh = lax.broadcasted_iota(jnp.int32, (HEAD_PAIR, 1), 0) // HEAD_DIM
    same_head = rowh == (lane // HEAD_DIM)
    rseq = lax.broadcasted_iota(jnp.int32, (tile_rows, 1), 0) // chunk

    @pl.when(step % steps_per_seq == 0)
    def _():
        st_ref[...] = s0_ref[...]

    pairs = range(n_pairs)
    lanes_of = lambda hp: slice(hp * HEAD_PAIR, (hp + 1) * HEAD_PAIR)
    half_masks = [(lane // HEAD_DIM) == half for half in range(2)]
    for ti in range(tiles):
        rows = slice(ti * tile_rows, (ti + 1) * tile_rows)
        qb = [q_ref[rows, lanes_of(hp)].astype(BF16) for hp in pairs]
        kf = [k_ref[rows, lanes_of(hp)] for hp in pairs]
        kb = [k.astype(BF16) for k in kf]
        vb = [v_ref[rows, lanes_of(hp)].astype(BF16) for hp in pairs]
        inner = [[_dot_nt(jnp.where(hm, qb[hp], jnp.zeros_like(qb[hp])), kb[hp]) for hm in half_masks]
                 for hp in pairs]
        scaled = [[(inner[hp][half] * dmask_ref[2 * hp + half]).astype(BF16) for half in range(2)] for hp in pairs]
        ret = [sum(jnp.where(half_masks[half], _dot(scaled[hp][half], vb[hp]), 0.0) for half in range(2))
               for hp in pairs]
        for hp in pairs:
            kd = (kf[hp] * kdec_ref[:, lanes_of(hp)]).astype(BF16)
            cdec = cdec_ref[:, lanes_of(hp)]
            carried = []
            for s in range(n_seq):
                srows = slice(s * chunk, (s + 1) * chunk)
                state = st_ref[s, hp]
                carried.append(_dot(qb[hp][srows], state.astype(BF16)))
                kd_s = kd if n_seq == 1 else jnp.where(rseq == s, kd, jnp.zeros_like(kd))
                st_ref[s, hp] = cdec * state + jnp.where(same_head, _dot_tn(kd_s, vb[hp]), 0.0)
            carried = carried[0] if n_seq == 1 else jnp.concatenate(carried, axis=0)
            ret[hp] = ret[hp] + carried * qdec_ref[:, lanes_of(hp)]
        for hp in pairs:
            normed = jnp.zeros_like(ret[hp])
            for hm in half_masks:
                mu = jnp.sum(jnp.where(hm, ret[hp], 0.0), axis=1, keepdims=True) * (1.0 / HEAD_DIM)
                d = jnp.where(hm, ret[hp] - mu, 0.0)
                var = jnp.sum(d * d, axis=1, keepdims=True) * (1.0 / HEAD_DIM)
                normed = normed + d * lax.rsqrt(var + GN_EPS)
            o_ref[rows, lanes_of(hp)] = (_silu(g_ref[rows, lanes_of(hp)]) * normed).astype(o_ref.dtype)

    @pl.when(step % steps_per_seq == steps_per_seq - 1)
    def _():
        sout_ref[...] = st_ref[...]


def _retention_tables(n_heads, n_seq, chunk):
    log_g = jnp.log1p(-jnp.exp2(-5.0 - jnp.arange(n_heads, dtype=F32)))
    idx = jnp.arange(chunk, dtype=F32)
    diff = idx[:, None] - idx[None, :]
    dmask = jnp.where(diff >= 0, jnp.exp(log_g[:, None, None] * jnp.maximum(diff, 0.0)), 0.0)
    q_dec = jnp.exp(log_g[None, :] * (idx[:, None] + 1.0))
    k_dec = jnp.exp(log_g[None, :] * (chunk - 1.0 - idx[:, None]))
    c_dec = jnp.exp(log_g * chunk)
    seq_eye = jnp.eye(n_seq, dtype=F32)
    dmask = jnp.einsum("ab,hij->haibj", seq_eye, dmask).reshape(n_heads, n_seq * chunk, n_seq * chunk)
    lanes = lambda t: jnp.repeat(t, HEAD_DIM, axis=-1)
    return dmask, jnp.tile(lanes(q_dec), (n_seq, 1)), jnp.tile(lanes(k_dec), (n_seq, 1)), lanes(c_dec[None, :])


def retention_gated(q, k, v, g, s0_bd, *, n_seq, chunk, tiles, steps_per_seq):
    t, w = q.shape
    n_pairs = w // HEAD_PAIR
    rows = n_seq * chunk * tiles
    n_steps = t // rows
    dmask, qdec, kdec, cdec = _retention_tables(w // HEAD_DIM, n_seq, chunk)
    tile_rows = n_seq * chunk
    row = pl.BlockSpec((rows, w), lambda i: (i, 0))
    st = pl.BlockSpec((n_seq, n_pairs, HEAD_PAIR, HEAD_PAIR), lambda i: (i // steps_per_seq, 0, 0, 0))
    const = lambda shape: pl.BlockSpec(shape, lambda i: (0,) * len(shape))
    return pl.pallas_call(
        functools.partial(_retention_kernel, n_seq=n_seq, chunk=chunk, tiles=tiles, steps_per_seq=steps_per_seq),
        grid=(n_steps,),
        in_specs=[row, row, row, row, st, const(dmask.shape), const((tile_rows, w)), const((tile_rows, w)),
                  const((1, w))],
        out_specs=[row, st],
        out_shape=[jax.ShapeDtypeStruct((t, w), BF16), jax.ShapeDtypeStruct(s0_bd.shape, F32)],
        scratch_shapes=[pltpu.VMEM((n_seq, n_pairs, HEAD_PAIR, HEAD_PAIR), F32)],
        compiler_params=_params("arbitrary"),
    )(q, k, v, g, s0_bd, dmask, qdec, kdec, cdec)


def _states_to_block_diag(s):
    b, h, dk, dv = s.shape
    sp = s.reshape(b, h // 2, 2, dk, dv)
    z = jnp.zeros_like(sp[:, :, 0])
    top = jnp.concatenate([sp[:, :, 0], z], axis=-1)
    bot = jnp.concatenate([z, sp[:, :, 1]], axis=-1)
    return jnp.concatenate([top, bot], axis=-2)


def _block_diag_to_states(s):
    b, p = s.shape[:2]
    a = s[:, :, :HEAD_DIM, :HEAD_DIM]
    c = s[:, :, HEAD_DIM:, HEAD_DIM:]
    return jnp.stack([a, c], axis=2).reshape(b, 2 * p, HEAD_DIM, HEAD_DIM)


def _merge_ln_kernel(a_ref, r_ref, wa_ref, wr_ref, x_ref, g_ref, b_ref, o_ref, *, alpha):
    mix = _dot(a_ref[...].astype(BF16), wa_ref[...]) + _dot(r_ref[...].astype(BF16), wr_ref[...])
    o_ref[...] = _layer_norm(alpha * x_ref[...] + mix, g_ref[...], b_ref[...])


def merge_ln(attn, ret, w_out, x, g, b, *, layer, alpha, tm):
    t, d = x.shape
    w = attn.shape[1]
    row = lambda n: pl.BlockSpec((tm, n), lambda i: (i, 0))
    const = lambda r, c: pl.BlockSpec((r, c), lambda i: (0, 0))
    return pl.pallas_call(
        functools.partial(_merge_ln_kernel, alpha=alpha),
        grid=(t // tm,),
        in_specs=[row(w), row(w), _layer_weight(layer, w, d, 0), _layer_weight(layer, w, d, 1), row(d),
                  const(1, d), const(1, d)],
        out_specs=row(d),
        out_shape=jax.ShapeDtypeStruct((t, d), F32),
        compiler_params=_params("parallel"),
    )(attn, ret, w_out, w_out, x, g, b)


def _s5_scan_kernel(x_ref, s0re_ref, s0im_ref, wbre_ref, wbim_ref, are_ref, aim_ref, wcre_ref, wcim_ref, d_ref,
                    y_ref, sre_ref, sim_ref, xs_ref, bre_ref, bim_ref, stre_ref, stim_ref, *, pack):
    step = pl.program_id(0)
    nb, tt, d = x_ref.shape
    b8 = xs_ref.shape[1] // tt
    ns = stre_ref.shape[1]
    n_kb = d // LANES
    per_kb = ns // n_kb
    state_rows = slice((pack - 1) * nb, pack * nb)

    @pl.when(step == 0)
    def _():
        stre_ref[...] = jnp.zeros_like(stre_ref)
        stim_ref[...] = jnp.zeros_like(stim_ref)
        stre_ref[state_rows, :] = s0re_ref[...]
        stim_ref[state_rows, :] = s0im_ref[...]
        xs_ref[...] = jnp.zeros_like(xs_ref)

    for b in range(nb):
        for kb in range(n_kb):
            xs_ref[kb, pl.ds(b, tt, stride=b8), :] = x_ref[b, :, kb * LANES:(kb + 1) * LANES]

    n_out = wcre_ref.shape[0]
    kin = ns // n_out
    wout = d // n_out
    kb_per_chunk = kin // per_kb

    def project_in(c):
        for kb in range(c * kb_per_chunk, (c + 1) * kb_per_chunk):
            xk = xs_ref[kb].astype(BF16)
            cols = slice(kb * per_kb, (kb + 1) * per_kb)
            bre_ref[:, cols] = _dot(xk, wbre_ref[kb])
            bim_ref[:, cols] = _dot(xk, wbim_ref[kb])

    def scan_packed(cs):
        width = cs.stop - cs.start
        are = jnp.broadcast_to(are_ref[:, cs], (SUBLANES, width))
        aim = jnp.broadcast_to(aim_ref[:, cs], (SUBLANES, width))
        first = lax.broadcasted_iota(jnp.int32, (SUBLANES, 1), 0) < nb
        lo_re, lo_im = jnp.where(first, 0.0, are), jnp.where(first, 0.0, aim)
        hi_re = jnp.where(first, are, are * are - aim * aim)
        hi_im = jnp.where(first, aim, 2.0 * are * aim)
        sre, sim = stre_ref[:, cs], stim_ref[:, cs]
        for v in range(tt // pack):
            rows = slice(v * SUBLANES, (v + 1) * SUBLANES)
            bre, bim = bre_ref[rows, cs], bim_ref[rows, cs]
            rre, rim = pltpu.roll(bre, nb, 0), pltpu.roll(bim, nb, 0)
            pre = jnp.where(first, pltpu.roll(sre, nb, 0), sre)
            pim = jnp.where(first, pltpu.roll(sim, nb, 0), sim)
            sre = bre + (lo_re * rre - lo_im * rim) + (hi_re * pre - hi_im * pim)
            sim = bim + (lo_re * rim + lo_im * rre) + (hi_re * pim + hi_im * pre)
            bre_ref[rows, cs] = sre
            bim_ref[rows, cs] = sim
        stre_ref[:, cs] = sre
        stim_ref[:, cs] = sim

    def scan(c):
        if pack == 2:
            half = kin // 2
            for c0 in range(c * kin, (c + 1) * kin, half):
                scan_packed(slice(c0, c0 + half))
            return
        cs = slice(c * kin, (c + 1) * kin)
        are = jnp.broadcast_to(are_ref[:, cs], (SUBLANES, kin))
        aim = jnp.broadcast_to(aim_ref[:, cs], (SUBLANES, kin))
        for r0 in range(0, b8, SUBLANES):
            sre = stre_ref[r0:r0 + SUBLANES, cs]
            sim = stim_ref[r0:r0 + SUBLANES, cs]
            for t in range(tt):
                rows = slice(t * b8 + r0, t * b8 + r0 + SUBLANES)
                sre, sim = (are * sre - aim * sim + bre_ref[rows, cs], are * sim + aim * sre + bim_ref[rows, cs])
                bre_ref[rows, cs] = sre
                bim_ref[rows, cs] = sim
            stre_ref[r0:r0 + SUBLANES, cs] = sre
            stim_ref[r0:r0 + SUBLANES, cs] = sim

    def project_out(c):
        cs = slice(c * kin, (c + 1) * kin)
        y = _dot(bre_ref[:, cs].astype(BF16), wcre_ref[c]) + _dot(bim_ref[:, cs].astype(BF16), wcim_ref[c])
        for i in range(wout // LANES):
            kb = c * (wout // LANES) + i
            xs_ref[kb] = y[:, i * LANES:(i + 1) * LANES] + xs_ref[kb] * d_ref[:, kb * LANES:(kb + 1) * LANES]

    project_in(0)
    for c in range(n_out):
        if c + 1 < n_out:
            project_in(c + 1)
        scan(c)
        if c >= 1:
            project_out(c - 1)
    project_out(n_out - 1)

    for b in range(nb):
        for kb in range(n_kb):
            y_ref[b, :, kb * LANES:(kb + 1) * LANES] = xs_ref[kb, pl.ds(b, tt, stride=b8), :]

    @pl.when(step == pl.num_programs(0) - 1)
    def _():
        sre_ref[...] = stre_ref[state_rows, :]
        sim_ref[...] = stim_ref[state_rows, :]


def _s5_weights(lam_re, lam_im, b_re, b_im, c_re, c_im, log_dt):
    g, p, h = b_re.shape
    lam = lax.complex(lam_re.astype(F32), lam_im.astype(F32))
    dt = jnp.exp(log_dt.astype(F32))[:, None]
    lam_bar = jnp.exp(lam * dt)
    b_bar = ((lam_bar - 1.0) / lam)[:, :, None] * lax.complex(b_re.astype(F32), b_im.astype(F32))
    gpk = LANES // h
    n_kb = g // gpk
    eye = jnp.eye(gpk, dtype=F32)

    def in_blocks(m):
        m = m.reshape(n_kb, gpk, p, h)
        return jnp.einsum("kgph,gf->kghfp", m, eye).reshape(n_kb, gpk * h, gpk * p).astype(BF16)

    gpo = 2 * LANES // h
    n_out = g // gpo
    eye_o = jnp.eye(gpo, dtype=F32)

    def out_blocks(m):
        m = m.reshape(n_out, gpo, h, p)
        return jnp.einsum("kghp,gf->kgpfh", m, eye_o).reshape(n_out, gpo * p, gpo * h).astype(BF16)

    return (in_blocks(jnp.real(b_bar)), in_blocks(jnp.imag(b_bar)),
            jnp.real(lam_bar).reshape(1, g * p), jnp.imag(lam_bar).reshape(1, g * p),
            out_blocks(c_re.astype(F32)), out_blocks(-c_im.astype(F32)))


def s5_scan(x, s0_re, s0_im, weights, d_skip, *, tt):
    nb, seq, d = x.shape
    wbre, wbim, are, aim, wcre, wcim = weights
    ns = are.shape[1]
    pack = 2 if (2 * nb == SUBLANES and tt % 2 == 0) else 1
    b8 = nb if pack == 2 else -(-nb // SUBLANES) * SUBLANES
    rows = tt * b8
    state_rows = SUBLANES if pack == 2 else b8
    const = lambda a: pl.BlockSpec(a.shape, lambda i: (0,) * a.ndim)
    xblk = pl.BlockSpec((nb, tt, d), lambda i: (0, i, 0))
    sblk = pl.BlockSpec((nb, ns), lambda i: (0, 0))
    return pl.pallas_call(
        functools.partial(_s5_scan_kernel, pack=pack),
        grid=(seq // tt,),
        in_specs=[xblk, sblk, sblk, const(wbre), const(wbim), const(are), const(aim), const(wcre), const(wcim),
                  pl.BlockSpec((1, d), lambda i: (0, 0))],
        out_specs=[xblk, sblk, sblk],
        out_shape=[jax.ShapeDtypeStruct((nb, seq, d), F32), jax.ShapeDtypeStruct((nb, ns), F32),
                   jax.ShapeDtypeStruct((nb, ns), F32)],
        scratch_shapes=[pltpu.VMEM((d // LANES, rows, LANES), F32), pltpu.VMEM((rows, ns), F32),
                        pltpu.VMEM((rows, ns), F32),
                        pltpu.VMEM((state_rows, ns), F32), pltpu.VMEM((state_rows, ns), F32)],
        compiler_params=_params("arbitrary"),
    )(x, s0_re, s0_im, wbre, wbim, are, aim, wcre, wcim, d_skip)


def _s5_out_ln_kernel(y_ref, wo_ref, wg_ref, x_ref, g_ref, b_ref, o_ref, *, alpha):
    gl = jax.nn.gelu(y_ref[...]).astype(BF16)
    mix = _dot(gl, wo_ref[...]) * jax.nn.sigmoid(_dot(gl, wg_ref[...]))
    o_ref[...] = _layer_norm(alpha * x_ref[...] + mix, g_ref[...], b_ref[...])


def s5_out_ln(y, w_out, w_gate, x, g, b, *, layer, alpha, tm):
    t, d = x.shape
    row = pl.BlockSpec((tm, d), lambda i: (i, 0))
    const = lambda r, c: pl.BlockSpec((r, c), lambda i: (0, 0))
    return pl.pallas_call(
        functools.partial(_s5_out_ln_kernel, alpha=alpha),
        grid=(t // tm,),
        in_specs=[row, _layer_weight(layer, d, d), _layer_weight(layer, d, d), row, const(1, d), const(1, d)],
        out_specs=row,
        out_shape=jax.ShapeDtypeStruct((t, d), F32),
        compiler_params=_params("parallel"),
    )(y, w_out, w_gate, x, g, b)


class _TilePlan(NamedTuple):
    token_rows: int
    proj_rows: int
    s5_steps: int
    ret_tiles: int
    ret_seqs_per_tile: int


def _tile_plan(lp, ls):
    return _TilePlan(token_rows=min(1024, lp), proj_rows=min(512, lp), s5_steps=min(64, lp),
                     ret_tiles=min(4, lp // RET_CHUNK), ret_seqs_per_tile=max(RET_CHUNK // ls, 1))


def _rope_tables(pos, n_heads):
    half = HEAD_DIM // 2
    inv = ROPE_THETA ** (-jnp.arange(half, dtype=F32) / half)
    ang = pos.astype(F32)[:, None] * inv[None, :]
    cos = jnp.cos(ang)
    sin = jnp.sin(ang)
    cos_h = jnp.concatenate([cos, cos], axis=-1)
    sin_h = jnp.concatenate([-sin, sin], axis=-1)
    return jnp.tile(cos_h, (1, n_heads)), jnp.tile(sin_h, (1, n_heads))


def kernel(x_prompt, x_sample, cache_k, cache_v, page_table, state_ret, state_s5_re, state_s5_im, ffn1_w_gate, ffn1_w_up, ffn1_w_down, ffn2_w_gate, ffn2_w_up, ffn2_w_down, ln_g, ln_b, w_in_ab, w_out_ab, s5_lam_re, s5_lam_im, s5_b_re, s5_b_im, s5_c_re, s5_c_im, s5_d, s5_log_dt, s5_w_out, s5_w_gate):
    bp, lp, d = x_prompt.shape
    bs, ls, _ = x_sample.shape
    depth = ffn1_w_gate.shape[0]
    n_layers_ab, n_pool, page, a_heads, hd = cache_k.shape
    n_pages = page_table.shape[1]
    past_len = n_pages * page
    half = a_heads * hd
    n_heads = half // HEAD_DIM
    assert hd == HEAD_DIM and lp % MOBA_BLOCK == 0 and past_len % MOBA_BLOCK == 0 and ls <= MOBA_BLOCK
    assert MOBA_BLOCK % page == 0 and lp % page == 0 and lp % RET_CHUNK == 0
    alpha = (2 * depth) ** 0.25
    tp, ts = bp * lp, bs * ls
    plan = _tile_plan(lp, ls)
    tm_p, tm_proj = plan.token_rows, plan.proj_rows

    xp = x_prompt.reshape(tp, d)
    xs = x_sample.reshape(ts, d)
    bf = lambda a: a.astype(BF16)
    f1 = (bf(ffn1_w_gate), bf(ffn1_w_up), bf(ffn1_w_down))
    f2 = (bf(ffn2_w_gate), bf(ffn2_w_up), bf(ffn2_w_down))
    w_in, w_out = bf(w_in_ab), bf(w_out_ab)
    w_o, w_g = bf(s5_w_out), bf(s5_w_gate)
    cos_p, sin_p = _rope_tables(jnp.arange(lp, dtype=jnp.int32), n_heads)
    cos_s, sin_s = _rope_tables(past_len + jnp.arange(ls, dtype=jnp.int32), n_heads)
    cos_s, sin_s = jnp.tile(cos_s, (bs, 1)), jnp.tile(sin_s, (bs, 1))
    pool_kt = jnp.transpose(cache_k, (0, 1, 3, 4, 2)).reshape(n_layers_ab * n_pool, half, page)
    pool_vt = jnp.transpose(cache_v, (0, 1, 3, 4, 2)).reshape(n_layers_ab * n_pool, half, page)
    seq_per_tile, ret_tiles_p = plan.ret_seqs_per_tile, plan.ret_tiles

    kpg = jnp.zeros((n_layers_ab, tp // page, half, page), F32)
    vpg = jnp.zeros((n_layers_ab, tp // page, half, page), F32)
    k_s, v_s, r_p, r_s = [], [], [], []
    sre_p, sim_p, sre_s, sim_s = [], [], [], []
    for layer in range(depth):
        li = layer // 2
        g = lambda i: ln_g[layer, i][None, :]
        b = lambda i: ln_b[layer, i][None, :]
        xp, xs = ffn_ln(xp, xs, *f1, g(0), b(0), layer=layer, alpha=alpha, tm=tm_p)
        if layer % 2 == 0:
            qa, kbf, vt, ksum, kpg, vpg, qb, kb, vb, gb = ab_project(xp, w_in, cos_p, sin_p, layer=li, tm=tm_proj,
                                                                     seq_len=lp, pages=(kpg, vpg), page=page)
            attn = moba_prompt(qa, ksum, kbf, vt, batch=bp, seq_len=lp)
            zero_state = jnp.zeros((bp, n_heads // 2, HEAD_PAIR, HEAD_PAIR), F32)
            ret, s_fin = retention_gated(qb, kb, vb, gb, zero_state, n_seq=1, chunk=RET_CHUNK, tiles=ret_tiles_p,
                                         steps_per_seq=lp // (RET_CHUNK * ret_tiles_p))
            xp = merge_ln(attn, ret, w_out, xp, g(1), b(1), layer=li, alpha=alpha, tm=tm_p)
            r_p.append(_block_diag_to_states(s_fin))
            qa, ka, va, qb, kb, vb, gb = ab_project(xs, w_in, cos_s, sin_s, layer=li, tm=ts, seq_len=ts)
            page_ids = (page_table.astype(jnp.int32) + li * n_pool).reshape(-1)
            attn = moba_sample(qa, ka, va, pool_kt, pool_vt, page_ids, batch=bs, n_pages=n_pages)
            ret, s_fin = retention_gated(qb, kb, vb, gb, _states_to_block_diag(state_ret[li]), n_seq=seq_per_tile,
                                         chunk=ls, tiles=1, steps_per_seq=1)
            xs = merge_ln(attn, ret, w_out, xs, g(1), b(1), layer=li, alpha=alpha, tm=ts)
            k_s.append(ka.reshape(bs, ls, a_heads, hd))
            v_s.append(va.reshape(bs, ls, a_heads, hd))
            r_s.append(_block_diag_to_states(s_fin))
        else:
            weights = _s5_weights(s5_lam_re[li], s5_lam_im[li], s5_b_re[li], s5_b_im[li], s5_c_re[li],
                                  s5_c_im[li], s5_log_dt[li])
            n_state = weights[2].shape[1]
            d_skip = s5_d[li][None, :]
            zero = jnp.zeros((bp, n_state), F32)
            y, a_re, a_im = s5_scan(xp.reshape(bp, lp, d), zero, zero, weights, d_skip, tt=plan.s5_steps)
            xp = s5_out_ln(y.reshape(tp, d), w_o, w_g, xp, g(1), b(1), layer=li, alpha=alpha, tm=tm_p)
            sre_p.append(a_re.reshape(bp, -1, S5_STATE))
            sim_p.append(a_im.reshape(bp, -1, S5_STATE))
            y, a_re, a_im = s5_scan(xs.reshape(bs, ls, d), state_s5_re[li].reshape(bs, n_state),
                                    state_s5_im[li].reshape(bs, n_state), weights, d_skip, tt=ls)
            xs = s5_out_ln(y.reshape(ts, d), w_o, w_g, xs, g(1), b(1), layer=li, alpha=alpha, tm=ts)
            sre_s.append(a_re.reshape(bs, -1, S5_STATE))
            sim_s.append(a_im.reshape(bs, -1, S5_STATE))
        xp, xs = ffn_ln(xp, xs, *f2, g(2), b(2), layer=layer, alpha=alpha, tm=tm_p)
    unpage = lambda t: jnp.transpose(t.reshape(n_layers_ab, bp, lp // page, a_heads, hd, page), (0, 1, 2, 5, 3, 4))
    return (xp.reshape(bp, lp, d), xs.reshape(bs, ls, d), unpage(kpg), unpage(vpg), jnp.stack(k_s),
            jnp.stack(v_s), jnp.stack(r_p), jnp.stack(r_s), jnp.stack(sre_p), jnp.stack(sim_p),
            jnp.stack(sre_s), jnp.stack(sim_s))
```

```python
import functools
import math
from typing import NamedTuple

import jax
import jax.numpy as jnp
from jax import lax
from jax.experimental import pallas as pl
from jax.experimental.pallas import tpu as pltpu

F32 = jnp.float32
BF16 = jnp.bfloat16

HEAD_DIM = 64
HEAD_PAIR = 2 * HEAD_DIM
MOBA_BLOCK = 256
MOBA_TOPK = 3
MOBA_BLOCKS_PER_TRIP = 4
RET_CHUNK = 128
S5_GROUP = 16
S5_STATE = 64
ROPE_THETA = 10000.0
LN_EPS = 1e-5
GN_EPS = 1e-6
NEG_INF = -1e30
LOG2_E = math.log2(math.e)
SUBLANES = 8
BF16_SUBLANES = 16
LANES = 128
VMEM_LIMIT = 48 * 1024 * 1024


def _dot(a, b, precision=None):
    return jnp.dot(a, b, preferred_element_type=F32, precision=precision)


def _dot_nt(a, b, precision=None):
    return lax.dot_general(a, b, (((1,), (1,)), ((), ())), preferred_element_type=F32, precision=precision)


def _dot_tn(a, b):
    return lax.dot_general(a, b, (((0,), (0,)), ((), ())), preferred_element_type=F32)


def _layer_norm(r, g, b):
    mu = jnp.mean(r, -1, keepdims=True)
    d = r - mu
    var = jnp.mean(d * d, -1, keepdims=True)
    return d * lax.rsqrt(var + LN_EPS) * g + b


def _silu(x):
    return x * jax.nn.sigmoid(x)


def _params(*sem):
    return pltpu.CompilerParams(dimension_semantics=sem, vmem_limit_bytes=VMEM_LIMIT)


def _ffn_ln_kernel(xp_ref, xs_ref, wg_ref, wu_ref, wd_ref, g_ref, b_ref, op_ref, os_ref, a_ref, *, alpha, tf):
    def run(x_ref, o_ref):
        rows = x_ref.shape[0]
        xb = x_ref[...].astype(BF16)
        for c in range(wg_ref.shape[1] // tf):
            cols = slice(c * tf, (c + 1) * tf)
            hg = _dot(xb, wg_ref[:, cols])
            hu = _dot(xb, wu_ref[:, cols])
            a_ref[0:rows, cols] = (_silu(hg) * hu).astype(BF16)
        r = alpha * x_ref[...] + 0.5 * _dot(a_ref[0:rows, :], wd_ref[...])
        o_ref[...] = _layer_norm(r, g_ref[...], b_ref[...])

    last = pl.num_programs(0) - 1
    pl.when(pl.program_id(0) < last)(lambda: run(xp_ref, op_ref))
    pl.when(pl.program_id(0) == last)(lambda: run(xs_ref, os_ref))


def _layer_weight(layer, r, c, row_blk=0, **kw):
    return pl.BlockSpec((None, r, c), lambda *_: (layer, row_blk, 0), **kw)


def ffn_ln(xp, xs, wg, wu, wd, g, b, *, layer, alpha, tm, tf=256):
    tp, d = xp.shape
    ts = xs.shape[0]
    f = wg.shape[2]
    n_p = tp // tm
    assert ts <= tm
    resident = lambda r, c: _layer_weight(layer, r, c, pipeline_mode=pl.Buffered(1))
    p_rows = pl.BlockSpec((tm, d), lambda i: (jnp.minimum(i, n_p - 1), 0))
    s_rows = pl.BlockSpec((ts, d), lambda i: (0, 0))
    return pl.pallas_call(
        functools.partial(_ffn_ln_kernel, alpha=alpha, tf=tf),
        grid=(n_p + 1,),
        in_specs=[p_rows, s_rows, resident(d, f), resident(d, f), resident(f, d),
                  pl.BlockSpec((1, d), lambda i: (0, 0)), pl.BlockSpec((1, d), lambda i: (0, 0))],
        out_specs=[p_rows, s_rows],
        out_shape=[jax.ShapeDtypeStruct((tp, d), F32), jax.ShapeDtypeStruct((ts, d), F32)],
        scratch_shapes=[pltpu.VMEM((tm, f), BF16)],
        compiler_params=_params("arbitrary"),
    )(xp, xs, wg, wu, wd, g, b)


def _rope(y, cos, sin_signed):
    width = y.shape[-1]
    lane = lax.broadcasted_iota(jnp.int32, (1, width), 1)
    first = (lane % HEAD_DIM) < (HEAD_DIM // 2)
    rot = jnp.where(first, pltpu.roll(y, width - HEAD_DIM // 2, 1), pltpu.roll(y, HEAD_DIM // 2, 1))
    return y * cos + rot * sin_signed


def _ab_proj_kernel(*refs, half, scale, page):
    if page is None:
        x_ref, w_ref, cos_ref, sin_ref, qa_ref, ka_ref, va_ref, qb_ref, kb_ref, vb_ref, gb_ref = refs
    else:
        (x_ref, w_ref, cos_ref, sin_ref, _, _, qa_ref, kbf_ref, vt_ref, ksum_ref, kpg_ref, vpg_ref,
         qb_ref, kb_ref, vb_ref, gb_ref) = refs
    xb = x_ref[...].astype(BF16)
    cos = cos_ref[...]
    sin = sin_ref[...]
    col = lambda c: _dot(xb, w_ref[:, c * half:(c + 1) * half])
    qa_ref[...] = _rope(col(0), cos, sin) * scale
    ka = _rope(col(1), cos, sin)
    va = col(2)
    if page is None:
        ka_ref[...] = ka
        va_ref[...] = va
    else:
        kbf_ref[...] = ka.astype(BF16)
        for n in range(ka.shape[0] // MOBA_BLOCK):
            blk = slice(n * MOBA_BLOCK, (n + 1) * MOBA_BLOCK)
            ksum_ref[n] = jnp.sum(ka[blk], axis=0, keepdims=True)
            vt_ref[n] = va[blk].T.astype(BF16)
        for n in range(ka.shape[0] // page):
            rows = slice(n * page, (n + 1) * page)
            kpg_ref[n] = ka[rows].T
            vpg_ref[n] = va[rows].T
    qb_ref[...] = _rope(col(3), cos, sin)
    kb_ref[...] = _rope(col(4), cos, sin) * scale
    vb_ref[...] = col(5)
    gb_ref[...] = col(6)


def ab_project(x, w_in, cos, sin, *, layer, tm, seq_len, pages=None, page=None):
    t, d = x.shape
    half = w_in.shape[2] // 7
    tiles_per_seq = max(seq_len // tm, 1)
    tab = pl.BlockSpec((tm, half), lambda i: (i % tiles_per_seq, 0))
    row = pl.BlockSpec((tm, half), lambda i: (i, 0))
    f32o = jax.ShapeDtypeStruct((t, half), F32)
    in_specs = [pl.BlockSpec((tm, d), lambda i: (i, 0)), _layer_weight(layer, d, 7 * half), tab, tab]
    operands = [x, w_in, cos, sin]
    aliases = {}
    if pages is None:
        out_specs = [row] * 7
        out_shape = [f32o] * 7
    else:
        nblk = tm // MOBA_BLOCK
        page_blk = pl.BlockSpec((None, tm // page, half, page), lambda i: (layer, i, 0, 0))
        in_specs += [pl.BlockSpec(memory_space=pl.ANY)] * 2
        operands += list(pages)
        aliases = {4: 4, 5: 5}
        out_specs = [row, row,
                     pl.BlockSpec((nblk, half, MOBA_BLOCK), lambda i: (i, 0, 0)),
                     pl.BlockSpec((nblk, 1, half), lambda i: (i, 0, 0)),
                     page_blk, page_blk, row, row, row, row]
        out_shape = [f32o, jax.ShapeDtypeStruct((t, half), BF16),
                     jax.ShapeDtypeStruct((t // MOBA_BLOCK, half, MOBA_BLOCK), BF16),
                     jax.ShapeDtypeStruct((t // MOBA_BLOCK, 1, half), F32),
                     jax.ShapeDtypeStruct(pages[0].shape, F32), jax.ShapeDtypeStruct(pages[1].shape, F32),
                     f32o, f32o, f32o, f32o]
    return pl.pallas_call(
        functools.partial(_ab_proj_kernel, half=half, scale=HEAD_DIM ** -0.5, page=page),
        grid=(t // tm,),
        in_specs=in_specs,
        out_specs=out_specs,
        out_shape=out_shape,
        input_output_aliases=aliases,
        compiler_params=_params("parallel"),
    )(*operands)


def _topk_select(gate, blk, n_valid, axis):
    gate = jnp.where(blk < n_valid, gate, NEG_INF)
    taken = jnp.zeros(gate.shape, F32)
    blk_f = blk.astype(F32)
    for _ in range(MOBA_TOPK):
        top = jnp.max(gate, axis=axis, keepdims=True)
        first = jnp.min(jnp.where(gate == top, blk_f, float(gate.shape[axis])), axis=axis, keepdims=True)
        hit = blk_f == first
        taken = jnp.where(hit, 1.0, taken)
        gate = jnp.where(hit, -jnp.inf, gate)
    return jnp.where(blk < n_valid, taken, 0.0)


def _moba_prompt_kernel(q_ref, ksum_ref, k_ref, vt_ref, o_ref, qm_ref, sel_ref, m_ref, l_ref, acc_ref, s_ref):
    j = pl.program_id(1)
    tq = q_ref.shape[0]
    n_heads = q_ref.shape[1] // HEAD_DIM
    lane = lax.broadcasted_iota(jnp.int32, (1, HEAD_PAIR), 1)
    key_i = lax.broadcasted_iota(jnp.int32, (MOBA_BLOCK, tq), 0)
    qry_i = lax.broadcasted_iota(jnp.int32, (MOBA_BLOCK, tq), 1)
    causal = key_i <= qry_i

    nb = ksum_ref.shape[0]
    blk = lax.broadcasted_iota(jnp.int32, (nb, 1), 0)
    for hp in range(n_heads // 2):
        lanes = slice(hp * HEAD_PAIR, (hp + 1) * HEAD_PAIR)
        qp = q_ref[:, lanes]
        kmean = ksum_ref[:, 0, lanes] * (1.0 / MOBA_BLOCK)
        masks = [(lane // HEAD_DIM) == half for half in range(2)]
        gate = _dot_nt(jnp.concatenate([jnp.where(hm, kmean, 0.0) for hm in masks], axis=0), qp,
                       precision=lax.Precision.HIGHEST)
        for half, hm in enumerate(masks):
            h = 2 * hp + half
            sel_ref[h] = _topk_select(gate[half * nb:(half + 1) * nb], blk, j, 0)
            qm_ref[h] = jnp.where(hm, qp * LOG2_E, 0.0).astype(BF16)

    ones_rows = jnp.ones((BF16_SUBLANES, MOBA_BLOCK), BF16)

    def update(blocks, first):
        def rows0(n):
            return pl.multiple_of((j if n is None else n) * MOBA_BLOCK, MOBA_BLOCK)

        def picked(n, h):
            return sel_ref[h, pl.ds(n, 1), :] > 0.0

        def scores(slot, n, h):
            k_pair = k_ref[pl.ds(rows0(n), MOBA_BLOCK), (h // 2) * HEAD_PAIR:(h // 2 + 1) * HEAD_PAIR]
            s_ref[slot, h] = _dot_nt(k_pair, qm_ref[h])

        def softmax_values(slot, n, h):
            if first:
                s = jnp.where(causal, s_ref[slot, h], NEG_INF)
                m_new = jnp.max(s, axis=0, keepdims=True)
            else:
                s = s_ref[slot, h]
                cmax = jnp.where(picked(n, h), jnp.max(s, axis=0, keepdims=True), NEG_INF)
                m_new = jnp.maximum(m_ref[h], cmax)
                a = jnp.exp2(m_ref[h] - m_new)
            p = jnp.exp2(s - m_new).astype(BF16)
            m_ref[h] = m_new
            v_h = vt_ref[j if n is None else n, h * HEAD_DIM:(h + 1) * HEAD_DIM, :]
            res = _dot(jnp.concatenate([v_h, ones_rows], axis=0), p)
            pv, psum = res[:HEAD_DIM], res[HEAD_DIM:HEAD_DIM + 1]
            if first:
                acc_ref[h] = pv
                l_ref[h] = psum
            else:
                acc_ref[h] = a * acc_ref[h] + jnp.where(picked(n, h), pv, 0.0)
                l_ref[h] = a * l_ref[h] + jnp.where(picked(n, h), psum, 0.0)

        heads = range(n_heads)
        stages = (scores, softmax_values)
        for t in range(len(stages) + len(blocks) - 1):
            for h in heads:
                for slot, n in enumerate(blocks):
                    if 0 <= t - slot < len(stages):
                        stages[t - slot](slot, n, h)

    update([None], True)

    n_slots = s_ref.shape[0]

    def body(i, carry):
        update([n_slots * i + k for k in range(n_slots)], False)
        return carry

    lax.fori_loop(0, j // n_slots, body, 0)
    for rem in range(1, n_slots):
        @pl.when(j % n_slots == rem)
        def _(rem=rem):
            update([j - rem + k for k in range(rem)], False)

    out_t = jnp.concatenate([acc_ref[h] / l_ref[h] for h in range(n_heads)], axis=0)
    o_ref[...] = out_t.T.astype(o_ref.dtype)


def moba_prompt(q, ksum, kbf, vt, *, batch, seq_len):
    t, w = q.shape
    nb = seq_len // MOBA_BLOCK
    n_heads = w // HEAD_DIM
    return pl.pallas_call(
        _moba_prompt_kernel,
        grid=(batch, nb),
        in_specs=[pl.BlockSpec((MOBA_BLOCK, w), lambda b, j: (b * nb + j, 0)),
                  pl.BlockSpec((nb, 1, w), lambda b, j: (b, 0, 0)),
                  pl.BlockSpec((seq_len, w), lambda b, j: (b, 0)),
                  pl.BlockSpec((nb, w, MOBA_BLOCK), lambda b, j: (b, 0, 0))],
        out_specs=pl.BlockSpec((MOBA_BLOCK, w), lambda b, j: (b * nb + j, 0)),
        out_shape=jax.ShapeDtypeStruct((t, w), BF16),
        scratch_shapes=[pltpu.VMEM((n_heads, MOBA_BLOCK, HEAD_PAIR), BF16),
                        pltpu.VMEM((n_heads, nb, MOBA_BLOCK), F32),
                        pltpu.VMEM((n_heads, 1, MOBA_BLOCK), F32),
                        pltpu.VMEM((n_heads, 1, MOBA_BLOCK), F32),
                        pltpu.VMEM((n_heads, HEAD_DIM, MOBA_BLOCK), F32),
                        pltpu.VMEM((MOBA_BLOCKS_PER_TRIP, n_heads, MOBA_BLOCK, MOBA_BLOCK), F32)],
        compiler_params=_params("parallel", "arbitrary"),
    )(q, ksum, kbf, vt)


def _moba_sample_kernel(pt_ref, q_ref, knew_ref, vnew_ref, *rest, pages_per_step, page, n_heads):
    del pt_ref
    kpages = rest[:pages_per_step]
    vpages = rest[pages_per_step:2 * pages_per_step]
    o_ref = rest[2 * pages_per_step]
    qf_ref, qb_ref, ksum_ref, m_ref, l_ref, acc_ref, kpad_ref, vpad_ref = rest[2 * pages_per_step + 1:]
    s_idx = pl.program_id(1)
    n_steps = pl.num_programs(1)
    lq, w = q_ref.shape
    rows = n_heads * lq
    nb = acc_ref.shape[0]
    pages_per_blk = MOBA_BLOCK // page
    blks_per_step = pages_per_step // pages_per_blk
    rowhead = lax.broadcasted_iota(jnp.int32, (rows, 1), 0) // lq
    lanehead = lax.broadcasted_iota(jnp.int32, (1, w), 1) // HEAD_DIM
    blk_lane = lax.broadcasted_iota(jnp.int32, (1, LANES), 1)

    @pl.when(s_idx == 0)
    def _():
        qt = jnp.concatenate([q_ref[...]] * n_heads, axis=0)
        qbd = jnp.where(rowhead == lanehead, qt, 0.0)
        qf_ref[...] = qbd
        qb_ref[...] = qbd.astype(BF16)
        ksum_ref[...] = jnp.zeros_like(ksum_ref)
        m_ref[...] = jnp.zeros_like(m_ref)
        l_ref[...] = jnp.zeros_like(l_ref)

    qb = qb_ref[...]

    def partial_softmax(s):
        m = jnp.max(s, axis=1, keepdims=True)
        e = jnp.exp(s - m)
        return m, jnp.sum(e, axis=1, keepdims=True), e.astype(BF16)

    blks = range(blks_per_step)
    pages_of = lambda bi: range(bi * pages_per_blk, (bi + 1) * pages_per_blk)
    here = [blk_lane == s_idx * blks_per_step + bi for bi in blks]
    kt = [jnp.concatenate([kpages[i][0] for i in pages_of(bi)], axis=1) for bi in blks]
    scores = [_dot(qb, kt[bi].astype(BF16)) for bi in blks]
    ksum = ksum_ref[...]
    for bi in blks:
        ksum = jnp.where(here[bi], jnp.sum(kt[bi], axis=1, keepdims=True), ksum)
    ksum_ref[...] = ksum
    stats = [partial_softmax(scores[bi]) for bi in blks]
    for bi in blks:
        vt = jnp.concatenate([vpages[i][0] for i in pages_of(bi)], axis=1)
        acc_ref[s_idx * blks_per_step + bi] = _dot_nt(stats[bi][2], vt.astype(BF16))
    m_all, l_all = m_ref[...], l_ref[...]
    for bi in blks:
        m_all = jnp.where(here[bi], stats[bi][0], m_all)
        l_all = jnp.where(here[bi], stats[bi][1], l_all)
    m_ref[...] = m_all
    l_ref[...] = l_all

    @pl.when(s_idx == n_steps - 1)
    def _():
        kpad_ref[...] = jnp.zeros_like(kpad_ref)
        vpad_ref[...] = jnp.zeros_like(vpad_ref)
        kpad_ref[0:lq, :] = knew_ref[...]
        vpad_ref[0:lq, :] = vnew_ref[...]
        s_own = _dot_nt(qb, kpad_ref[...].astype(BF16))
        tq = lax.broadcasted_iota(jnp.int32, s_own.shape, 0) % lq
        tk = lax.broadcasted_iota(jnp.int32, s_own.shape, 1)
        m_own, l_own, e_own = partial_softmax(jnp.where(tk <= tq, s_own, NEG_INF))
        acc_own = _dot(e_own, vpad_ref[...].astype(BF16))

        gate = _dot(qf_ref[...], ksum_ref[...] * (1.0 / MOBA_BLOCK), precision=lax.Precision.HIGHEST)
        sel = _topk_select(gate[:, :nb], blk_lane[:, :nb], nb, 1) > 0.0
        m_blk = m_ref[:, :nb]
        m_all = jnp.maximum(m_own, jnp.max(jnp.where(sel, m_blk, NEG_INF), axis=1, keepdims=True))
        wgt = jnp.where(sel, jnp.exp(m_blk - m_all), 0.0)
        w_own = jnp.exp(m_own - m_all)
        den = w_own * l_own + jnp.sum(wgt * l_ref[:, :nb], axis=1, keepdims=True)
        num = w_own * acc_own
        for n in range(nb):
            num = num + wgt[:, n:n + 1] * acc_ref[n]
        o_all = num / den
        out = jnp.zeros((lq, w), F32)
        for h in range(n_heads):
            out = out + jnp.where(lanehead == h, o_all[h * lq:(h + 1) * lq], 0.0)
        o_ref[...] = out


def moba_sample(q, knew, vnew, pool_kt, pool_vt, page_ids, *, batch, n_pages, pages_per_step=32):
    t, w = q.shape
    lq = t // batch
    page = pool_kt.shape[2]
    n_heads = w // HEAD_DIM
    rows = n_heads * lq
    nb = n_pages * page // MOBA_BLOCK
    assert nb <= LANES and pages_per_step % (MOBA_BLOCK // page) == 0
    seq_blk = pl.BlockSpec((lq, w), lambda b, s, pt: (b, 0))

    def page_spec(i):
        return pl.BlockSpec((1, w, page), lambda b, s, pt: (pt[b * n_pages + s * pages_per_step + i], 0, 0))

    grid_spec = pltpu.PrefetchScalarGridSpec(
        num_scalar_prefetch=1,
        grid=(batch, n_pages // pages_per_step),
        in_specs=[seq_blk, seq_blk, seq_blk] + [page_spec(i) for i in range(pages_per_step)] * 2,
        out_specs=seq_blk,
        scratch_shapes=[pltpu.VMEM((rows, w), F32), pltpu.VMEM((rows, w), BF16), pltpu.VMEM((w, LANES), F32),
                        pltpu.VMEM((rows, LANES), F32), pltpu.VMEM((rows, LANES), F32),
                        pltpu.VMEM((nb, rows, w), F32),
                        pltpu.VMEM((LANES, w), F32), pltpu.VMEM((LANES, w), F32)],
    )
    return pl.pallas_call(
        functools.partial(_moba_sample_kernel, pages_per_step=pages_per_step, page=page, n_heads=n_heads),
        grid_spec=grid_spec,
        out_shape=jax.ShapeDtypeStruct((t, w), F32),
        compiler_params=_params("parallel", "arbitrary"),
    )(page_ids, q, knew, vnew, *([pool_kt] * pages_per_step), *([pool_vt] * pages_per_step))


def _retention_kernel(q_ref, k_ref, v_ref, g_ref, s0_ref, dmask_ref, qdec_ref, kdec_ref, cdec_ref,
                      o_ref, sout_ref, st_ref, *, n_seq, chunk, tiles, steps_per_seq):
    step = pl.program_id(0)
    tile_rows = n_seq * chunk
    n_pairs = q_ref.shape[1] // HEAD_PAIR
    lane = lax.broadcasted_iota(jnp.int32, (1, HEAD_PAIR), 1)
    rowh = lax.broadcasted_iota(jnp.int32, (HEAD_PAIR, 1), 0) // HEAD_DIM
    same_head = rowh == (lane // HEAD_DIM)
    rseq = lax.broadcasted_iota(jnp.int32, (tile_rows, 1), 0) // chunk

    @pl.when(step % steps_per_seq == 0)
    def _():
        st_ref[...] = s0_ref[...]

    pairs = range(n_pairs)
    lanes_of = lambda hp: slice(hp * HEAD_PAIR, (hp + 1) * HEAD_PAIR)
    half_masks = [(lane // HEAD_DIM) == half for half in range(2)]
    for ti in range(tiles):
        rows = slice(ti * tile_rows, (ti + 1) * tile_rows)
        qb = [q_ref[rows, lanes_of(hp)].astype(BF16) for hp in pairs]
        kf = [k_ref[rows, lanes_of(hp)] for hp in pairs]
        kb = [k.astype(BF16) for k in kf]
        vb = [v_ref[rows, lanes_of(hp)].astype(BF16) for hp in pairs]
        inner = [[_dot_nt(jnp.where(hm, qb[hp], jnp.zeros_like(qb[hp])), kb[hp]) for hm in half_masks]
                 for hp in pairs]
        scaled = [[(inner[hp][half] * dmask_ref[2 * hp + half]).astype(BF16) for half in range(2)] for hp in pairs]
        ret = [sum(jnp.where(half_masks[half], _dot(scaled[hp][half], vb[hp]), 0.0) for half in range(2))
               for hp in pairs]
        for hp in pairs:
            kd = (kf[hp] * kdec_ref[:, lanes_of(hp)]).astype(BF16)
            cdec = cdec_ref[:, lanes_of(hp)]
            carried = []
            for s in range(n_seq):
                srows = slice(s * chunk, (s + 1) * chunk)
                state = st_ref[s, hp]
                carried.append(_dot(qb[hp][srows], state.astype(BF16)))
                kd_s = kd if n_seq == 1 else jnp.where(rseq == s, kd, jnp.zeros_like(kd))
                st_ref[s, hp] = cdec * state + jnp.where(same_head, _dot_tn(kd_s, vb[hp]), 0.0)
            carried = carried[0] if n_seq == 1 else jnp.concatenate(carried, axis=0)
            ret[hp] = ret[hp] + carried * qdec_ref[:, lanes_of(hp)]
        for hp in pairs:
            normed = jnp.zeros_like(ret[hp])
            for hm in half_masks:
                mu = jnp.sum(jnp.where(hm, ret[hp], 0.0), axis=1, keepdims=True) * (1.0 / HEAD_DIM)
                d = jnp.where(hm, ret[hp] - mu, 0.0)
                var = jnp.sum(d * d, axis=1, keepdims=True) * (1.0 / HEAD_DIM)
                normed = normed + d * lax.rsqrt(var + GN_EPS)
            o_ref[rows, lanes_of(hp)] = (_silu(g_ref[rows, lanes_of(hp)]) * normed).astype(o_ref.dtype)

    @pl.when(step % steps_per_seq == steps_per_seq - 1)
    def _():
        sout_ref[...] = st_ref[...]


def _retention_tables(n_heads, n_seq, chunk):
    log_g = jnp.log1p(-jnp.exp2(-5.0 - jnp.arange(n_heads, dtype=F32)))
    idx = jnp.arange(chunk, dtype=F32)
    diff = idx[:, None] - idx[None, :]
    dmask = jnp.where(diff >= 0, jnp.exp(log_g[:, None, None] * jnp.maximum(diff, 0.0)), 0.0)
    q_dec = jnp.exp(log_g[None, :] * (idx[:, None] + 1.0))
    k_dec = jnp.exp(log_g[None, :] * (chunk - 1.0 - idx[:, None]))
    c_dec = jnp.exp(log_g * chunk)
    seq_eye = jnp.eye(n_seq, dtype=F32)
    dmask = jnp.einsum("ab,hij->haibj", seq_eye, dmask).reshape(n_heads, n_seq * chunk, n_seq * chunk)
    lanes = lambda t: jnp.repeat(t, HEAD_DIM, axis=-1)
    return dmask, jnp.tile(lanes(q_dec), (n_seq, 1)), jnp.tile(lanes(k_dec), (n_seq, 1)), lanes(c_dec[None, :])


def retention_gated(q, k, v, g, s0_bd, *, n_seq, chunk, tiles, steps_per_seq):
    t, w = q.shape
    n_pairs = w // HEAD_PAIR
    rows = n_seq * chunk * tiles
    n_steps = t // rows
    dmask, qdec, kdec, cdec = _retention_tables(w // HEAD_DIM, n_seq, chunk)
    tile_rows = n_seq * chunk
    row = pl.BlockSpec((rows, w), lambda i: (i, 0))
    st = pl.BlockSpec((n_seq, n_pairs, HEAD_PAIR, HEAD_PAIR), lambda i: (i // steps_per_seq, 0, 0, 0))
    const = lambda shape: pl.BlockSpec(shape, lambda i: (0,) * len(shape))
    return pl.pallas_call(
        functools.partial(_retention_kernel, n_seq=n_seq, chunk=chunk, tiles=tiles, steps_per_seq=steps_per_seq),
        grid=(n_steps,),
        in_specs=[row, row, row, row, st, const(dmask.shape), const((tile_rows, w)), const((tile_rows, w)),
                  const((1, w))],
        out_specs=[row, st],
        out_shape=[jax.ShapeDtypeStruct((t, w), BF16), jax.ShapeDtypeStruct(s0_bd.shape, F32)],
        scratch_shapes=[pltpu.VMEM((n_seq, n_pairs, HEAD_PAIR, HEAD_PAIR), F32)],
        compiler_params=_params("arbitrary"),
    )(q, k, v, g, s0_bd, dmask, qdec, kdec, cdec)


def _states_to_block_diag(s):
    b, h, dk, dv = s.shape
    sp = s.reshape(b, h // 2, 2, dk, dv)
    z = jnp.zeros_like(sp[:, :, 0])
    top = jnp.concatenate([sp[:, :, 0], z], axis=-1)
    bot = jnp.concatenate([z, sp[:, :, 1]], axis=-1)
    return jnp.concatenate([top, bot], axis=-2)


def _block_diag_to_states(s):
    b, p = s.shape[:2]
    a = s[:, :, :HEAD_DIM, :HEAD_DIM]
    c = s[:, :, HEAD_DIM:, HEAD_DIM:]
    return jnp.stack([a, c], axis=2).reshape(b, 2 * p, HEAD_DIM, HEAD_DIM)


def _merge_ln_kernel(a_ref, r_ref, wa_ref, wr_ref, x_ref, g_ref, b_ref, o_ref, *, alpha):
    mix = _dot(a_ref[...].astype(BF16), wa_ref[...]) + _dot(r_ref[...].astype(BF16), wr_ref[...])
    o_ref[...] = _layer_norm(alpha * x_ref[...] + mix, g_ref[...], b_ref[...])


def merge_ln(attn, ret, w_out, x, g, b, *, layer, alpha, tm):
    t, d = x.shape
    w = attn.shape[1]
    row = lambda n: pl.BlockSpec((tm, n), lambda i: (i, 0))
    const = lambda r, c: pl.BlockSpec((r, c), lambda i: (0, 0))
    return pl.pallas_call(
        functools.partial(_merge_ln_kernel, alpha=alpha),
        grid=(t // tm,),
        in_specs=[row(w), row(w), _layer_weight(layer, w, d, 0), _layer_weight(layer, w, d, 1), row(d),
                  const(1, d), const(1, d)],
        out_specs=row(d),
        out_shape=jax.ShapeDtypeStruct((t, d), F32),
        compiler_params=_params("parallel"),
    )(attn, ret, w_out, w_out, x, g, b)


def _s5_scan_kernel(x_ref, s0re_ref, s0im_ref, wbre_ref, wbim_ref, are_ref, aim_ref, wcre_ref, wcim_ref, d_ref,
                    y_ref, sre_ref, sim_ref, xs_ref, bre_ref, bim_ref, stre_ref, stim_ref, xtail_ref, *, pack):
    step = pl.program_id(0)
    nb, tt, d = x_ref.shape
    b8 = xs_ref.shape[1] // tt
    ns = stre_ref.shape[1]
    n_kb = d // LANES
    per_kb = ns // n_kb
    state_rows = slice((pack - 1) * nb, pack * nb)

    @pl.when(step == 0)
    def _():
        stre_ref[...] = jnp.zeros_like(stre_ref)
        stim_ref[...] = jnp.zeros_like(stim_ref)
        for r in range(pack):
            stre_ref[r * nb:(r + 1) * nb, :] = s0re_ref[...]
            stim_ref[r * nb:(r + 1) * nb, :] = s0im_ref[...]
        xs_ref[...] = jnp.zeros_like(xs_ref)
        xtail_ref[...] = jnp.zeros_like(xtail_ref)

    for b in range(nb):
        for kb in range(n_kb):
            xs_ref[kb, pl.ds(b, tt, stride=b8), :] = x_ref[b, :, kb * LANES:(kb + 1) * LANES]

    n_out = wcre_ref.shape[0]
    kin = ns // n_out
    wout = d // n_out
    kb_per_chunk = kin // per_kb

    first_half = lax.broadcasted_iota(jnp.int32, (SUBLANES, 1), 0) < nb

    def project_in(c):
        for kb in range(c * kb_per_chunk, (c + 1) * kb_per_chunk):
            cols = slice(kb * per_kb, (kb + 1) * per_kb)
            x_t = xs_ref[kb]
            if pack == 1:
                lhs = x_t.astype(BF16)
                w_re, w_im = wbre_ref[kb, 0:LANES, :], wbim_ref[kb, 0:LANES, :]
            else:
                x_prev = pltpu.roll(x_t, nb, 0)
                head = jnp.where(first_half, pltpu.roll(xtail_ref[kb], nb, 0), x_prev[0:SUBLANES])
                x_prev = jnp.concatenate([head, x_prev[SUBLANES:]], axis=0)
                xtail_ref[kb] = x_t[x_t.shape[0] - SUBLANES:]
                lhs = jnp.concatenate([x_t, x_prev], axis=1).astype(BF16)
                w_re, w_im = wbre_ref[kb], wbim_ref[kb]
            bre_ref[:, cols] = _dot(lhs, w_re)
            bim_ref[:, cols] = _dot(lhs, w_im)

    def scan_packed(cs):
        width = cs.stop - cs.start
        are = jnp.broadcast_to(are_ref[:, cs], (SUBLANES, width))
        aim = jnp.broadcast_to(aim_ref[:, cs], (SUBLANES, width))
        a2re, a2im = are * are - aim * aim, 2.0 * are * aim
        sre, sim = stre_ref[:, cs], stim_ref[:, cs]
        for v in range(tt // pack):
            rows = slice(v * SUBLANES, (v + 1) * SUBLANES)
            if v == 0:
                start = jnp.logical_and(first_half, step == 0)
                cre, cim = jnp.where(start, are, a2re), jnp.where(start, aim, a2im)
            else:
                cre, cim = a2re, a2im
            sre, sim = (bre_ref[rows, cs] + (cre * sre - cim * sim), bim_ref[rows, cs] + (cre * sim + cim * sre))
            bre_ref[rows, cs] = sre
            bim_ref[rows, cs] = sim
        stre_ref[:, cs] = sre
        stim_ref[:, cs] = sim

    def scan(c):
        if pack == 2:
            half = kin // 2
            for c0 in range(c * kin, (c + 1) * kin, half):
                scan_packed(slice(c0, c0 + half))
            return
        cs = slice(c * kin, (c + 1) * kin)
        are = jnp.broadcast_to(are_ref[:, cs], (SUBLANES, kin))
        aim = jnp.broadcast_to(aim_ref[:, cs], (SUBLANES, kin))
        for r0 in range(0, b8, SUBLANES):
            sre = stre_ref[r0:r0 + SUBLANES, cs]
            sim = stim_ref[r0:r0 + SUBLANES, cs]
            for t in range(tt):
                rows = slice(t * b8 + r0, t * b8 + r0 + SUBLANES)
                sre, sim = (are * sre - aim * sim + bre_ref[rows, cs], are * sim + aim * sre + bim_ref[rows, cs])
                bre_ref[rows, cs] = sre
                bim_ref[rows, cs] = sim
            stre_ref[r0:r0 + SUBLANES, cs] = sre
            stim_ref[r0:r0 + SUBLANES, cs] = sim

    def project_out(c):
        cs = slice(c * kin, (c + 1) * kin)
        y = _dot(bre_ref[:, cs].astype(BF16), wcre_ref[c]) + _dot(bim_ref[:, cs].astype(BF16), wcim_ref[c])
        for i in range(wout // LANES):
            kb = c * (wout // LANES) + i
            xs_ref[kb] = y[:, i * LANES:(i + 1) * LANES] + xs_ref[kb] * d_ref[:, kb * LANES:(kb + 1) * LANES]

    project_in(0)
    for c in range(n_out):
        if c + 1 < n_out:
            project_in(c + 1)
        scan(c)
        if c >= 1:
            project_out(c - 1)
    project_out(n_out - 1)

    for b in range(nb):
        for kb in range(n_kb):
            y_ref[b, :, kb * LANES:(kb + 1) * LANES] = xs_ref[kb, pl.ds(b, tt, stride=b8), :]

    @pl.when(step == pl.num_programs(0) - 1)
    def _():
        sre_ref[...] = stre_ref[state_rows, :]
        sim_ref[...] = stim_ref[state_rows, :]


def _s5_weights(lam_re, lam_im, b_re, b_im, c_re, c_im, log_dt):
    g, p, h = b_re.shape
    lam = lax.complex(lam_re.astype(F32), lam_im.astype(F32))
    dt = jnp.exp(log_dt.astype(F32))[:, None]
    lam_bar = jnp.exp(lam * dt)
    b_bar = ((lam_bar - 1.0) / lam)[:, :, None] * lax.complex(b_re.astype(F32), b_im.astype(F32))
    gpk = LANES // h
    n_kb = g // gpk
    eye = jnp.eye(gpk, dtype=F32)

    def in_blocks(m):
        m = m.reshape(n_kb, gpk, p, h)
        return jnp.einsum("kgph,gf->kghfp", m, eye).reshape(n_kb, gpk * h, gpk * p).astype(BF16)

    gpo = 2 * LANES // h
    n_out = g // gpo
    eye_o = jnp.eye(gpo, dtype=F32)

    def out_blocks(m):
        m = m.reshape(n_out, gpo, h, p)
        return jnp.einsum("kghp,gf->kgpfh", m, eye_o).reshape(n_out, gpo * p, gpo * h).astype(BF16)

    ab_bar = lam_bar[:, :, None] * b_bar
    two_step = lambda part: jnp.concatenate([in_blocks(part(b_bar)), in_blocks(part(ab_bar))], axis=1)
    return (two_step(jnp.real), two_step(jnp.imag),
            jnp.real(lam_bar).reshape(1, g * p), jnp.imag(lam_bar).reshape(1, g * p),
            out_blocks(c_re.astype(F32)), out_blocks(-c_im.astype(F32)))


def s5_scan(x, s0_re, s0_im, weights, d_skip, *, tt):
    nb, seq, d = x.shape
    wbre, wbim, are, aim, wcre, wcim = weights
    ns = are.shape[1]
    pack = 2 if (2 * nb == SUBLANES and tt % 2 == 0) else 1
    b8 = nb if pack == 2 else -(-nb // SUBLANES) * SUBLANES
    rows = tt * b8
    state_rows = SUBLANES if pack == 2 else b8
    const = lambda a: pl.BlockSpec(a.shape, lambda i: (0,) * a.ndim)
    xblk = pl.BlockSpec((nb, tt, d), lambda i: (0, i, 0))
    sblk = pl.BlockSpec((nb, ns), lambda i: (0, 0))
    return pl.pallas_call(
        functools.partial(_s5_scan_kernel, pack=pack),
        grid=(seq // tt,),
        in_specs=[xblk, sblk, sblk, const(wbre), const(wbim), const(are), const(aim), const(wcre), const(wcim),
                  pl.BlockSpec((1, d), lambda i: (0, 0))],
        out_specs=[xblk, sblk, sblk],
        out_shape=[jax.ShapeDtypeStruct((nb, seq, d), F32), jax.ShapeDtypeStruct((nb, ns), F32),
                   jax.ShapeDtypeStruct((nb, ns), F32)],
        scratch_shapes=[pltpu.VMEM((d // LANES, rows, LANES), F32), pltpu.VMEM((rows, ns), F32),
                        pltpu.VMEM((rows, ns), F32),
                        pltpu.VMEM((state_rows, ns), F32), pltpu.VMEM((state_rows, ns), F32),
                        pltpu.VMEM((d // LANES, SUBLANES, LANES), F32)],
        compiler_params=_params("arbitrary"),
    )(x, s0_re, s0_im, wbre, wbim, are, aim, wcre, wcim, d_skip)


def _s5_out_ln_kernel(y_ref, wo_ref, wg_ref, x_ref, g_ref, b_ref, o_ref, *, alpha):
    gl = jax.nn.gelu(y_ref[...]).astype(BF16)
    mix = _dot(gl, wo_ref[...]) * jax.nn.sigmoid(_dot(gl, wg_ref[...]))
    o_ref[...] = _layer_norm(alpha * x_ref[...] + mix, g_ref[...], b_ref[...])


def s5_out_ln(y, w_out, w_gate, x, g, b, *, layer, alpha, tm):
    t, d = x.shape
    row = pl.BlockSpec((tm, d), lambda i: (i, 0))
    const = lambda r, c: pl.BlockSpec((r, c), lambda i: (0, 0))
    return pl.pallas_call(
        functools.partial(_s5_out_ln_kernel, alpha=alpha),
        grid=(t // tm,),
        in_specs=[row, _layer_weight(layer, d, d), _layer_weight(layer, d, d), row, const(1, d), const(1, d)],
        out_specs=row,
        out_shape=jax.ShapeDtypeStruct((t, d), F32),
        compiler_params=_params("parallel"),
    )(y, w_out, w_gate, x, g, b)


class _TilePlan(NamedTuple):
    token_rows: int
    proj_rows: int
    s5_steps: int
    ret_tiles: int
    ret_seqs_per_tile: int


def _tile_plan(lp, ls):
    return _TilePlan(token_rows=min(1024, lp), proj_rows=min(512, lp), s5_steps=min(64, lp),
                     ret_tiles=min(4, lp // RET_CHUNK), ret_seqs_per_tile=max(RET_CHUNK // ls, 1))


def _rope_tables(pos, n_heads):
    half = HEAD_DIM // 2
    inv = ROPE_THETA ** (-jnp.arange(half, dtype=F32) / half)
    ang = pos.astype(F32)[:, None] * inv[None, :]
    cos = jnp.cos(ang)
    sin = jnp.sin(ang)
    cos_h = jnp.concatenate([cos, cos], axis=-1)
    sin_h = jnp.concatenate([-sin, sin], axis=-1)
    return jnp.tile(cos_h, (1, n_heads)), jnp.tile(sin_h, (1, n_heads))


def kernel(x_prompt, x_sample, cache_k, cache_v, page_table, state_ret, state_s5_re, state_s5_im, ffn1_w_gate, ffn1_w_up, ffn1_w_down, ffn2_w_gate, ffn2_w_up, ffn2_w_down, ln_g, ln_b, w_in_ab, w_out_ab, s5_lam_re, s5_lam_im, s5_b_re, s5_b_im, s5_c_re, s5_c_im, s5_d, s5_log_dt, s5_w_out, s5_w_gate):
    bp, lp, d = x_prompt.shape
    bs, ls, _ = x_sample.shape
    depth = ffn1_w_gate.shape[0]
    n_layers_ab, n_pool, page, a_heads, hd = cache_k.shape
    n_pages = page_table.shape[1]
    past_len = n_pages * page
    half = a_heads * hd
    n_heads = half // HEAD_DIM
    assert hd == HEAD_DIM and lp % MOBA_BLOCK == 0 and past_len % MOBA_BLOCK == 0 and ls <= MOBA_BLOCK
    assert MOBA_BLOCK % page == 0 and lp % page == 0 and lp % RET_CHUNK == 0
    alpha = (2 * depth) ** 0.25
    tp, ts = bp * lp, bs * ls
    plan = _tile_plan(lp, ls)
    tm_p, tm_proj = plan.token_rows, plan.proj_rows

    xp = x_prompt.reshape(tp, d)
    xs = x_sample.reshape(ts, d)
    bf = lambda a: a.astype(BF16)
    f1 = (bf(ffn1_w_gate), bf(ffn1_w_up), bf(ffn1_w_down))
    f2 = (bf(ffn2_w_gate), bf(ffn2_w_up), bf(ffn2_w_down))
    w_in, w_out = bf(w_in_ab), bf(w_out_ab)
    w_o, w_g = bf(s5_w_out), bf(s5_w_gate)
    cos_p, sin_p = _rope_tables(jnp.arange(lp, dtype=jnp.int32), n_heads)
    cos_s, sin_s = _rope_tables(past_len + jnp.arange(ls, dtype=jnp.int32), n_heads)
    cos_s, sin_s = jnp.tile(cos_s, (bs, 1)), jnp.tile(sin_s, (bs, 1))
    pool_kt = jnp.transpose(cache_k, (0, 1, 3, 4, 2)).reshape(n_layers_ab * n_pool, half, page)
    pool_vt = jnp.transpose(cache_v, (0, 1, 3, 4, 2)).reshape(n_layers_ab * n_pool, half, page)
    seq_per_tile, ret_tiles_p = plan.ret_seqs_per_tile, plan.ret_tiles

    kpg = jnp.zeros((n_layers_ab, tp // page, half, page), F32)
    vpg = jnp.zeros((n_layers_ab, tp // page, half, page), F32)
    k_s, v_s, r_p, r_s = [], [], [], []
    sre_p, sim_p, sre_s, sim_s = [], [], [], []
    for layer in range(depth):
        li = layer // 2
        g = lambda i: ln_g[layer, i][None, :]
        b = lambda i: ln_b[layer, i][None, :]
        xp, xs = ffn_ln(xp, xs, *f1, g(0), b(0), layer=layer, alpha=alpha, tm=tm_p)
        if layer % 2 == 0:
            qa, kbf, vt, ksum, kpg, vpg, qb, kb, vb, gb = ab_project(xp, w_in, cos_p, sin_p, layer=li, tm=tm_proj,
                                                                     seq_len=lp, pages=(kpg, vpg), page=page)
            attn = moba_prompt(qa, ksum, kbf, vt, batch=bp, seq_len=lp)
            zero_state = jnp.zeros((bp, n_heads // 2, HEAD_PAIR, HEAD_PAIR), F32)
            ret, s_fin = retention_gated(qb, kb, vb, gb, zero_state, n_seq=1, chunk=RET_CHUNK, tiles=ret_tiles_p,
                                         steps_per_seq=lp // (RET_CHUNK * ret_tiles_p))
            xp = merge_ln(attn, ret, w_out, xp, g(1), b(1), layer=li, alpha=alpha, tm=tm_p)
            r_p.append(_block_diag_to_states(s_fin))
            qa, ka, va, qb, kb, vb, gb = ab_project(xs, w_in, cos_s, sin_s, layer=li, tm=ts, seq_len=ts)
            page_ids = (page_table.astype(jnp.int32) + li * n_pool).reshape(-1)
            attn = moba_sample(qa, ka, va, pool_kt, pool_vt, page_ids, batch=bs, n_pages=n_pages)
            ret, s_fin = retention_gated(qb, kb, vb, gb, _states_to_block_diag(state_ret[li]), n_seq=seq_per_tile,
                                         chunk=ls, tiles=1, steps_per_seq=1)
            xs = merge_ln(attn, ret, w_out, xs, g(1), b(1), layer=li, alpha=alpha, tm=ts)
            k_s.append(ka.reshape(bs, ls, a_heads, hd))
            v_s.append(va.reshape(bs, ls, a_heads, hd))
            r_s.append(_block_diag_to_states(s_fin))
        else:
            weights = _s5_weights(s5_lam_re[li], s5_lam_im[li], s5_b_re[li], s5_b_im[li], s5_c_re[li],
                                  s5_c_im[li], s5_log_dt[li])
            n_state = weights[2].shape[1]
            d_skip = s5_d[li][None, :]
            zero = jnp.zeros((bp, n_state), F32)
            y, a_re, a_im = s5_scan(xp.reshape(bp, lp, d), zero, zero, weights, d_skip, tt=plan.s5_steps)
            xp = s5_out_ln(y.reshape(tp, d), w_o, w_g, xp, g(1), b(1), layer=li, alpha=alpha, tm=tm_p)
            sre_p.append(a_re.reshape(bp, -1, S5_STATE))
            sim_p.append(a_im.reshape(bp, -1, S5_STATE))
            y, a_re, a_im = s5_scan(xs.reshape(bs, ls, d), state_s5_re[li].reshape(bs, n_state),
                                    state_s5_im[li].reshape(bs, n_state), weights, d_skip, tt=ls)
            xs = s5_out_ln(y.reshape(ts, d), w_o, w_g, xs, g(1), b(1), layer=li, alpha=alpha, tm=ts)
            sre_s.append(a_re.reshape(bs, -1, S5_STATE))
            sim_s.append(a_im.reshape(bs, -1, S5_STATE))
        xp, xs = ffn_ln(xp, xs, *f2, g(2), b(2), layer=layer, alpha=alpha, tm=tm_p)
    unpage = lambda t: jnp.transpose(t.reshape(n_layers_ab, bp, lp // page, a_heads, hd, page), (0, 1, 2, 5, 3, 4))
    return (xp.reshape(bp, lp, d), xs.reshape(bs, ls, d), unpage(kpg), unpage(vpg), jnp.stack(k_s),
            jnp.stack(v_s), jnp.stack(r_p), jnp.stack(r_s), jnp.stack(sre_p), jnp.stack(sim_p),
            jnp.stack(sre_s), jnp.stack(sim_s))
```

```python
import functools
import math
from typing import NamedTuple

import jax
import jax.numpy as jnp
from jax import lax
from jax.experimental import pallas as pl
from jax.experimental.pallas import tpu as pltpu

F32 = jnp.float32
BF16 = jnp.bfloat16

HEAD_DIM = 64
HEAD_PAIR = 2 * HEAD_DIM
MOBA_BLOCK = 256
MOBA_TOPK = 3
MOBA_BLOCKS_PER_TRIP = 4
RET_CHUNK = 128
S5_STATE = 64
ROPE_THETA = 10000.0
LN_EPS = 1e-5
GN_EPS = 1e-6
NEG_INF = -1e30
LOG2_E = math.log2(math.e)
SUBLANES = 8
BF16_SUBLANES = 16
LANES = 128
VMEM_LIMIT = 48 * 1024 * 1024


def _dot(a, b, precision=None):
    return jnp.dot(a, b, preferred_element_type=F32, precision=precision)


def _dot_nt(a, b, precision=None):
    return lax.dot_general(a, b, (((1,), (1,)), ((), ())), preferred_element_type=F32, precision=precision)


def _dot_tn(a, b):
    return lax.dot_general(a, b, (((0,), (0,)), ((), ())), preferred_element_type=F32)


def _layer_norm(r, g, b):
    mu = jnp.mean(r, -1, keepdims=True)
    d = r - mu
    var = jnp.mean(d * d, -1, keepdims=True)
    return d * lax.rsqrt(var + LN_EPS) * g + b


def _silu(x):
    return x * jax.nn.sigmoid(x)


def _params(*sem):
    return pltpu.CompilerParams(dimension_semantics=sem, vmem_limit_bytes=VMEM_LIMIT)


def _ffn_ln_kernel(xp_ref, xs_ref, wg_ref, wu_ref, wd_ref, g_ref, b_ref, op_ref, os_ref, a_ref, *, alpha, tf):
    def run(x_ref, o_ref):
        rows = x_ref.shape[0]
        xb = x_ref[...].astype(BF16)
        for c in range(wg_ref.shape[1] // tf):
            cols = slice(c * tf, (c + 1) * tf)
            hg = _dot(xb, wg_ref[:, cols])
            hu = _dot(xb, wu_ref[:, cols])
            a_ref[0:rows, cols] = (_silu(hg) * hu).astype(BF16)
        r = alpha * x_ref[...] + 0.5 * _dot(a_ref[0:rows, :], wd_ref[...])
        o_ref[...] = _layer_norm(r, g_ref[...], b_ref[...])

    last = pl.num_programs(0) - 1
    pl.when(pl.program_id(0) < last)(lambda: run(xp_ref, op_ref))
    pl.when(pl.program_id(0) == last)(lambda: run(xs_ref, os_ref))


def _layer_weight(layer, r, c, row_blk=0, **kw):
    return pl.BlockSpec((None, r, c), lambda *_: (layer, row_blk, 0), **kw)


def ffn_ln(xp, xs, wg, wu, wd, g, b, *, layer, alpha, tm, tf=256):
    tp, d = xp.shape
    ts = xs.shape[0]
    f = wg.shape[2]
    n_p = tp // tm
    assert ts <= tm
    resident = lambda r, c: _layer_weight(layer, r, c, pipeline_mode=pl.Buffered(1))
    p_rows = pl.BlockSpec((tm, d), lambda i: (jnp.minimum(i, n_p - 1), 0))
    s_rows = pl.BlockSpec((ts, d), lambda i: (0, 0))
    return pl.pallas_call(
        functools.partial(_ffn_ln_kernel, alpha=alpha, tf=tf),
        grid=(n_p + 1,),
        in_specs=[p_rows, s_rows, resident(d, f), resident(d, f), resident(f, d),
                  pl.BlockSpec((1, d), lambda i: (0, 0)), pl.BlockSpec((1, d), lambda i: (0, 0))],
        out_specs=[p_rows, s_rows],
        out_shape=[jax.ShapeDtypeStruct((tp, d), F32), jax.ShapeDtypeStruct((ts, d), F32)],
        scratch_shapes=[pltpu.VMEM((tm, f), BF16)],
        compiler_params=_params("arbitrary"),
    )(xp, xs, wg, wu, wd, g, b)


def _rope(y, cos, sin_signed):
    width = y.shape[-1]
    lane = lax.broadcasted_iota(jnp.int32, (1, width), 1)
    first = (lane % HEAD_DIM) < (HEAD_DIM // 2)
    rot = jnp.where(first, pltpu.roll(y, width - HEAD_DIM // 2, 1), pltpu.roll(y, HEAD_DIM // 2, 1))
    return y * cos + rot * sin_signed


def _ab_proj_kernel(*refs, half, scale, page):
    if page is None:
        x_ref, w_ref, cos_ref, sin_ref, qa_ref, ka_ref, va_ref, qb_ref, kb_ref, vb_ref, gb_ref = refs
    else:
        (x_ref, w_ref, cos_ref, sin_ref, _, _, qa_ref, kbf_ref, vt_ref, ksum_ref, kpg_ref, vpg_ref,
         qb_ref, kb_ref, vb_ref, gb_ref) = refs
    xb = x_ref[...].astype(BF16)
    cos = cos_ref[...]
    sin = sin_ref[...]
    col = lambda c: _dot(xb, w_ref[:, c * half:(c + 1) * half])
    qa_ref[...] = _rope(col(0), cos, sin) * scale
    ka = _rope(col(1), cos, sin)
    va = col(2)
    if page is None:
        ka_ref[...] = ka
        va_ref[...] = va
    else:
        kbf_ref[...] = ka.astype(BF16)
        for n in range(ka.shape[0] // MOBA_BLOCK):
            blk = slice(n * MOBA_BLOCK, (n + 1) * MOBA_BLOCK)
            ksum_ref[n] = jnp.sum(ka[blk], axis=0, keepdims=True)
            vt_ref[n] = va[blk].T.astype(BF16)
        for n in range(ka.shape[0] // page):
            rows = slice(n * page, (n + 1) * page)
            kpg_ref[n] = ka[rows].T
            vpg_ref[n] = va[rows].T
    qb_ref[...] = _rope(col(3), cos, sin)
    kb_ref[...] = _rope(col(4), cos, sin) * scale
    vb_ref[...] = col(5)
    gb_ref[...] = col(6)


def ab_project(x, w_in, cos, sin, *, layer, tm, seq_len, pages=None, page=None):
    t, d = x.shape
    half = w_in.shape[2] // 7
    tiles_per_seq = max(seq_len // tm, 1)
    tab = pl.BlockSpec((tm, half), lambda i: (i % tiles_per_seq, 0))
    row = pl.BlockSpec((tm, half), lambda i: (i, 0))
    f32o = jax.ShapeDtypeStruct((t, half), F32)
    in_specs = [pl.BlockSpec((tm, d), lambda i: (i, 0)), _layer_weight(layer, d, 7 * half), tab, tab]
    operands = [x, w_in, cos, sin]
    aliases = {}
    if pages is None:
        out_specs = [row] * 7
        out_shape = [f32o] * 7
    else:
        nblk = tm // MOBA_BLOCK
        page_blk = pl.BlockSpec((None, tm // page, half, page), lambda i: (layer, i, 0, 0))
        in_specs += [pl.BlockSpec(memory_space=pl.ANY)] * 2
        operands += list(pages)
        aliases = {4: 4, 5: 5}
        out_specs = [row, row,
                     pl.BlockSpec((nblk, half, MOBA_BLOCK), lambda i: (i, 0, 0)),
                     pl.BlockSpec((nblk, 1, half), lambda i: (i, 0, 0)),
                     page_blk, page_blk, row, row, row, row]
        out_shape = [f32o, jax.ShapeDtypeStruct((t, half), BF16),
                     jax.ShapeDtypeStruct((t // MOBA_BLOCK, half, MOBA_BLOCK), BF16),
                     jax.ShapeDtypeStruct((t // MOBA_BLOCK, 1, half), F32),
                     jax.ShapeDtypeStruct(pages[0].shape, F32), jax.ShapeDtypeStruct(pages[1].shape, F32),
                     f32o, f32o, f32o, f32o]
    return pl.pallas_call(
        functools.partial(_ab_proj_kernel, half=half, scale=HEAD_DIM ** -0.5, page=page),
        grid=(t // tm,),
        in_specs=in_specs,
        out_specs=out_specs,
        out_shape=out_shape,
        input_output_aliases=aliases,
        compiler_params=_params("parallel"),
    )(*operands)


def _topk_select(gate, blk, n_valid, axis):
    gate = jnp.where(blk < n_valid, gate, NEG_INF)
    taken = jnp.zeros(gate.shape, F32)
    blk_f = blk.astype(F32)
    for _ in range(MOBA_TOPK):
        top = jnp.max(gate, axis=axis, keepdims=True)
        first = jnp.min(jnp.where(gate == top, blk_f, float(gate.shape[axis])), axis=axis, keepdims=True)
        hit = blk_f == first
        taken = jnp.where(hit, 1.0, taken)
        gate = jnp.where(hit, -jnp.inf, gate)
    return jnp.where(blk < n_valid, taken, 0.0)


def _moba_prompt_kernel(q_ref, ksum_ref, k_ref, vt_ref, o_ref, qm_ref, sel_ref, m_ref, l_ref, acc_ref, s_ref):
    j = pl.program_id(1)
    tq = q_ref.shape[0]
    n_heads = q_ref.shape[1] // HEAD_DIM
    lane = lax.broadcasted_iota(jnp.int32, (1, HEAD_PAIR), 1)
    key_i = lax.broadcasted_iota(jnp.int32, (MOBA_BLOCK, tq), 0)
    qry_i = lax.broadcasted_iota(jnp.int32, (MOBA_BLOCK, tq), 1)
    causal = key_i <= qry_i

    nb = ksum_ref.shape[0]
    blk = lax.broadcasted_iota(jnp.int32, (nb, 1), 0)
    for hp in range(n_heads // 2):
        lanes = slice(hp * HEAD_PAIR, (hp + 1) * HEAD_PAIR)
        qp = q_ref[:, lanes]
        kmean = ksum_ref[:, 0, lanes] * (1.0 / MOBA_BLOCK)
        masks = [(lane // HEAD_DIM) == half for half in range(2)]
        gate = _dot_nt(jnp.concatenate([jnp.where(hm, kmean, 0.0) for hm in masks], axis=0), qp,
                       precision=lax.Precision.HIGHEST)
        for half, hm in enumerate(masks):
            h = 2 * hp + half
            sel_ref[h] = _topk_select(gate[half * nb:(half + 1) * nb], blk, j, 0)
            qm_ref[h] = jnp.where(hm, qp * LOG2_E, 0.0).astype(BF16)

    ones_rows = jnp.ones((BF16_SUBLANES, MOBA_BLOCK), BF16)

    def update(blocks, first):
        def rows0(n):
            return pl.multiple_of((j if n is None else n) * MOBA_BLOCK, MOBA_BLOCK)

        def picked(n, h):
            return sel_ref[h, pl.ds(n, 1), :] > 0.0

        def scores(slot, n, h):
            k_pair = k_ref[pl.ds(rows0(n), MOBA_BLOCK), (h // 2) * HEAD_PAIR:(h // 2 + 1) * HEAD_PAIR]
            s_ref[slot, h] = _dot_nt(k_pair, qm_ref[h])

        def softmax_values(slot, n, h):
            if first:
                s = jnp.where(causal, s_ref[slot, h], NEG_INF)
                m_new = jnp.max(s, axis=0, keepdims=True)
            else:
                s = s_ref[slot, h]
                cmax = jnp.where(picked(n, h), jnp.max(s, axis=0, keepdims=True), NEG_INF)
                m_new = jnp.maximum(m_ref[h], cmax)
                a = jnp.exp2(m_ref[h] - m_new)
            p = jnp.exp2(s - m_new).astype(BF16)
            m_ref[h] = m_new
            v_h = vt_ref[j if n is None else n, h * HEAD_DIM:(h + 1) * HEAD_DIM, :]
            res = _dot(jnp.concatenate([v_h, ones_rows], axis=0), p)
            pv, psum = res[:HEAD_DIM], res[HEAD_DIM:HEAD_DIM + 1]
            if first:
                acc_ref[h] = pv
                l_ref[h] = psum
            else:
                acc_ref[h] = a * acc_ref[h] + jnp.where(picked(n, h), pv, 0.0)
                l_ref[h] = a * l_ref[h] + jnp.where(picked(n, h), psum, 0.0)

        heads = range(n_heads)
        stages = (scores, softmax_values)
        for t in range(len(stages) + len(blocks) - 1):
            for h in heads:
                for slot, n in enumerate(blocks):
                    if 0 <= t - slot < len(stages):
                        stages[t - slot](slot, n, h)

    update([None], True)

    n_slots = s_ref.shape[0]

    def body(i, carry):
        update([n_slots * i + k for k in range(n_slots)], False)
        return carry

    lax.fori_loop(0, j // n_slots, body, 0)
    for rem in range(1, n_slots):
        @pl.when(j % n_slots == rem)
        def _(rem=rem):
            update([j - rem + k for k in range(rem)], False)

    out_t = jnp.concatenate([acc_ref[h] / l_ref[h] for h in range(n_heads)], axis=0)
    o_ref[...] = out_t.T.astype(o_ref.dtype)


def moba_prompt(q, ksum, kbf, vt, *, batch, seq_len):
    t, w = q.shape
    nb = seq_len // MOBA_BLOCK
    n_heads = w // HEAD_DIM
    return pl.pallas_call(
        _moba_prompt_kernel,
        grid=(batch, nb),
        in_specs=[pl.BlockSpec((MOBA_BLOCK, w), lambda b, j: (b * nb + j, 0)),
                  pl.BlockSpec((nb, 1, w), lambda b, j: (b, 0, 0)),
                  pl.BlockSpec((seq_len, w), lambda b, j: (b, 0)),
                  pl.BlockSpec((nb, w, MOBA_BLOCK), lambda b, j: (b, 0, 0))],
        out_specs=pl.BlockSpec((MOBA_BLOCK, w), lambda b, j: (b * nb + j, 0)),
        out_shape=jax.ShapeDtypeStruct((t, w), BF16),
        scratch_shapes=[pltpu.VMEM((n_heads, MOBA_BLOCK, HEAD_PAIR), BF16),
                        pltpu.VMEM((n_heads, nb, MOBA_BLOCK), F32),
                        pltpu.VMEM((n_heads, 1, MOBA_BLOCK), F32),
                        pltpu.VMEM((n_heads, 1, MOBA_BLOCK), F32),
                        pltpu.VMEM((n_heads, HEAD_DIM, MOBA_BLOCK), F32),
                        pltpu.VMEM((MOBA_BLOCKS_PER_TRIP, n_heads, MOBA_BLOCK, MOBA_BLOCK), F32)],
        compiler_params=_params("parallel", "arbitrary"),
    )(q, ksum, kbf, vt)


def _moba_sample_kernel(pt_ref, q_ref, knew_ref, vnew_ref, *rest, pages_per_step, page, n_heads):
    del pt_ref
    kpages = rest[:pages_per_step]
    vpages = rest[pages_per_step:2 * pages_per_step]
    o_ref = rest[2 * pages_per_step]
    qf_ref, qb_ref, ksum_ref, m_ref, l_ref, acc_ref, kpad_ref, vpad_ref = rest[2 * pages_per_step + 1:]
    s_idx = pl.program_id(1)
    n_steps = pl.num_programs(1)
    lq, w = q_ref.shape
    rows = n_heads * lq
    nb = acc_ref.shape[0]
    pages_per_blk = MOBA_BLOCK // page
    blks_per_step = pages_per_step // pages_per_blk
    rowhead = lax.broadcasted_iota(jnp.int32, (rows, 1), 0) // lq
    lanehead = lax.broadcasted_iota(jnp.int32, (1, w), 1) // HEAD_DIM
    blk_lane = lax.broadcasted_iota(jnp.int32, (1, LANES), 1)

    @pl.when(s_idx == 0)
    def _():
        qt = jnp.concatenate([q_ref[...]] * n_heads, axis=0)
        qbd = jnp.where(rowhead == lanehead, qt, 0.0)
        qf_ref[...] = qbd
        qb_ref[...] = qbd.astype(BF16)
        ksum_ref[...] = jnp.zeros_like(ksum_ref)
        m_ref[...] = jnp.zeros_like(m_ref)
        l_ref[...] = jnp.zeros_like(l_ref)

    qb = qb_ref[...]

    def partial_softmax(s):
        m = jnp.max(s, axis=1, keepdims=True)
        e = jnp.exp(s - m)
        return m, jnp.sum(e, axis=1, keepdims=True), e.astype(BF16)

    blks = range(blks_per_step)
    pages_of = lambda bi: range(bi * pages_per_blk, (bi + 1) * pages_per_blk)
    here = [blk_lane == s_idx * blks_per_step + bi for bi in blks]
    kt = [jnp.concatenate([kpages[i][0] for i in pages_of(bi)], axis=1) for bi in blks]
    scores = [_dot(qb, kt[bi].astype(BF16)) for bi in blks]
    ksum = ksum_ref[...]
    for bi in blks:
        ksum = jnp.where(here[bi], jnp.sum(kt[bi], axis=1, keepdims=True), ksum)
    ksum_ref[...] = ksum
    stats = [partial_softmax(scores[bi]) for bi in blks]
    for bi in blks:
        vt = jnp.concatenate([vpages[i][0] for i in pages_of(bi)], axis=1)
        acc_ref[s_idx * blks_per_step + bi] = _dot_nt(stats[bi][2], vt.astype(BF16))
    m_all, l_all = m_ref[...], l_ref[...]
    for bi in blks:
        m_all = jnp.where(here[bi], stats[bi][0], m_all)
        l_all = jnp.where(here[bi], stats[bi][1], l_all)
    m_ref[...] = m_all
    l_ref[...] = l_all

    @pl.when(s_idx == n_steps - 1)
    def _():
        kpad_ref[...] = jnp.zeros_like(kpad_ref)
        vpad_ref[...] = jnp.zeros_like(vpad_ref)
        kpad_ref[0:lq, :] = knew_ref[...]
        vpad_ref[0:lq, :] = vnew_ref[...]
        s_own = _dot_nt(qb, kpad_ref[...].astype(BF16))
        tq = lax.broadcasted_iota(jnp.int32, s_own.shape, 0) % lq
        tk = lax.broadcasted_iota(jnp.int32, s_own.shape, 1)
        m_own, l_own, e_own = partial_softmax(jnp.where(tk <= tq, s_own, NEG_INF))
        acc_own = _dot(e_own, vpad_ref[...].astype(BF16))

        gate = _dot(qf_ref[...], ksum_ref[...] * (1.0 / MOBA_BLOCK), precision=lax.Precision.HIGHEST)
        sel = _topk_select(gate[:, :nb], blk_lane[:, :nb], nb, 1) > 0.0
        m_blk = m_ref[:, :nb]
        m_all = jnp.maximum(m_own, jnp.max(jnp.where(sel, m_blk, NEG_INF), axis=1, keepdims=True))
        wgt = jnp.where(sel, jnp.exp(m_blk - m_all), 0.0)
        w_own = jnp.exp(m_own - m_all)
        den = w_own * l_own + jnp.sum(wgt * l_ref[:, :nb], axis=1, keepdims=True)
        num = w_own * acc_own
        for n in range(nb):
            num = num + wgt[:, n:n + 1] * acc_ref[n]
        o_all = num / den
        out = jnp.zeros((lq, w), F32)
        for h in range(n_heads):
            out = out + jnp.where(lanehead == h, o_all[h * lq:(h + 1) * lq], 0.0)
        o_ref[...] = out


def moba_sample(q, knew, vnew, pool_kt, pool_vt, page_ids, *, batch, n_pages, pages_per_step=32):
    t, w = q.shape
    lq = t // batch
    page = pool_kt.shape[2]
    n_heads = w // HEAD_DIM
    rows = n_heads * lq
    nb = n_pages * page // MOBA_BLOCK
    assert nb <= LANES and pages_per_step % (MOBA_BLOCK // page) == 0
    seq_blk = pl.BlockSpec((lq, w), lambda b, s, pt: (b, 0))

    def page_spec(i):
        return pl.BlockSpec((1, w, page), lambda b, s, pt: (pt[b * n_pages + s * pages_per_step + i], 0, 0))

    grid_spec = pltpu.PrefetchScalarGridSpec(
        num_scalar_prefetch=1,
        grid=(batch, n_pages // pages_per_step),
        in_specs=[seq_blk, seq_blk, seq_blk] + [page_spec(i) for i in range(pages_per_step)] * 2,
        out_specs=seq_blk,
        scratch_shapes=[pltpu.VMEM((rows, w), F32), pltpu.VMEM((rows, w), BF16), pltpu.VMEM((w, LANES), F32),
                        pltpu.VMEM((rows, LANES), F32), pltpu.VMEM((rows, LANES), F32),
                        pltpu.VMEM((nb, rows, w), F32),
                        pltpu.VMEM((LANES, w), F32), pltpu.VMEM((LANES, w), F32)],
    )
    return pl.pallas_call(
        functools.partial(_moba_sample_kernel, pages_per_step=pages_per_step, page=page, n_heads=n_heads),
        grid_spec=grid_spec,
        out_shape=jax.ShapeDtypeStruct((t, w), F32),
        compiler_params=_params("parallel", "arbitrary"),
    )(page_ids, q, knew, vnew, *([pool_kt] * pages_per_step), *([pool_vt] * pages_per_step))


def _retention_kernel(q_ref, k_ref, v_ref, g_ref, s0_ref, dmask_ref, qdec_ref, kdec_ref, cdec_ref,
                      attn_ref, x_ref, wa_ref, wr_ref, lg_ref, lb_ref,
                      o_ref, sout_ref, st_ref, gated_ref, *, n_seq, chunk, tiles, steps_per_seq, alpha):
    step = pl.program_id(0)
    tile_rows = n_seq * chunk
    n_pairs = q_ref.shape[1] // HEAD_PAIR
    lane = lax.broadcasted_iota(jnp.int32, (1, HEAD_PAIR), 1)
    rowh = lax.broadcasted_iota(jnp.int32, (HEAD_PAIR, 1), 0) // HEAD_DIM
    same_head = rowh == (lane // HEAD_DIM)
    rseq = lax.broadcasted_iota(jnp.int32, (tile_rows, 1), 0) // chunk

    @pl.when(step % steps_per_seq == 0)
    def _():
        st_ref[...] = s0_ref[...]

    pairs = range(n_pairs)
    lanes_of = lambda hp: slice(hp * HEAD_PAIR, (hp + 1) * HEAD_PAIR)
    half_masks = [(lane // HEAD_DIM) == half for half in range(2)]
    for ti in range(tiles):
        rows = slice(ti * tile_rows, (ti + 1) * tile_rows)
        qb = [q_ref[rows, lanes_of(hp)].astype(BF16) for hp in pairs]
        kf = [k_ref[rows, lanes_of(hp)] for hp in pairs]
        kb = [k.astype(BF16) for k in kf]
        vb = [v_ref[rows, lanes_of(hp)].astype(BF16) for hp in pairs]
        inner = [[_dot_nt(jnp.where(hm, qb[hp], jnp.zeros_like(qb[hp])), kb[hp]) for hm in half_masks]
                 for hp in pairs]
        scaled = [[(inner[hp][half] * dmask_ref[2 * hp + half]).astype(BF16) for half in range(2)] for hp in pairs]
        ret = [sum(jnp.where(half_masks[half], _dot(scaled[hp][half], vb[hp]), 0.0) for half in range(2))
               for hp in pairs]
        for hp in pairs:
            kd = (kf[hp] * kdec_ref[:, lanes_of(hp)]).astype(BF16)
            cdec = cdec_ref[:, lanes_of(hp)]
            carried = []
            for s in range(n_seq):
                srows = slice(s * chunk, (s + 1) * chunk)
                state = st_ref[s, hp]
                carried.append(_dot(qb[hp][srows], state.astype(BF16)))
                kd_s = kd if n_seq == 1 else jnp.where(rseq == s, kd, jnp.zeros_like(kd))
                st_ref[s, hp] = cdec * state + jnp.where(same_head, _dot_tn(kd_s, vb[hp]), 0.0)
            carried = carried[0] if n_seq == 1 else jnp.concatenate(carried, axis=0)
            ret[hp] = ret[hp] + carried * qdec_ref[:, lanes_of(hp)]
        for hp in pairs:
            normed = jnp.zeros_like(ret[hp])
            for hm in half_masks:
                mu = jnp.sum(jnp.where(hm, ret[hp], 0.0), axis=1, keepdims=True) * (1.0 / HEAD_DIM)
                d = jnp.where(hm, ret[hp] - mu, 0.0)
                var = jnp.sum(d * d, axis=1, keepdims=True) * (1.0 / HEAD_DIM)
                normed = normed + d * lax.rsqrt(var + GN_EPS)
            gated_ref[rows, lanes_of(hp)] = (_silu(g_ref[rows, lanes_of(hp)]) * normed).astype(BF16)

    mix = _dot(attn_ref[...].astype(BF16), wa_ref[...]) + _dot(gated_ref[...], wr_ref[...])
    o_ref[...] = _layer_norm(alpha * x_ref[...] + mix, lg_ref[...], lb_ref[...])

    @pl.when(step % steps_per_seq == steps_per_seq - 1)
    def _():
        sout_ref[...] = st_ref[...]


def _retention_tables(n_heads, n_seq, chunk):
    log_g = jnp.log1p(-jnp.exp2(-5.0 - jnp.arange(n_heads, dtype=F32)))
    idx = jnp.arange(chunk, dtype=F32)
    diff = idx[:, None] - idx[None, :]
    dmask = jnp.where(diff >= 0, jnp.exp(log_g[:, None, None] * jnp.maximum(diff, 0.0)), 0.0)
    q_dec = jnp.exp(log_g[None, :] * (idx[:, None] + 1.0))
    k_dec = jnp.exp(log_g[None, :] * (chunk - 1.0 - idx[:, None]))
    c_dec = jnp.exp(log_g * chunk)
    seq_eye = jnp.eye(n_seq, dtype=F32)
    dmask = jnp.einsum("ab,hij->haibj", seq_eye, dmask).reshape(n_heads, n_seq * chunk, n_seq * chunk)
    lanes = lambda t: jnp.repeat(t, HEAD_DIM, axis=-1)
    return dmask, jnp.tile(lanes(q_dec), (n_seq, 1)), jnp.tile(lanes(k_dec), (n_seq, 1)), lanes(c_dec[None, :])


def retention_merge_ln(q, k, v, g, s0_bd, attn, x, w_out, ln_g, ln_b, *, layer, alpha, n_seq, chunk, tiles,
                       steps_per_seq):
    t, w = q.shape
    d = x.shape[1]
    n_pairs = w // HEAD_PAIR
    rows = n_seq * chunk * tiles
    n_steps = t // rows
    dmask, qdec, kdec, cdec = _retention_tables(w // HEAD_DIM, n_seq, chunk)
    tile_rows = n_seq * chunk
    row = pl.BlockSpec((rows, w), lambda i: (i, 0))
    xrow = pl.BlockSpec((rows, d), lambda i: (i, 0))
    st = pl.BlockSpec((n_seq, n_pairs, HEAD_PAIR, HEAD_PAIR), lambda i: (i // steps_per_seq, 0, 0, 0))
    const = lambda shape: pl.BlockSpec(shape, lambda i: (0,) * len(shape))
    return pl.pallas_call(
        functools.partial(_retention_kernel, n_seq=n_seq, chunk=chunk, tiles=tiles, steps_per_seq=steps_per_seq,
                          alpha=alpha),
        grid=(n_steps,),
        in_specs=[row, row, row, row, st, const(dmask.shape), const((tile_rows, w)), const((tile_rows, w)),
                  const((1, w)), row, xrow, _layer_weight(layer, w, d, 0), _layer_weight(layer, w, d, 1),
                  const((1, d)), const((1, d))],
        out_specs=[xrow, st],
        out_shape=[jax.ShapeDtypeStruct((t, d), F32), jax.ShapeDtypeStruct(s0_bd.shape, F32)],
        scratch_shapes=[pltpu.VMEM((n_seq, n_pairs, HEAD_PAIR, HEAD_PAIR), F32), pltpu.VMEM((rows, w), BF16)],
        compiler_params=_params("arbitrary"),
    )(q, k, v, g, s0_bd, dmask, qdec, kdec, cdec, attn, x, w_out, w_out, ln_g, ln_b)


def _states_to_block_diag(s):
    b, h, dk, dv = s.shape
    sp = s.reshape(b, h // 2, 2, dk, dv)
    z = jnp.zeros_like(sp[:, :, 0])
    top = jnp.concatenate([sp[:, :, 0], z], axis=-1)
    bot = jnp.concatenate([z, sp[:, :, 1]], axis=-1)
    return jnp.concatenate([top, bot], axis=-2)


def _block_diag_to_states(s):
    b, p = s.shape[:2]
    a = s[:, :, :HEAD_DIM, :HEAD_DIM]
    c = s[:, :, HEAD_DIM:, HEAD_DIM:]
    return jnp.stack([a, c], axis=2).reshape(b, 2 * p, HEAD_DIM, HEAD_DIM)


def _s5_scan_kernel(x_ref, s0re_ref, s0im_ref, wbre_ref, wbim_ref, are_ref, aim_ref, wcre_ref, wcim_ref, d_ref,
                    y_ref, sre_ref, sim_ref, xs_ref, bre_ref, bim_ref, stre_ref, stim_ref, xtail_ref, *, pack):
    step = pl.program_id(0)
    nb, tt, d = x_ref.shape
    b8 = xs_ref.shape[1] // tt
    ns = stre_ref.shape[1]
    n_kb = d // LANES
    per_kb = ns // n_kb
    state_rows = slice((pack - 1) * nb, pack * nb)

    @pl.when(step == 0)
    def _():
        stre_ref[...] = jnp.zeros_like(stre_ref)
        stim_ref[...] = jnp.zeros_like(stim_ref)
        for r in range(pack):
            stre_ref[r * nb:(r + 1) * nb, :] = s0re_ref[...]
            stim_ref[r * nb:(r + 1) * nb, :] = s0im_ref[...]
        xs_ref[...] = jnp.zeros_like(xs_ref)
        xtail_ref[...] = jnp.zeros_like(xtail_ref)

    for b in range(nb):
        for kb in range(n_kb):
            xs_ref[kb, pl.ds(b, tt, stride=b8), :] = x_ref[b, :, kb * LANES:(kb + 1) * LANES]

    n_out = wcre_ref.shape[0]
    kin = ns // n_out
    wout = d // n_out
    kb_per_chunk = kin // per_kb

    first_half = lax.broadcasted_iota(jnp.int32, (SUBLANES, 1), 0) < nb

    def project_in(c):
        for kb in range(c * kb_per_chunk, (c + 1) * kb_per_chunk):
            cols = slice(kb * per_kb, (kb + 1) * per_kb)
            x_t = xs_ref[kb]
            if pack == 1:
                lhs = x_t.astype(BF16)
                w_re, w_im = wbre_ref[kb, 0:LANES, :], wbim_ref[kb, 0:LANES, :]
            else:
                x_prev = pltpu.roll(x_t, nb, 0)
                head = jnp.where(first_half, pltpu.roll(xtail_ref[kb], nb, 0), x_prev[0:SUBLANES])
                x_prev = jnp.concatenate([head, x_prev[SUBLANES:]], axis=0)
                xtail_ref[kb] = x_t[x_t.shape[0] - SUBLANES:]
                lhs = jnp.concatenate([x_t, x_prev], axis=1).astype(BF16)
                w_re, w_im = wbre_ref[kb], wbim_ref[kb]
            bre_ref[:, cols] = _dot(lhs, w_re)
            bim_ref[:, cols] = _dot(lhs, w_im)

    def scan_packed(cs):
        width = cs.stop - cs.start
        are = jnp.broadcast_to(are_ref[:, cs], (SUBLANES, width))
        aim = jnp.broadcast_to(aim_ref[:, cs], (SUBLANES, width))
        a2re, a2im = are * are - aim * aim, 2.0 * are * aim
        sre, sim = stre_ref[:, cs], stim_ref[:, cs]
        for v in range(tt // pack):
            rows = slice(v * SUBLANES, (v + 1) * SUBLANES)
            if v == 0:
                start = jnp.logical_and(first_half, step == 0)
                cre, cim = jnp.where(start, are, a2re), jnp.where(start, aim, a2im)
            else:
                cre, cim = a2re, a2im
            sre, sim = (bre_ref[rows, cs] + (cre * sre - cim * sim), bim_ref[rows, cs] + (cre * sim + cim * sre))
            bre_ref[rows, cs] = sre
            bim_ref[rows, cs] = sim
        stre_ref[:, cs] = sre
        stim_ref[:, cs] = sim

    def scan(c):
        if pack == 2:
            half = kin // 2
            for c0 in range(c * kin, (c + 1) * kin, half):
                scan_packed(slice(c0, c0 + half))
            return
        cs = slice(c * kin, (c + 1) * kin)
        are = jnp.broadcast_to(are_ref[:, cs], (SUBLANES, kin))
        aim = jnp.broadcast_to(aim_ref[:, cs], (SUBLANES, kin))
        for r0 in range(0, b8, SUBLANES):
            sre = stre_ref[r0:r0 + SUBLANES, cs]
            sim = stim_ref[r0:r0 + SUBLANES, cs]
            for t in range(tt):
                rows = slice(t * b8 + r0, t * b8 + r0 + SUBLANES)
                sre, sim = (are * sre - aim * sim + bre_ref[rows, cs], are * sim + aim * sre + bim_ref[rows, cs])
                bre_ref[rows, cs] = sre
                bim_ref[rows, cs] = sim
            stre_ref[r0:r0 + SUBLANES, cs] = sre
            stim_ref[r0:r0 + SUBLANES, cs] = sim

    def project_out(c):
        cs = slice(c * kin, (c + 1) * kin)
        y = _dot(bre_ref[:, cs].astype(BF16), wcre_ref[c]) + _dot(bim_ref[:, cs].astype(BF16), wcim_ref[c])
        for i in range(wout // LANES):
            kb = c * (wout // LANES) + i
            xs_ref[kb] = y[:, i * LANES:(i + 1) * LANES] + xs_ref[kb] * d_ref[:, kb * LANES:(kb + 1) * LANES]

    project_in(0)
    for c in range(n_out):
        if c + 1 < n_out:
            project_in(c + 1)
        scan(c)
        if c >= 1:
            project_out(c - 1)
    project_out(n_out - 1)

    for b in range(nb):
        for kb in range(n_kb):
            y_ref[b, :, kb * LANES:(kb + 1) * LANES] = xs_ref[kb, pl.ds(b, tt, stride=b8), :]

    @pl.when(step == pl.num_programs(0) - 1)
    def _():
        sre_ref[...] = stre_ref[state_rows, :]
        sim_ref[...] = stim_ref[state_rows, :]


def _s5_weights(lam_re, lam_im, b_re, b_im, c_re, c_im, log_dt):
    g, p, h = b_re.shape
    lam = lax.complex(lam_re.astype(F32), lam_im.astype(F32))
    dt = jnp.exp(log_dt.astype(F32))[:, None]
    lam_bar = jnp.exp(lam * dt)
    b_bar = ((lam_bar - 1.0) / lam)[:, :, None] * lax.complex(b_re.astype(F32), b_im.astype(F32))
    gpk = LANES // h
    n_kb = g // gpk
    eye = jnp.eye(gpk, dtype=F32)

    def in_blocks(m):
        m = m.reshape(n_kb, gpk, p, h)
        return jnp.einsum("kgph,gf->kghfp", m, eye).reshape(n_kb, gpk * h, gpk * p).astype(BF16)

    gpo = 2 * LANES // h
    n_out = g // gpo
    eye_o = jnp.eye(gpo, dtype=F32)

    def out_blocks(m):
        m = m.reshape(n_out, gpo, h, p)
        return jnp.einsum("kghp,gf->kgpfh", m, eye_o).reshape(n_out, gpo * p, gpo * h).astype(BF16)

    ab_bar = lam_bar[:, :, None] * b_bar
    two_step = lambda part: jnp.concatenate([in_blocks(part(b_bar)), in_blocks(part(ab_bar))], axis=1)
    return (two_step(jnp.real), two_step(jnp.imag),
            jnp.real(lam_bar).reshape(1, g * p), jnp.imag(lam_bar).reshape(1, g * p),
            out_blocks(c_re.astype(F32)), out_blocks(-c_im.astype(F32)))


def s5_scan(x, s0_re, s0_im, weights, d_skip, *, tt):
    nb, seq, d = x.shape
    wbre, wbim, are, aim, wcre, wcim = weights
    ns = are.shape[1]
    pack = 2 if (2 * nb == SUBLANES and tt % 2 == 0) else 1
    b8 = nb if pack == 2 else -(-nb // SUBLANES) * SUBLANES
    rows = tt * b8
    state_rows = SUBLANES if pack == 2 else b8
    const = lambda a: pl.BlockSpec(a.shape, lambda i: (0,) * a.ndim)
    xblk = pl.BlockSpec((nb, tt, d), lambda i: (0, i, 0))
    sblk = pl.BlockSpec((nb, ns), lambda i: (0, 0))
    return pl.pallas_call(
        functools.partial(_s5_scan_kernel, pack=pack),
        grid=(seq // tt,),
        in_specs=[xblk, sblk, sblk, const(wbre), const(wbim), const(are), const(aim), const(wcre), const(wcim),
                  pl.BlockSpec((1, d), lambda i: (0, 0))],
        out_specs=[xblk, sblk, sblk],
        out_shape=[jax.ShapeDtypeStruct((nb, seq, d), F32), jax.ShapeDtypeStruct((nb, ns), F32),
                   jax.ShapeDtypeStruct((nb, ns), F32)],
        scratch_shapes=[pltpu.VMEM((d // LANES, rows, LANES), F32), pltpu.VMEM((rows, ns), F32),
                        pltpu.VMEM((rows, ns), F32),
                        pltpu.VMEM((state_rows, ns), F32), pltpu.VMEM((state_rows, ns), F32),
                        pltpu.VMEM((d // LANES, SUBLANES, LANES), F32)],
        compiler_params=_params("arbitrary"),
    )(x, s0_re, s0_im, wbre, wbim, are, aim, wcre, wcim, d_skip)


def _s5_out_ln_kernel(y_ref, wo_ref, wg_ref, x_ref, g_ref, b_ref, o_ref, *, alpha):
    gl = jax.nn.gelu(y_ref[...]).astype(BF16)
    mix = _dot(gl, wo_ref[...]) * jax.nn.sigmoid(_dot(gl, wg_ref[...]))
    o_ref[...] = _layer_norm(alpha * x_ref[...] + mix, g_ref[...], b_ref[...])


def s5_out_ln(y, w_out, w_gate, x, g, b, *, layer, alpha, tm):
    t, d = x.shape
    row = pl.BlockSpec((tm, d), lambda i: (i, 0))
    const = lambda r, c: pl.BlockSpec((r, c), lambda i: (0, 0))
    return pl.pallas_call(
        functools.partial(_s5_out_ln_kernel, alpha=alpha),
        grid=(t // tm,),
        in_specs=[row, _layer_weight(layer, d, d), _layer_weight(layer, d, d), row, const(1, d), const(1, d)],
        out_specs=row,
        out_shape=jax.ShapeDtypeStruct((t, d), F32),
        compiler_params=_params("parallel"),
    )(y, w_out, w_gate, x, g, b)


class _TilePlan(NamedTuple):
    token_rows: int
    proj_rows: int
    s5_steps: int
    ret_tiles: int
    ret_seqs_per_tile: int


def _tile_plan(lp, ls):
    return _TilePlan(token_rows=min(1024, lp), proj_rows=min(512, lp), s5_steps=min(64, lp),
                     ret_tiles=min(4, lp // RET_CHUNK), ret_seqs_per_tile=max(RET_CHUNK // ls, 1))


def _rope_tables(pos, n_heads):
    half = HEAD_DIM // 2
    inv = ROPE_THETA ** (-jnp.arange(half, dtype=F32) / half)
    ang = pos.astype(F32)[:, None] * inv[None, :]
    cos = jnp.cos(ang)
    sin = jnp.sin(ang)
    cos_h = jnp.concatenate([cos, cos], axis=-1)
    sin_h = jnp.concatenate([-sin, sin], axis=-1)
    return jnp.tile(cos_h, (1, n_heads)), jnp.tile(sin_h, (1, n_heads))


def kernel(x_prompt, x_sample, cache_k, cache_v, page_table, state_ret, state_s5_re, state_s5_im, ffn1_w_gate, ffn1_w_up, ffn1_w_down, ffn2_w_gate, ffn2_w_up, ffn2_w_down, ln_g, ln_b, w_in_ab, w_out_ab, s5_lam_re, s5_lam_im, s5_b_re, s5_b_im, s5_c_re, s5_c_im, s5_d, s5_log_dt, s5_w_out, s5_w_gate):
    bp, lp, d = x_prompt.shape
    bs, ls, _ = x_sample.shape
    depth = ffn1_w_gate.shape[0]
    n_layers_ab, n_pool, page, a_heads, hd = cache_k.shape
    n_pages = page_table.shape[1]
    past_len = n_pages * page
    half = a_heads * hd
    n_heads = half // HEAD_DIM
    assert hd == HEAD_DIM and lp % MOBA_BLOCK == 0 and past_len % MOBA_BLOCK == 0 and ls <= MOBA_BLOCK
    assert MOBA_BLOCK % page == 0 and lp % page == 0 and lp % RET_CHUNK == 0
    alpha = (2 * depth) ** 0.25
    tp, ts = bp * lp, bs * ls
    plan = _tile_plan(lp, ls)
    tm_p, tm_proj = plan.token_rows, plan.proj_rows

    xp = x_prompt.reshape(tp, d)
    xs = x_sample.reshape(ts, d)
    bf = lambda a: a.astype(BF16)
    f1 = (bf(ffn1_w_gate), bf(ffn1_w_up), bf(ffn1_w_down))
    f2 = (bf(ffn2_w_gate), bf(ffn2_w_up), bf(ffn2_w_down))
    w_in, w_out = bf(w_in_ab), bf(w_out_ab)
    w_o, w_g = bf(s5_w_out), bf(s5_w_gate)
    cos_p, sin_p = _rope_tables(jnp.arange(lp, dtype=jnp.int32), n_heads)
    cos_s, sin_s = _rope_tables(past_len + jnp.arange(ls, dtype=jnp.int32), n_heads)
    cos_s, sin_s = jnp.tile(cos_s, (bs, 1)), jnp.tile(sin_s, (bs, 1))
    pool_kt = jnp.transpose(cache_k, (0, 1, 3, 4, 2)).reshape(n_layers_ab * n_pool, half, page)
    pool_vt = jnp.transpose(cache_v, (0, 1, 3, 4, 2)).reshape(n_layers_ab * n_pool, half, page)
    seq_per_tile, ret_tiles_p = plan.ret_seqs_per_tile, plan.ret_tiles

    kpg = jnp.zeros((n_layers_ab, tp // page, half, page), F32)
    vpg = jnp.zeros((n_layers_ab, tp // page, half, page), F32)
    k_s, v_s, r_p, r_s = [], [], [], []
    sre_p, sim_p, sre_s, sim_s = [], [], [], []
    for layer in range(depth):
        li = layer // 2
        g = lambda i: ln_g[layer, i][None, :]
        b = lambda i: ln_b[layer, i][None, :]
        xp, xs = ffn_ln(xp, xs, *f1, g(0), b(0), layer=layer, alpha=alpha, tm=tm_p)
        if layer % 2 == 0:
            qa, kbf, vt, ksum, kpg, vpg, qb, kb, vb, gb = ab_project(xp, w_in, cos_p, sin_p, layer=li, tm=tm_proj,
                                                                     seq_len=lp, pages=(kpg, vpg), page=page)
            attn = moba_prompt(qa, ksum, kbf, vt, batch=bp, seq_len=lp)
            zero_state = jnp.zeros((bp, n_heads // 2, HEAD_PAIR, HEAD_PAIR), F32)
            xp, s_fin = retention_merge_ln(qb, kb, vb, gb, zero_state, attn, xp, w_out, g(1), b(1), layer=li,
                                           alpha=alpha, n_seq=1, chunk=RET_CHUNK, tiles=ret_tiles_p,
                                           steps_per_seq=lp // (RET_CHUNK * ret_tiles_p))
            r_p.append(_block_diag_to_states(s_fin))
            qa, ka, va, qb, kb, vb, gb = ab_project(xs, w_in, cos_s, sin_s, layer=li, tm=ts, seq_len=ts)
            page_ids = (page_table.astype(jnp.int32) + li * n_pool).reshape(-1)
            attn = moba_sample(qa, ka, va, pool_kt, pool_vt, page_ids, batch=bs, n_pages=n_pages)
            xs, s_fin = retention_merge_ln(qb, kb, vb, gb, _states_to_block_diag(state_ret[li]), attn, xs, w_out,
                                           g(1), b(1), layer=li, alpha=alpha, n_seq=seq_per_tile, chunk=ls, tiles=1,
                                           steps_per_seq=1)
            k_s.append(ka.reshape(bs, ls, a_heads, hd))
            v_s.append(va.reshape(bs, ls, a_heads, hd))
            r_s.append(_block_diag_to_states(s_fin))
        else:
            weights = _s5_weights(s5_lam_re[li], s5_lam_im[li], s5_b_re[li], s5_b_im[li], s5_c_re[li],
                                  s5_c_im[li], s5_log_dt[li])
            n_state = weights[2].shape[1]
            d_skip = s5_d[li][None, :]
            zero = jnp.zeros((bp, n_state), F32)
            y, a_re, a_im = s5_scan(xp.reshape(bp, lp, d), zero, zero, weights, d_skip, tt=plan.s5_steps)
            xp = s5_out_ln(y.reshape(tp, d), w_o, w_g, xp, g(1), b(1), layer=li, alpha=alpha, tm=tm_p)
            sre_p.append(a_re.reshape(bp, -1, S5_STATE))
            sim_p.append(a_im.reshape(bp, -1, S5_STATE))
            y, a_re, a_im = s5_scan(xs.reshape(bs, ls, d), state_s5_re[li].reshape(bs, n_state),
                                    state_s5_im[li].reshape(bs, n_state), weights, d_skip, tt=ls)
            xs = s5_out_ln(y.reshape(ts, d), w_o, w_g, xs, g(1), b(1), layer=li, alpha=alpha, tm=ts)
            sre_s.append(a_re.reshape(bs, -1, S5_STATE))
            sim_s.append(a_im.reshape(bs, -1, S5_STATE))
        xp, xs = ffn_ln(xp, xs, *f2, g(2), b(2), layer=layer, alpha=alpha, tm=tm_p)
    unpage = lambda t: jnp.transpose(t.reshape(n_layers_ab, bp, lp // page, a_heads, hd, page), (0, 1, 2, 5, 3, 4))
    return (xp.reshape(bp, lp, d), xs.reshape(bs, ls, d), unpage(kpg), unpage(vpg), jnp.stack(k_s),
            jnp.stack(v_s), jnp.stack(r_p), jnp.stack(r_s), jnp.stack(sre_p), jnp.stack(sim_p),
            jnp.stack(sre_s), jnp.stack(sim_s))
```

```python
import functools
import math
from typing import NamedTuple

import jax
import jax.numpy as jnp
from jax import lax
from jax.experimental import pallas as pl
from jax.experimental.pallas import tpu as pltpu

F32 = jnp.float32
BF16 = jnp.bfloat16

HEAD_DIM = 64
HEAD_PAIR = 2 * HEAD_DIM
MOBA_BLOCK = 256
MOBA_TOPK = 3
MOBA_BLOCKS_PER_TRIP = 4
RET_CHUNK = 128
S5_STATE = 64
ROPE_THETA = 10000.0
LN_EPS = 1e-5
GN_EPS = 1e-6
NEG_INF = -1e30
LOG2_E = math.log2(math.e)
SUBLANES = 8
BF16_SUBLANES = 16
LANES = 128
VMEM_LIMIT = 48 * 1024 * 1024


def _dot(a, b, precision=None):
    return jnp.dot(a, b, preferred_element_type=F32, precision=precision)


def _dot_nt(a, b, precision=None):
    return lax.dot_general(a, b, (((1,), (1,)), ((), ())), preferred_element_type=F32, precision=precision)


def _dot_tn(a, b):
    return lax.dot_general(a, b, (((0,), (0,)), ((), ())), preferred_element_type=F32)


def _layer_norm(r, g, b):
    mu = jnp.mean(r, -1, keepdims=True)
    d = r - mu
    var = jnp.mean(d * d, -1, keepdims=True)
    return d * lax.rsqrt(var + LN_EPS) * g + b


def _silu(x):
    return x * jax.nn.sigmoid(x)


def _params(*sem):
    return pltpu.CompilerParams(dimension_semantics=sem, vmem_limit_bytes=VMEM_LIMIT)


def _ffn_ln_kernel(xp_ref, xs_ref, wg_ref, wu_ref, wd_ref, g_ref, b_ref, op_ref, os_ref, a_ref, *, alpha, tf):
    def run(x_ref, o_ref):
        rows = x_ref.shape[0]
        xb = x_ref[...].astype(BF16)
        for c in range(wg_ref.shape[1] // tf):
            cols = slice(c * tf, (c + 1) * tf)
            hg = _dot(xb, wg_ref[:, cols])
            hu = _dot(xb, wu_ref[:, cols])
            a_ref[0:rows, cols] = (_silu(hg) * hu).astype(BF16)
        r = alpha * x_ref[...] + 0.5 * _dot(a_ref[0:rows, :], wd_ref[...])
        o_ref[...] = _layer_norm(r, g_ref[...], b_ref[...])

    last = pl.num_programs(0) - 1
    pl.when(pl.program_id(0) < last)(lambda: run(xp_ref, op_ref))
    pl.when(pl.program_id(0) == last)(lambda: run(xs_ref, os_ref))


def _layer_weight(layer, r, c, row_blk=0, **kw):
    return pl.BlockSpec((None, r, c), lambda *_: (layer, row_blk, 0), **kw)


def ffn_ln(xp, xs, wg, wu, wd, g, b, *, layer, alpha, tm, tf=256):
    tp, d = xp.shape
    ts = xs.shape[0]
    f = wg.shape[2]
    n_p = tp // tm
    assert ts <= tm
    resident = lambda r, c: _layer_weight(layer, r, c, pipeline_mode=pl.Buffered(1))
    p_rows = pl.BlockSpec((tm, d), lambda i: (jnp.minimum(i, n_p - 1), 0))
    s_rows = pl.BlockSpec((ts, d), lambda i: (0, 0))
    return pl.pallas_call(
        functools.partial(_ffn_ln_kernel, alpha=alpha, tf=tf),
        grid=(n_p + 1,),
        in_specs=[p_rows, s_rows, resident(d, f), resident(d, f), resident(f, d),
                  pl.BlockSpec((1, d), lambda i: (0, 0)), pl.BlockSpec((1, d), lambda i: (0, 0))],
        out_specs=[p_rows, s_rows],
        out_shape=[jax.ShapeDtypeStruct((tp, d), F32), jax.ShapeDtypeStruct((ts, d), F32)],
        scratch_shapes=[pltpu.VMEM((tm, f), BF16)],
        compiler_params=_params("arbitrary"),
    )(xp, xs, wg, wu, wd, g, b)


def _rope(y, cos, sin_signed):
    width = y.shape[-1]
    lane = lax.broadcasted_iota(jnp.int32, (1, width), 1)
    first = (lane % HEAD_DIM) < (HEAD_DIM // 2)
    rot = jnp.where(first, pltpu.roll(y, width - HEAD_DIM // 2, 1), pltpu.roll(y, HEAD_DIM // 2, 1))
    return y * cos + rot * sin_signed


def _ab_proj_kernel(*refs, half, scale, page, layer, new_slots):
    if page is None:
        x_ref, w_ref, cos_ref, sin_ref, qa_ref, ka_ref, va_ref, qb_ref, kb_ref, vb_ref, gb_ref = refs
    else:
        x_ref, w_ref, cos_ref, sin_ref = refs[:4]
        qa_ref, kbf_ref, vt_ref, ksum_ref, kpg_ref, vpg_ref, qb_ref, kb_ref, vb_ref, gb_ref = refs[-10:]
        if new_slots is not None:
            for slot in range(new_slots):
                if slot != layer:
                    kpg_ref[slot] = jnp.zeros(kpg_ref.shape[1:], F32)
                    vpg_ref[slot] = jnp.zeros(vpg_ref.shape[1:], F32)
            kpg_ref, vpg_ref = kpg_ref.at[layer], vpg_ref.at[layer]
    xb = x_ref[...].astype(BF16)
    cos = cos_ref[...]
    sin = sin_ref[...]
    col = lambda c: _dot(xb, w_ref[:, c * half:(c + 1) * half])
    qa_ref[...] = _rope(col(0), cos, sin) * scale
    ka = _rope(col(1), cos, sin)
    va = col(2)
    if page is None:
        ka_ref[...] = ka
        va_ref[...] = va
    else:
        kbf_ref[...] = ka.astype(BF16)
        for n in range(ka.shape[0] // MOBA_BLOCK):
            blk = slice(n * MOBA_BLOCK, (n + 1) * MOBA_BLOCK)
            ksum_ref[n] = jnp.sum(ka[blk], axis=0, keepdims=True)
            vt_ref[n] = va[blk].T.astype(BF16)
        for n in range(ka.shape[0] // page):
            rows = slice(n * page, (n + 1) * page)
            kpg_ref[n] = ka[rows].T
            vpg_ref[n] = va[rows].T
    qb_ref[...] = _rope(col(3), cos, sin)
    kb_ref[...] = _rope(col(4), cos, sin) * scale
    vb_ref[...] = col(5)
    gb_ref[...] = col(6)


def ab_project(x, w_in, cos, sin, *, layer, tm, seq_len, page=None, pages=None):
    t, d = x.shape
    n_layers = w_in.shape[0]
    half = w_in.shape[2] // 7
    tiles_per_seq = max(seq_len // tm, 1)
    tab = pl.BlockSpec((tm, half), lambda i: (i % tiles_per_seq, 0))
    row = pl.BlockSpec((tm, half), lambda i: (i, 0))
    f32o = jax.ShapeDtypeStruct((t, half), F32)
    in_specs = [pl.BlockSpec((tm, d), lambda i: (i, 0)), _layer_weight(layer, d, 7 * half), tab, tab]
    operands = [x, w_in, cos, sin]
    aliases = {}
    if page is None:
        out_specs = [row] * 7
        out_shape = [f32o] * 7
    else:
        nblk = tm // MOBA_BLOCK
        pages_shape = jax.ShapeDtypeStruct((n_layers, t // page, half, page), F32)
        if pages is None:
            page_blk = pl.BlockSpec((n_layers, tm // page, half, page), lambda i: (0, i, 0, 0))
        else:
            page_blk = pl.BlockSpec((None, tm // page, half, page), lambda i: (layer, i, 0, 0))
            in_specs += [pl.BlockSpec(memory_space=pl.ANY)] * 2
            operands += list(pages)
            aliases = {4: 4, 5: 5}
        out_specs = [row, row,
                     pl.BlockSpec((nblk, half, MOBA_BLOCK), lambda i: (i, 0, 0)),
                     pl.BlockSpec((nblk, 1, half), lambda i: (i, 0, 0)),
                     page_blk, page_blk, row, row, row, row]
        out_shape = [f32o, jax.ShapeDtypeStruct((t, half), BF16),
                     jax.ShapeDtypeStruct((t // MOBA_BLOCK, half, MOBA_BLOCK), BF16),
                     jax.ShapeDtypeStruct((t // MOBA_BLOCK, 1, half), F32),
                     pages_shape, pages_shape, f32o, f32o, f32o, f32o]
    return pl.pallas_call(
        functools.partial(_ab_proj_kernel, half=half, scale=HEAD_DIM ** -0.5, page=page, layer=layer,
                          new_slots=n_layers if (page is not None and pages is None) else None),
        grid=(t // tm,),
        in_specs=in_specs,
        out_specs=out_specs,
        out_shape=out_shape,
        input_output_aliases=aliases,
        compiler_params=_params("parallel"),
    )(*operands)


def _topk_select(gate, blk, n_valid, axis):
    gate = jnp.where(blk < n_valid, gate, NEG_INF)
    taken = jnp.zeros(gate.shape, F32)
    blk_f = blk.astype(F32)
    for _ in range(MOBA_TOPK):
        top = jnp.max(gate, axis=axis, keepdims=True)
        first = jnp.min(jnp.where(gate == top, blk_f, float(gate.shape[axis])), axis=axis, keepdims=True)
        hit = blk_f == first
        taken = jnp.where(hit, 1.0, taken)
        gate = jnp.where(hit, -jnp.inf, gate)
    return jnp.where(blk < n_valid, taken, 0.0)


def _moba_prompt_kernel(q_ref, ksum_ref, k_ref, vt_ref, o_ref, qm_ref, sel_ref, m_ref, l_ref, acc_ref, s_ref):
    j = pl.program_id(1)
    tq = q_ref.shape[0]
    n_heads = q_ref.shape[1] // HEAD_DIM
    lane = lax.broadcasted_iota(jnp.int32, (1, HEAD_PAIR), 1)
    key_i = lax.broadcasted_iota(jnp.int32, (MOBA_BLOCK, tq), 0)
    qry_i = lax.broadcasted_iota(jnp.int32, (MOBA_BLOCK, tq), 1)
    causal = key_i <= qry_i

    nb = ksum_ref.shape[0]
    blk = lax.broadcasted_iota(jnp.int32, (nb, 1), 0)
    for hp in range(n_heads // 2):
        lanes = slice(hp * HEAD_PAIR, (hp + 1) * HEAD_PAIR)
        qp = q_ref[:, lanes]
        kmean = ksum_ref[:, 0, lanes] * (1.0 / MOBA_BLOCK)
        masks = [(lane // HEAD_DIM) == half for half in range(2)]
        gate = _dot_nt(jnp.concatenate([jnp.where(hm, kmean, 0.0) for hm in masks], axis=0), qp,
                       precision=lax.Precision.HIGHEST)
        for half, hm in enumerate(masks):
            h = 2 * hp + half
            sel_ref[h] = _topk_select(gate[half * nb:(half + 1) * nb], blk, j, 0)
            qm_ref[h] = jnp.where(hm, qp * LOG2_E, 0.0).astype(BF16)

    ones_rows = jnp.ones((BF16_SUBLANES, MOBA_BLOCK), BF16)

    def update(blocks, first):
        def rows0(n):
            return pl.multiple_of((j if n is None else n) * MOBA_BLOCK, MOBA_BLOCK)

        def picked(n, h):
            return sel_ref[h, pl.ds(n, 1), :] > 0.0

        def scores(slot, n, h):
            k_pair = k_ref[pl.ds(rows0(n), MOBA_BLOCK), (h // 2) * HEAD_PAIR:(h // 2 + 1) * HEAD_PAIR]
            s_ref[slot, h] = _dot_nt(k_pair, qm_ref[h])

        def softmax_values(slot, n, h):
            if first:
                s = jnp.where(causal, s_ref[slot, h], NEG_INF)
                m_new = jnp.max(s, axis=0, keepdims=True)
            else:
                s = s_ref[slot, h]
                cmax = jnp.where(picked(n, h), jnp.max(s, axis=0, keepdims=True), NEG_INF)
                m_new = jnp.maximum(m_ref[h], cmax)
                a = jnp.exp2(m_ref[h] - m_new)
            p = jnp.exp2(s - m_new).astype(BF16)
            m_ref[h] = m_new
            v_h = vt_ref[j if n is None else n, h * HEAD_DIM:(h + 1) * HEAD_DIM, :]
            res = _dot(jnp.concatenate([v_h, ones_rows], axis=0), p)
            pv, psum = res[:HEAD_DIM], res[HEAD_DIM:HEAD_DIM + 1]
            if first:
                acc_ref[h] = pv
                l_ref[h] = psum
            else:
                acc_ref[h] = a * acc_ref[h] + jnp.where(picked(n, h), pv, 0.0)
                l_ref[h] = a * l_ref[h] + jnp.where(picked(n, h), psum, 0.0)

        heads = range(n_heads)
        stages = (scores, softmax_values)
        for t in range(len(stages) + len(blocks) - 1):
            for h in heads:
                for slot, n in enumerate(blocks):
                    if 0 <= t - slot < len(stages):
                        stages[t - slot](slot, n, h)

    update([None], True)

    n_slots = s_ref.shape[0]

    def body(i, carry):
        update([n_slots * i + k for k in range(n_slots)], False)
        return carry

    lax.fori_loop(0, j // n_slots, body, 0)
    for rem in range(1, n_slots):
        @pl.when(j % n_slots == rem)
        def _(rem=rem):
            update([j - rem + k for k in range(rem)], False)

    out_t = jnp.concatenate([acc_ref[h] / l_ref[h] for h in range(n_heads)], axis=0)
    o_ref[...] = out_t.T.astype(o_ref.dtype)


def moba_prompt(q, ksum, kbf, vt, *, batch, seq_len):
    t, w = q.shape
    nb = seq_len // MOBA_BLOCK
    n_heads = w // HEAD_DIM
    return pl.pallas_call(
        _moba_prompt_kernel,
        grid=(batch, nb),
        in_specs=[pl.BlockSpec((MOBA_BLOCK, w), lambda b, j: (b * nb + j, 0)),
                  pl.BlockSpec((nb, 1, w), lambda b, j: (b, 0, 0)),
                  pl.BlockSpec((seq_len, w), lambda b, j: (b, 0)),
                  pl.BlockSpec((nb, w, MOBA_BLOCK), lambda b, j: (b, 0, 0))],
        out_specs=pl.BlockSpec((MOBA_BLOCK, w), lambda b, j: (b * nb + j, 0)),
        out_shape=jax.ShapeDtypeStruct((t, w), BF16),
        scratch_shapes=[pltpu.VMEM((n_heads, MOBA_BLOCK, HEAD_PAIR), BF16),
                        pltpu.VMEM((n_heads, nb, MOBA_BLOCK), F32),
                        pltpu.VMEM((n_heads, 1, MOBA_BLOCK), F32),
                        pltpu.VMEM((n_heads, 1, MOBA_BLOCK), F32),
                        pltpu.VMEM((n_heads, HEAD_DIM, MOBA_BLOCK), F32),
                        pltpu.VMEM((MOBA_BLOCKS_PER_TRIP, n_heads, MOBA_BLOCK, MOBA_BLOCK), F32)],
        compiler_params=_params("parallel", "arbitrary"),
    )(q, ksum, kbf, vt)


def _moba_sample_kernel(pt_ref, q_ref, knew_ref, vnew_ref, *rest, pages_per_step, page, n_heads):
    del pt_ref
    kpages = rest[:pages_per_step]
    vpages = rest[pages_per_step:2 * pages_per_step]
    o_ref = rest[2 * pages_per_step]
    qf_ref, qb_ref, ksum_ref, m_ref, l_ref, acc_ref, kpad_ref, vpad_ref = rest[2 * pages_per_step + 1:]
    s_idx = pl.program_id(1)
    n_steps = pl.num_programs(1)
    lq, w = q_ref.shape
    rows = n_heads * lq
    nb = acc_ref.shape[0]
    pages_per_blk = MOBA_BLOCK // page
    blks_per_step = pages_per_step // pages_per_blk
    rowhead = lax.broadcasted_iota(jnp.int32, (rows, 1), 0) // lq
    lanehead = lax.broadcasted_iota(jnp.int32, (1, w), 1) // HEAD_DIM
    blk_lane = lax.broadcasted_iota(jnp.int32, (1, LANES), 1)

    @pl.when(s_idx == 0)
    def _():
        qt = jnp.concatenate([q_ref[...]] * n_heads, axis=0)
        qbd = jnp.where(rowhead == lanehead, qt, 0.0)
        qf_ref[...] = qbd
        qb_ref[...] = qbd.astype(BF16)
        ksum_ref[...] = jnp.zeros_like(ksum_ref)
        m_ref[...] = jnp.zeros_like(m_ref)
        l_ref[...] = jnp.zeros_like(l_ref)

    qb = qb_ref[...]

    def partial_softmax(s):
        m = jnp.max(s, axis=1, keepdims=True)
        e = jnp.exp(s - m)
        return m, jnp.sum(e, axis=1, keepdims=True), e.astype(BF16)

    blks = range(blks_per_step)
    pages_of = lambda bi: range(bi * pages_per_blk, (bi + 1) * pages_per_blk)
    here = [blk_lane == s_idx * blks_per_step + bi for bi in blks]
    kt = [jnp.concatenate([kpages[i][0] for i in pages_of(bi)], axis=1) for bi in blks]
    scores = [_dot(qb, kt[bi].astype(BF16)) for bi in blks]
    ksum = ksum_ref[...]
    for bi in blks:
        ksum = jnp.where(here[bi], jnp.sum(kt[bi], axis=1, keepdims=True), ksum)
    ksum_ref[...] = ksum
    stats = [partial_softmax(scores[bi]) for bi in blks]
    for bi in blks:
        vt = jnp.concatenate([vpages[i][0] for i in pages_of(bi)], axis=1)
        acc_ref[s_idx * blks_per_step + bi] = _dot_nt(stats[bi][2], vt.astype(BF16))
    m_all, l_all = m_ref[...], l_ref[...]
    for bi in blks:
        m_all = jnp.where(here[bi], stats[bi][0], m_all)
        l_all = jnp.where(here[bi], stats[bi][1], l_all)
    m_ref[...] = m_all
    l_ref[...] = l_all

    @pl.when(s_idx == n_steps - 1)
    def _():
        kpad_ref[...] = jnp.zeros_like(kpad_ref)
        vpad_ref[...] = jnp.zeros_like(vpad_ref)
        kpad_ref[0:lq, :] = knew_ref[...]
        vpad_ref[0:lq, :] = vnew_ref[...]
        s_own = _dot_nt(qb, kpad_ref[...].astype(BF16))
        tq = lax.broadcasted_iota(jnp.int32, s_own.shape, 0) % lq
        tk = lax.broadcasted_iota(jnp.int32, s_own.shape, 1)
        m_own, l_own, e_own = partial_softmax(jnp.where(tk <= tq, s_own, NEG_INF))
        acc_own = _dot(e_own, vpad_ref[...].astype(BF16))

        gate = _dot(qf_ref[...], ksum_ref[...] * (1.0 / MOBA_BLOCK), precision=lax.Precision.HIGHEST)
        sel = _topk_select(gate[:, :nb], blk_lane[:, :nb], nb, 1) > 0.0
        m_blk = m_ref[:, :nb]
        m_all = jnp.maximum(m_own, jnp.max(jnp.where(sel, m_blk, NEG_INF), axis=1, keepdims=True))
        wgt = jnp.where(sel, jnp.exp(m_blk - m_all), 0.0)
        w_own = jnp.exp(m_own - m_all)
        den = w_own * l_own + jnp.sum(wgt * l_ref[:, :nb], axis=1, keepdims=True)
        num = w_own * acc_own
        for n in range(nb):
            num = num + wgt[:, n:n + 1] * acc_ref[n]
        o_all = num / den
        out = jnp.zeros((lq, w), F32)
        for h in range(n_heads):
            out = out + jnp.where(lanehead == h, o_all[h * lq:(h + 1) * lq], 0.0)
        o_ref[...] = out


def moba_sample(q, knew, vnew, pool_kt, pool_vt, page_ids, *, batch, n_pages, pages_per_step=32):
    t, w = q.shape
    lq = t // batch
    page = pool_kt.shape[2]
    n_heads = w // HEAD_DIM
    rows = n_heads * lq
    nb = n_pages * page // MOBA_BLOCK
    assert nb <= LANES and pages_per_step % (MOBA_BLOCK // page) == 0
    seq_blk = pl.BlockSpec((lq, w), lambda b, s, pt: (b, 0))

    def page_spec(i):
        return pl.BlockSpec((1, w, page), lambda b, s, pt: (pt[b * n_pages + s * pages_per_step + i], 0, 0))

    grid_spec = pltpu.PrefetchScalarGridSpec(
        num_scalar_prefetch=1,
        grid=(batch, n_pages // pages_per_step),
        in_specs=[seq_blk, seq_blk, seq_blk] + [page_spec(i) for i in range(pages_per_step)] * 2,
        out_specs=seq_blk,
        scratch_shapes=[pltpu.VMEM((rows, w), F32), pltpu.VMEM((rows, w), BF16), pltpu.VMEM((w, LANES), F32),
                        pltpu.VMEM((rows, LANES), F32), pltpu.VMEM((rows, LANES), F32),
                        pltpu.VMEM((nb, rows, w), F32),
                        pltpu.VMEM((LANES, w), F32), pltpu.VMEM((LANES, w), F32)],
    )
    return pl.pallas_call(
        functools.partial(_moba_sample_kernel, pages_per_step=pages_per_step, page=page, n_heads=n_heads),
        grid_spec=grid_spec,
        out_shape=jax.ShapeDtypeStruct((t, w), F32),
        compiler_params=_params("parallel", "arbitrary"),
    )(page_ids, q, knew, vnew, *([pool_kt] * pages_per_step), *([pool_vt] * pages_per_step))


def _retention_kernel(q_ref, k_ref, v_ref, g_ref, s0_ref, dmask_ref, qdec_ref, kdec_ref, cdec_ref,
                      attn_ref, x_ref, wa_ref, wr_ref, lg_ref, lb_ref,
                      o_ref, sout_ref, st_ref, gated_ref, *, n_seq, chunk, tiles, steps_per_seq, alpha):
    step = pl.program_id(0)
    tile_rows = n_seq * chunk
    n_pairs = q_ref.shape[1] // HEAD_PAIR
    lane = lax.broadcasted_iota(jnp.int32, (1, HEAD_PAIR), 1)
    rowh = lax.broadcasted_iota(jnp.int32, (HEAD_PAIR, 1), 0) // HEAD_DIM
    same_head = rowh == (lane // HEAD_DIM)
    rseq = lax.broadcasted_iota(jnp.int32, (tile_rows, 1), 0) // chunk

    @pl.when(step % steps_per_seq == 0)
    def _():
        st_ref[...] = s0_ref[...]

    pairs = range(n_pairs)
    lanes_of = lambda hp: slice(hp * HEAD_PAIR, (hp + 1) * HEAD_PAIR)
    half_masks = [(lane // HEAD_DIM) == half for half in range(2)]
    for ti in range(tiles):
        rows = slice(ti * tile_rows, (ti + 1) * tile_rows)
        qb = [q_ref[rows, lanes_of(hp)].astype(BF16) for hp in pairs]
        kf = [k_ref[rows, lanes_of(hp)] for hp in pairs]
        kb = [k.astype(BF16) for k in kf]
        vb = [v_ref[rows, lanes_of(hp)].astype(BF16) for hp in pairs]
        inner = [[_dot_nt(jnp.where(hm, qb[hp], jnp.zeros_like(qb[hp])), kb[hp]) for hm in half_masks]
                 for hp in pairs]
        scaled = [[(inner[hp][half] * dmask_ref[2 * hp + half]).astype(BF16) for half in range(2)] for hp in pairs]
        ret = [sum(jnp.where(half_masks[half], _dot(scaled[hp][half], vb[hp]), 0.0) for half in range(2))
               for hp in pairs]
        for hp in pairs:
            kd = (kf[hp] * kdec_ref[:, lanes_of(hp)]).astype(BF16)
            cdec = cdec_ref[:, lanes_of(hp)]
            carried = []
            for s in range(n_seq):
                srows = slice(s * chunk, (s + 1) * chunk)
                state = st_ref[s, hp]
                carried.append(_dot(qb[hp][srows], state.astype(BF16)))
                kd_s = kd if n_seq == 1 else jnp.where(rseq == s, kd, jnp.zeros_like(kd))
                st_ref[s, hp] = cdec * state + jnp.where(same_head, _dot_tn(kd_s, vb[hp]), 0.0)
            carried = carried[0] if n_seq == 1 else jnp.concatenate(carried, axis=0)
            ret[hp] = ret[hp] + carried * qdec_ref[:, lanes_of(hp)]
        for hp in pairs:
            normed = jnp.zeros_like(ret[hp])
            for hm in half_masks:
                mu = jnp.sum(jnp.where(hm, ret[hp], 0.0), axis=1, keepdims=True) * (1.0 / HEAD_DIM)
                d = jnp.where(hm, ret[hp] - mu, 0.0)
                var = jnp.sum(d * d, axis=1, keepdims=True) * (1.0 / HEAD_DIM)
                normed = normed + d * lax.rsqrt(var + GN_EPS)
            gated_ref[rows, lanes_of(hp)] = (_silu(g_ref[rows, lanes_of(hp)]) * normed).astype(BF16)

    mix = _dot(attn_ref[...].astype(BF16), wa_ref[...]) + _dot(gated_ref[...], wr_ref[...])
    o_ref[...] = _layer_norm(alpha * x_ref[...] + mix, lg_ref[...], lb_ref[...])

    @pl.when(step % steps_per_seq == steps_per_seq - 1)
    def _():
        sout_ref[...] = st_ref[...]


def _retention_tables(n_heads, n_seq, chunk):
    log_g = jnp.log1p(-jnp.exp2(-5.0 - jnp.arange(n_heads, dtype=F32)))
    idx = jnp.arange(chunk, dtype=F32)
    diff = idx[:, None] - idx[None, :]
    dmask = jnp.where(diff >= 0, jnp.exp(log_g[:, None, None] * jnp.maximum(diff, 0.0)), 0.0)
    q_dec = jnp.exp(log_g[None, :] * (idx[:, None] + 1.0))
    k_dec = jnp.exp(log_g[None, :] * (chunk - 1.0 - idx[:, None]))
    c_dec = jnp.exp(log_g * chunk)
    seq_eye = jnp.eye(n_seq, dtype=F32)
    dmask = jnp.einsum("ab,hij->haibj", seq_eye, dmask).reshape(n_heads, n_seq * chunk, n_seq * chunk)
    lanes = lambda t: jnp.repeat(t, HEAD_DIM, axis=-1)
    return dmask, jnp.tile(lanes(q_dec), (n_seq, 1)), jnp.tile(lanes(k_dec), (n_seq, 1)), lanes(c_dec[None, :])


def retention_merge_ln(q, k, v, g, s0_bd, attn, x, w_out, ln_g, ln_b, *, layer, alpha, n_seq, chunk, tiles,
                       steps_per_seq):
    t, w = q.shape
    d = x.shape[1]
    n_pairs = w // HEAD_PAIR
    rows = n_seq * chunk * tiles
    n_steps = t // rows
    dmask, qdec, kdec, cdec = _retention_tables(w // HEAD_DIM, n_seq, chunk)
    tile_rows = n_seq * chunk
    row = pl.BlockSpec((rows, w), lambda i: (i, 0))
    xrow = pl.BlockSpec((rows, d), lambda i: (i, 0))
    st = pl.BlockSpec((n_seq, n_pairs, HEAD_PAIR, HEAD_PAIR), lambda i: (i // steps_per_seq, 0, 0, 0))
    const = lambda shape: pl.BlockSpec(shape, lambda i: (0,) * len(shape))
    return pl.pallas_call(
        functools.partial(_retention_kernel, n_seq=n_seq, chunk=chunk, tiles=tiles, steps_per_seq=steps_per_seq,
                          alpha=alpha),
        grid=(n_steps,),
        in_specs=[row, row, row, row, st, const(dmask.shape), const((tile_rows, w)), const((tile_rows, w)),
                  const((1, w)), row, xrow, _layer_weight(layer, w, d, 0), _layer_weight(layer, w, d, 1),
                  const((1, d)), const((1, d))],
        out_specs=[xrow, st],
        out_shape=[jax.ShapeDtypeStruct((t, d), F32), jax.ShapeDtypeStruct(s0_bd.shape, F32)],
        scratch_shapes=[pltpu.VMEM((n_seq, n_pairs, HEAD_PAIR, HEAD_PAIR), F32), pltpu.VMEM((rows, w), BF16)],
        compiler_params=_params("arbitrary"),
    )(q, k, v, g, s0_bd, dmask, qdec, kdec, cdec, attn, x, w_out, w_out, ln_g, ln_b)


def _states_to_block_diag(s):
    b, h, dk, dv = s.shape
    sp = s.reshape(b, h // 2, 2, dk, dv)
    z = jnp.zeros_like(sp[:, :, 0])
    top = jnp.concatenate([sp[:, :, 0], z], axis=-1)
    bot = jnp.concatenate([z, sp[:, :, 1]], axis=-1)
    return jnp.concatenate([top, bot], axis=-2)


def _block_diag_to_states(s):
    b, p = s.shape[:2]
    a = s[:, :, :HEAD_DIM, :HEAD_DIM]
    c = s[:, :, HEAD_DIM:, HEAD_DIM:]
    return jnp.stack([a, c], axis=2).reshape(b, 2 * p, HEAD_DIM, HEAD_DIM)


def _s5_scan_kernel(x_ref, s0re_ref, s0im_ref, wbre_ref, wbim_ref, are_ref, aim_ref, wcre_ref, wcim_ref, d_ref,
                    y_ref, sre_ref, sim_ref, xs_ref, bre_ref, bim_ref, stre_ref, stim_ref, xtail_ref, *, pack):
    step = pl.program_id(0)
    nb, tt, d = x_ref.shape
    b8 = xs_ref.shape[1] // tt
    ns = stre_ref.shape[1]
    n_kb = d // LANES
    per_kb = ns // n_kb
    state_rows = slice((pack - 1) * nb, pack * nb)

    @pl.when(step == 0)
    def _():
        stre_ref[...] = jnp.zeros_like(stre_ref)
        stim_ref[...] = jnp.zeros_like(stim_ref)
        for r in range(pack):
            stre_ref[r * nb:(r + 1) * nb, :] = s0re_ref[...]
            stim_ref[r * nb:(r + 1) * nb, :] = s0im_ref[...]
        xs_ref[...] = jnp.zeros_like(xs_ref)
        xtail_ref[...] = jnp.zeros_like(xtail_ref)

    for b in range(nb):
        for kb in range(n_kb):
            xs_ref[kb, pl.ds(b, tt, stride=b8), :] = x_ref[b, :, kb * LANES:(kb + 1) * LANES]

    n_out = wcre_ref.shape[0]
    kin = ns // n_out
    wout = d // n_out
    kb_per_chunk = kin // per_kb

    first_half = lax.broadcasted_iota(jnp.int32, (SUBLANES, 1), 0) < nb

    def project_in(c):
        for kb in range(c * kb_per_chunk, (c + 1) * kb_per_chunk):
            cols = slice(kb * per_kb, (kb + 1) * per_kb)
            x_t = xs_ref[kb]
            if pack == 1:
                lhs = x_t.astype(BF16)
                w_re, w_im = wbre_ref[kb, 0:LANES, :], wbim_ref[kb, 0:LANES, :]
            else:
                x_prev = pltpu.roll(x_t, nb, 0)
                head = jnp.where(first_half, pltpu.roll(xtail_ref[kb], nb, 0), x_prev[0:SUBLANES])
                x_prev = jnp.concatenate([head, x_prev[SUBLANES:]], axis=0)
                xtail_ref[kb] = x_t[x_t.shape[0] - SUBLANES:]
                lhs = jnp.concatenate([x_t, x_prev], axis=1).astype(BF16)
                w_re, w_im = wbre_ref[kb], wbim_ref[kb]
            bre_ref[:, cols] = _dot(lhs, w_re)
            bim_ref[:, cols] = _dot(lhs, w_im)

    def scan_packed(cs):
        width = cs.stop - cs.start
        are = jnp.broadcast_to(are_ref[:, cs], (SUBLANES, width))
        aim = jnp.broadcast_to(aim_ref[:, cs], (SUBLANES, width))
        a2re, a2im = are * are - aim * aim, 2.0 * are * aim
        sre, sim = stre_ref[:, cs], stim_ref[:, cs]
        for v in range(tt // pack):
            rows = slice(v * SUBLANES, (v + 1) * SUBLANES)
            if v == 0:
                start = jnp.logical_and(first_half, step == 0)
                cre, cim = jnp.where(start, are, a2re), jnp.where(start, aim, a2im)
            else:
                cre, cim = a2re, a2im
            sre, sim = (bre_ref[rows, cs] + (cre * sre - cim * sim), bim_ref[rows, cs] + (cre * sim + cim * sre))
            bre_ref[rows, cs] = sre
            bim_ref[rows, cs] = sim
        stre_ref[:, cs] = sre
        stim_ref[:, cs] = sim

    def scan(c):
        if pack == 2:
            half = kin // 2
            for c0 in range(c * kin, (c + 1) * kin, half):
                scan_packed(slice(c0, c0 + half))
            return
        cs = slice(c * kin, (c + 1) * kin)
        are = jnp.broadcast_to(are_ref[:, cs], (SUBLANES, kin))
        aim = jnp.broadcast_to(aim_ref[:, cs], (SUBLANES, kin))
        for r0 in range(0, b8, SUBLANES):
            sre = stre_ref[r0:r0 + SUBLANES, cs]
            sim = stim_ref[r0:r0 + SUBLANES, cs]
            for t in range(tt):
                rows = slice(t * b8 + r0, t * b8 + r0 + SUBLANES)
                sre, sim = (are * sre - aim * sim + bre_ref[rows, cs], are * sim + aim * sre + bim_ref[rows, cs])
                bre_ref[rows, cs] = sre
                bim_ref[rows, cs] = sim
            stre_ref[r0:r0 + SUBLANES, cs] = sre
            stim_ref[r0:r0 + SUBLANES, cs] = sim

    def project_out(c):
        cs = slice(c * kin, (c + 1) * kin)
        y = _dot(bre_ref[:, cs].astype(BF16), wcre_ref[c]) + _dot(bim_ref[:, cs].astype(BF16), wcim_ref[c])
        for i in range(wout // LANES):
            kb = c * (wout // LANES) + i
            xs_ref[kb] = y[:, i * LANES:(i + 1) * LANES] + xs_ref[kb] * d_ref[:, kb * LANES:(kb + 1) * LANES]

    project_in(0)
    for c in range(n_out):
        if c + 1 < n_out:
            project_in(c + 1)
        scan(c)
        if c >= 1:
            project_out(c - 1)
    project_out(n_out - 1)

    for b in range(nb):
        for kb in range(n_kb):
            y_ref[b, :, kb * LANES:(kb + 1) * LANES] = xs_ref[kb, pl.ds(b, tt, stride=b8), :]

    @pl.when(step == pl.num_programs(0) - 1)
    def _():
        sre_ref[...] = stre_ref[state_rows, :]
        sim_ref[...] = stim_ref[state_rows, :]


def _s5_weights(lam_re, lam_im, b_re, b_im, c_re, c_im, log_dt):
    g, p, h = b_re.shape
    lam = lax.complex(lam_re.astype(F32), lam_im.astype(F32))
    dt = jnp.exp(log_dt.astype(F32))[:, None]
    lam_bar = jnp.exp(lam * dt)
    b_bar = ((lam_bar - 1.0) / lam)[:, :, None] * lax.complex(b_re.astype(F32), b_im.astype(F32))
    gpk = LANES // h
    n_kb = g // gpk
    eye = jnp.eye(gpk, dtype=F32)

    def in_blocks(m):
        m = m.reshape(n_kb, gpk, p, h)
        return jnp.einsum("kgph,gf->kghfp", m, eye).reshape(n_kb, gpk * h, gpk * p).astype(BF16)

    gpo = 2 * LANES // h
    n_out = g // gpo
    eye_o = jnp.eye(gpo, dtype=F32)

    def out_blocks(m):
        m = m.reshape(n_out, gpo, h, p)
        return jnp.einsum("kghp,gf->kgpfh", m, eye_o).reshape(n_out, gpo * p, gpo * h).astype(BF16)

    ab_bar = lam_bar[:, :, None] * b_bar
    two_step = lambda part: jnp.concatenate([in_blocks(part(b_bar)), in_blocks(part(ab_bar))], axis=1)
    return (two_step(jnp.real), two_step(jnp.imag),
            jnp.real(lam_bar).reshape(1, g * p), jnp.imag(lam_bar).reshape(1, g * p),
            out_blocks(c_re.astype(F32)), out_blocks(-c_im.astype(F32)))


def s5_scan(x, s0_re, s0_im, weights, d_skip, *, tt):
    nb, seq, d = x.shape
    wbre, wbim, are, aim, wcre, wcim = weights
    ns = are.shape[1]
    pack = 2 if (2 * nb == SUBLANES and tt % 2 == 0) else 1
    b8 = nb if pack == 2 else -(-nb // SUBLANES) * SUBLANES
    rows = tt * b8
    state_rows = SUBLANES if pack == 2 else b8
    const = lambda a: pl.BlockSpec(a.shape, lambda i: (0,) * a.ndim)
    xblk = pl.BlockSpec((nb, tt, d), lambda i: (0, i, 0))
    sblk = pl.BlockSpec((nb, ns), lambda i: (0, 0))
    return pl.pallas_call(
        functools.partial(_s5_scan_kernel, pack=pack),
        grid=(seq // tt,),
        in_specs=[xblk, sblk, sblk, const(wbre), const(wbim), const(are), const(aim), const(wcre), const(wcim),
                  pl.BlockSpec((1, d), lambda i: (0, 0))],
        out_specs=[xblk, sblk, sblk],
        out_shape=[jax.ShapeDtypeStruct((nb, seq, d), F32), jax.ShapeDtypeStruct((nb, ns), F32),
                   jax.ShapeDtypeStruct((nb, ns), F32)],
        scratch_shapes=[pltpu.VMEM((d // LANES, rows, LANES), F32), pltpu.VMEM((rows, ns), F32),
                        pltpu.VMEM((rows, ns), F32),
                        pltpu.VMEM((state_rows, ns), F32), pltpu.VMEM((state_rows, ns), F32),
                        pltpu.VMEM((d // LANES, SUBLANES, LANES), F32)],
        compiler_params=_params("arbitrary"),
    )(x, s0_re, s0_im, wbre, wbim, are, aim, wcre, wcim, d_skip)


def _s5_out_ln_kernel(y_ref, wo_ref, wg_ref, x_ref, g_ref, b_ref, o_ref, *, alpha):
    gl = jax.nn.gelu(y_ref[...]).astype(BF16)
    mix = _dot(gl, wo_ref[...]) * jax.nn.sigmoid(_dot(gl, wg_ref[...]))
    o_ref[...] = _layer_norm(alpha * x_ref[...] + mix, g_ref[...], b_ref[...])


def s5_out_ln(y, w_out, w_gate, x, g, b, *, layer, alpha, tm):
    t, d = x.shape
    row = pl.BlockSpec((tm, d), lambda i: (i, 0))
    const = lambda r, c: pl.BlockSpec((r, c), lambda i: (0, 0))
    return pl.pallas_call(
        functools.partial(_s5_out_ln_kernel, alpha=alpha),
        grid=(t // tm,),
        in_specs=[row, _layer_weight(layer, d, d), _layer_weight(layer, d, d), row, const(1, d), const(1, d)],
        out_specs=row,
        out_shape=jax.ShapeDtypeStruct((t, d), F32),
        compiler_params=_params("parallel"),
    )(y, w_out, w_gate, x, g, b)


class _TilePlan(NamedTuple):
    token_rows: int
    proj_rows: int
    s5_steps: int
    ret_tiles: int
    ret_seqs_per_tile: int


def _tile_plan(lp, ls):
    return _TilePlan(token_rows=min(1024, lp), proj_rows=min(512, lp), s5_steps=min(64, lp),
                     ret_tiles=min(4, lp // RET_CHUNK), ret_seqs_per_tile=max(RET_CHUNK // ls, 1))


def _rope_tables(pos, n_heads):
    half = HEAD_DIM // 2
    inv = ROPE_THETA ** (-jnp.arange(half, dtype=F32) / half)
    ang = pos.astype(F32)[:, None] * inv[None, :]
    cos = jnp.cos(ang)
    sin = jnp.sin(ang)
    cos_h = jnp.concatenate([cos, cos], axis=-1)
    sin_h = jnp.concatenate([-sin, sin], axis=-1)
    return jnp.tile(cos_h, (1, n_heads)), jnp.tile(sin_h, (1, n_heads))


def kernel(x_prompt, x_sample, cache_k, cache_v, page_table, state_ret, state_s5_re, state_s5_im, ffn1_w_gate, ffn1_w_up, ffn1_w_down, ffn2_w_gate, ffn2_w_up, ffn2_w_down, ln_g, ln_b, w_in_ab, w_out_ab, s5_lam_re, s5_lam_im, s5_b_re, s5_b_im, s5_c_re, s5_c_im, s5_d, s5_log_dt, s5_w_out, s5_w_gate):
    bp, lp, d = x_prompt.shape
    bs, ls, _ = x_sample.shape
    depth = ffn1_w_gate.shape[0]
    n_layers_ab, n_pool, page, a_heads, hd = cache_k.shape
    n_pages = page_table.shape[1]
    past_len = n_pages * page
    half = a_heads * hd
    n_heads = half // HEAD_DIM
    assert hd == HEAD_DIM and lp % MOBA_BLOCK == 0 and past_len % MOBA_BLOCK == 0 and ls <= MOBA_BLOCK
    assert MOBA_BLOCK % page == 0 and lp % page == 0 and lp % RET_CHUNK == 0
    alpha = (2 * depth) ** 0.25
    tp, ts = bp * lp, bs * ls
    plan = _tile_plan(lp, ls)
    tm_p, tm_proj = plan.token_rows, plan.proj_rows

    xp = x_prompt.reshape(tp, d)
    xs = x_sample.reshape(ts, d)
    bf = lambda a: a.astype(BF16)
    f1 = (bf(ffn1_w_gate), bf(ffn1_w_up), bf(ffn1_w_down))
    f2 = (bf(ffn2_w_gate), bf(ffn2_w_up), bf(ffn2_w_down))
    w_in, w_out = bf(w_in_ab), bf(w_out_ab)
    w_o, w_g = bf(s5_w_out), bf(s5_w_gate)
    cos_p, sin_p = _rope_tables(jnp.arange(lp, dtype=jnp.int32), n_heads)
    cos_s, sin_s = _rope_tables(past_len + jnp.arange(ls, dtype=jnp.int32), n_heads)
    cos_s, sin_s = jnp.tile(cos_s, (bs, 1)), jnp.tile(sin_s, (bs, 1))
    pool_kt = jnp.transpose(cache_k, (0, 1, 3, 4, 2)).reshape(n_layers_ab * n_pool, half, page)
    pool_vt = jnp.transpose(cache_v, (0, 1, 3, 4, 2)).reshape(n_layers_ab * n_pool, half, page)
    seq_per_tile, ret_tiles_p = plan.ret_seqs_per_tile, plan.ret_tiles

    cache_pages = None
    k_s, v_s, r_p, r_s = [], [], [], []
    sre_p, sim_p, sre_s, sim_s = [], [], [], []
    for layer in range(depth):
        li = layer // 2
        g = lambda i: ln_g[layer, i][None, :]
        b = lambda i: ln_b[layer, i][None, :]
        xp, xs = ffn_ln(xp, xs, *f1, g(0), b(0), layer=layer, alpha=alpha, tm=tm_p)
        if layer % 2 == 0:
            qa, kbf, vt, ksum, kpg, vpg, qb, kb, vb, gb = ab_project(xp, w_in, cos_p, sin_p, layer=li, tm=tm_proj,
                                                                     seq_len=lp, page=page, pages=cache_pages)
            cache_pages = (kpg, vpg)
            attn = moba_prompt(qa, ksum, kbf, vt, batch=bp, seq_len=lp)
            zero_state = jnp.zeros((bp, n_heads // 2, HEAD_PAIR, HEAD_PAIR), F32)
            xp, s_fin = retention_merge_ln(qb, kb, vb, gb, zero_state, attn, xp, w_out, g(1), b(1), layer=li,
                                           alpha=alpha, n_seq=1, chunk=RET_CHUNK, tiles=ret_tiles_p,
                                           steps_per_seq=lp // (RET_CHUNK * ret_tiles_p))
            r_p.append(_block_diag_to_states(s_fin))
            qa, ka, va, qb, kb, vb, gb = ab_project(xs, w_in, cos_s, sin_s, layer=li, tm=ts, seq_len=ts)
            page_ids = (page_table.astype(jnp.int32) + li * n_pool).reshape(-1)
            attn = moba_sample(qa, ka, va, pool_kt, pool_vt, page_ids, batch=bs, n_pages=n_pages)
            xs, s_fin = retention_merge_ln(qb, kb, vb, gb, _states_to_block_diag(state_ret[li]), attn, xs, w_out,
                                           g(1), b(1), layer=li, alpha=alpha, n_seq=seq_per_tile, chunk=ls, tiles=1,
                                           steps_per_seq=1)
            k_s.append(ka.reshape(bs, ls, a_heads, hd))
            v_s.append(va.reshape(bs, ls, a_heads, hd))
            r_s.append(_block_diag_to_states(s_fin))
        else:
            weights = _s5_weights(s5_lam_re[li], s5_lam_im[li], s5_b_re[li], s5_b_im[li], s5_c_re[li],
                                  s5_c_im[li], s5_log_dt[li])
            n_state = weights[2].shape[1]
            d_skip = s5_d[li][None, :]
            zero = jnp.zeros((bp, n_state), F32)
            y, a_re, a_im = s5_scan(xp.reshape(bp, lp, d), zero, zero, weights, d_skip, tt=plan.s5_steps)
            xp = s5_out_ln(y.reshape(tp, d), w_o, w_g, xp, g(1), b(1), layer=li, alpha=alpha, tm=tm_p)
            sre_p.append(a_re.reshape(bp, -1, S5_STATE))
            sim_p.append(a_im.reshape(bp, -1, S5_STATE))
            y, a_re, a_im = s5_scan(xs.reshape(bs, ls, d), state_s5_re[li].reshape(bs, n_state),
                                    state_s5_im[li].reshape(bs, n_state), weights, d_skip, tt=ls)
            xs = s5_out_ln(y.reshape(ts, d), w_o, w_g, xs, g(1), b(1), layer=li, alpha=alpha, tm=ts)
            sre_s.append(a_re.reshape(bs, -1, S5_STATE))
            sim_s.append(a_im.reshape(bs, -1, S5_STATE))
        xp, xs = ffn_ln(xp, xs, *f2, g(2), b(2), layer=layer, alpha=alpha, tm=tm_p)
    unpage = lambda t: jnp.transpose(t.reshape(n_layers_ab, bp, lp // page, a_heads, hd, page), (0, 1, 2, 5, 3, 4))
    return (xp.reshape(bp, lp, d), xs.reshape(bs, ls, d), unpage(kpg), unpage(vpg), jnp.stack(k_s),
            jnp.stack(v_s), jnp.stack(r_p), jnp.stack(r_s), jnp.stack(sre_p), jnp.stack(sim_p),
            jnp.stack(sre_s), jnp.stack(sim_s))
```

```python
import functools
import math
from typing import NamedTuple

import jax
import jax.numpy as jnp
from jax import lax
from jax.experimental import pallas as pl
from jax.experimental.pallas import tpu as pltpu

F32 = jnp.float32
BF16 = jnp.bfloat16

HEAD_DIM = 64
HEAD_PAIR = 2 * HEAD_DIM
MOBA_BLOCK = 256
MOBA_TOPK = 3
MOBA_BLOCKS_PER_TRIP = 4
RET_CHUNK = 128
S5_STATE = 64
ROPE_THETA = 10000.0
LN_EPS = 1e-5
GN_EPS = 1e-6
NEG_INF = -1e30
LOG2_E = math.log2(math.e)
SUBLANES = 8
BF16_SUBLANES = 16
LANES = 128
VMEM_LIMIT = 48 * 1024 * 1024


def _dot(a, b, precision=None):
    return jnp.dot(a, b, preferred_element_type=F32, precision=precision)


def _dot_nt(a, b, precision=None):
    return lax.dot_general(a, b, (((1,), (1,)), ((), ())), preferred_element_type=F32, precision=precision)


def _dot_tn(a, b):
    return lax.dot_general(a, b, (((0,), (0,)), ((), ())), preferred_element_type=F32)


def _layer_norm(r, g, b):
    mu = jnp.mean(r, -1, keepdims=True)
    d = r - mu
    var = jnp.mean(d * d, -1, keepdims=True)
    return d * lax.rsqrt(var + LN_EPS) * g + b


def _silu(x):
    return x * jax.nn.sigmoid(x)


def _params(*sem):
    return pltpu.CompilerParams(dimension_semantics=sem, vmem_limit_bytes=VMEM_LIMIT)


def _ffn_ln_kernel(xp_ref, xs_ref, wg_ref, wu_ref, wd_ref, g_ref, b_ref, op_ref, os_ref, a_ref, *, alpha, tf):
    def run(x_ref, o_ref):
        rows = x_ref.shape[0]
        xb = x_ref[...].astype(BF16)
        for c in range(wg_ref.shape[1] // tf):
            cols = slice(c * tf, (c + 1) * tf)
            hg = _dot(xb, wg_ref[:, cols])
            hu = _dot(xb, wu_ref[:, cols])
            a_ref[0:rows, cols] = (_silu(hg) * hu).astype(BF16)
        r = alpha * x_ref[...] + 0.5 * _dot(a_ref[0:rows, :], wd_ref[...])
        o_ref[...] = _layer_norm(r, g_ref[...], b_ref[...])

    last = pl.num_programs(0) - 1
    pl.when(pl.program_id(0) < last)(lambda: run(xp_ref, op_ref))
    pl.when(pl.program_id(0) == last)(lambda: run(xs_ref, os_ref))


def _layer_weight(layer, r, c, row_blk=0, **kw):
    return pl.BlockSpec((None, r, c), lambda *_: (layer, row_blk, 0), **kw)


def ffn_ln(xp, xs, wg, wu, wd, g, b, *, layer, alpha, tm, tf=256):
    tp, d = xp.shape
    ts = xs.shape[0]
    f = wg.shape[2]
    n_p = tp // tm
    assert ts <= tm
    resident = lambda r, c: _layer_weight(layer, r, c, pipeline_mode=pl.Buffered(1))
    p_rows = pl.BlockSpec((tm, d), lambda i: (jnp.minimum(i, n_p - 1), 0))
    s_rows = pl.BlockSpec((ts, d), lambda i: (0, 0))
    return pl.pallas_call(
        functools.partial(_ffn_ln_kernel, alpha=alpha, tf=tf),
        grid=(n_p + 1,),
        in_specs=[p_rows, s_rows, resident(d, f), resident(d, f), resident(f, d),
                  pl.BlockSpec((1, d), lambda i: (0, 0)), pl.BlockSpec((1, d), lambda i: (0, 0))],
        out_specs=[p_rows, s_rows],
        out_shape=[jax.ShapeDtypeStruct((tp, d), F32), jax.ShapeDtypeStruct((ts, d), F32)],
        scratch_shapes=[pltpu.VMEM((tm, f), BF16)],
        compiler_params=_params("arbitrary"),
    )(xp, xs, wg, wu, wd, g, b)


def _rope(y, cos, sin_signed):
    width = y.shape[-1]
    lane = lax.broadcasted_iota(jnp.int32, (1, width), 1)
    first = (lane % HEAD_DIM) < (HEAD_DIM // 2)
    rot = jnp.where(first, pltpu.roll(y, width - HEAD_DIM // 2, 1), pltpu.roll(y, HEAD_DIM // 2, 1))
    return y * cos + rot * sin_signed


def _ab_proj_kernel(*refs, half, scale, page, layer, new_slots):
    if page is None:
        x_ref, w_ref, cos_ref, sin_ref, qa_ref, ka_ref, va_ref, qb_ref, kb_ref, vb_ref, gb_ref = refs
    else:
        x_ref, w_ref, cos_ref, sin_ref = refs[:4]
        qa_ref, kbf_ref, vt_ref, ksum_ref, kpg_ref, vpg_ref, qb_ref, kb_ref, vb_ref, gb_ref = refs[-10:]
        if new_slots is not None:
            for slot in range(new_slots):
                if slot != layer:
                    kpg_ref[slot] = jnp.zeros(kpg_ref.shape[1:], F32)
                    vpg_ref[slot] = jnp.zeros(vpg_ref.shape[1:], F32)
            kpg_ref, vpg_ref = kpg_ref.at[layer], vpg_ref.at[layer]
    xb = x_ref[...].astype(BF16)
    cos = cos_ref[...]
    sin = sin_ref[...]
    col = lambda c: _dot(xb, w_ref[:, c * half:(c + 1) * half])
    qa_ref[...] = _rope(col(0), cos, sin) * scale
    ka = _rope(col(1), cos, sin)
    va = col(2)
    if page is None:
        ka_ref[...] = ka
        va_ref[...] = va
    else:
        kbf_ref[...] = ka.astype(BF16)
        for n in range(ka.shape[0] // MOBA_BLOCK):
            blk = slice(n * MOBA_BLOCK, (n + 1) * MOBA_BLOCK)
            ksum_ref[n] = jnp.sum(ka[blk], axis=0, keepdims=True)
            vt_ref[n] = va[blk].T.astype(BF16)
        for n in range(ka.shape[0] // page):
            rows = slice(n * page, (n + 1) * page)
            kpg_ref[n] = ka[rows].T
            vpg_ref[n] = va[rows].T
    qb_ref[...] = _rope(col(3), cos, sin)
    kb_ref[...] = _rope(col(4), cos, sin) * scale
    vb_ref[...] = col(5)
    gb_ref[...] = col(6)


def ab_project(x, w_in, cos, sin, *, layer, tm, seq_len, page=None, pages=None):
    t, d = x.shape
    n_layers = w_in.shape[0]
    half = w_in.shape[2] // 7
    tiles_per_seq = max(seq_len // tm, 1)
    tab = pl.BlockSpec((tm, half), lambda i: (i % tiles_per_seq, 0))
    row = pl.BlockSpec((tm, half), lambda i: (i, 0))
    f32o = jax.ShapeDtypeStruct((t, half), F32)
    in_specs = [pl.BlockSpec((tm, d), lambda i: (i, 0)), _layer_weight(layer, d, 7 * half), tab, tab]
    operands = [x, w_in, cos, sin]
    aliases = {}
    if page is None:
        out_specs = [row] * 7
        out_shape = [f32o] * 7
    else:
        nblk = tm // MOBA_BLOCK
        pages_shape = jax.ShapeDtypeStruct((n_layers, t // page, half, page), F32)
        if pages is None:
            page_blk = pl.BlockSpec((n_layers, tm // page, half, page), lambda i: (0, i, 0, 0))
        else:
            page_blk = pl.BlockSpec((None, tm // page, half, page), lambda i: (layer, i, 0, 0))
            in_specs += [pl.BlockSpec(memory_space=pl.ANY)] * 2
            operands += list(pages)
            aliases = {4: 4, 5: 5}
        out_specs = [row, row,
                     pl.BlockSpec((nblk, half, MOBA_BLOCK), lambda i: (i, 0, 0)),
                     pl.BlockSpec((nblk, 1, half), lambda i: (i, 0, 0)),
                     page_blk, page_blk, row, row, row, row]
        out_shape = [f32o, jax.ShapeDtypeStruct((t, half), BF16),
                     jax.ShapeDtypeStruct((t // MOBA_BLOCK, half, MOBA_BLOCK), BF16),
                     jax.ShapeDtypeStruct((t // MOBA_BLOCK, 1, half), F32),
                     pages_shape, pages_shape, f32o, f32o, f32o, f32o]
    return pl.pallas_call(
        functools.partial(_ab_proj_kernel, half=half, scale=HEAD_DIM ** -0.5, page=page, layer=layer,
                          new_slots=n_layers if (page is not None and pages is None) else None),
        grid=(t // tm,),
        in_specs=in_specs,
        out_specs=out_specs,
        out_shape=out_shape,
        input_output_aliases=aliases,
        compiler_params=_params("parallel"),
    )(*operands)


def _topk_select(gate, blk, n_valid, axis):
    gate = jnp.where(blk < n_valid, gate, NEG_INF)
    taken = jnp.zeros(gate.shape, F32)
    blk_f = blk.astype(F32)
    for _ in range(MOBA_TOPK):
        top = jnp.max(gate, axis=axis, keepdims=True)
        first = jnp.min(jnp.where(gate == top, blk_f, float(gate.shape[axis])), axis=axis, keepdims=True)
        hit = blk_f == first
        taken = jnp.where(hit, 1.0, taken)
        gate = jnp.where(hit, -jnp.inf, gate)
    return jnp.where(blk < n_valid, taken, 0.0)


def _moba_prompt_kernel(q_ref, ksum_ref, k_ref, vt_ref, o_ref, qm_ref, sel_ref, m_ref, l_ref, acc_ref, s_ref):
    j = pl.program_id(1)
    tq = q_ref.shape[0]
    n_heads = q_ref.shape[1] // HEAD_DIM
    lane = lax.broadcasted_iota(jnp.int32, (1, HEAD_PAIR), 1)
    key_i = lax.broadcasted_iota(jnp.int32, (MOBA_BLOCK, tq), 0)
    qry_i = lax.broadcasted_iota(jnp.int32, (MOBA_BLOCK, tq), 1)
    causal = key_i <= qry_i

    nb = ksum_ref.shape[0]
    blk = lax.broadcasted_iota(jnp.int32, (nb, 1), 0)
    for hp in range(n_heads // 2):
        lanes = slice(hp * HEAD_PAIR, (hp + 1) * HEAD_PAIR)
        qp = q_ref[:, lanes]
        kmean = ksum_ref[:, 0, lanes] * (1.0 / MOBA_BLOCK)
        masks = [(lane // HEAD_DIM) == half for half in range(2)]
        gate = _dot_nt(jnp.concatenate([jnp.where(hm, kmean, 0.0) for hm in masks], axis=0), qp,
                       precision=lax.Precision.HIGHEST)
        for half, hm in enumerate(masks):
            h = 2 * hp + half
            sel_ref[h] = _topk_select(gate[half * nb:(half + 1) * nb], blk, j, 0)
            qm_ref[h] = jnp.where(hm, qp * LOG2_E, 0.0).astype(BF16)

    ones_rows = jnp.ones((BF16_SUBLANES, MOBA_BLOCK), BF16)

    def update(blocks, first):
        def rows0(n):
            return pl.multiple_of((j if n is None else n) * MOBA_BLOCK, MOBA_BLOCK)

        def picked(n, h):
            return sel_ref[h, pl.ds(n, 1), :] > 0.0

        def scores(slot, n, h):
            k_pair = k_ref[pl.ds(rows0(n), MOBA_BLOCK), (h // 2) * HEAD_PAIR:(h // 2 + 1) * HEAD_PAIR]
            s_ref[slot, h] = _dot_nt(k_pair, qm_ref[h])

        def softmax_values(slot, n, h):
            if first:
                s = jnp.where(causal, s_ref[slot, h], NEG_INF)
                m_new = jnp.max(s, axis=0, keepdims=True)
            else:
                s = s_ref[slot, h]
                cmax = jnp.where(picked(n, h), jnp.max(s, axis=0, keepdims=True), NEG_INF)
                m_new = jnp.maximum(m_ref[h], cmax)
                a = jnp.exp2(m_ref[h] - m_new)
            p = jnp.exp2(s - m_new).astype(BF16)
            m_ref[h] = m_new
            v_h = vt_ref[j if n is None else n, h * HEAD_DIM:(h + 1) * HEAD_DIM, :]
            res = _dot(jnp.concatenate([v_h, ones_rows], axis=0), p)
            pv, psum = res[:HEAD_DIM], res[HEAD_DIM:HEAD_DIM + 1]
            if first:
                acc_ref[h] = pv
                l_ref[h] = psum
            else:
                acc_ref[h] = a * acc_ref[h] + jnp.where(picked(n, h), pv, 0.0)
                l_ref[h] = a * l_ref[h] + jnp.where(picked(n, h), psum, 0.0)

        heads = range(n_heads)
        stages = (scores, softmax_values)
        for t in range(len(stages) + len(blocks) - 1):
            for h in heads:
                for slot, n in enumerate(blocks):
                    if 0 <= t - slot < len(stages):
                        stages[t - slot](slot, n, h)

    update([None], True)

    n_slots = s_ref.shape[0]

    def body(i, carry):
        update([n_slots * i + k for k in range(n_slots)], False)
        return carry

    lax.fori_loop(0, j // n_slots, body, 0)
    for rem in range(1, n_slots):
        @pl.when(j % n_slots == rem)
        def _(rem=rem):
            update([j - rem + k for k in range(rem)], False)

    out_t = jnp.concatenate([acc_ref[h] / l_ref[h] for h in range(n_heads)], axis=0)
    o_ref[...] = out_t.T.astype(o_ref.dtype)


def moba_prompt(q, ksum, kbf, vt, *, batch, seq_len):
    t, w = q.shape
    nb = seq_len // MOBA_BLOCK
    n_heads = w // HEAD_DIM
    return pl.pallas_call(
        _moba_prompt_kernel,
        grid=(batch, nb),
        in_specs=[pl.BlockSpec((MOBA_BLOCK, w), lambda b, j: (b * nb + j, 0)),
                  pl.BlockSpec((nb, 1, w), lambda b, j: (b, 0, 0)),
                  pl.BlockSpec((seq_len, w), lambda b, j: (b, 0)),
                  pl.BlockSpec((nb, w, MOBA_BLOCK), lambda b, j: (b, 0, 0))],
        out_specs=pl.BlockSpec((MOBA_BLOCK, w), lambda b, j: (b * nb + j, 0)),
        out_shape=jax.ShapeDtypeStruct((t, w), BF16),
        scratch_shapes=[pltpu.VMEM((n_heads, MOBA_BLOCK, HEAD_PAIR), BF16),
                        pltpu.VMEM((n_heads, nb, MOBA_BLOCK), F32),
                        pltpu.VMEM((n_heads, 1, MOBA_BLOCK), F32),
                        pltpu.VMEM((n_heads, 1, MOBA_BLOCK), F32),
                        pltpu.VMEM((n_heads, HEAD_DIM, MOBA_BLOCK), F32),
                        pltpu.VMEM((MOBA_BLOCKS_PER_TRIP, n_heads, MOBA_BLOCK, MOBA_BLOCK), F32)],
        compiler_params=_params("parallel", "arbitrary"),
    )(q, ksum, kbf, vt)


def _moba_sample_kernel(pt_ref, q_ref, knew_ref, vnew_ref, *rest, pages_per_step, page, n_heads):
    del pt_ref
    kpages = rest[:pages_per_step]
    vpages = rest[pages_per_step:2 * pages_per_step]
    o_ref = rest[2 * pages_per_step]
    qf_ref, qb_ref, ksum_ref, m_ref, l_ref, acc_ref, kpad_ref, vpad_ref = rest[2 * pages_per_step + 1:]
    s_idx = pl.program_id(1)
    n_steps = pl.num_programs(1)
    lq, w = q_ref.shape
    rows = n_heads * lq
    nb = acc_ref.shape[0]
    pages_per_blk = MOBA_BLOCK // page
    blks_per_step = pages_per_step // pages_per_blk
    rowhead = lax.broadcasted_iota(jnp.int32, (rows, 1), 0) // lq
    lanehead = lax.broadcasted_iota(jnp.int32, (1, w), 1) // HEAD_DIM
    blk_lane = lax.broadcasted_iota(jnp.int32, (1, LANES), 1)

    @pl.when(s_idx == 0)
    def _():
        qt = jnp.concatenate([q_ref[...]] * n_heads, axis=0)
        qbd = jnp.where(rowhead == lanehead, qt, 0.0)
        qf_ref[...] = qbd
        qb_ref[...] = qbd.astype(BF16)
        ksum_ref[...] = jnp.zeros_like(ksum_ref)
        m_ref[...] = jnp.zeros_like(m_ref)
        l_ref[...] = jnp.zeros_like(l_ref)

    qb = qb_ref[...]

    def partial_softmax(s):
        m = jnp.max(s, axis=1, keepdims=True)
        e = jnp.exp(s - m)
        return m, jnp.sum(e, axis=1, keepdims=True), e.astype(BF16)

    blks = range(blks_per_step)
    pages_of = lambda bi: range(bi * pages_per_blk, (bi + 1) * pages_per_blk)
    here = [blk_lane == s_idx * blks_per_step + bi for bi in blks]
    kt = [jnp.concatenate([kpages[i][0] for i in pages_of(bi)], axis=1) for bi in blks]
    scores = [_dot(qb, kt[bi].astype(BF16)) for bi in blks]
    ksum = ksum_ref[...]
    for bi in blks:
        ksum = jnp.where(here[bi], jnp.sum(kt[bi], axis=1, keepdims=True), ksum)
    ksum_ref[...] = ksum
    stats = [partial_softmax(scores[bi]) for bi in blks]
    for bi in blks:
        vt = jnp.concatenate([vpages[i][0] for i in pages_of(bi)], axis=1)
        acc_ref[s_idx * blks_per_step + bi] = _dot_nt(stats[bi][2], vt.astype(BF16))
    m_all, l_all = m_ref[...], l_ref[...]
    for bi in blks:
        m_all = jnp.where(here[bi], stats[bi][0], m_all)
        l_all = jnp.where(here[bi], stats[bi][1], l_all)
    m_ref[...] = m_all
    l_ref[...] = l_all

    @pl.when(s_idx == n_steps - 1)
    def _():
        kpad_ref[...] = jnp.zeros_like(kpad_ref)
        vpad_ref[...] = jnp.zeros_like(vpad_ref)
        kpad_ref[0:lq, :] = knew_ref[...]
        vpad_ref[0:lq, :] = vnew_ref[...]
        s_own = _dot_nt(qb, kpad_ref[...].astype(BF16))
        tq = lax.broadcasted_iota(jnp.int32, s_own.shape, 0) % lq
        tk = lax.broadcasted_iota(jnp.int32, s_own.shape, 1)
        m_own, l_own, e_own = partial_softmax(jnp.where(tk <= tq, s_own, NEG_INF))
        acc_own = _dot(e_own, vpad_ref[...].astype(BF16))

        gate = _dot(qf_ref[...], ksum_ref[...] * (1.0 / MOBA_BLOCK), precision=lax.Precision.HIGHEST)
        sel = _topk_select(gate[:, :nb], blk_lane[:, :nb], nb, 1) > 0.0
        m_blk = m_ref[:, :nb]
        m_all = jnp.maximum(m_own, jnp.max(jnp.where(sel, m_blk, NEG_INF), axis=1, keepdims=True))
        wgt = jnp.where(sel, jnp.exp(m_blk - m_all), 0.0)
        w_own = jnp.exp(m_own - m_all)
        den = w_own * l_own + jnp.sum(wgt * l_ref[:, :nb], axis=1, keepdims=True)
        num = w_own * acc_own
        for n in range(nb):
            num = num + wgt[:, n:n + 1] * acc_ref[n]
        o_all = num / den
        out = jnp.zeros((lq, w), F32)
        for h in range(n_heads):
            out = out + jnp.where(lanehead == h, o_all[h * lq:(h + 1) * lq], 0.0)
        o_ref[...] = out


def moba_sample(q, knew, vnew, pool_kt, pool_vt, page_ids, *, batch, n_pages, pages_per_step=32):
    t, w = q.shape
    lq = t // batch
    page = pool_kt.shape[2]
    n_heads = w // HEAD_DIM
    rows = n_heads * lq
    nb = n_pages * page // MOBA_BLOCK
    assert nb <= LANES and pages_per_step % (MOBA_BLOCK // page) == 0
    seq_blk = pl.BlockSpec((lq, w), lambda b, s, pt: (b, 0))

    def page_spec(i):
        return pl.BlockSpec((1, w, page), lambda b, s, pt: (pt[b * n_pages + s * pages_per_step + i], 0, 0))

    grid_spec = pltpu.PrefetchScalarGridSpec(
        num_scalar_prefetch=1,
        grid=(batch, n_pages // pages_per_step),
        in_specs=[seq_blk, seq_blk, seq_blk] + [page_spec(i) for i in range(pages_per_step)] * 2,
        out_specs=seq_blk,
        scratch_shapes=[pltpu.VMEM((rows, w), F32), pltpu.VMEM((rows, w), BF16), pltpu.VMEM((w, LANES), F32),
                        pltpu.VMEM((rows, LANES), F32), pltpu.VMEM((rows, LANES), F32),
                        pltpu.VMEM((nb, rows, w), F32),
                        pltpu.VMEM((LANES, w), F32), pltpu.VMEM((LANES, w), F32)],
    )
    return pl.pallas_call(
        functools.partial(_moba_sample_kernel, pages_per_step=pages_per_step, page=page, n_heads=n_heads),
        grid_spec=grid_spec,
        out_shape=jax.ShapeDtypeStruct((t, w), F32),
        compiler_params=_params("parallel", "arbitrary"),
    )(page_ids, q, knew, vnew, *([pool_kt] * pages_per_step), *([pool_vt] * pages_per_step))


def _retention_kernel(q_ref, k_ref, v_ref, g_ref, s0_ref, dmask_ref, qdec_ref, kdec_ref, cdec_ref,
                      attn_ref, x_ref, wa_ref, wr_ref, lg_ref, lb_ref,
                      o_ref, sout_ref, st_ref, gated_ref, *, n_seq, chunk, tiles, steps_per_seq, alpha):
    step = pl.program_id(0)
    tile_rows = n_seq * chunk
    n_pairs = q_ref.shape[1] // HEAD_PAIR
    lane = lax.broadcasted_iota(jnp.int32, (1, HEAD_PAIR), 1)
    rowh = lax.broadcasted_iota(jnp.int32, (HEAD_PAIR, 1), 0) // HEAD_DIM
    same_head = rowh == (lane // HEAD_DIM)
    rseq = lax.broadcasted_iota(jnp.int32, (tile_rows, 1), 0) // chunk

    @pl.when(step % steps_per_seq == 0)
    def _():
        st_ref[...] = s0_ref[...]

    pairs = range(n_pairs)
    lanes_of = lambda hp: slice(hp * HEAD_PAIR, (hp + 1) * HEAD_PAIR)
    half_masks = [(lane // HEAD_DIM) == half for half in range(2)]
    for ti in range(tiles):
        rows = slice(ti * tile_rows, (ti + 1) * tile_rows)
        qb = [q_ref[rows, lanes_of(hp)].astype(BF16) for hp in pairs]
        kf = [k_ref[rows, lanes_of(hp)] for hp in pairs]
        kb = [k.astype(BF16) for k in kf]
        vb = [v_ref[rows, lanes_of(hp)].astype(BF16) for hp in pairs]
        inner = [[_dot_nt(jnp.where(hm, qb[hp], jnp.zeros_like(qb[hp])), kb[hp]) for hm in half_masks]
                 for hp in pairs]
        scaled = [[(inner[hp][half] * dmask_ref[2 * hp + half]).astype(BF16) for half in range(2)] for hp in pairs]
        ret = [sum(jnp.where(half_masks[half], _dot(scaled[hp][half], vb[hp]), 0.0) for half in range(2))
               for hp in pairs]
        for hp in pairs:
            kd = (kf[hp] * kdec_ref[:, lanes_of(hp)]).astype(BF16)
            cdec = cdec_ref[:, lanes_of(hp)]
            carried = []
            for s in range(n_seq):
                srows = slice(s * chunk, (s + 1) * chunk)
                state = st_ref[s, hp]
                carried.append(_dot(qb[hp][srows], state.astype(BF16)))
                kd_s = kd if n_seq == 1 else jnp.where(rseq == s, kd, jnp.zeros_like(kd))
                st_ref[s, hp] = cdec * state + jnp.where(same_head, _dot_tn(kd_s, vb[hp]), 0.0)
            carried = carried[0] if n_seq == 1 else jnp.concatenate(carried, axis=0)
            ret[hp] = ret[hp] + carried * qdec_ref[:, lanes_of(hp)]
        for hp in pairs:
            normed = jnp.zeros_like(ret[hp])
            for hm in half_masks:
                mu = jnp.sum(jnp.where(hm, ret[hp], 0.0), axis=1, keepdims=True) * (1.0 / HEAD_DIM)
                d = jnp.where(hm, ret[hp] - mu, 0.0)
                var = jnp.sum(d * d, axis=1, keepdims=True) * (1.0 / HEAD_DIM)
                normed = normed + d * lax.rsqrt(var + GN_EPS)
            gated_ref[rows, lanes_of(hp)] = (_silu(g_ref[rows, lanes_of(hp)]) * normed).astype(BF16)

    mix = _dot(attn_ref[...].astype(BF16), wa_ref[...]) + _dot(gated_ref[...], wr_ref[...])
    o_ref[...] = _layer_norm(alpha * x_ref[...] + mix, lg_ref[...], lb_ref[...])

    @pl.when(step % steps_per_seq == steps_per_seq - 1)
    def _():
        sout_ref[...] = st_ref[...]


def _retention_tables(n_heads, n_seq, chunk):
    log_g = jnp.log1p(-jnp.exp2(-5.0 - jnp.arange(n_heads, dtype=F32)))
    idx = jnp.arange(chunk, dtype=F32)
    diff = idx[:, None] - idx[None, :]
    dmask = jnp.where(diff >= 0, jnp.exp(log_g[:, None, None] * jnp.maximum(diff, 0.0)), 0.0)
    q_dec = jnp.exp(log_g[None, :] * (idx[:, None] + 1.0))
    k_dec = jnp.exp(log_g[None, :] * (chunk - 1.0 - idx[:, None]))
    c_dec = jnp.exp(log_g * chunk)
    seq_eye = jnp.eye(n_seq, dtype=F32)
    dmask = jnp.einsum("ab,hij->haibj", seq_eye, dmask).reshape(n_heads, n_seq * chunk, n_seq * chunk)
    lanes = lambda t: jnp.repeat(t, HEAD_DIM, axis=-1)
    return dmask, jnp.tile(lanes(q_dec), (n_seq, 1)), jnp.tile(lanes(k_dec), (n_seq, 1)), lanes(c_dec[None, :])


def retention_merge_ln(q, k, v, g, s0_bd, attn, x, w_out, ln_g, ln_b, *, layer, alpha, n_seq, chunk, tiles,
                       steps_per_seq):
    t, w = q.shape
    d = x.shape[1]
    n_pairs = w // HEAD_PAIR
    rows = n_seq * chunk * tiles
    n_steps = t // rows
    dmask, qdec, kdec, cdec = _retention_tables(w // HEAD_DIM, n_seq, chunk)
    tile_rows = n_seq * chunk
    row = pl.BlockSpec((rows, w), lambda i: (i, 0))
    xrow = pl.BlockSpec((rows, d), lambda i: (i, 0))
    st = pl.BlockSpec((n_seq, n_pairs, HEAD_PAIR, HEAD_PAIR), lambda i: (i // steps_per_seq, 0, 0, 0))
    const = lambda shape: pl.BlockSpec(shape, lambda i: (0,) * len(shape))
    return pl.pallas_call(
        functools.partial(_retention_kernel, n_seq=n_seq, chunk=chunk, tiles=tiles, steps_per_seq=steps_per_seq,
                          alpha=alpha),
        grid=(n_steps,),
        in_specs=[row, row, row, row, st, const(dmask.shape), const((tile_rows, w)), const((tile_rows, w)),
                  const((1, w)), row, xrow, _layer_weight(layer, w, d, 0), _layer_weight(layer, w, d, 1),
                  const((1, d)), const((1, d))],
        out_specs=[xrow, st],
        out_shape=[jax.ShapeDtypeStruct((t, d), F32), jax.ShapeDtypeStruct(s0_bd.shape, F32)],
        scratch_shapes=[pltpu.VMEM((n_seq, n_pairs, HEAD_PAIR, HEAD_PAIR), F32), pltpu.VMEM((rows, w), BF16)],
        compiler_params=_params("arbitrary"),
    )(q, k, v, g, s0_bd, dmask, qdec, kdec, cdec, attn, x, w_out, w_out, ln_g, ln_b)


def _states_to_block_diag(s):
    b, h, dk, dv = s.shape
    sp = s.reshape(b, h // 2, 2, dk, dv)
    z = jnp.zeros_like(sp[:, :, 0])
    top = jnp.concatenate([sp[:, :, 0], z], axis=-1)
    bot = jnp.concatenate([z, sp[:, :, 1]], axis=-1)
    return jnp.concatenate([top, bot], axis=-2)


def _block_diag_to_states(s):
    b, p = s.shape[:2]
    a = s[:, :, :HEAD_DIM, :HEAD_DIM]
    c = s[:, :, HEAD_DIM:, HEAD_DIM:]
    return jnp.stack([a, c], axis=2).reshape(b, 2 * p, HEAD_DIM, HEAD_DIM)


def _s5_scan_kernel(x_ref, s0re_ref, s0im_ref, wbre_ref, wbim_ref, are_ref, aim_ref, wcre_ref, wcim_ref, d_ref,
                    y_ref, sre_ref, sim_ref, xs_ref, bre_ref, bim_ref, stre_ref, stim_ref, xtail_ref, *, pack):
    step = pl.program_id(0)
    nb, tt, d = x_ref.shape
    b8 = xs_ref.shape[1] // tt
    ns = stre_ref.shape[1]
    n_kb = d // LANES
    per_kb = ns // n_kb
    state_rows = slice((pack - 1) * nb, pack * nb)

    @pl.when(step == 0)
    def _():
        stre_ref[...] = jnp.zeros_like(stre_ref)
        stim_ref[...] = jnp.zeros_like(stim_ref)
        for r in range(pack):
            stre_ref[r * nb:(r + 1) * nb, :] = s0re_ref[...]
            stim_ref[r * nb:(r + 1) * nb, :] = s0im_ref[...]
        xs_ref[...] = jnp.zeros_like(xs_ref)
        xtail_ref[...] = jnp.zeros_like(xtail_ref)

    for b in range(nb):
        for kb in range(n_kb):
            xs_ref[kb, pl.ds(b, tt, stride=b8), :] = x_ref[b, :, kb * LANES:(kb + 1) * LANES]

    n_out = wcre_ref.shape[0]
    kin = ns // n_out
    wout = d // n_out
    kb_per_chunk = kin // per_kb

    first_half = lax.broadcasted_iota(jnp.int32, (SUBLANES, 1), 0) < nb

    def project_in(c):
        for kb in range(c * kb_per_chunk, (c + 1) * kb_per_chunk):
            cols = slice(kb * per_kb, (kb + 1) * per_kb)
            x_t = xs_ref[kb]
            if pack == 1:
                lhs = x_t.astype(BF16)
                w_re, w_im = wbre_ref[kb, 0:LANES, :], wbim_ref[kb, 0:LANES, :]
            else:
                x_prev = pltpu.roll(x_t, nb, 0)
                head = jnp.where(first_half, pltpu.roll(xtail_ref[kb], nb, 0), x_prev[0:SUBLANES])
                x_prev = jnp.concatenate([head, x_prev[SUBLANES:]], axis=0)
                xtail_ref[kb] = x_t[x_t.shape[0] - SUBLANES:]
                lhs = jnp.concatenate([x_t, x_prev], axis=1).astype(BF16)
                w_re, w_im = wbre_ref[kb], wbim_ref[kb]
            bre_ref[:, cols] = _dot(lhs, w_re)
            bim_ref[:, cols] = _dot(lhs, w_im)

    def scan_packed(cs):
        width = cs.stop - cs.start
        are = jnp.broadcast_to(are_ref[:, cs], (SUBLANES, width))
        aim = jnp.broadcast_to(aim_ref[:, cs], (SUBLANES, width))
        a2re, a2im = are * are - aim * aim, 2.0 * are * aim
        sre, sim = stre_ref[:, cs], stim_ref[:, cs]
        for v in range(tt // pack):
            rows = slice(v * SUBLANES, (v + 1) * SUBLANES)
            if v == 0:
                start = jnp.logical_and(first_half, step == 0)
                cre, cim = jnp.where(start, are, a2re), jnp.where(start, aim, a2im)
            else:
                cre, cim = a2re, a2im
            sre, sim = (bre_ref[rows, cs] + (cre * sre - cim * sim), bim_ref[rows, cs] + (cre * sim + cim * sre))
            bre_ref[rows, cs] = sre
            bim_ref[rows, cs] = sim
        stre_ref[:, cs] = sre
        stim_ref[:, cs] = sim

    def scan(c):
        if pack == 2:
            half = kin // 2
            for c0 in range(c * kin, (c + 1) * kin, half):
                scan_packed(slice(c0, c0 + half))
            return
        cs = slice(c * kin, (c + 1) * kin)
        are = jnp.broadcast_to(are_ref[:, cs], (SUBLANES, kin))
        aim = jnp.broadcast_to(aim_ref[:, cs], (SUBLANES, kin))
        for r0 in range(0, b8, SUBLANES):
            sre = stre_ref[r0:r0 + SUBLANES, cs]
            sim = stim_ref[r0:r0 + SUBLANES, cs]
            for t in range(tt):
                rows = slice(t * b8 + r0, t * b8 + r0 + SUBLANES)
                sre, sim = (are * sre - aim * sim + bre_ref[rows, cs], are * sim + aim * sre + bim_ref[rows, cs])
                bre_ref[rows, cs] = sre
                bim_ref[rows, cs] = sim
            stre_ref[r0:r0 + SUBLANES, cs] = sre
            stim_ref[r0:r0 + SUBLANES, cs] = sim

    def project_out(c):
        cs = slice(c * kin, (c + 1) * kin)
        y = _dot(bre_ref[:, cs].astype(BF16), wcre_ref[c]) + _dot(bim_ref[:, cs].astype(BF16), wcim_ref[c])
        for i in range(wout // LANES):
            kb = c * (wout // LANES) + i
            xs_ref[kb] = y[:, i * LANES:(i + 1) * LANES] + xs_ref[kb] * d_ref[:, kb * LANES:(kb + 1) * LANES]

    project_in(0)
    for c in range(n_out):
        if c + 1 < n_out:
            project_in(c + 1)
        scan(c)
        if c >= 1:
            project_out(c - 1)
    project_out(n_out - 1)

    for b in range(nb):
        for kb in range(n_kb):
            y_ref[b, :, kb * LANES:(kb + 1) * LANES] = xs_ref[kb, pl.ds(b, tt, stride=b8), :]

    @pl.when(step == pl.num_programs(0) - 1)
    def _():
        sre_ref[...] = stre_ref[state_rows, :]
        sim_ref[...] = stim_ref[state_rows, :]


def _s5_weights(lam_re, lam_im, b_re, b_im, c_re, c_im, log_dt):
    g, p, h = b_re.shape
    lam = lax.complex(lam_re.astype(F32), lam_im.astype(F32))
    dt = jnp.exp(log_dt.astype(F32))[:, None]
    lam_bar = jnp.exp(lam * dt)
    b_bar = ((lam_bar - 1.0) / lam)[:, :, None] * lax.complex(b_re.astype(F32), b_im.astype(F32))
    gpk = LANES // h
    n_kb = g // gpk
    eye = jnp.eye(gpk, dtype=F32)

    def in_blocks(m):
        m = m.reshape(n_kb, gpk, p, h)
        return jnp.einsum("kgph,gf->kghfp", m, eye).reshape(n_kb, gpk * h, gpk * p).astype(BF16)

    gpo = 2 * LANES // h
    n_out = g // gpo
    eye_o = jnp.eye(gpo, dtype=F32)

    def out_blocks(m):
        m = m.reshape(n_out, gpo, h, p)
        return jnp.einsum("kghp,gf->kgpfh", m, eye_o).reshape(n_out, gpo * p, gpo * h).astype(BF16)

    ab_bar = lam_bar[:, :, None] * b_bar
    two_step = lambda part: jnp.concatenate([in_blocks(part(b_bar)), in_blocks(part(ab_bar))], axis=1)
    return (two_step(jnp.real), two_step(jnp.imag),
            jnp.real(lam_bar).reshape(1, g * p), jnp.imag(lam_bar).reshape(1, g * p),
            out_blocks(c_re.astype(F32)), out_blocks(-c_im.astype(F32)))


def s5_scan(x, s0_re, s0_im, weights, d_skip, *, tt):
    nb, seq, d = x.shape
    wbre, wbim, are, aim, wcre, wcim = weights
    ns = are.shape[1]
    pack = 2 if (2 * nb == SUBLANES and tt % 2 == 0) else 1
    b8 = nb if pack == 2 else -(-nb // SUBLANES) * SUBLANES
    rows = tt * b8
    state_rows = SUBLANES if pack == 2 else b8
    const = lambda a: pl.BlockSpec(a.shape, lambda i: (0,) * a.ndim)
    xblk = pl.BlockSpec((nb, tt, d), lambda i: (0, i, 0))
    sblk = pl.BlockSpec((nb, ns), lambda i: (0, 0))
    return pl.pallas_call(
        functools.partial(_s5_scan_kernel, pack=pack),
        grid=(seq // tt,),
        in_specs=[xblk, sblk, sblk, const(wbre), const(wbim), const(are), const(aim), const(wcre), const(wcim),
                  pl.BlockSpec((1, d), lambda i: (0, 0))],
        out_specs=[xblk, sblk, sblk],
        out_shape=[jax.ShapeDtypeStruct((nb, seq, d), F32), jax.ShapeDtypeStruct((nb, ns), F32),
                   jax.ShapeDtypeStruct((nb, ns), F32)],
        scratch_shapes=[pltpu.VMEM((d // LANES, rows, LANES), F32), pltpu.VMEM((rows, ns), F32),
                        pltpu.VMEM((rows, ns), F32),
                        pltpu.VMEM((state_rows, ns), F32), pltpu.VMEM((state_rows, ns), F32),
                        pltpu.VMEM((d // LANES, SUBLANES, LANES), F32)],
        compiler_params=_params("arbitrary"),
    )(x, s0_re, s0_im, wbre, wbim, are, aim, wcre, wcim, d_skip)


def _s5_out_ln_kernel(y_ref, wo_ref, wg_ref, x_ref, g_ref, b_ref, o_ref, *, alpha):
    gl = jax.nn.gelu(y_ref[...]).astype(BF16)
    mix = _dot(gl, wo_ref[...]) * jax.nn.sigmoid(_dot(gl, wg_ref[...]))
    o_ref[...] = _layer_norm(alpha * x_ref[...] + mix, g_ref[...], b_ref[...])


def s5_out_ln(y, w_out, w_gate, x, g, b, *, layer, alpha, tm):
    t, d = x.shape
    row = pl.BlockSpec((tm, d), lambda i: (i, 0))
    const = lambda r, c: pl.BlockSpec((r, c), lambda i: (0, 0))
    return pl.pallas_call(
        functools.partial(_s5_out_ln_kernel, alpha=alpha),
        grid=(t // tm,),
        in_specs=[row, _layer_weight(layer, d, d), _layer_weight(layer, d, d), row, const(1, d), const(1, d)],
        out_specs=row,
        out_shape=jax.ShapeDtypeStruct((t, d), F32),
        compiler_params=_params("parallel"),
    )(y, w_out, w_gate, x, g, b)


class _TilePlan(NamedTuple):
    token_rows: int
    proj_rows: int
    s5_steps: int
    ret_tiles: int
    ret_seqs_per_tile: int


def _tile_plan(lp, ls):
    return _TilePlan(token_rows=min(1024, lp), proj_rows=min(512, lp), s5_steps=min(128, lp),
                     ret_tiles=min(4, lp // RET_CHUNK), ret_seqs_per_tile=max(RET_CHUNK // ls, 1))


def _rope_tables(pos, n_heads):
    half = HEAD_DIM // 2
    inv = ROPE_THETA ** (-jnp.arange(half, dtype=F32) / half)
    ang = pos.astype(F32)[:, None] * inv[None, :]
    cos = jnp.cos(ang)
    sin = jnp.sin(ang)
    cos_h = jnp.concatenate([cos, cos], axis=-1)
    sin_h = jnp.concatenate([-sin, sin], axis=-1)
    return jnp.tile(cos_h, (1, n_heads)), jnp.tile(sin_h, (1, n_heads))


def kernel(x_prompt, x_sample, cache_k, cache_v, page_table, state_ret, state_s5_re, state_s5_im, ffn1_w_gate, ffn1_w_up, ffn1_w_down, ffn2_w_gate, ffn2_w_up, ffn2_w_down, ln_g, ln_b, w_in_ab, w_out_ab, s5_lam_re, s5_lam_im, s5_b_re, s5_b_im, s5_c_re, s5_c_im, s5_d, s5_log_dt, s5_w_out, s5_w_gate):
    bp, lp, d = x_prompt.shape
    bs, ls, _ = x_sample.shape
    depth = ffn1_w_gate.shape[0]
    n_layers_ab, n_pool, page, a_heads, hd = cache_k.shape
    n_pages = page_table.shape[1]
    past_len = n_pages * page
    half = a_heads * hd
    n_heads = half // HEAD_DIM
    assert hd == HEAD_DIM and lp % MOBA_BLOCK == 0 and past_len % MOBA_BLOCK == 0 and ls <= MOBA_BLOCK
    assert MOBA_BLOCK % page == 0 and lp % page == 0 and lp % RET_CHUNK == 0
    alpha = (2 * depth) ** 0.25
    tp, ts = bp * lp, bs * ls
    plan = _tile_plan(lp, ls)
    tm_p, tm_proj = plan.token_rows, plan.proj_rows

    xp = x_prompt.reshape(tp, d)
    xs = x_sample.reshape(ts, d)
    bf = lambda a: a.astype(BF16)
    f1 = (bf(ffn1_w_gate), bf(ffn1_w_up), bf(ffn1_w_down))
    f2 = (bf(ffn2_w_gate), bf(ffn2_w_up), bf(ffn2_w_down))
    w_in, w_out = bf(w_in_ab), bf(w_out_ab)
    w_o, w_g = bf(s5_w_out), bf(s5_w_gate)
    cos_p, sin_p = _rope_tables(jnp.arange(lp, dtype=jnp.int32), n_heads)
    cos_s, sin_s = _rope_tables(past_len + jnp.arange(ls, dtype=jnp.int32), n_heads)
    cos_s, sin_s = jnp.tile(cos_s, (bs, 1)), jnp.tile(sin_s, (bs, 1))
    pool_kt = jnp.transpose(cache_k, (0, 1, 3, 4, 2)).reshape(n_layers_ab * n_pool, half, page)
    pool_vt = jnp.transpose(cache_v, (0, 1, 3, 4, 2)).reshape(n_layers_ab * n_pool, half, page)
    seq_per_tile, ret_tiles_p = plan.ret_seqs_per_tile, plan.ret_tiles

    cache_pages = None
    k_s, v_s, r_p, r_s = [], [], [], []
    sre_p, sim_p, sre_s, sim_s = [], [], [], []
    for layer in range(depth):
        li = layer // 2
        g = lambda i: ln_g[layer, i][None, :]
        b = lambda i: ln_b[layer, i][None, :]
        xp, xs = ffn_ln(xp, xs, *f1, g(0), b(0), layer=layer, alpha=alpha, tm=tm_p)
        if layer % 2 == 0:
            qa, kbf, vt, ksum, kpg, vpg, qb, kb, vb, gb = ab_project(xp, w_in, cos_p, sin_p, layer=li, tm=tm_proj,
                                                                     seq_len=lp, page=page, pages=cache_pages)
            cache_pages = (kpg, vpg)
            attn = moba_prompt(qa, ksum, kbf, vt, batch=bp, seq_len=lp)
            zero_state = jnp.zeros((bp, n_heads // 2, HEAD_PAIR, HEAD_PAIR), F32)
            xp, s_fin = retention_merge_ln(qb, kb, vb, gb, zero_state, attn, xp, w_out, g(1), b(1), layer=li,
                                           alpha=alpha, n_seq=1, chunk=RET_CHUNK, tiles=ret_tiles_p,
                                           steps_per_seq=lp // (RET_CHUNK * ret_tiles_p))
            r_p.append(_block_diag_to_states(s_fin))
            qa, ka, va, qb, kb, vb, gb = ab_project(xs, w_in, cos_s, sin_s, layer=li, tm=ts, seq_len=ts)
            page_ids = (page_table.astype(jnp.int32) + li * n_pool).reshape(-1)
            attn = moba_sample(qa, ka, va, pool_kt, pool_vt, page_ids, batch=bs, n_pages=n_pages)
            xs, s_fin = retention_merge_ln(qb, kb, vb, gb, _states_to_block_diag(state_ret[li]), attn, xs, w_out,
                                           g(1), b(1), layer=li, alpha=alpha, n_seq=seq_per_tile, chunk=ls, tiles=1,
                                           steps_per_seq=1)
            k_s.append(ka.reshape(bs, ls, a_heads, hd))
            v_s.append(va.reshape(bs, ls, a_heads, hd))
            r_s.append(_block_diag_to_states(s_fin))
        else:
            weights = _s5_weights(s5_lam_re[li], s5_lam_im[li], s5_b_re[li], s5_b_im[li], s5_c_re[li],
                                  s5_c_im[li], s5_log_dt[li])
            n_state = weights[2].shape[1]
            d_skip = s5_d[li][None, :]
            zero = jnp.zeros((bp, n_state), F32)
            y, a_re, a_im = s5_scan(xp.reshape(bp, lp, d), zero, zero, weights, d_skip, tt=plan.s5_steps)
            xp = s5_out_ln(y.reshape(tp, d), w_o, w_g, xp, g(1), b(1), layer=li, alpha=alpha, tm=tm_p)
            sre_p.append(a_re.reshape(bp, -1, S5_STATE))
            sim_p.append(a_im.reshape(bp, -1, S5_STATE))
            y, a_re, a_im = s5_scan(xs.reshape(bs, ls, d), state_s5_re[li].reshape(bs, n_state),
                                    state_s5_im[li].reshape(bs, n_state), weights, d_skip, tt=ls)
            xs = s5_out_ln(y.reshape(ts, d), w_o, w_g, xs, g(1), b(1), layer=li, alpha=alpha, tm=ts)
            sre_s.append(a_re.reshape(bs, -1, S5_STATE))
            sim_s.append(a_im.reshape(bs, -1, S5_STATE))
        xp, xs = ffn_ln(xp, xs, *f2, g(2), b(2), layer=layer, alpha=alpha, tm=tm_p)
    unpage = lambda t: jnp.transpose(t.reshape(n_layers_ab, bp, lp // page, a_heads, hd, page), (0, 1, 2, 5, 3, 4))
    return (xp.reshape(bp, lp, d), xs.reshape(bs, ls, d), unpage(kpg), unpage(vpg), jnp.stack(k_s),
            jnp.stack(v_s), jnp.stack(r_p), jnp.stack(r_s), jnp.stack(sre_p), jnp.stack(sim_p),
            jnp.stack(sre_s), jnp.stack(sim_s))
```

```python
import functools
import math
from typing import NamedTuple

import jax
import jax.numpy as jnp
from jax import lax
from jax.experimental import pallas as pl
from jax.experimental.pallas import tpu as pltpu

F32 = jnp.float32
BF16 = jnp.bfloat16

HEAD_DIM = 64
HEAD_PAIR = 2 * HEAD_DIM
MOBA_BLOCK = 256
MOBA_TOPK = 3
MOBA_BLOCKS_PER_TRIP = 4
RET_CHUNK = 128
S5_STATE = 64
ROPE_THETA = 10000.0
LN_EPS = 1e-5
GN_EPS = 1e-6
NEG_INF = -1e30
LOG2_E = math.log2(math.e)
SUBLANES = 8
BF16_SUBLANES = 16
LANES = 128
VMEM_LIMIT = 48 * 1024 * 1024


def _dot(a, b, precision=None):
    return jnp.dot(a, b, preferred_element_type=F32, precision=precision)


def _dot_nt(a, b, precision=None):
    return lax.dot_general(a, b, (((1,), (1,)), ((), ())), preferred_element_type=F32, precision=precision)


def _dot_tn(a, b):
    return lax.dot_general(a, b, (((0,), (0,)), ((), ())), preferred_element_type=F32)


def _layer_norm(r, g, b):
    mu = jnp.mean(r, -1, keepdims=True)
    d = r - mu
    var = jnp.mean(d * d, -1, keepdims=True)
    return d * lax.rsqrt(var + LN_EPS) * g + b


def _silu(x):
    return x * jax.nn.sigmoid(x)


def _params(*sem):
    return pltpu.CompilerParams(dimension_semantics=sem, vmem_limit_bytes=VMEM_LIMIT)


def _ffn_ln_kernel(xp_ref, xs_ref, wg_ref, wu_ref, wd_ref, g_ref, b_ref, op_ref, os_ref, a_ref, *, alpha, tf):
    def run(x_ref, o_ref):
        rows = x_ref.shape[0]
        xb = x_ref[...].astype(BF16)
        for c in range(wg_ref.shape[1] // tf):
            cols = slice(c * tf, (c + 1) * tf)
            hg = _dot(xb, wg_ref[:, cols])
            hu = _dot(xb, wu_ref[:, cols])
            a_ref[0:rows, cols] = (_silu(hg) * hu).astype(BF16)
        r = alpha * x_ref[...] + 0.5 * _dot(a_ref[0:rows, :], wd_ref[...])
        o_ref[...] = _layer_norm(r, g_ref[...], b_ref[...])

    last = pl.num_programs(0) - 1
    pl.when(pl.program_id(0) < last)(lambda: run(xp_ref, op_ref))
    pl.when(pl.program_id(0) == last)(lambda: run(xs_ref, os_ref))


def _layer_weight(layer, r, c, row_blk=0, **kw):
    return pl.BlockSpec((None, r, c), lambda *_: (layer, row_blk, 0), **kw)


def ffn_ln(xp, xs, wg, wu, wd, g, b, *, layer, alpha, tm, tf=256):
    tp, d = xp.shape
    ts = xs.shape[0]
    f = wg.shape[2]
    n_p = tp // tm
    assert ts <= tm
    resident = lambda r, c: _layer_weight(layer, r, c, pipeline_mode=pl.Buffered(1))
    p_rows = pl.BlockSpec((tm, d), lambda i: (jnp.minimum(i, n_p - 1), 0))
    s_rows = pl.BlockSpec((ts, d), lambda i: (0, 0))
    return pl.pallas_call(
        functools.partial(_ffn_ln_kernel, alpha=alpha, tf=tf),
        grid=(n_p + 1,),
        in_specs=[p_rows, s_rows, resident(d, f), resident(d, f), resident(f, d),
                  pl.BlockSpec((1, d), lambda i: (0, 0)), pl.BlockSpec((1, d), lambda i: (0, 0))],
        out_specs=[p_rows, s_rows],
        out_shape=[jax.ShapeDtypeStruct((tp, d), F32), jax.ShapeDtypeStruct((ts, d), F32)],
        scratch_shapes=[pltpu.VMEM((tm, f), BF16)],
        compiler_params=_params("arbitrary"),
    )(xp, xs, wg, wu, wd, g, b)


def _rope(y, cos, sin_signed):
    width = y.shape[-1]
    lane = lax.broadcasted_iota(jnp.int32, (1, width), 1)
    first = (lane % HEAD_DIM) < (HEAD_DIM // 2)
    rot = jnp.where(first, pltpu.roll(y, width - HEAD_DIM // 2, 1), pltpu.roll(y, HEAD_DIM // 2, 1))
    return y * cos + rot * sin_signed


def _ab_proj_kernel(*refs, half, scale, page, layer, new_slots):
    if page is None:
        x_ref, w_ref, cos_ref, sin_ref, qa_ref, ka_ref, va_ref, qb_ref, kb_ref, vb_ref, gb_ref = refs
    else:
        x_ref, w_ref, cos_ref, sin_ref = refs[:4]
        qa_ref, kbf_ref, vt_ref, ksum_ref, kpg_ref, vpg_ref, qb_ref, kb_ref, vb_ref, gb_ref = refs[-10:]
        if new_slots is not None:
            for slot in range(new_slots):
                if slot != layer:
                    kpg_ref[slot] = jnp.zeros(kpg_ref.shape[1:], F32)
                    vpg_ref[slot] = jnp.zeros(vpg_ref.shape[1:], F32)
            kpg_ref, vpg_ref = kpg_ref.at[layer], vpg_ref.at[layer]
    xb = x_ref[...].astype(BF16)
    cos = cos_ref[...]
    sin = sin_ref[...]
    col = lambda c: _dot(xb, w_ref[:, c * half:(c + 1) * half])
    qa_ref[...] = _rope(col(0), cos, sin) * scale
    ka = _rope(col(1), cos, sin)
    va = col(2)
    if page is None:
        ka_ref[...] = ka
        va_ref[...] = va
    else:
        kbf_ref[...] = ka.astype(BF16)
        for n in range(ka.shape[0] // MOBA_BLOCK):
            blk = slice(n * MOBA_BLOCK, (n + 1) * MOBA_BLOCK)
            ksum_ref[n] = jnp.sum(ka[blk], axis=0, keepdims=True)
            vt_ref[n] = va[blk].T.astype(BF16)
        for n in range(ka.shape[0] // page):
            rows = slice(n * page, (n + 1) * page)
            kpg_ref[n] = ka[rows].T
            vpg_ref[n] = va[rows].T
    qb_ref[...] = _rope(col(3), cos, sin)
    kb_ref[...] = _rope(col(4), cos, sin) * scale
    vb_ref[...] = col(5)
    gb_ref[...] = col(6)


def ab_project(x, w_in, cos, sin, *, layer, tm, seq_len, page=None, pages=None):
    t, d = x.shape
    n_layers = w_in.shape[0]
    half = w_in.shape[2] // 7
    tiles_per_seq = max(seq_len // tm, 1)
    tab = pl.BlockSpec((tm, half), lambda i: (i % tiles_per_seq, 0))
    row = pl.BlockSpec((tm, half), lambda i: (i, 0))
    f32o = jax.ShapeDtypeStruct((t, half), F32)
    in_specs = [pl.BlockSpec((tm, d), lambda i: (i, 0)), _layer_weight(layer, d, 7 * half), tab, tab]
    operands = [x, w_in, cos, sin]
    aliases = {}
    if page is None:
        out_specs = [row] * 7
        out_shape = [f32o] * 7
    else:
        nblk = tm // MOBA_BLOCK
        pages_shape = jax.ShapeDtypeStruct((n_layers, t // page, half, page), F32)
        if pages is None:
            page_blk = pl.BlockSpec((n_layers, tm // page, half, page), lambda i: (0, i, 0, 0))
        else:
            page_blk = pl.BlockSpec((None, tm // page, half, page), lambda i: (layer, i, 0, 0))
            in_specs += [pl.BlockSpec(memory_space=pl.ANY)] * 2
            operands += list(pages)
            aliases = {4: 4, 5: 5}
        out_specs = [row, row,
                     pl.BlockSpec((nblk, half, MOBA_BLOCK), lambda i: (i, 0, 0)),
                     pl.BlockSpec((nblk, 1, half), lambda i: (i, 0, 0)),
                     page_blk, page_blk, row, row, row, row]
        out_shape = [f32o, jax.ShapeDtypeStruct((t, half), BF16),
                     jax.ShapeDtypeStruct((t // MOBA_BLOCK, half, MOBA_BLOCK), BF16),
                     jax.ShapeDtypeStruct((t // MOBA_BLOCK, 1, half), F32),
                     pages_shape, pages_shape, f32o, f32o, f32o, f32o]
    return pl.pallas_call(
        functools.partial(_ab_proj_kernel, half=half, scale=HEAD_DIM ** -0.5, page=page, layer=layer,
                          new_slots=n_layers if (page is not None and pages is None) else None),
        grid=(t // tm,),
        in_specs=in_specs,
        out_specs=out_specs,
        out_shape=out_shape,
        input_output_aliases=aliases,
        compiler_params=_params("parallel"),
    )(*operands)


def _topk_select(gate, blk, n_valid, axis):
    gate = jnp.where(blk < n_valid, gate, NEG_INF)
    taken = jnp.zeros(gate.shape, F32)
    blk_f = blk.astype(F32)
    for _ in range(MOBA_TOPK):
        top = jnp.max(gate, axis=axis, keepdims=True)
        first = jnp.min(jnp.where(gate == top, blk_f, float(gate.shape[axis])), axis=axis, keepdims=True)
        hit = blk_f == first
        taken = jnp.where(hit, 1.0, taken)
        gate = jnp.where(hit, -jnp.inf, gate)
    return jnp.where(blk < n_valid, taken, 0.0)


def _moba_prompt_kernel(q_ref, ksum_ref, k_ref, vt_ref, o_ref, qm_ref, sel_ref, m_ref, l_ref, acc_ref, s_ref):
    j = pl.program_id(1)
    tq = q_ref.shape[0]
    n_heads = q_ref.shape[1] // HEAD_DIM
    lane = lax.broadcasted_iota(jnp.int32, (1, HEAD_PAIR), 1)
    key_i = lax.broadcasted_iota(jnp.int32, (MOBA_BLOCK, tq), 0)
    qry_i = lax.broadcasted_iota(jnp.int32, (MOBA_BLOCK, tq), 1)
    causal = key_i <= qry_i

    nb = ksum_ref.shape[0]
    blk = lax.broadcasted_iota(jnp.int32, (nb, 1), 0)
    for hp in range(n_heads // 2):
        lanes = slice(hp * HEAD_PAIR, (hp + 1) * HEAD_PAIR)
        qp = q_ref[:, lanes]
        kmean = ksum_ref[:, 0, lanes] * (1.0 / MOBA_BLOCK)
        masks = [(lane // HEAD_DIM) == half for half in range(2)]
        gate = _dot_nt(jnp.concatenate([jnp.where(hm, kmean, 0.0) for hm in masks], axis=0), qp,
                       precision=lax.Precision.HIGHEST)
        for half, hm in enumerate(masks):
            h = 2 * hp + half
            sel_ref[h] = _topk_select(gate[half * nb:(half + 1) * nb], blk, j, 0)
            qm_ref[h] = jnp.where(hm, qp * LOG2_E, 0.0).astype(BF16)

    ones_rows = jnp.ones((BF16_SUBLANES, MOBA_BLOCK), BF16)

    def update(blocks, first):
        def rows0(n):
            return pl.multiple_of((j if n is None else n) * MOBA_BLOCK, MOBA_BLOCK)

        def picked(n, h):
            return sel_ref[h, pl.ds(n, 1), :] > 0.0

        def scores(slot, n, h):
            k_pair = k_ref[pl.ds(rows0(n), MOBA_BLOCK), (h // 2) * HEAD_PAIR:(h // 2 + 1) * HEAD_PAIR]
            s_ref[slot, h] = _dot_nt(k_pair, qm_ref[h])

        def softmax_values(slot, n, h):
            if first:
                s = jnp.where(causal, s_ref[slot, h], NEG_INF)
                m_new = jnp.max(s, axis=0, keepdims=True)
            else:
                s = s_ref[slot, h]
                cmax = jnp.where(picked(n, h), jnp.max(s, axis=0, keepdims=True), NEG_INF)
                m_new = jnp.maximum(m_ref[h], cmax)
                a = jnp.exp2(m_ref[h] - m_new)
            p = jnp.exp2(s - m_new).astype(BF16)
            m_ref[h] = m_new
            v_h = vt_ref[j if n is None else n, h * HEAD_DIM:(h + 1) * HEAD_DIM, :]
            res = _dot(jnp.concatenate([v_h, ones_rows], axis=0), p)
            pv, psum = res[:HEAD_DIM], res[HEAD_DIM:HEAD_DIM + 1]
            if first:
                acc_ref[h] = pv
                l_ref[h] = psum
            else:
                acc_ref[h] = a * acc_ref[h] + jnp.where(picked(n, h), pv, 0.0)
                l_ref[h] = a * l_ref[h] + jnp.where(picked(n, h), psum, 0.0)

        heads = range(n_heads)
        stages = (scores, softmax_values)
        for t in range(len(stages) + len(blocks) - 1):
            for h in heads:
                for slot, n in enumerate(blocks):
                    if 0 <= t - slot < len(stages):
                        stages[t - slot](slot, n, h)

    update([None], True)

    n_slots = s_ref.shape[0]

    def body(i, carry):
        update([n_slots * i + k for k in range(n_slots)], False)
        return carry

    lax.fori_loop(0, j // n_slots, body, 0)
    for rem in range(1, n_slots):
        @pl.when(j % n_slots == rem)
        def _(rem=rem):
            update([j - rem + k for k in range(rem)], False)

    out_t = jnp.concatenate([acc_ref[h] / l_ref[h] for h in range(n_heads)], axis=0)
    o_ref[...] = out_t.T.astype(o_ref.dtype)


def moba_prompt(q, ksum, kbf, vt, *, batch, seq_len):
    t, w = q.shape
    nb = seq_len // MOBA_BLOCK
    n_heads = w // HEAD_DIM
    return pl.pallas_call(
        _moba_prompt_kernel,
        grid=(batch, nb),
        in_specs=[pl.BlockSpec((MOBA_BLOCK, w), lambda b, j: (b * nb + j, 0)),
                  pl.BlockSpec((nb, 1, w), lambda b, j: (b, 0, 0)),
                  pl.BlockSpec((seq_len, w), lambda b, j: (b, 0)),
                  pl.BlockSpec((nb, w, MOBA_BLOCK), lambda b, j: (b, 0, 0))],
        out_specs=pl.BlockSpec((MOBA_BLOCK, w), lambda b, j: (b * nb + j, 0)),
        out_shape=jax.ShapeDtypeStruct((t, w), BF16),
        scratch_shapes=[pltpu.VMEM((n_heads, MOBA_BLOCK, HEAD_PAIR), BF16),
                        pltpu.VMEM((n_heads, nb, MOBA_BLOCK), F32),
                        pltpu.VMEM((n_heads, 1, MOBA_BLOCK), F32),
                        pltpu.VMEM((n_heads, 1, MOBA_BLOCK), F32),
                        pltpu.VMEM((n_heads, HEAD_DIM, MOBA_BLOCK), F32),
                        pltpu.VMEM((MOBA_BLOCKS_PER_TRIP, n_heads, MOBA_BLOCK, MOBA_BLOCK), F32)],
        compiler_params=_params("parallel", "arbitrary"),
    )(q, ksum, kbf, vt)


def _moba_sample_kernel(pt_ref, q_ref, knew_ref, vnew_ref, *rest, pages_per_step, page, n_heads):
    del pt_ref
    kpages = rest[:pages_per_step]
    vpages = rest[pages_per_step:2 * pages_per_step]
    o_ref = rest[2 * pages_per_step]
    qf_ref, qb_ref, ksum_ref, m_ref, l_ref, acc_ref, kpad_ref, vpad_ref = rest[2 * pages_per_step + 1:]
    s_idx = pl.program_id(1)
    n_steps = pl.num_programs(1)
    lq, w = q_ref.shape
    rows = n_heads * lq
    nb = acc_ref.shape[0]
    pages_per_blk = MOBA_BLOCK // page
    blks_per_step = pages_per_step // pages_per_blk
    rowhead = lax.broadcasted_iota(jnp.int32, (rows, 1), 0) // lq
    lanehead = lax.broadcasted_iota(jnp.int32, (1, w), 1) // HEAD_DIM
    blk_lane = lax.broadcasted_iota(jnp.int32, (1, LANES), 1)

    @pl.when(s_idx == 0)
    def _():
        qt = jnp.concatenate([q_ref[...]] * n_heads, axis=0)
        qbd = jnp.where(rowhead == lanehead, qt, 0.0)
        qf_ref[...] = qbd
        qb_ref[...] = qbd.astype(BF16)
        ksum_ref[...] = jnp.zeros_like(ksum_ref)
        m_ref[...] = jnp.zeros_like(m_ref)
        l_ref[...] = jnp.zeros_like(l_ref)

    qb = qb_ref[...]

    def partial_softmax(s):
        m = jnp.max(s, axis=1, keepdims=True)
        e = jnp.exp(s - m)
        return m, jnp.sum(e, axis=1, keepdims=True), e.astype(BF16)

    blks = range(blks_per_step)
    pages_of = lambda bi: range(bi * pages_per_blk, (bi + 1) * pages_per_blk)
    here = [blk_lane == s_idx * blks_per_step + bi for bi in blks]
    kt = [jnp.concatenate([kpages[i][0] for i in pages_of(bi)], axis=1) for bi in blks]
    scores = [_dot(qb, kt[bi].astype(BF16)) for bi in blks]
    ksum = ksum_ref[...]
    for bi in blks:
        ksum = jnp.where(here[bi], jnp.sum(kt[bi], axis=1, keepdims=True), ksum)
    ksum_ref[...] = ksum
    stats = [partial_softmax(scores[bi]) for bi in blks]
    for bi in blks:
        vt = jnp.concatenate([vpages[i][0] for i in pages_of(bi)], axis=1)
        acc_ref[s_idx * blks_per_step + bi] = _dot_nt(stats[bi][2], vt.astype(BF16))
    m_all, l_all = m_ref[...], l_ref[...]
    for bi in blks:
        m_all = jnp.where(here[bi], stats[bi][0], m_all)
        l_all = jnp.where(here[bi], stats[bi][1], l_all)
    m_ref[...] = m_all
    l_ref[...] = l_all

    @pl.when(s_idx == n_steps - 1)
    def _():
        kpad_ref[...] = jnp.zeros_like(kpad_ref)
        vpad_ref[...] = jnp.zeros_like(vpad_ref)
        kpad_ref[0:lq, :] = knew_ref[...]
        vpad_ref[0:lq, :] = vnew_ref[...]
        s_own = _dot_nt(qb, kpad_ref[...].astype(BF16))
        tq = lax.broadcasted_iota(jnp.int32, s_own.shape, 0) % lq
        tk = lax.broadcasted_iota(jnp.int32, s_own.shape, 1)
        m_own, l_own, e_own = partial_softmax(jnp.where(tk <= tq, s_own, NEG_INF))
        acc_own = _dot(e_own, vpad_ref[...].astype(BF16))

        gate = _dot(qf_ref[...], ksum_ref[...] * (1.0 / MOBA_BLOCK), precision=lax.Precision.HIGHEST)
        sel = _topk_select(gate[:, :nb], blk_lane[:, :nb], nb, 1) > 0.0
        m_blk = m_ref[:, :nb]
        m_all = jnp.maximum(m_own, jnp.max(jnp.where(sel, m_blk, NEG_INF), axis=1, keepdims=True))
        wgt = jnp.where(sel, jnp.exp(m_blk - m_all), 0.0)
        w_own = jnp.exp(m_own - m_all)
        den = w_own * l_own + jnp.sum(wgt * l_ref[:, :nb], axis=1, keepdims=True)
        num = w_own * acc_own
        for n in range(nb):
            num = num + wgt[:, n:n + 1] * acc_ref[n]
        o_all = num / den
        out = jnp.zeros((lq, w), F32)
        for h in range(n_heads):
            out = out + jnp.where(lanehead == h, o_all[h * lq:(h + 1) * lq], 0.0)
        o_ref[...] = out


def moba_sample(q, knew, vnew, pool_kt, pool_vt, page_ids, *, batch, n_pages, pages_per_step=32):
    t, w = q.shape
    lq = t // batch
    page = pool_kt.shape[2]
    n_heads = w // HEAD_DIM
    rows = n_heads * lq
    nb = n_pages * page // MOBA_BLOCK
    assert nb <= LANES and pages_per_step % (MOBA_BLOCK // page) == 0
    seq_blk = pl.BlockSpec((lq, w), lambda b, s, pt: (b, 0))

    def page_spec(i):
        return pl.BlockSpec((1, w, page), lambda b, s, pt: (pt[b * n_pages + s * pages_per_step + i], 0, 0))

    grid_spec = pltpu.PrefetchScalarGridSpec(
        num_scalar_prefetch=1,
        grid=(batch, n_pages // pages_per_step),
        in_specs=[seq_blk, seq_blk, seq_blk] + [page_spec(i) for i in range(pages_per_step)] * 2,
        out_specs=seq_blk,
        scratch_shapes=[pltpu.VMEM((rows, w), F32), pltpu.VMEM((rows, w), BF16), pltpu.VMEM((w, LANES), F32),
                        pltpu.VMEM((rows, LANES), F32), pltpu.VMEM((rows, LANES), F32),
                        pltpu.VMEM((nb, rows, w), F32),
                        pltpu.VMEM((LANES, w), F32), pltpu.VMEM((LANES, w), F32)],
    )
    return pl.pallas_call(
        functools.partial(_moba_sample_kernel, pages_per_step=pages_per_step, page=page, n_heads=n_heads),
        grid_spec=grid_spec,
        out_shape=jax.ShapeDtypeStruct((t, w), F32),
        compiler_params=_params("parallel", "arbitrary"),
    )(page_ids, q, knew, vnew, *([pool_kt] * pages_per_step), *([pool_vt] * pages_per_step))


def _retention_kernel(q_ref, k_ref, v_ref, g_ref, s0_ref, dmask_ref, qdec_ref, kdec_ref, cdec_ref,
                      attn_ref, x_ref, wa_ref, wr_ref, lg_ref, lb_ref,
                      o_ref, sout_ref, st_ref, gated_ref, *, n_seq, chunk, tiles, steps_per_seq, alpha):
    step = pl.program_id(0)
    tile_rows = n_seq * chunk
    n_pairs = q_ref.shape[1] // HEAD_PAIR
    lane = lax.broadcasted_iota(jnp.int32, (1, HEAD_PAIR), 1)
    rowh = lax.broadcasted_iota(jnp.int32, (HEAD_PAIR, 1), 0) // HEAD_DIM
    same_head = rowh == (lane // HEAD_DIM)
    rseq = lax.broadcasted_iota(jnp.int32, (tile_rows, 1), 0) // chunk

    @pl.when(step % steps_per_seq == 0)
    def _():
        st_ref[...] = s0_ref[...]

    pairs = range(n_pairs)
    lanes_of = lambda hp: slice(hp * HEAD_PAIR, (hp + 1) * HEAD_PAIR)
    half_masks = [(lane // HEAD_DIM) == half for half in range(2)]
    for ti in range(tiles):
        rows = slice(ti * tile_rows, (ti + 1) * tile_rows)
        qb = [q_ref[rows, lanes_of(hp)].astype(BF16) for hp in pairs]
        kf = [k_ref[rows, lanes_of(hp)] for hp in pairs]
        kb = [k.astype(BF16) for k in kf]
        vb = [v_ref[rows, lanes_of(hp)].astype(BF16) for hp in pairs]
        inner = [[_dot_nt(jnp.where(hm, qb[hp], jnp.zeros_like(qb[hp])), kb[hp]) for hm in half_masks]
                 for hp in pairs]
        scaled = [[(inner[hp][half] * dmask_ref[2 * hp + half]).astype(BF16) for half in range(2)] for hp in pairs]
        ret = [sum(jnp.where(half_masks[half], _dot(scaled[hp][half], vb[hp]), 0.0) for half in range(2))
               for hp in pairs]
        for hp in pairs:
            kd = (kf[hp] * kdec_ref[:, lanes_of(hp)]).astype(BF16)
            cdec = cdec_ref[:, lanes_of(hp)]
            carried = []
            for s in range(n_seq):
                srows = slice(s * chunk, (s + 1) * chunk)
                state = st_ref[s, hp]
                carried.append(_dot(qb[hp][srows], state.astype(BF16)))
                kd_s = kd if n_seq == 1 else jnp.where(rseq == s, kd, jnp.zeros_like(kd))
                st_ref[s, hp] = cdec * state + jnp.where(same_head, _dot_tn(kd_s, vb[hp]), 0.0)
            carried = carried[0] if n_seq == 1 else jnp.concatenate(carried, axis=0)
            ret[hp] = ret[hp] + carried * qdec_ref[:, lanes_of(hp)]
        for hp in pairs:
            normed = jnp.zeros_like(ret[hp])
            for hm in half_masks:
                mu = jnp.sum(jnp.where(hm, ret[hp], 0.0), axis=1, keepdims=True) * (1.0 / HEAD_DIM)
                d = jnp.where(hm, ret[hp] - mu, 0.0)
                var = jnp.sum(d * d, axis=1, keepdims=True) * (1.0 / HEAD_DIM)
                normed = normed + d * lax.rsqrt(var + GN_EPS)
            gated_ref[rows, lanes_of(hp)] = (_silu(g_ref[rows, lanes_of(hp)]) * normed).astype(BF16)

    mix = _dot(attn_ref[...].astype(BF16), wa_ref[...]) + _dot(gated_ref[...], wr_ref[...])
    o_ref[...] = _layer_norm(alpha * x_ref[...] + mix, lg_ref[...], lb_ref[...])

    @pl.when(step % steps_per_seq == steps_per_seq - 1)
    def _():
        sout_ref[...] = st_ref[...]


def _retention_tables(n_heads, n_seq, chunk):
    log_g = jnp.log1p(-jnp.exp2(-5.0 - jnp.arange(n_heads, dtype=F32)))
    idx = jnp.arange(chunk, dtype=F32)
    diff = idx[:, None] - idx[None, :]
    dmask = jnp.where(diff >= 0, jnp.exp(log_g[:, None, None] * jnp.maximum(diff, 0.0)), 0.0)
    q_dec = jnp.exp(log_g[None, :] * (idx[:, None] + 1.0))
    k_dec = jnp.exp(log_g[None, :] * (chunk - 1.0 - idx[:, None]))
    c_dec = jnp.exp(log_g * chunk)
    seq_eye = jnp.eye(n_seq, dtype=F32)
    dmask = jnp.einsum("ab,hij->haibj", seq_eye, dmask).reshape(n_heads, n_seq * chunk, n_seq * chunk)
    lanes = lambda t: jnp.repeat(t, HEAD_DIM, axis=-1)
    return dmask, jnp.tile(lanes(q_dec), (n_seq, 1)), jnp.tile(lanes(k_dec), (n_seq, 1)), lanes(c_dec[None, :])


def retention_merge_ln(q, k, v, g, s0_bd, attn, x, w_out, ln_g, ln_b, *, layer, alpha, n_seq, chunk, tiles,
                       steps_per_seq):
    t, w = q.shape
    d = x.shape[1]
    n_pairs = w // HEAD_PAIR
    rows = n_seq * chunk * tiles
    n_steps = t // rows
    dmask, qdec, kdec, cdec = _retention_tables(w // HEAD_DIM, n_seq, chunk)
    tile_rows = n_seq * chunk
    row = pl.BlockSpec((rows, w), lambda i: (i, 0))
    xrow = pl.BlockSpec((rows, d), lambda i: (i, 0))
    st = pl.BlockSpec((n_seq, n_pairs, HEAD_PAIR, HEAD_PAIR), lambda i: (i // steps_per_seq, 0, 0, 0))
    const = lambda shape: pl.BlockSpec(shape, lambda i: (0,) * len(shape))
    return pl.pallas_call(
        functools.partial(_retention_kernel, n_seq=n_seq, chunk=chunk, tiles=tiles, steps_per_seq=steps_per_seq,
                          alpha=alpha),
        grid=(n_steps,),
        in_specs=[row, row, row, row, st, const(dmask.shape), const((tile_rows, w)), const((tile_rows, w)),
                  const((1, w)), row, xrow, _layer_weight(layer, w, d, 0), _layer_weight(layer, w, d, 1),
                  const((1, d)), const((1, d))],
        out_specs=[xrow, st],
        out_shape=[jax.ShapeDtypeStruct((t, d), F32), jax.ShapeDtypeStruct(s0_bd.shape, F32)],
        scratch_shapes=[pltpu.VMEM((n_seq, n_pairs, HEAD_PAIR, HEAD_PAIR), F32), pltpu.VMEM((rows, w), BF16)],
        compiler_params=_params("arbitrary"),
    )(q, k, v, g, s0_bd, dmask, qdec, kdec, cdec, attn, x, w_out, w_out, ln_g, ln_b)


def _states_to_block_diag(s):
    b, h, dk, dv = s.shape
    sp = s.reshape(b, h // 2, 2, dk, dv)
    z = jnp.zeros_like(sp[:, :, 0])
    top = jnp.concatenate([sp[:, :, 0], z], axis=-1)
    bot = jnp.concatenate([z, sp[:, :, 1]], axis=-1)
    return jnp.concatenate([top, bot], axis=-2)


def _block_diag_to_states(s):
    b, p = s.shape[:2]
    a = s[:, :, :HEAD_DIM, :HEAD_DIM]
    c = s[:, :, HEAD_DIM:, HEAD_DIM:]
    return jnp.stack([a, c], axis=2).reshape(b, 2 * p, HEAD_DIM, HEAD_DIM)


def _s5_scan_kernel(x_ref, s0re_ref, s0im_ref, wbre_ref, wbim_ref, are_ref, aim_ref, wcre_ref, wcim_ref, d_ref,
                    y_ref, sre_ref, sim_ref, xs_ref, bre_ref, bim_ref, stre_ref, stim_ref, xtail_ref, *, pack):
    step = pl.program_id(0)
    nb, tt, d = x_ref.shape
    b8 = xs_ref.shape[1] // tt
    ns = stre_ref.shape[1]
    n_kb = d // LANES
    per_kb = ns // n_kb
    state_rows = slice((pack - 1) * nb, pack * nb)

    @pl.when(step == 0)
    def _():
        stre_ref[...] = jnp.zeros_like(stre_ref)
        stim_ref[...] = jnp.zeros_like(stim_ref)
        for r in range(pack):
            stre_ref[r * nb:(r + 1) * nb, :] = s0re_ref[...]
            stim_ref[r * nb:(r + 1) * nb, :] = s0im_ref[...]
        xs_ref[...] = jnp.zeros_like(xs_ref)
        xtail_ref[...] = jnp.zeros_like(xtail_ref)

    for b in range(nb):
        for kb in range(n_kb):
            xs_ref[kb, pl.ds(b, tt, stride=b8), :] = x_ref[b, :, kb * LANES:(kb + 1) * LANES]

    n_out = wcre_ref.shape[0]
    kin = ns // n_out
    wout = d // n_out
    kb_per_chunk = kin // per_kb

    first_half = lax.broadcasted_iota(jnp.int32, (SUBLANES, 1), 0) < nb

    def project_in(c):
        for kb in range(c * kb_per_chunk, (c + 1) * kb_per_chunk):
            cols = slice(kb * per_kb, (kb + 1) * per_kb)
            x_t = xs_ref[kb]
            if pack == 1:
                lhs = x_t.astype(BF16)
                w_re, w_im = wbre_ref[kb, 0:LANES, :], wbim_ref[kb, 0:LANES, :]
            else:
                x_prev = pltpu.roll(x_t, nb, 0)
                head = jnp.where(first_half, pltpu.roll(xtail_ref[kb], nb, 0), x_prev[0:SUBLANES])
                x_prev = jnp.concatenate([head, x_prev[SUBLANES:]], axis=0)
                xtail_ref[kb] = x_t[x_t.shape[0] - SUBLANES:]
                lhs = jnp.concatenate([x_t, x_prev], axis=1).astype(BF16)
                w_re, w_im = wbre_ref[kb], wbim_ref[kb]
            bre_ref[:, cols] = _dot(lhs, w_re)
            bim_ref[:, cols] = _dot(lhs, w_im)

    def scan_packed(cs):
        width = cs.stop - cs.start
        are = jnp.broadcast_to(are_ref[:, cs], (SUBLANES, width))
        aim = jnp.broadcast_to(aim_ref[:, cs], (SUBLANES, width))
        a2re, a2im = are * are - aim * aim, 2.0 * are * aim
        sre, sim = stre_ref[:, cs], stim_ref[:, cs]
        for v in range(tt // pack):
            rows = slice(v * SUBLANES, (v + 1) * SUBLANES)
            if v == 0:
                start = jnp.logical_and(first_half, step == 0)
                cre, cim = jnp.where(start, are, a2re), jnp.where(start, aim, a2im)
            else:
                cre, cim = a2re, a2im
            sre, sim = (bre_ref[rows, cs] + (cre * sre - cim * sim), bim_ref[rows, cs] + (cre * sim + cim * sre))
            bre_ref[rows, cs] = sre
            bim_ref[rows, cs] = sim
        stre_ref[:, cs] = sre
        stim_ref[:, cs] = sim

    def scan(c):
        if pack == 2:
            half = kin // 2
            for c0 in range(c * kin, (c + 1) * kin, half):
                scan_packed(slice(c0, c0 + half))
            return
        cs = slice(c * kin, (c + 1) * kin)
        are = jnp.broadcast_to(are_ref[:, cs], (SUBLANES, kin))
        aim = jnp.broadcast_to(aim_ref[:, cs], (SUBLANES, kin))
        for r0 in range(0, b8, SUBLANES):
            sre = stre_ref[r0:r0 + SUBLANES, cs]
            sim = stim_ref[r0:r0 + SUBLANES, cs]
            for t in range(tt):
                rows = slice(t * b8 + r0, t * b8 + r0 + SUBLANES)
                sre, sim = (are * sre - aim * sim + bre_ref[rows, cs], are * sim + aim * sre + bim_ref[rows, cs])
                bre_ref[rows, cs] = sre
                bim_ref[rows, cs] = sim
            stre_ref[r0:r0 + SUBLANES, cs] = sre
            stim_ref[r0:r0 + SUBLANES, cs] = sim

    def project_out(c):
        cs = slice(c * kin, (c + 1) * kin)
        y = _dot(bre_ref[:, cs].astype(BF16), wcre_ref[c]) + _dot(bim_ref[:, cs].astype(BF16), wcim_ref[c])
        for i in range(wout // LANES):
            kb = c * (wout // LANES) + i
            xs_ref[kb] = y[:, i * LANES:(i + 1) * LANES] + xs_ref[kb] * d_ref[:, kb * LANES:(kb + 1) * LANES]

    project_in(0)
    for c in range(n_out):
        if c + 1 < n_out:
            project_in(c + 1)
        scan(c)
        if c >= 1:
            project_out(c - 1)
    project_out(n_out - 1)

    for b in range(nb):
        for kb in range(n_kb):
            y_ref[b, :, kb * LANES:(kb + 1) * LANES] = xs_ref[kb, pl.ds(b, tt, stride=b8), :]

    @pl.when(step == pl.num_programs(0) - 1)
    def _():
        sre_ref[...] = stre_ref[state_rows, :]
        sim_ref[...] = stim_ref[state_rows, :]


def _s5_weights(lam_re, lam_im, b_re, b_im, c_re, c_im, log_dt):
    g, p, h = b_re.shape
    lam = lax.complex(lam_re.astype(F32), lam_im.astype(F32))
    dt = jnp.exp(log_dt.astype(F32))[:, None]
    lam_bar = jnp.exp(lam * dt)
    b_bar = ((lam_bar - 1.0) / lam)[:, :, None] * lax.complex(b_re.astype(F32), b_im.astype(F32))
    gpk = LANES // h
    n_kb = g // gpk
    eye = jnp.eye(gpk, dtype=F32)

    def in_blocks(m):
        m = m.reshape(n_kb, gpk, p, h)
        return jnp.einsum("kgph,gf->kghfp", m, eye).reshape(n_kb, gpk * h, gpk * p).astype(BF16)

    gpo = 2 * LANES // h
    n_out = g // gpo
    eye_o = jnp.eye(gpo, dtype=F32)

    def out_blocks(m):
        m = m.reshape(n_out, gpo, h, p)
        return jnp.einsum("kghp,gf->kgpfh", m, eye_o).reshape(n_out, gpo * p, gpo * h).astype(BF16)

    ab_bar = lam_bar[:, :, None] * b_bar
    two_step = lambda part: jnp.concatenate([in_blocks(part(b_bar)), in_blocks(part(ab_bar))], axis=1)
    return (two_step(jnp.real), two_step(jnp.imag),
            jnp.real(lam_bar).reshape(1, g * p), jnp.imag(lam_bar).reshape(1, g * p),
            out_blocks(c_re.astype(F32)), out_blocks(-c_im.astype(F32)))


def s5_scan(x, s0_re, s0_im, weights, d_skip, *, tt):
    nb, seq, d = x.shape
    wbre, wbim, are, aim, wcre, wcim = weights
    ns = are.shape[1]
    pack = 2 if (2 * nb == SUBLANES and tt % 2 == 0) else 1
    b8 = nb if pack == 2 else -(-nb // SUBLANES) * SUBLANES
    rows = tt * b8
    state_rows = SUBLANES if pack == 2 else b8
    const = lambda a: pl.BlockSpec(a.shape, lambda i: (0,) * a.ndim)
    xblk = pl.BlockSpec((nb, tt, d), lambda i: (0, i, 0))
    sblk = pl.BlockSpec((nb, ns), lambda i: (0, 0))
    return pl.pallas_call(
        functools.partial(_s5_scan_kernel, pack=pack),
        grid=(seq // tt,),
        in_specs=[xblk, sblk, sblk, const(wbre), const(wbim), const(are), const(aim), const(wcre), const(wcim),
                  pl.BlockSpec((1, d), lambda i: (0, 0))],
        out_specs=[xblk, sblk, sblk],
        out_shape=[jax.ShapeDtypeStruct((nb, seq, d), F32), jax.ShapeDtypeStruct((nb, ns), F32),
                   jax.ShapeDtypeStruct((nb, ns), F32)],
        scratch_shapes=[pltpu.VMEM((d // LANES, rows, LANES), F32), pltpu.VMEM((rows, ns), F32),
                        pltpu.VMEM((rows, ns), F32),
                        pltpu.VMEM((state_rows, ns), F32), pltpu.VMEM((state_rows, ns), F32),
                        pltpu.VMEM((d // LANES, SUBLANES, LANES), F32)],
        compiler_params=_params("arbitrary"),
    )(x, s0_re, s0_im, wbre, wbim, are, aim, wcre, wcim, d_skip)


def _s5_out_ln_kernel(y_ref, wo_ref, wg_ref, x_ref, g_ref, b_ref, o_ref, *, alpha):
    gl = jax.nn.gelu(y_ref[...]).astype(BF16)
    mix = _dot(gl, wo_ref[...]) * jax.nn.sigmoid(_dot(gl, wg_ref[...]))
    o_ref[...] = _layer_norm(alpha * x_ref[...] + mix, g_ref[...], b_ref[...])


def s5_out_ln(y, w_out, w_gate, x, g, b, *, layer, alpha, tm):
    t, d = x.shape
    row = pl.BlockSpec((tm, d), lambda i: (i, 0))
    const = lambda r, c: pl.BlockSpec((r, c), lambda i: (0, 0))
    return pl.pallas_call(
        functools.partial(_s5_out_ln_kernel, alpha=alpha),
        grid=(t // tm,),
        in_specs=[row, _layer_weight(layer, d, d), _layer_weight(layer, d, d), row, const(1, d), const(1, d)],
        out_specs=row,
        out_shape=jax.ShapeDtypeStruct((t, d), F32),
        compiler_params=_params("parallel"),
    )(y, w_out, w_gate, x, g, b)


class _TilePlan(NamedTuple):
    token_rows: int
    proj_rows: int
    s5_steps: int
    ret_tiles: int
    ret_seqs_per_tile: int


def _tile_plan(lp, ls):
    return _TilePlan(token_rows=min(1024, lp), proj_rows=min(512, lp), s5_steps=min(128, lp),
                     ret_tiles=min(8, lp // RET_CHUNK), ret_seqs_per_tile=max(RET_CHUNK // ls, 1))


def _rope_tables(pos, n_heads):
    half = HEAD_DIM // 2
    inv = ROPE_THETA ** (-jnp.arange(half, dtype=F32) / half)
    ang = pos.astype(F32)[:, None] * inv[None, :]
    cos = jnp.cos(ang)
    sin = jnp.sin(ang)
    cos_h = jnp.concatenate([cos, cos], axis=-1)
    sin_h = jnp.concatenate([-sin, sin], axis=-1)
    return jnp.tile(cos_h, (1, n_heads)), jnp.tile(sin_h, (1, n_heads))


def kernel(x_prompt, x_sample, cache_k, cache_v, page_table, state_ret, state_s5_re, state_s5_im, ffn1_w_gate, ffn1_w_up, ffn1_w_down, ffn2_w_gate, ffn2_w_up, ffn2_w_down, ln_g, ln_b, w_in_ab, w_out_ab, s5_lam_re, s5_lam_im, s5_b_re, s5_b_im, s5_c_re, s5_c_im, s5_d, s5_log_dt, s5_w_out, s5_w_gate):
    bp, lp, d = x_prompt.shape
    bs, ls, _ = x_sample.shape
    depth = ffn1_w_gate.shape[0]
    n_layers_ab, n_pool, page, a_heads, hd = cache_k.shape
    n_pages = page_table.shape[1]
    past_len = n_pages * page
    half = a_heads * hd
    n_heads = half // HEAD_DIM
    assert hd == HEAD_DIM and lp % MOBA_BLOCK == 0 and past_len % MOBA_BLOCK == 0 and ls <= MOBA_BLOCK
    assert MOBA_BLOCK % page == 0 and lp % page == 0 and lp % RET_CHUNK == 0
    alpha = (2 * depth) ** 0.25
    tp, ts = bp * lp, bs * ls
    plan = _tile_plan(lp, ls)
    tm_p, tm_proj = plan.token_rows, plan.proj_rows

    xp = x_prompt.reshape(tp, d)
    xs = x_sample.reshape(ts, d)
    bf = lambda a: a.astype(BF16)
    f1 = (bf(ffn1_w_gate), bf(ffn1_w_up), bf(ffn1_w_down))
    f2 = (bf(ffn2_w_gate), bf(ffn2_w_up), bf(ffn2_w_down))
    w_in, w_out = bf(w_in_ab), bf(w_out_ab)
    w_o, w_g = bf(s5_w_out), bf(s5_w_gate)
    cos_p, sin_p = _rope_tables(jnp.arange(lp, dtype=jnp.int32), n_heads)
    cos_s, sin_s = _rope_tables(past_len + jnp.arange(ls, dtype=jnp.int32), n_heads)
    cos_s, sin_s = jnp.tile(cos_s, (bs, 1)), jnp.tile(sin_s, (bs, 1))
    pool_kt = jnp.transpose(cache_k, (0, 1, 3, 4, 2)).reshape(n_layers_ab * n_pool, half, page)
    pool_vt = jnp.transpose(cache_v, (0, 1, 3, 4, 2)).reshape(n_layers_ab * n_pool, half, page)
    seq_per_tile, ret_tiles_p = plan.ret_seqs_per_tile, plan.ret_tiles

    cache_pages = None
    k_s, v_s, r_p, r_s = [], [], [], []
    sre_p, sim_p, sre_s, sim_s = [], [], [], []
    for layer in range(depth):
        li = layer // 2
        g = lambda i: ln_g[layer, i][None, :]
        b = lambda i: ln_b[layer, i][None, :]
        xp, xs = ffn_ln(xp, xs, *f1, g(0), b(0), layer=layer, alpha=alpha, tm=tm_p)
        if layer % 2 == 0:
            qa, kbf, vt, ksum, kpg, vpg, qb, kb, vb, gb = ab_project(xp, w_in, cos_p, sin_p, layer=li, tm=tm_proj,
                                                                     seq_len=lp, page=page, pages=cache_pages)
            cache_pages = (kpg, vpg)
            attn = moba_prompt(qa, ksum, kbf, vt, batch=bp, seq_len=lp)
            zero_state = jnp.zeros((bp, n_heads // 2, HEAD_PAIR, HEAD_PAIR), F32)
            xp, s_fin = retention_merge_ln(qb, kb, vb, gb, zero_state, attn, xp, w_out, g(1), b(1), layer=li,
                                           alpha=alpha, n_seq=1, chunk=RET_CHUNK, tiles=ret_tiles_p,
                                           steps_per_seq=lp // (RET_CHUNK * ret_tiles_p))
            r_p.append(_block_diag_to_states(s_fin))
            qa, ka, va, qb, kb, vb, gb = ab_project(xs, w_in, cos_s, sin_s, layer=li, tm=ts, seq_len=ts)
            page_ids = (page_table.astype(jnp.int32) + li * n_pool).reshape(-1)
            attn = moba_sample(qa, ka, va, pool_kt, pool_vt, page_ids, batch=bs, n_pages=n_pages)
            xs, s_fin = retention_merge_ln(qb, kb, vb, gb, _states_to_block_diag(state_ret[li]), attn, xs, w_out,
                                           g(1), b(1), layer=li, alpha=alpha, n_seq=seq_per_tile, chunk=ls, tiles=1,
                                           steps_per_seq=1)
            k_s.append(ka.reshape(bs, ls, a_heads, hd))
            v_s.append(va.reshape(bs, ls, a_heads, hd))
            r_s.append(_block_diag_to_states(s_fin))
        else:
            weights = _s5_weights(s5_lam_re[li], s5_lam_im[li], s5_b_re[li], s5_b_im[li], s5_c_re[li],
                                  s5_c_im[li], s5_log_dt[li])
            n_state = weights[2].shape[1]
            d_skip = s5_d[li][None, :]
            zero = jnp.zeros((bp, n_state), F32)
            y, a_re, a_im = s5_scan(xp.reshape(bp, lp, d), zero, zero, weights, d_skip, tt=plan.s5_steps)
            xp = s5_out_ln(y.reshape(tp, d), w_o, w_g, xp, g(1), b(1), layer=li, alpha=alpha, tm=tm_p)
            sre_p.append(a_re.reshape(bp, -1, S5_STATE))
            sim_p.append(a_im.reshape(bp, -1, S5_STATE))
            y, a_re, a_im = s5_scan(xs.reshape(bs, ls, d), state_s5_re[li].reshape(bs, n_state),
                                    state_s5_im[li].reshape(bs, n_state), weights, d_skip, tt=ls)
            xs = s5_out_ln(y.reshape(ts, d), w_o, w_g, xs, g(1), b(1), layer=li, alpha=alpha, tm=ts)
            sre_s.append(a_re.reshape(bs, -1, S5_STATE))
            sim_s.append(a_im.reshape(bs, -1, S5_STATE))
        xp, xs = ffn_ln(xp, xs, *f2, g(2), b(2), layer=layer, alpha=alpha, tm=tm_p)
    unpage = lambda t: jnp.transpose(t.reshape(n_layers_ab, bp, lp // page, a_heads, hd, page), (0, 1, 2, 5, 3, 4))
    return (xp.reshape(bp, lp, d), xs.reshape(bs, ls, d), unpage(kpg), unpage(vpg), jnp.stack(k_s),
            jnp.stack(v_s), jnp.stack(r_p), jnp.stack(r_s), jnp.stack(sre_p), jnp.stack(sim_p),
            jnp.stack(sre_s), jnp.stack(sim_s))
```
